```python
import math
import jax, jax.numpy as jnp
from jax import lax
import numpy as np

D_MODEL = 2048
BATCH = 4
SEQ = 2048
DEPTH = 1
DEC_BATCH = 128
DEC_SEQ = 8
PAST_LEN = 16384
PAGE_SIZE = 128

H_GLA = 4
DK_GLA = 128
DV_GLA = 256
GLA_LOWRANK = 16
GLA_GATE_NORM = 16.0
GLA_CHUNK = 16
H_RET = 4
DK_RET = 256
DV_RET = 256
RET_CHUNK = 64
ROPE_BASE = 10000.0
GLA_QK = H_GLA * DK_GLA
GLA_V = H_GLA * DV_GLA
RET_QK = H_RET * DK_RET
RET_V = H_RET * DV_RET
MIX_WIDTH = GLA_V + RET_V
IN_SIZES = (GLA_QK, GLA_QK, GLA_V, GLA_LOWRANK, GLA_V, RET_QK, RET_QK, RET_V, RET_V)
IN_COLS = sum(IN_SIZES)
N_GROUPS = 4
EXPERTS_PER_GROUP = 8
N_EXPERTS = N_GROUPS * EXPERTS_PER_GROUP
TOP_K_IN_GROUP = 2
D_EXPERT = 512
EPS = 1e-6

kernel_name = 'hymba_gla_retnet_hmoe_step'


def rmsnorm(x, w):
    xf = x.astype(jnp.float32)
    y = xf * lax.rsqrt(jnp.mean(xf * xf, axis=-1, keepdims=True) + EPS) * w.astype(jnp.float32)
    return y.astype(x.dtype)


def head_rmsnorm(o):
    return o * lax.rsqrt(jnp.mean(o * o, axis=-1, keepdims=True) + EPS)


def rope(x, pos):
    half = x.shape[-1] // 2
    inv = ROPE_BASE ** (-jnp.arange(half, dtype=jnp.float32) / half)
    ang = pos[:, None] * inv[None, :]
    cos = jnp.cos(ang)[None, :, None, :]
    sin = jnp.sin(ang)[None, :, None, :]
    x1, x2 = x[..., :half], x[..., half:]
    return jnp.concatenate([x1 * cos - x2 * sin, x1 * sin + x2 * cos], axis=-1)


def to_chunks(a, c):
    B, T, H, d = a.shape
    return a.reshape(B, T // c, c, H, d).transpose(1, 0, 3, 2, 4)


def from_chunks(o):
    n, B, H, c, d = o.shape
    return o.transpose(1, 0, 3, 2, 4).reshape(B, n * c, H, d)


def gla_chunked(q, k, v, log_g, s0):
    T = q.shape[1]
    c = math.gcd(GLA_CHUNK, T)
    mask = jnp.tril(jnp.ones((c, c), dtype=bool))

    def step(S, inp):
        qc, kc, vc, gc = inp
        b = jnp.cumsum(gc, axis=-2)
        b_last = b[..., -1:, :]
        q_t = qc * jnp.exp(b)
        k_t = kc * jnp.exp(-b)
        att = jnp.where(mask, jnp.einsum('bhtk,bhsk->bhts', q_t, k_t), 0.0)
        o = jnp.einsum('bhts,bhsv->bhtv', att, vc) + jnp.einsum('bhtk,bhkv->bhtv', q_t, S)
        k_dec = kc * jnp.exp(b_last - b)
        S = jnp.exp(b_last)[..., 0, :, None] * S + jnp.einsum('bhsk,bhsv->bhkv', k_dec, vc)
        return S, o

    S, o = lax.scan(step, s0, (to_chunks(q, c), to_chunks(k, c), to_chunks(v, c), to_chunks(log_g, c)))
    return from_chunks(o), S


def retention_chunked(q, k, v, s0):
    T = q.shape[1]
    c = math.gcd(RET_CHUNK, T)
    lg = jnp.log(1.0 - 2.0 ** (-5.0 - jnp.arange(H_RET, dtype=jnp.float32)))
    idx = jnp.arange(c, dtype=jnp.float32)
    diff = idx[:, None] - idx[None, :]
    dmat = jnp.where(diff >= 0, jnp.exp(lg[:, None, None] * jnp.maximum(diff, 0.0)), 0.0)
    q_dec = jnp.exp(lg[:, None] * (idx + 1.0))[None, :, :, None]
    k_dec = jnp.exp(lg[:, None] * (c - 1.0 - idx))[None, :, :, None]
    c_dec = jnp.exp(lg * c)[None, :, None, None]

    def step(S, inp):
        qc, kc, vc = inp
        att = jnp.einsum('bhtk,bhsk->bhts', qc, kc) * dmat[None]
        o = jnp.einsum('bhts,bhsv->bhtv', att, vc) + jnp.einsum('bhtk,bhkv->bhtv', qc, S) * q_dec
        S = c_dec * S + jnp.einsum('bhsk,bhsv->bhkv', kc * k_dec, vc)
        return S, o

    S, o = lax.scan(step, s0, (to_chunks(q, c), to_chunks(k, c), to_chunks(v, c)))
    return from_chunks(o), S


def parallel_mixer(h, pos, s_gla, s_ret, w_in, w_gk2, b_gk, gla_norm_w, w_out):
    B, T, _ = h.shape
    proj = (h @ w_in).astype(jnp.float32)
    splits = [int(s) for s in np.cumsum(IN_SIZES)[:-1]]
    q_a, k_a, v_a, lr_a, g_a, q_r, k_r, v_r, g_r = jnp.split(proj, splits, axis=-1)
    q_a = q_a.reshape(B, T, H_GLA, DK_GLA) * (DK_GLA ** -0.5)
    k_a = k_a.reshape(B, T, H_GLA, DK_GLA)
    v_a = v_a.reshape(B, T, H_GLA, DV_GLA)
    log_g = jax.nn.log_sigmoid(lr_a @ w_gk2.astype(jnp.float32) + b_gk.astype(jnp.float32)) / GLA_GATE_NORM
    log_g = log_g.reshape(B, T, H_GLA, DK_GLA)
    o_a, s_gla_new = gla_chunked(q_a, k_a, v_a, log_g, s_gla.astype(jnp.float32))
    o_a = (head_rmsnorm(o_a) * gla_norm_w.astype(jnp.float32)).reshape(B, T, GLA_V) * jax.nn.silu(g_a)
    q_r = rope(q_r.reshape(B, T, H_RET, DK_RET), pos)
    k_r = rope(k_r.reshape(B, T, H_RET, DK_RET), pos) * (DK_RET ** -0.5)
    v_r = v_r.reshape(B, T, H_RET, DV_RET)
    o_r, s_ret_new = retention_chunked(q_r, k_r, v_r, s_ret.astype(jnp.float32))
    o_r = head_rmsnorm(o_r).reshape(B, T, RET_V) * jax.nn.silu(g_r)
    o = jnp.concatenate([o_a, o_r], axis=-1).astype(h.dtype)
    return o @ w_out, s_gla_new, s_ret_new


def hierarchical_moe(h, w_rg, b_rg, w_re, b_re, w_gate, w_up, w_down):
    B, T, D = h.shape
    hf = h.reshape(B * T, D)
    p_g = jax.nn.softmax((hf @ w_rg).astype(jnp.float32) + b_rg.astype(jnp.float32), axis=-1)
    g_w, g_idx = lax.top_k(p_g, 1)
    lg_e = ((hf @ w_re).astype(jnp.float32) + b_re.astype(jnp.float32)).reshape(-1, N_GROUPS, EXPERTS_PER_GROUP)
    lg_sel = jnp.take_along_axis(lg_e, g_idx[:, :, None], axis=1)[:, 0]
    e_w, e_idx = lax.top_k(jax.nn.softmax(lg_sel, axis=-1), TOP_K_IN_GROUP)
    e_w = e_w / jnp.sum(e_w, axis=-1, keepdims=True)
    gates = g_w * e_w
    expert_ids = g_idx * EXPERTS_PER_GROUP + e_idx
    combine = jnp.einsum('nk,nke->ne', gates, jax.nn.one_hot(expert_ids, N_EXPERTS, dtype=jnp.float32))
    hid = jax.nn.silu(jnp.einsum('nd,edf->nef', hf, w_gate)) * jnp.einsum('nd,edf->nef', hf, w_up)
    hid = hid * combine[:, :, None].astype(hid.dtype)
    out = jnp.einsum('nef,efd->nd', hid, w_down)
    return out.reshape(B, T, D)


def run_trunk(x, pos, s_gla_all, s_ret_all, ln_attn, w_in, w_gk2, b_gk, gla_norm_w, w_out,
              ln_ffn, w_router_group, b_router_group, w_router_expert, b_router_expert,
              w_exp_gate, w_exp_up, w_exp_down, ln_final):
    new_gla, new_ret = [], []
    for l in range(DEPTH):
        mix, s_a, s_r = parallel_mixer(rmsnorm(x, ln_attn[l]), pos, s_gla_all[l], s_ret_all[l],
                                       w_in[l], w_gk2[l], b_gk[l], gla_norm_w[l], w_out[l])
        x = x + mix
        x = x + hierarchical_moe(rmsnorm(x, ln_ffn[l]), w_router_group[l], b_router_group[l],
                                 w_router_expert[l], b_router_expert[l],
                                 w_exp_gate[l], w_exp_up[l], w_exp_down[l])
        new_gla.append(s_a)
        new_ret.append(s_r)
    return rmsnorm(x, ln_final), jnp.stack(new_gla, axis=0), jnp.stack(new_ret, axis=0)


def setup_inputs(seed: int = 0) -> dict:
    key = jax.random.key(seed)
    ks = jax.random.split(key, 20)
    f32 = jnp.float32
    nrm = lambda k, shape, s: jax.random.normal(k, shape, f32) * s
    return {
        'x_prompt': nrm(ks[0], (BATCH, SEQ, D_MODEL), 1.0),
        'x_sample': nrm(ks[1], (DEC_BATCH, DEC_SEQ, D_MODEL), 1.0),
        'state_gla': nrm(ks[2], (DEPTH, DEC_BATCH, H_GLA, DK_GLA, DV_GLA), 0.1),
        'state_ret': nrm(ks[3], (DEPTH, DEC_BATCH, H_RET, DK_RET, DV_RET), 0.1),
        'ln_attn': 1.0 + nrm(ks[4], (DEPTH, D_MODEL), 0.02),
        'w_in': nrm(ks[5], (DEPTH, D_MODEL, IN_COLS), D_MODEL ** -0.5),
        'w_gk2': nrm(ks[6], (DEPTH, GLA_LOWRANK, GLA_QK), GLA_LOWRANK ** -0.5),
        'b_gk': nrm(ks[7], (DEPTH, GLA_QK), 0.1),
        'gla_norm_w': 1.0 + nrm(ks[8], (DEPTH, DV_GLA), 0.02),
        'w_out': nrm(ks[9], (DEPTH, MIX_WIDTH, D_MODEL), MIX_WIDTH ** -0.5),
        'ln_ffn': 1.0 + nrm(ks[10], (DEPTH, D_MODEL), 0.02),
        'w_router_group': nrm(ks[11], (DEPTH, D_MODEL, N_GROUPS), D_MODEL ** -0.5),
        'b_router_group': nrm(ks[12], (DEPTH, N_GROUPS), 0.01),
        'w_router_expert': nrm(ks[13], (DEPTH, D_MODEL, N_EXPERTS), D_MODEL ** -0.5),
        'b_router_expert': nrm(ks[14], (DEPTH, N_EXPERTS), 0.01),
        'w_exp_gate': nrm(ks[15], (DEPTH, N_EXPERTS, D_MODEL, D_EXPERT), D_MODEL ** -0.5),
        'w_exp_up': nrm(ks[16], (DEPTH, N_EXPERTS, D_MODEL, D_EXPERT), D_MODEL ** -0.5),
        'w_exp_down': nrm(ks[17], (DEPTH, N_EXPERTS, D_EXPERT, D_MODEL), D_EXPERT ** -0.5),
        'ln_final': 1.0 + nrm(ks[18], (D_MODEL,), 0.02),
    }


def reference(x_prompt, x_sample, state_gla, state_ret, ln_attn, w_in, w_gk2, b_gk, gla_norm_w,
              w_out, ln_ffn, w_router_group, b_router_group, w_router_expert, b_router_expert,
              w_exp_gate, w_exp_up, w_exp_down, ln_final):
    weights = (ln_attn, w_in, w_gk2, b_gk, gla_norm_w, w_out, ln_ffn, w_router_group,
               b_router_group, w_router_expert, b_router_expert, w_exp_gate, w_exp_up,
               w_exp_down, ln_final)
    Bp, Tp, _ = x_prompt.shape
    pos_p = jnp.arange(Tp, dtype=jnp.float32)
    zg = jnp.zeros((DEPTH, Bp, H_GLA, DK_GLA, DV_GLA), jnp.float32)
    zr = jnp.zeros((DEPTH, Bp, H_RET, DK_RET, DV_RET), jnp.float32)
    y_prompt, sg_p, sr_p = run_trunk(x_prompt, pos_p, zg, zr, *weights)
    Ts = x_sample.shape[1]
    pos_s = jnp.arange(Ts, dtype=jnp.float32) + float(PAST_LEN)
    y_sample, sg_s, sr_s = run_trunk(x_sample, pos_s, state_gla, state_ret, *weights)
    return (y_prompt, y_sample,
            sg_p.astype(x_prompt.dtype), sr_p.astype(x_prompt.dtype),
            sg_s.astype(state_gla.dtype), sr_s.astype(state_ret.dtype))
```

```python
import functools
import math

import jax
import jax.numpy as jnp
from jax import lax
from jax.experimental import pallas as pl
from jax.experimental.pallas import tpu as pltpu

F32 = jnp.float32
BF16 = jnp.bfloat16
I32 = jnp.int32

D_MODEL = 2048
PAST_LEN = 16384
H_GLA, DK_GLA, DV_GLA = 4, 128, 256
GLA_LOWRANK = 16
GLA_GATE_NORM = 16.0
GLA_SUBCHUNK = 16
H_RET, DK_RET, DV_RET = 4, 256, 256
ROPE_BASE = 10000.0
N_GROUPS, EXPERTS_PER_GROUP, D_EXPERT = 4, 8, 512
N_EXPERTS = N_GROUPS * EXPERTS_PER_GROUP
EPS = 1e-6

LANES = 128
SUBLANES = 8
V7X_SCOPED_VMEM_BYTES = 56 * 1024 * 1024

GLA_QK = H_GLA * DK_GLA
GLA_V = H_GLA * DV_GLA
RET_QK = H_RET * DK_RET
RET_V = H_RET * DV_RET
C_QA, C_KA, C_VA, C_GA = 0, GLA_QK, 2 * GLA_QK, 2 * GLA_QK + GLA_V
C_QR = C_GA + GLA_V
C_KR, C_VR, C_GR = C_QR + RET_QK, C_QR + 2 * RET_QK, C_QR + 2 * RET_QK + RET_V
C_LR = C_GR + RET_V
P_COLS = C_LR + LANES
IN_COLS_SRC = 2 * GLA_QK + GLA_V + GLA_LOWRANK + GLA_V + 2 * RET_QK + 2 * RET_V

PROJ_TM = 256
PROJ_COL_CHUNKS = 3
MOE_TM = 256
GLA_CHUNK = 128
RET_CHUNK = 128

_HI = lax.Precision.HIGHEST


def _dot(a, b, **kw):
    return jnp.dot(a, b, preferred_element_type=F32, **kw)


def _dot_nt(a, b):
    return lax.dot_general(a, b, (((1,), (1,)), ((), ())), preferred_element_type=F32)


def _dot_tn(a, b, **kw):
    return lax.dot_general(a, b, (((0,), (0,)), ((), ())), preferred_element_type=F32, **kw)


def _rms(x):
    return x * lax.rsqrt(jnp.mean(x * x, axis=-1, keepdims=True) + EPS)


def _silu(x):
    return x * jax.nn.sigmoid(x)


def _inproj_body(x_ref, ln_ref, w_ref, o_ref):
    h = (_rms(x_ref[...]) * ln_ref[...]).astype(BF16)
    cw = P_COLS // PROJ_COL_CHUNKS
    for c in range(PROJ_COL_CHUNKS):
        o_ref[:, c * cw:(c + 1) * cw] = _dot(h, w_ref[:, c * cw:(c + 1) * cw])


def _inproj(x, ln, w):
    n = x.shape[0]
    return pl.pallas_call(
        _inproj_body,
        out_shape=jax.ShapeDtypeStruct((n, P_COLS), F32),
        grid=(n // PROJ_TM,),
        in_specs=[pl.BlockSpec((PROJ_TM, D_MODEL), lambda i: (i, 0)),
                  pl.BlockSpec((1, D_MODEL), lambda i: (0, 0)),
                  pl.BlockSpec((D_MODEL, P_COLS), lambda i: (0, 0), pipeline_mode=pl.Buffered(1))],
        out_specs=pl.BlockSpec((PROJ_TM, P_COLS), lambda i: (i, 0)),
        compiler_params=pltpu.CompilerParams(dimension_semantics=("arbitrary",),
                                             vmem_limit_bytes=V7X_SCOPED_VMEM_BYTES),
        name="inproj",
    )(x, ln, w)


def _seg_cumsum(x, seg):
    pos = lax.broadcasted_iota(I32, x.shape, 0) & (seg - 1)
    s = 1
    while s < seg:
        x = x + jnp.where(pos >= s, pltpu.roll(x, s, axis=0), 0.0)
        s *= 2
    return x


def _block_ends(p, m):
    rows, w = p.shape
    return jnp.concatenate(
        [jnp.broadcast_to(p[j * m + m - 1:j * m + m, :], (m, w)) for j in range(rows // m)], axis=0)


def _block_starts(p, m, chunk):
    rows, w = p.shape
    pieces = []
    for j in range(rows // m):
        if (j * m) % chunk == 0:
            pieces.append(jnp.zeros((m, w), F32))
        else:
            pieces.append(jnp.broadcast_to(p[j * m - 1:j * m, :], (m, w)))
    return jnp.concatenate(pieces, axis=0)


def _gla_body(*refs, chunk, nseq, has_init):
    if has_init:
        (q_ref, k_ref, v_ref, g_ref, lr_ref, wgk_ref, bgk_ref, nw_ref, s0_ref,
         o_ref, sout_ref, s_scr) = refs
    else:
        (q_ref, k_ref, v_ref, g_ref, lr_ref, wgk_ref, bgk_ref, nw_ref,
         o_ref, sout_ref, s_scr) = refs
    t = pl.program_id(1)

    @pl.when(t == 0)
    def _():
        if has_init:
            s_scr[...] = s0_ref[...]
        else:
            s_scr[...] = jnp.zeros_like(s_scr)

    x = _dot(lr_ref[...], wgk_ref[...], precision=_HI) + bgk_ref[...]
    log_g = -(jnp.maximum(-x, 0.0) + jnp.log1p(jnp.exp(-jnp.abs(x)))) / GLA_GATE_NORM
    p = _seg_cumsum(log_g, chunk)
    p_end = _block_ends(p, chunk)
    eq_c = jnp.exp(p)
    ek_c = jnp.exp(p_end - p)
    sub = min(GLA_SUBCHUNK, chunk)
    p_sub = p - _block_starts(p, sub, chunk)
    eq_d = jnp.exp(p_sub)
    ek_d = jnp.exp(-p_sub)
    levels = []
    m = sub
    while m < chunk:
        levels.append((m, jnp.exp(p - _block_starts(p, m, chunk)), jnp.exp(_block_ends(p, m) - p)))
        m *= 2

    ti = lax.broadcasted_iota(I32, (chunk, chunk), 0)
    si = lax.broadcasted_iota(I32, (chunk, chunk), 1)
    blk = lambda idx, size: idx >> (size.bit_length() - 1)
    mask_d = (blk(ti, sub) == blk(si, sub)) & (si <= ti)
    masks = [((blk(ti, m) & 1) == 1) & (blk(si, m) == blk(ti, m) - 1) for (m, _, _) in levels]
    sel8 = (lax.broadcasted_iota(I32, (8, DV_GLA), 0) == 0).astype(F32)
    nw = nw_ref[...]

    outs = [[None] * nseq for _ in range(H_GLA)]
    for b in range(nseq):
        r0 = b * chunk
        rows = slice(r0, r0 + chunk)
        for h in range(H_GLA):
            kc = slice(h * DK_GLA, (h + 1) * DK_GLA)
            vc = slice(h * DV_GLA, (h + 1) * DV_GLA)
            q = q_ref[rows, kc] * (DK_GLA ** -0.5)
            k = k_ref[rows, kc]
            v = v_ref[rows, vc].astype(BF16)
            att = jnp.where(mask_d, _dot_nt((q * eq_d[rows, kc]).astype(BF16),
                                            (k * ek_d[rows, kc]).astype(BF16)), 0.0)
            for (m, eq_m, ek_m), mask in zip(levels, masks):
                att = jnp.where(mask, _dot_nt((q * eq_m[rows, kc]).astype(BF16),
                                              (k * ek_m[rows, kc]).astype(BF16)), att)
            s_old = s_scr[b, h]
            o = _dot(att.astype(BF16), v) + _dot((q * eq_c[rows, kc]).astype(BF16), s_old.astype(BF16))
            kv = _dot_tn((k * ek_c[rows, kc]).astype(BF16), v)
            tot = jnp.broadcast_to(p[r0 + chunk - 1:r0 + chunk, kc], (8, DK_GLA))
            decay = jnp.exp(_dot_tn(tot, sel8, precision=_HI))
            s_scr[b, h] = decay * s_old + kv
            outs[h][b] = _rms(o) * nw * _silu(g_ref[rows, vc])
    for h in range(H_GLA):
        o_ref[:, h * DV_GLA:(h + 1) * DV_GLA] = jnp.concatenate(outs[h], axis=0).astype(BF16)

    @pl.when(t == pl.num_programs(1) - 1)
    def _():
        sout_ref[...] = s_scr[...]


def _gla(proj, w_gk2p, b_gk, nw, s0, *, row0, n_seq, seq_len, chunk, nseq):
    rows = nseq * chunk
    nt = seq_len // chunk
    base = row0 // rows

    def rmap(col):
        return lambda b, t: (base + b * nt + t, col)

    in_specs = [pl.BlockSpec((rows, GLA_QK), rmap(C_QA // GLA_QK)),
                pl.BlockSpec((rows, GLA_QK), rmap(C_KA // GLA_QK)),
                pl.BlockSpec((rows, GLA_V), rmap(C_VA // GLA_V)),
                pl.BlockSpec((rows, GLA_V), rmap(C_GA // GLA_V)),
                pl.BlockSpec((rows, LANES), rmap(C_LR // LANES)),
                pl.BlockSpec((LANES, GLA_QK), lambda b, t: (0, 0)),
                pl.BlockSpec((1, GLA_QK), lambda b, t: (0, 0)),
                pl.BlockSpec((1, DV_GLA), lambda b, t: (0, 0))]
    args = [proj, proj, proj, proj, proj, w_gk2p, b_gk, nw]
    state_spec = pl.BlockSpec((nseq, H_GLA, DK_GLA, DV_GLA), lambda b, t: (b, 0, 0, 0))
    if s0 is not None:
        in_specs.append(state_spec)
        args.append(s0)
    return pl.pallas_call(
        functools.partial(_gla_body, chunk=chunk, nseq=nseq, has_init=s0 is not None),
        out_shape=(jax.ShapeDtypeStruct((n_seq * seq_len, GLA_V), BF16),
                   jax.ShapeDtypeStruct((n_seq, H_GLA, DK_GLA, DV_GLA), F32)),
        grid=(n_seq // nseq, nt),
        in_specs=in_specs,
        out_specs=(pl.BlockSpec((rows, GLA_V), lambda b, t: (b * nt + t, 0)), state_spec),
        scratch_shapes=[pltpu.VMEM((nseq, H_GLA, DK_GLA, DV_GLA), F32)],
        compiler_params=pltpu.CompilerParams(dimension_semantics=("arbitrary", "arbitrary"),
                                             vmem_limit_bytes=V7X_SCOPED_VMEM_BYTES),
        name="gla_init" if s0 is not None else "gla",
    )(*args)


def _ret_body(*refs, chunk, nseq, has_init):
    if has_init:
        (q_ref, k_ref, v_ref, g_ref, cos_ref, sin_ref, s0_ref, o_ref, sout_ref, s_scr) = refs
    else:
        (q_ref, k_ref, v_ref, g_ref, cos_ref, sin_ref, o_ref, sout_ref, s_scr) = refs
    t = pl.program_id(1)

    @pl.when(t == 0)
    def _():
        if has_init:
            s_scr[...] = s0_ref[...]
        else:
            s_scr[...] = jnp.zeros_like(s_scr)

    half = DK_RET // 2
    ti = lax.broadcasted_iota(I32, (chunk, chunk), 0)
    si = lax.broadcasted_iota(I32, (chunk, chunk), 1)
    diff = (ti - si).astype(F32)
    idx = lax.broadcasted_iota(I32, (chunk, 1), 0).astype(F32)

    outs = [[None] * nseq for _ in range(H_RET)]
    for h in range(H_RET):
        lg = math.log(1.0 - 2.0 ** (-5.0 - h))
        dmat = jnp.where(diff >= 0, jnp.exp(lg * jnp.maximum(diff, 0.0)), 0.0)
        q_dec = jnp.exp(lg * (idx + 1.0))
        k_dec = jnp.exp(lg * (chunk - 1.0 - idx))
        c_dec = math.exp(lg * chunk)
        for b in range(nseq):
            rows = slice(b * chunk, (b + 1) * chunk)
            cos = cos_ref[rows, :]
            sin = sin_ref[rows, :]
            c1 = slice(h * DK_RET, h * DK_RET + half)
            c2 = slice(h * DK_RET + half, (h + 1) * DK_RET)
            vc = slice(h * DV_RET, (h + 1) * DV_RET)
            q1, q2 = q_ref[rows, c1], q_ref[rows, c2]
            k1, k2 = k_ref[rows, c1], k_ref[rows, c2]
            q = jnp.concatenate([q1 * cos - q2 * sin, q1 * sin + q2 * cos], axis=-1)
            k = jnp.concatenate([k1 * cos - k2 * sin, k1 * sin + k2 * cos], axis=-1) * (DK_RET ** -0.5)
            v = v_ref[rows, vc].astype(BF16)
            qb = q.astype(BF16)
            s_old = s_scr[b, h]
            att = _dot_nt(qb, k.astype(BF16)) * dmat
            o = _dot(att.astype(BF16), v) + _dot(qb, s_old.astype(BF16)) * q_dec
            s_scr[b, h] = c_dec * s_old + _dot_tn((k * k_dec).astype(BF16), v)
            outs[h][b] = _rms(o) * _silu(g_ref[rows, vc])
    for h in range(H_RET):
        o_ref[:, h * DV_RET:(h + 1) * DV_RET] = jnp.concatenate(outs[h], axis=0).astype(BF16)

    @pl.when(t == pl.num_programs(1) - 1)
    def _():
        sout_ref[...] = s_scr[...]


def _ret(proj, cos, sin, s0, *, row0, n_seq, seq_len, chunk, nseq):
    rows = nseq * chunk
    nt = seq_len // chunk
    base = row0 // rows

    def rmap(col):
        return lambda b, t: (base + b * nt + t, col)

    tab_spec = pl.BlockSpec((rows, DK_RET // 2), lambda b, t: (t, 0))
    in_specs = [pl.BlockSpec((rows, RET_QK), rmap(C_QR // RET_QK)),
                pl.BlockSpec((rows, RET_QK), rmap(C_KR // RET_QK)),
                pl.BlockSpec((rows, RET_V), rmap(C_VR // RET_V)),
                pl.BlockSpec((rows, RET_V), rmap(C_GR // RET_V)),
                tab_spec, tab_spec]
    args = [proj, proj, proj, proj, cos, sin]
    state_spec = pl.BlockSpec((nseq, H_RET, DK_RET, DV_RET), lambda b, t: (b, 0, 0, 0))
    if s0 is not None:
        in_specs.append(state_spec)
        args.append(s0)
    return pl.pallas_call(
        functools.partial(_ret_body, chunk=chunk, nseq=nseq, has_init=s0 is not None),
        out_shape=(jax.ShapeDtypeStruct((n_seq * seq_len, RET_V), BF16),
                   jax.ShapeDtypeStruct((n_seq, H_RET, DK_RET, DV_RET), F32)),
        grid=(n_seq // nseq, nt),
        in_specs=in_specs,
        out_specs=(pl.BlockSpec((rows, RET_V), lambda b, t: (b * nt + t, 0)), state_spec),
        scratch_shapes=[pltpu.VMEM((nseq, H_RET, DK_RET, DV_RET), F32)],
        compiler_params=pltpu.CompilerParams(dimension_semantics=("arbitrary", "arbitrary"),
                                             vmem_limit_bytes=V7X_SCOPED_VMEM_BYTES),
        name="ret_init" if s0 is not None else "ret",
    )(*args)


def _rope_tables(pos, reps):
    half = DK_RET // 2
    inv = ROPE_BASE ** (-jnp.arange(half, dtype=F32) / half)
    ang = pos[:, None] * inv[None, :]
    return jnp.tile(jnp.cos(ang), (reps, 1)), jnp.tile(jnp.sin(ang), (reps, 1))


def _outproj_body(oap_ref, oas_ref, orp_ref, ors_ref, x_ref, w_ref, ln_ref, wrt_ref, brt_ref,
                  x1_ref, hn_ref, route_ref, *, n_prompt_blocks):
    is_prompt = pl.program_id(0) < n_prompt_blocks
    oa = jnp.where(is_prompt, oap_ref[...], oas_ref[...])
    orr = jnp.where(is_prompt, orp_ref[...], ors_ref[...])
    x1 = x_ref[...] + _dot(oa, w_ref[:GLA_V, :]) + _dot(orr, w_ref[GLA_V:, :])
    x1_ref[...] = x1
    hn = _rms(x1) * ln_ref[...]
    hn_ref[...] = hn

    logits = _dot(hn, wrt_ref[...], precision=_HI) + brt_ref[...]
    lane_i = lax.broadcasted_iota(I32, logits.shape, 1)
    lane = lane_i.astype(F32)
    grp = (lane_i >> (EXPERTS_PER_GROUP.bit_length() - 1)).astype(F32)
    neg = -jnp.inf
    far = float(LANES)

    gl = jnp.where((lane_i >= N_EXPERTS) & (lane_i < N_EXPERTS + N_GROUPS), logits, neg)
    gmax = jnp.max(gl, axis=-1, keepdims=True)
    g_w = 1.0 / jnp.sum(jnp.exp(gl - gmax), axis=-1, keepdims=True)
    g_idx = jnp.min(jnp.where(gl == gmax, lane, far), axis=-1, keepdims=True) - float(N_EXPERTS)

    el = jnp.where((lane_i < N_EXPERTS) & (grp == g_idx), logits, neg)
    m1 = jnp.max(el, axis=-1, keepdims=True)
    esum = jnp.sum(jnp.exp(el - m1), axis=-1, keepdims=True)
    i1 = jnp.min(jnp.where(el == m1, lane, far), axis=-1, keepdims=True)
    el2 = jnp.where(lane == i1, neg, el)
    m2 = jnp.max(el2, axis=-1, keepdims=True)
    i2 = jnp.min(jnp.where(el2 == m2, lane, far), axis=-1, keepdims=True)
    p1 = 1.0 / esum
    p2 = jnp.exp(m2 - m1) / esum
    gate1 = g_w * (p1 / (p1 + p2))
    gate2 = g_w * (p2 / (p1 + p2))
    route_ref[...] = jnp.where(lane_i == 0, i1,
                               jnp.where(lane_i == 1, i2,
                                         jnp.where(lane_i == 2, gate1, jnp.where(lane_i == 3, gate2, 0.0))))


def _outproj_router(oa_p, oa_s, or_p, or_s, x, w_out, ln, w_rt, b_rt):
    n = x.shape[0]
    nbp = oa_p.shape[0] // PROJ_TM
    pmap = lambda i: (jnp.minimum(i, nbp - 1), 0)
    smap = lambda i: (jnp.maximum(i - nbp, 0), 0)
    row = lambda i: (i, 0)
    fixed = lambda i: (0, 0)
    return pl.pallas_call(
        functools.partial(_outproj_body, n_prompt_blocks=nbp),
        out_shape=(jax.ShapeDtypeStruct((n, D_MODEL), F32),
                   jax.ShapeDtypeStruct((n, D_MODEL), F32),
                   jax.ShapeDtypeStruct((n, LANES), F32)),
        grid=(n // PROJ_TM,),
        in_specs=[pl.BlockSpec((PROJ_TM, GLA_V), pmap), pl.BlockSpec((PROJ_TM, GLA_V), smap),
                  pl.BlockSpec((PROJ_TM, RET_V), pmap), pl.BlockSpec((PROJ_TM, RET_V), smap),
                  pl.BlockSpec((PROJ_TM, D_MODEL), row),
                  pl.BlockSpec((GLA_V + RET_V, D_MODEL), fixed, pipeline_mode=pl.Buffered(1)),
                  pl.BlockSpec((1, D_MODEL), fixed),
                  pl.BlockSpec((D_MODEL, LANES), fixed),
                  pl.BlockSpec((1, LANES), fixed)],
        out_specs=(pl.BlockSpec((PROJ_TM, D_MODEL), row), pl.BlockSpec((PROJ_TM, D_MODEL), row),
                   pl.BlockSpec((PROJ_TM, LANES), row)),
        compiler_params=pltpu.CompilerParams(dimension_semantics=("arbitrary",),
                                             vmem_limit_bytes=V7X_SCOPED_VMEM_BYTES),
        name="outproj_router",
    )(oa_p, oa_s, or_p, or_s, x, w_out, ln, w_rt, b_rt)


def _moe_body(texp_ref, nval_ref, dst_ref, hn_hbm, gate_ref, wg_ref, wu_ref, wd_ref, y_hbm,
              xbuf, ybuf, wg_b, wu_b, wd_b, gsem, ssem):
    tm = MOE_TM
    i = pl.program_id(0)
    nt = pl.num_programs(0)
    slot = i % 2

    def gather_rows(tile, slot_):
        def body(r, c):
            tok = dst_ref[tile * tm + r] >> 1
            pltpu.make_async_copy(hn_hbm.at[pl.ds(tok, 1), :], xbuf.at[slot_, pl.ds(r, 1), :],
                                  gsem.at[slot_]).start()
            return c
        lax.fori_loop(0, tm, body, 0)

    def wait_scatter(slot_, n_rows):
        n_grp = pl.multiple_of((n_rows // SUBLANES) * SUBLANES, SUBLANES)

        @pl.when(n_grp > 0)
        def _():
            pltpu.make_async_copy(ybuf.at[slot_, pl.ds(0, n_grp), :], y_hbm.at[pl.ds(0, n_grp), :],
                                  ssem.at[slot_]).wait()

        def body(r, c):
            pltpu.make_async_copy(ybuf.at[slot_, pl.ds(0, 1), :], y_hbm.at[pl.ds(0, 1), :],
                                  ssem.at[slot_]).wait()
            return c
        lax.fori_loop(0, n_rows - n_grp, body, 0)

    @pl.when((i == 0) & (nval_ref[0] > 0))
    def _():
        gather_rows(0, 0)

    nxt = jnp.minimum(i + 1, nt - 1)

    @pl.when((i + 1 < nt) & (nval_ref[nxt] > 0))
    def _():
        gather_rows(i + 1, 1 - slot)

    prev2 = jnp.maximum(i - 2, 0)

    @pl.when((i >= 2) & (nval_ref[prev2] > 0))
    def _():
        wait_scatter(slot, nval_ref[prev2])

    nv = nval_ref[i]

    @pl.when(nv > 0)
    def _():
        pltpu.make_async_copy(hn_hbm.at[pl.ds(0, tm), :], xbuf.at[slot], gsem.at[slot]).wait()
        prev = jnp.maximum(i - 1, 0)

        @pl.when((i == 0) | (texp_ref[i] != texp_ref[prev]))
        def _():
            wg_b[...] = wg_ref[0].astype(BF16)
            wu_b[...] = wu_ref[0].astype(BF16)
            wd_b[...] = wd_ref[0].astype(BF16)

        h = xbuf[slot].astype(BF16)
        a = _dot(h, wg_b[...])
        u = _dot(h, wu_b[...])
        hid = (_silu(a) * u) * gate_ref[:, 0:1]
        ybuf[slot] = _dot(hid.astype(BF16), wd_b[...])

        def body(r, c):
            dst = dst_ref[i * tm + r]
            pltpu.make_async_copy(ybuf.at[slot, pl.ds(r, 1), :], y_hbm.at[pl.ds(dst, 1), :],
                                  ssem.at[slot]).start()
            return c
        lax.fori_loop(0, nv, body, 0)

    @pl.when(i == nt - 1)
    def _():
        prev1 = jnp.maximum(i - 1, 0)

        @pl.when((i >= 1) & (nval_ref[prev1] > 0))
        def _():
            wait_scatter(1 - slot, nval_ref[prev1])

        @pl.when(nv > 0)
        def _():
            wait_scatter(slot, nv)


def _moe(tile_expert, n_valid, dst_sorted, hn, gate_b, w_gate, w_up, w_down):
    n = hn.shape[0]
    n_tiles = tile_expert.shape[0]
    wmap = lambda i, te, nv, asg: (te[i], 0, 0)
    return pl.pallas_call(
        _moe_body,
        out_shape=jax.ShapeDtypeStruct((2 * n, D_MODEL), F32),
        grid_spec=pltpu.PrefetchScalarGridSpec(
            num_scalar_prefetch=3,
            grid=(n_tiles,),
            in_specs=[pl.BlockSpec(memory_space=pl.ANY),
                      pl.BlockSpec((MOE_TM, LANES), lambda i, te, nv, asg: (i, 0)),
                      pl.BlockSpec((1, D_MODEL, D_EXPERT), wmap),
                      pl.BlockSpec((1, D_MODEL, D_EXPERT), wmap),
                      pl.BlockSpec((1, D_EXPERT, D_MODEL), wmap)],
            out_specs=pl.BlockSpec(memory_space=pl.ANY),
            scratch_shapes=[pltpu.VMEM((2, MOE_TM, D_MODEL), F32),
                            pltpu.VMEM((2, MOE_TM, D_MODEL), F32),
                            pltpu.VMEM((D_MODEL, D_EXPERT), BF16),
                            pltpu.VMEM((D_MODEL, D_EXPERT), BF16),
                            pltpu.VMEM((D_EXPERT, D_MODEL), BF16),
                            pltpu.SemaphoreType.DMA((2,)),
                            pltpu.SemaphoreType.DMA((2,))]),
        compiler_params=pltpu.CompilerParams(dimension_semantics=("arbitrary",),
                                             vmem_limit_bytes=V7X_SCOPED_VMEM_BYTES),
        name="moe",
    )(tile_expert, n_valid, dst_sorted, hn, gate_b, w_gate, w_up, w_down)


def _routing_tables(route, n):
    tm = MOE_TM
    n_asg = 2 * n
    n_tiles = n_asg // tm + N_EXPERTS
    e_flat = route[:, 0:2].astype(I32).reshape(n_asg)
    g_flat = route[:, 2:4].reshape(n_asg)
    onehot = (e_flat[:, None] == jnp.arange(N_EXPERTS, dtype=I32)[None, :]).astype(I32)
    csum = jnp.cumsum(onehot, axis=0)
    counts = csum[-1]
    rank = jnp.take_along_axis(csum, e_flat[:, None], axis=1)[:, 0] - 1
    tiles = (counts + tm - 1) // tm
    tile_end = jnp.cumsum(tiles)
    tile_start = tile_end - tiles
    pos = tile_start[e_flat] * tm + rank
    dst_sorted = jnp.zeros((n_tiles * tm,), I32).at[pos].set(jnp.arange(n_asg, dtype=I32))
    gate_sorted = jnp.zeros((n_tiles * tm,), F32).at[pos].set(g_flat)
    tidx = jnp.arange(n_tiles, dtype=I32)
    t_exp = jnp.sum((tidx[:, None] >= tile_end[None, :]).astype(I32), axis=1)
    used = tidx < tile_end[-1]
    last_exp = jnp.sum((tile_end[-1] - 1 >= tile_end).astype(I32))
    t_exp = jnp.where(used, t_exp, last_exp)
    n_valid = jnp.where(used, jnp.clip(counts[t_exp] - (tidx - tile_start[t_exp]) * tm, 0, tm), 0)
    gate_b = jnp.broadcast_to(gate_sorted[:, None], (n_tiles * tm, LANES))
    return t_exp.astype(I32), n_valid.astype(I32), dst_sorted, gate_b


def _final_body(x1_ref, y_ref, ln_ref, o_ref):
    z = x1_ref[...] + y_ref[:, :D_MODEL] + y_ref[:, D_MODEL:]
    o_ref[...] = _rms(z) * ln_ref[...]


def _final(x1, y2, ln, *, row0, rows):
    base = row0 // PROJ_TM
    return pl.pallas_call(
        _final_body,
        out_shape=jax.ShapeDtypeStruct((rows, D_MODEL), F32),
        grid=(rows // PROJ_TM,),
        in_specs=[pl.BlockSpec((PROJ_TM, D_MODEL), lambda i: (base + i, 0)),
                  pl.BlockSpec((PROJ_TM, 2 * D_MODEL), lambda i: (base + i, 0)),
                  pl.BlockSpec((1, D_MODEL), lambda i: (0, 0))],
        out_specs=pl.BlockSpec((PROJ_TM, D_MODEL), lambda i: (i, 0)),
        compiler_params=pltpu.CompilerParams(dimension_semantics=("arbitrary",)),
        name="final_norm",
    )(x1, y2, ln)


def kernel(x_prompt, x_sample, state_gla, state_ret, ln_attn, w_in, w_gk2, b_gk, gla_norm_w, w_out, ln_ffn, w_router_group, b_router_group, w_router_expert, b_router_expert, w_exp_gate, w_exp_up, w_exp_down, ln_final):
    bp, tp, d = x_prompt.shape
    bs, ts, _ = x_sample.shape
    assert d == D_MODEL and w_in.shape == (1, D_MODEL, IN_COLS_SRC)
    n_p, n_s = bp * tp, bs * ts
    n = n_p + n_s
    assert n_p % PROJ_TM == 0 and n_s % PROJ_TM == 0 and tp % GLA_CHUNK == 0 and tp % RET_CHUNK == 0

    x = jnp.concatenate([x_prompt.reshape(n_p, d), x_sample.reshape(n_s, d)], axis=0)

    lr0 = 2 * GLA_QK + GLA_V
    w = w_in[0]
    w_in_p = jnp.concatenate([w[:, :lr0], w[:, lr0 + GLA_LOWRANK:], w[:, lr0:lr0 + GLA_LOWRANK],
                              jnp.zeros((d, LANES - GLA_LOWRANK), w.dtype)], axis=1).astype(BF16)
    w_gk2p = jnp.concatenate([w_gk2[0], jnp.zeros((LANES - GLA_LOWRANK, GLA_QK), F32)], axis=0)
    w_rt = jnp.concatenate([w_router_expert[0], w_router_group[0],
                            jnp.zeros((d, LANES - N_EXPERTS - N_GROUPS), F32)], axis=1)
    b_rt = jnp.concatenate([b_router_expert[0], b_router_group[0],
                            jnp.zeros((LANES - N_EXPERTS - N_GROUPS,), F32)])[None, :]

    proj = _inproj(x, ln_attn, w_in_p)

    nw = gla_norm_w
    bgk = b_gk
    oa_p, sg_p = _gla(proj, w_gk2p, bgk, nw, None, row0=0, n_seq=bp, seq_len=tp, chunk=GLA_CHUNK, nseq=1)
    oa_s, sg_s = _gla(proj, w_gk2p, bgk, nw, state_gla[0], row0=n_p, n_seq=bs, seq_len=ts, chunk=ts, nseq=8)

    cos_p, sin_p = _rope_tables(jnp.arange(tp, dtype=F32), 1)
    ret_nseq_s = 4
    cos_s, sin_s = _rope_tables(jnp.arange(ts, dtype=F32) + float(PAST_LEN), ret_nseq_s)
    or_p, sr_p = _ret(proj, cos_p, sin_p, None, row0=0, n_seq=bp, seq_len=tp, chunk=RET_CHUNK, nseq=1)
    or_s, sr_s = _ret(proj, cos_s, sin_s, state_ret[0], row0=n_p, n_seq=bs, seq_len=ts, chunk=ts, nseq=ret_nseq_s)

    x1, hn, route = _outproj_router(oa_p, oa_s, or_p, or_s, x, w_out[0].astype(BF16), ln_ffn, w_rt, b_rt)

    t_exp, n_valid, dst_sorted, gate_b = _routing_tables(route, n)
    y2 = _moe(t_exp, n_valid, dst_sorted, hn, gate_b, w_exp_gate[0], w_exp_up[0], w_exp_down[0])
    y2 = y2.reshape(y2.shape[0] // 2, 2 * D_MODEL)

    ln_f = ln_final[None, :]
    y_p = _final(x1, y2, ln_f, row0=0, rows=n_p).reshape(bp, tp, d)
    y_s = _final(x1, y2, ln_f, row0=n_p, rows=n_s).reshape(bs, ts, d)
    return (y_p, y_s, sg_p[None], sr_p[None], sg_s[None], sr_s[None])
```

```python
import functools
import math

import jax
import jax.numpy as jnp
from jax import lax
from jax.experimental import pallas as pl
from jax.experimental.pallas import tpu as pltpu

F32 = jnp.float32
BF16 = jnp.bfloat16
I32 = jnp.int32

D_MODEL = 2048
PAST_LEN = 16384
H_GLA, DK_GLA, DV_GLA = 4, 128, 256
GLA_LOWRANK = 16
GLA_GATE_NORM = 16.0
GLA_SUBCHUNK = 16
H_RET, DK_RET, DV_RET = 4, 256, 256
ROPE_BASE = 10000.0
N_GROUPS, EXPERTS_PER_GROUP, D_EXPERT = 4, 8, 512
N_EXPERTS = N_GROUPS * EXPERTS_PER_GROUP
EPS = 1e-6

LANES = 128
SUBLANES = 8
V7X_SCOPED_VMEM_BYTES = 56 * 1024 * 1024

GLA_QK = H_GLA * DK_GLA
GLA_V = H_GLA * DV_GLA
RET_QK = H_RET * DK_RET
RET_V = H_RET * DV_RET
C_QA, C_KA, C_VA, C_GA = 0, GLA_QK, 2 * GLA_QK, 2 * GLA_QK + GLA_V
C_QR = C_GA + GLA_V
C_KR, C_VR, C_GR = C_QR + RET_QK, C_QR + 2 * RET_QK, C_QR + 2 * RET_QK + RET_V
C_LR = C_GR + RET_V
P_COLS = C_LR + LANES
IN_COLS_SRC = 2 * GLA_QK + GLA_V + GLA_LOWRANK + GLA_V + 2 * RET_QK + 2 * RET_V

PROJ_TM = 256
W_CHUNK = 1024
MOE_TM = 256
GLA_CHUNK = 128
RET_CHUNK = 128

_HI = lax.Precision.HIGHEST


def _dot(a, b, **kw):
    return jnp.dot(a, b, preferred_element_type=F32, **kw)


def _dot_nt(a, b):
    return lax.dot_general(a, b, (((1,), (1,)), ((), ())), preferred_element_type=F32)


def _dot_tn(a, b, **kw):
    return lax.dot_general(a, b, (((0,), (0,)), ((), ())), preferred_element_type=F32, **kw)


def _rms(x):
    return x * lax.rsqrt(jnp.mean(x * x, axis=-1, keepdims=True) + EPS)


def _silu(x):
    return x * jax.nn.sigmoid(x)


def _wprep_body(a_ref, b_ref, o_ref):
    a = a_ref[...]
    shifted = jnp.concatenate([a[:, GLA_LOWRANK:], b_ref[:, :GLA_LOWRANK]], axis=1)
    o_ref[...] = jnp.where(pl.program_id(0) >= C_GA // W_CHUNK, shifted, a).astype(BF16)


def _wprep(w):
    per = W_CHUNK // LANES
    return pl.pallas_call(
        _wprep_body,
        out_shape=jax.ShapeDtypeStruct((D_MODEL, C_LR), BF16),
        grid=(C_LR // W_CHUNK,),
        in_specs=[pl.BlockSpec((D_MODEL, W_CHUNK), lambda j: (0, j)),
                  pl.BlockSpec((D_MODEL, LANES), lambda j: (0, per * (j + 1)))],
        out_specs=pl.BlockSpec((D_MODEL, W_CHUNK), lambda j: (0, j)),
        compiler_params=pltpu.CompilerParams(dimension_semantics=("arbitrary",),
                                             vmem_limit_bytes=V7X_SCOPED_VMEM_BYTES),
        name="wprep",
    )(w, w)


def _inproj_body(xp_ref, xs_ref, ln_ref, w_ref, wlr_ref, o_ref, *, n_prompt_blocks):
    x = jnp.where(pl.program_id(0) < n_prompt_blocks, xp_ref[...], xs_ref[...])
    h = (_rms(x) * ln_ref[...]).astype(BF16)
    for c in range(C_LR // W_CHUNK):
        o_ref[:, c * W_CHUNK:(c + 1) * W_CHUNK] = _dot(h, w_ref[:, c * W_CHUNK:(c + 1) * W_CHUNK])
    o_ref[:, C_LR:] = _dot(h, wlr_ref[...])


def _inproj(xp, xs, ln, w, w_lr):
    nbp = xp.shape[0] // PROJ_TM
    n = xp.shape[0] + xs.shape[0]
    fixed = lambda i: (0, 0)
    return pl.pallas_call(
        functools.partial(_inproj_body, n_prompt_blocks=nbp),
        out_shape=jax.ShapeDtypeStruct((n, P_COLS), F32),
        grid=(n // PROJ_TM,),
        in_specs=[pl.BlockSpec((PROJ_TM, D_MODEL), lambda i: (jnp.minimum(i, nbp - 1), 0)),
                  pl.BlockSpec((PROJ_TM, D_MODEL), lambda i: (jnp.maximum(i - nbp, 0), 0)),
                  pl.BlockSpec((1, D_MODEL), fixed),
                  pl.BlockSpec((D_MODEL, C_LR), fixed, pipeline_mode=pl.Buffered(1)),
                  pl.BlockSpec((D_MODEL, LANES), fixed)],
        out_specs=pl.BlockSpec((PROJ_TM, P_COLS), lambda i: (i, 0)),
        compiler_params=pltpu.CompilerParams(dimension_semantics=("arbitrary",),
                                             vmem_limit_bytes=V7X_SCOPED_VMEM_BYTES),
        name="inproj",
    )(xp, xs, ln, w, w_lr)


def _seg_cumsum(x, seg):
    pos = lax.broadcasted_iota(I32, x.shape, 0) & (seg - 1)
    s = 1
    while s < seg:
        x = x + jnp.where(pos >= s, pltpu.roll(x, s, axis=0), 0.0)
        s *= 2
    return x


def _block_ends(p, m):
    rows, w = p.shape
    return jnp.concatenate(
        [jnp.broadcast_to(p[j * m + m - 1:j * m + m, :], (m, w)) for j in range(rows // m)], axis=0)


def _block_starts(p, m, chunk):
    rows, w = p.shape
    pieces = []
    for j in range(rows // m):
        if (j * m) % chunk == 0:
            pieces.append(jnp.zeros((m, w), F32))
        else:
            pieces.append(jnp.broadcast_to(p[j * m - 1:j * m, :], (m, w)))
    return jnp.concatenate(pieces, axis=0)


def _gla_body(*refs, chunk, nseq, has_init):
    if has_init:
        (q_ref, k_ref, v_ref, g_ref, lr_ref, wgk_ref, bgk_ref, nw_ref, s0_ref,
         o_ref, sout_ref, s_scr) = refs
    else:
        (q_ref, k_ref, v_ref, g_ref, lr_ref, wgk_ref, bgk_ref, nw_ref,
         o_ref, sout_ref, s_scr) = refs
    t = pl.program_id(1)

    @pl.when(t == 0)
    def _():
        if has_init:
            s_scr[...] = s0_ref[...]
        else:
            s_scr[...] = jnp.zeros_like(s_scr)

    x = _dot(lr_ref[...], wgk_ref[...], precision=_HI) + bgk_ref[...]
    log_g = -(jnp.maximum(-x, 0.0) + jnp.log1p(jnp.exp(-jnp.abs(x)))) / GLA_GATE_NORM
    p = _seg_cumsum(log_g, chunk)
    p_end = _block_ends(p, chunk)
    eq_c = jnp.exp(p)
    ek_c = jnp.exp(p_end - p)
    sub = min(GLA_SUBCHUNK, chunk)
    p_sub = p - _block_starts(p, sub, chunk)
    eq_d = jnp.exp(p_sub)
    ek_d = jnp.exp(-p_sub)
    levels = []
    m = sub
    while m < chunk:
        levels.append((m, jnp.exp(p - _block_starts(p, m, chunk)), jnp.exp(_block_ends(p, m) - p)))
        m *= 2

    ti = lax.broadcasted_iota(I32, (chunk, chunk), 0)
    si = lax.broadcasted_iota(I32, (chunk, chunk), 1)
    blk = lambda idx, size: idx >> (size.bit_length() - 1)
    mask_d = (blk(ti, sub) == blk(si, sub)) & (si <= ti)
    masks = [((blk(ti, m) & 1) == 1) & (blk(si, m) == blk(ti, m) - 1) for (m, _, _) in levels]
    sel8 = (lax.broadcasted_iota(I32, (8, DV_GLA), 0) == 0).astype(F32)
    nw = nw_ref[...]

    outs = [[None] * nseq for _ in range(H_GLA)]
    for b in range(nseq):
        r0 = b * chunk
        rows = slice(r0, r0 + chunk)
        for h in range(H_GLA):
            kc = slice(h * DK_GLA, (h + 1) * DK_GLA)
            vc = slice(h * DV_GLA, (h + 1) * DV_GLA)
            q = q_ref[rows, kc] * (DK_GLA ** -0.5)
            k = k_ref[rows, kc]
            v = v_ref[rows, vc].astype(BF16)
            att = jnp.where(mask_d, _dot_nt((q * eq_d[rows, kc]).astype(BF16),
                                            (k * ek_d[rows, kc]).astype(BF16)), 0.0)
            for (m, eq_m, ek_m), mask in zip(levels, masks):
                att = jnp.where(mask, _dot_nt((q * eq_m[rows, kc]).astype(BF16),
                                              (k * ek_m[rows, kc]).astype(BF16)), att)
            s_old = s_scr[b, h]
            o = _dot(att.astype(BF16), v) + _dot((q * eq_c[rows, kc]).astype(BF16), s_old.astype(BF16))
            kv = _dot_tn((k * ek_c[rows, kc]).astype(BF16), v)
            tot = jnp.broadcast_to(p[r0 + chunk - 1:r0 + chunk, kc], (8, DK_GLA))
            decay = jnp.exp(_dot_tn(tot, sel8, precision=_HI))
            s_scr[b, h] = decay * s_old + kv
            outs[h][b] = _rms(o) * nw * _silu(g_ref[rows, vc])
    for h in range(H_GLA):
        o_ref[:, h * DV_GLA:(h + 1) * DV_GLA] = jnp.concatenate(outs[h], axis=0).astype(BF16)

    @pl.when(t == pl.num_programs(1) - 1)
    def _():
        sout_ref[...] = s_scr[...]


def _gla(proj, w_gk2p, b_gk, nw, s0, *, row0, n_seq, seq_len, chunk, nseq):
    rows = nseq * chunk
    nt = seq_len // chunk
    base = row0 // rows

    def rmap(col):
        return lambda b, t: (base + b * nt + t, col)

    in_specs = [pl.BlockSpec((rows, GLA_QK), rmap(C_QA // GLA_QK)),
                pl.BlockSpec((rows, GLA_QK), rmap(C_KA // GLA_QK)),
                pl.BlockSpec((rows, GLA_V), rmap(C_VA // GLA_V)),
                pl.BlockSpec((rows, GLA_V), rmap(C_GA // GLA_V)),
                pl.BlockSpec((rows, LANES), rmap(C_LR // LANES)),
                pl.BlockSpec((LANES, GLA_QK), lambda b, t: (0, 0)),
                pl.BlockSpec((1, GLA_QK), lambda b, t: (0, 0)),
                pl.BlockSpec((1, DV_GLA), lambda b, t: (0, 0))]
    args = [proj, proj, proj, proj, proj, w_gk2p, b_gk, nw]
    state_spec = pl.BlockSpec((nseq, H_GLA, DK_GLA, DV_GLA), lambda b, t: (b, 0, 0, 0))
    if s0 is not None:
        in_specs.append(state_spec)
        args.append(s0)
    return pl.pallas_call(
        functools.partial(_gla_body, chunk=chunk, nseq=nseq, has_init=s0 is not None),
        out_shape=(jax.ShapeDtypeStruct((n_seq * seq_len, GLA_V), BF16),
                   jax.ShapeDtypeStruct((n_seq, H_GLA, DK_GLA, DV_GLA), F32)),
        grid=(n_seq // nseq, nt),
        in_specs=in_specs,
        out_specs=(pl.BlockSpec((rows, GLA_V), lambda b, t: (b * nt + t, 0)), state_spec),
        scratch_shapes=[pltpu.VMEM((nseq, H_GLA, DK_GLA, DV_GLA), F32)],
        compiler_params=pltpu.CompilerParams(dimension_semantics=("arbitrary", "arbitrary"),
                                             vmem_limit_bytes=V7X_SCOPED_VMEM_BYTES),
        name="gla_init" if s0 is not None else "gla",
    )(*args)


def _ret_body(*refs, chunk, nseq, has_init):
    if has_init:
        (q_ref, k_ref, v_ref, g_ref, cos_ref, sin_ref, s0_ref, o_ref, sout_ref, s_scr) = refs
    else:
        (q_ref, k_ref, v_ref, g_ref, cos_ref, sin_ref, o_ref, sout_ref, s_scr) = refs
    t = pl.program_id(1)

    @pl.when(t == 0)
    def _():
        if has_init:
            s_scr[...] = s0_ref[...]
        else:
            s_scr[...] = jnp.zeros_like(s_scr)

    half = DK_RET // 2
    ti = lax.broadcasted_iota(I32, (chunk, chunk), 0)
    si = lax.broadcasted_iota(I32, (chunk, chunk), 1)
    diff = (ti - si).astype(F32)
    idx = lax.broadcasted_iota(I32, (chunk, 1), 0).astype(F32)

    outs = [[None] * nseq for _ in range(H_RET)]
    for h in range(H_RET):
        lg = math.log(1.0 - 2.0 ** (-5.0 - h))
        dmat = jnp.where(diff >= 0, jnp.exp(lg * jnp.maximum(diff, 0.0)), 0.0)
        q_dec = jnp.exp(lg * (idx + 1.0))
        k_dec = jnp.exp(lg * (chunk - 1.0 - idx))
        c_dec = math.exp(lg * chunk)
        for b in range(nseq):
            rows = slice(b * chunk, (b + 1) * chunk)
            cos = cos_ref[rows, :]
            sin = sin_ref[rows, :]
            c1 = slice(h * DK_RET, h * DK_RET + half)
            c2 = slice(h * DK_RET + half, (h + 1) * DK_RET)
            vc = slice(h * DV_RET, (h + 1) * DV_RET)
            q1, q2 = q_ref[rows, c1], q_ref[rows, c2]
            k1, k2 = k_ref[rows, c1], k_ref[rows, c2]
            q = jnp.concatenate([q1 * cos - q2 * sin, q1 * sin + q2 * cos], axis=-1)
            k = jnp.concatenate([k1 * cos - k2 * sin, k1 * sin + k2 * cos], axis=-1) * (DK_RET ** -0.5)
            v = v_ref[rows, vc].astype(BF16)
            qb = q.astype(BF16)
            s_old = s_scr[b, h]
            att = _dot_nt(qb, k.astype(BF16)) * dmat
            o = _dot(att.astype(BF16), v) + _dot(qb, s_old.astype(BF16)) * q_dec
            s_scr[b, h] = c_dec * s_old + _dot_tn((k * k_dec).astype(BF16), v)
            outs[h][b] = _rms(o) * _silu(g_ref[rows, vc])
    for h in range(H_RET):
        o_ref[:, h * DV_RET:(h + 1) * DV_RET] = jnp.concatenate(outs[h], axis=0).astype(BF16)

    @pl.when(t == pl.num_programs(1) - 1)
    def _():
        sout_ref[...] = s_scr[...]


def _ret(proj, cos, sin, s0, *, row0, n_seq, seq_len, chunk, nseq):
    rows = nseq * chunk
    nt = seq_len // chunk
    base = row0 // rows

    def rmap(col):
        return lambda b, t: (base + b * nt + t, col)

    tab_spec = pl.BlockSpec((rows, DK_RET // 2), lambda b, t: (t, 0))
    in_specs = [pl.BlockSpec((rows, RET_QK), rmap(C_QR // RET_QK)),
                pl.BlockSpec((rows, RET_QK), rmap(C_KR // RET_QK)),
                pl.BlockSpec((rows, RET_V), rmap(C_VR // RET_V)),
                pl.BlockSpec((rows, RET_V), rmap(C_GR // RET_V)),
                tab_spec, tab_spec]
    args = [proj, proj, proj, proj, cos, sin]
    state_spec = pl.BlockSpec((nseq, H_RET, DK_RET, DV_RET), lambda b, t: (b, 0, 0, 0))
    if s0 is not None:
        in_specs.append(state_spec)
        args.append(s0)
    return pl.pallas_call(
        functools.partial(_ret_body, chunk=chunk, nseq=nseq, has_init=s0 is not None),
        out_shape=(jax.ShapeDtypeStruct((n_seq * seq_len, RET_V), BF16),
                   jax.ShapeDtypeStruct((n_seq, H_RET, DK_RET, DV_RET), F32)),
        grid=(n_seq // nseq, nt),
        in_specs=in_specs,
        out_specs=(pl.BlockSpec((rows, RET_V), lambda b, t: (b * nt + t, 0)), state_spec),
        scratch_shapes=[pltpu.VMEM((nseq, H_RET, DK_RET, DV_RET), F32)],
        compiler_params=pltpu.CompilerParams(dimension_semantics=("arbitrary", "arbitrary"),
                                             vmem_limit_bytes=V7X_SCOPED_VMEM_BYTES),
        name="ret_init" if s0 is not None else "ret",
    )(*args)


def _rope_tables(pos, reps):
    half = DK_RET // 2
    inv = ROPE_BASE ** (-jnp.arange(half, dtype=F32) / half)
    ang = pos[:, None] * inv[None, :]
    return jnp.tile(jnp.cos(ang), (reps, 1)), jnp.tile(jnp.sin(ang), (reps, 1))


def _outproj_body(oap_ref, oas_ref, orp_ref, ors_ref, xp_ref, xs_ref, w_ref, ln_ref, wrt_ref, brt_ref,
                  x1_ref, hn_ref, route_ref, counts_ref, *, n_prompt_blocks):
    step = pl.program_id(0)
    is_prompt = step < n_prompt_blocks
    oa = jnp.where(is_prompt, oap_ref[...], oas_ref[...])
    orr = jnp.where(is_prompt, orp_ref[...], ors_ref[...])
    x = jnp.where(is_prompt, xp_ref[...], xs_ref[...])
    x1 = x + _dot(oa, w_ref[:GLA_V, :]) + _dot(orr, w_ref[GLA_V:, :])
    x1_ref[...] = x1
    hn = _rms(x1) * ln_ref[...]
    hn_ref[...] = hn

    logits = _dot(hn, wrt_ref[...], precision=_HI) + brt_ref[...]
    lane_i = lax.broadcasted_iota(I32, logits.shape, 1)
    lane = lane_i.astype(F32)
    grp = (lane_i >> (EXPERTS_PER_GROUP.bit_length() - 1)).astype(F32)
    neg = -jnp.inf
    far = float(LANES)

    gl = jnp.where((lane_i >= N_EXPERTS) & (lane_i < N_EXPERTS + N_GROUPS), logits, neg)
    gmax = jnp.max(gl, axis=-1, keepdims=True)
    g_w = 1.0 / jnp.sum(jnp.exp(gl - gmax), axis=-1, keepdims=True)
    g_idx = jnp.min(jnp.where(gl == gmax, lane, far), axis=-1, keepdims=True) - float(N_EXPERTS)

    el = jnp.where((lane_i < N_EXPERTS) & (grp == g_idx), logits, neg)
    m1 = jnp.max(el, axis=-1, keepdims=True)
    esum = jnp.sum(jnp.exp(el - m1), axis=-1, keepdims=True)
    i1 = jnp.min(jnp.where(el == m1, lane, far), axis=-1, keepdims=True)
    el2 = jnp.where(lane == i1, neg, el)
    m2 = jnp.max(el2, axis=-1, keepdims=True)
    i2 = jnp.min(jnp.where(el2 == m2, lane, far), axis=-1, keepdims=True)
    p1 = 1.0 / esum
    p2 = jnp.exp(m2 - m1) / esum
    gate1 = g_w * (p1 / (p1 + p2))
    gate2 = g_w * (p2 / (p1 + p2))
    @pl.when(step == 0)
    def _():
        counts_ref[...] = jnp.zeros_like(counts_ref)

    tm = logits.shape[0]
    earlier = (lax.broadcasted_iota(I32, (tm, tm), 0) > lax.broadcasted_iota(I32, (tm, tm), 1)).astype(BF16)
    running = counts_ref[0:1, :]
    ranks = []
    for idx in (i1, i2):
        onehot = (lane == idx).astype(F32)
        before = _dot(earlier, onehot.astype(BF16)) + running
        ranks.append(jnp.sum(onehot * before, axis=-1, keepdims=True))
        running = running + jnp.sum(onehot, axis=0, keepdims=True)
    counts_ref[...] = jnp.broadcast_to(running, counts_ref.shape)

    route_ref[...] = jnp.where(lane_i == 0, i1,
                               jnp.where(lane_i == 1, i2,
                                         jnp.where(lane_i == 2, gate1,
                                                   jnp.where(lane_i == 3, gate2,
                                                             jnp.where(lane_i == 4, ranks[0],
                                                                       jnp.where(lane_i == 5, ranks[1], 0.0))))))


def _outproj_router(oa_p, oa_s, or_p, or_s, xp, xs, w_out, ln, w_rt, b_rt):
    n = xp.shape[0] + xs.shape[0]
    nbp = xp.shape[0] // PROJ_TM
    pmap = lambda i: (jnp.minimum(i, nbp - 1), 0)
    smap = lambda i: (jnp.maximum(i - nbp, 0), 0)
    row = lambda i: (i, 0)
    fixed = lambda i: (0, 0)
    return pl.pallas_call(
        functools.partial(_outproj_body, n_prompt_blocks=nbp),
        out_shape=(jax.ShapeDtypeStruct((n, D_MODEL), F32),
                   jax.ShapeDtypeStruct((n, D_MODEL), F32),
                   jax.ShapeDtypeStruct((n, LANES), F32),
                   jax.ShapeDtypeStruct((SUBLANES, LANES), F32)),
        grid=(n // PROJ_TM,),
        in_specs=[pl.BlockSpec((PROJ_TM, GLA_V), pmap), pl.BlockSpec((PROJ_TM, GLA_V), smap),
                  pl.BlockSpec((PROJ_TM, RET_V), pmap), pl.BlockSpec((PROJ_TM, RET_V), smap),
                  pl.BlockSpec((PROJ_TM, D_MODEL), pmap), pl.BlockSpec((PROJ_TM, D_MODEL), smap),
                  pl.BlockSpec((GLA_V + RET_V, D_MODEL), fixed, pipeline_mode=pl.Buffered(1)),
                  pl.BlockSpec((1, D_MODEL), fixed),
                  pl.BlockSpec((D_MODEL, LANES), fixed),
                  pl.BlockSpec((1, LANES), fixed)],
        out_specs=(pl.BlockSpec((PROJ_TM, D_MODEL), row), pl.BlockSpec((PROJ_TM, D_MODEL), row),
                   pl.BlockSpec((PROJ_TM, LANES), row), pl.BlockSpec((SUBLANES, LANES), fixed)),
        compiler_params=pltpu.CompilerParams(dimension_semantics=("arbitrary",),
                                             vmem_limit_bytes=V7X_SCOPED_VMEM_BYTES),
        name="outproj_router",
    )(oa_p, oa_s, or_p, or_s, xp, xs, w_out, ln, w_rt, b_rt)


def _moe_body(texp_ref, nval_ref, pos_ref, hn_hbm, wg_ref, wu_ref, wd_ref, y_hbm,
              inv, xbuf, ybuf, wg_b, wu_b, wd_b, gsem, ssem, *, n_tokens):
    tm = MOE_TM
    i = pl.program_id(0)
    nt = pl.num_programs(0)
    slot = i % 2

    def for_rows(n_rows, fn):
        n_grp = n_rows // SUBLANES

        def group(g, c):
            for j in range(SUBLANES):
                fn(g * SUBLANES + j)
            return c
        lax.fori_loop(0, n_grp, group, 0)

        def single(r, c):
            fn(r)
            return c
        lax.fori_loop(n_grp * SUBLANES, n_rows, single, 0)

    def gather_copy(tok, r, slot_):
        return pltpu.make_async_copy(hn_hbm.at[pl.ds(tok, 1), :], xbuf.at[slot_, pl.ds(r, 1), :], gsem.at[slot_])

    def scatter_copy(r, dst, slot_):
        return pltpu.make_async_copy(ybuf.at[slot_, pl.ds(r, 1), :], y_hbm.at[pl.ds(dst, 1), :], ssem.at[slot_])

    def start_gather(tile, slot_):
        for_rows(nval_ref[tile], lambda r: gather_copy(inv[tile * tm + r] >> 1, r, slot_).start())

    def wait_rows(n_rows, group_copy, row_copy):
        n_grp = pl.multiple_of((n_rows // SUBLANES) * SUBLANES, SUBLANES)

        @pl.when(n_grp > 0)
        def _():
            group_copy(n_grp).wait()

        def single(r, c):
            row_copy().wait()
            return c
        lax.fori_loop(n_grp, n_rows, single, 0)

    def wait_gather(slot_, n_rows):
        wait_rows(n_rows,
                  lambda m: pltpu.make_async_copy(hn_hbm.at[pl.ds(0, m), :], xbuf.at[slot_, pl.ds(0, m), :],
                                                  gsem.at[slot_]),
                  lambda: gather_copy(0, 0, slot_))

    def wait_scatter(slot_, n_rows):
        wait_rows(n_rows,
                  lambda m: pltpu.make_async_copy(ybuf.at[slot_, pl.ds(0, m), :], y_hbm.at[pl.ds(0, m), :],
                                                  ssem.at[slot_]),
                  lambda: scatter_copy(0, 0, slot_))

    @pl.when(i == 0)
    def _():
        def body(a, c):
            inv[pos_ref[a]] = a
            return c
        lax.fori_loop(0, 2 * n_tokens, body, 0, unroll=8)
        xbuf[...] = jnp.zeros_like(xbuf)
        start_gather(0, 0)

    @pl.when(i + 1 < nt)
    def _():
        start_gather(jnp.minimum(i + 1, nt - 1), 1 - slot)

    @pl.when(i >= 2)
    def _():
        wait_scatter(slot, nval_ref[jnp.maximum(i - 2, 0)])

    nv = nval_ref[i]
    wait_gather(slot, nv)

    @pl.when(nv > 0)
    def _():
        @pl.when((i == 0) | (texp_ref[i] != texp_ref[jnp.maximum(i - 1, 0)]))
        def _():
            wg_b[...] = wg_ref[0].astype(BF16)
            wu_b[...] = wu_ref[0].astype(BF16)
            wd_b[...] = wd_ref[0].astype(BF16)

        h = xbuf[slot].astype(BF16)
        hid = _silu(_dot(h, wg_b[...])) * _dot(h, wu_b[...])
        ybuf[slot] = _dot(hid.astype(BF16), wd_b[...])

        def scatter_one(r):
            a = inv[i * tm + r]
            scatter_copy(r, (a & 1) * n_tokens + (a >> 1), slot).start()
        for_rows(nv, scatter_one)

    @pl.when(i == nt - 1)
    def _():
        @pl.when(i >= 1)
        def _():
            wait_scatter(1 - slot, nval_ref[jnp.maximum(i - 1, 0)])
        wait_scatter(slot, nv)


def _moe(tile_expert, n_valid, pos, hn, w_gate, w_up, w_down):
    n = hn.shape[0]
    n_tiles = tile_expert.shape[0]
    wmap = lambda i, te, nv, pos_: (te[i], 0, 0)
    return pl.pallas_call(
        functools.partial(_moe_body, n_tokens=n),
        out_shape=jax.ShapeDtypeStruct((2 * n, D_MODEL), F32),
        grid_spec=pltpu.PrefetchScalarGridSpec(
            num_scalar_prefetch=3,
            grid=(n_tiles,),
            in_specs=[pl.BlockSpec(memory_space=pl.ANY),
                      pl.BlockSpec((1, D_MODEL, D_EXPERT), wmap),
                      pl.BlockSpec((1, D_MODEL, D_EXPERT), wmap),
                      pl.BlockSpec((1, D_EXPERT, D_MODEL), wmap)],
            out_specs=pl.BlockSpec(memory_space=pl.ANY),
            scratch_shapes=[pltpu.SMEM((n_tiles * MOE_TM,), I32),
                            pltpu.VMEM((2, MOE_TM, D_MODEL), F32),
                            pltpu.VMEM((2, MOE_TM, D_MODEL), F32),
                            pltpu.VMEM((D_MODEL, D_EXPERT), BF16),
                            pltpu.VMEM((D_MODEL, D_EXPERT), BF16),
                            pltpu.VMEM((D_EXPERT, D_MODEL), BF16),
                            pltpu.SemaphoreType.DMA((2,)),
                            pltpu.SemaphoreType.DMA((2,))]),
        compiler_params=pltpu.CompilerParams(dimension_semantics=("arbitrary",),
                                             vmem_limit_bytes=V7X_SCOPED_VMEM_BYTES),
        name="moe",
    )(tile_expert, n_valid, pos, hn, w_gate, w_up, w_down)


def _routing_tables(route, counts, n):
    tm = MOE_TM
    n_tiles = 2 * n // tm + N_EXPERTS
    experts = route[:, 0:2].astype(I32)
    rank = route[:, 4:6].astype(I32)
    counts = counts[0, :N_EXPERTS].astype(I32)
    tiles = (counts + tm - 1) // tm
    tile_end = jnp.cumsum(tiles)
    tile_start = tile_end - tiles
    onehot = experts[:, :, None] == jnp.arange(N_EXPERTS, dtype=I32)[None, None, :]
    pos = jnp.sum(jnp.where(onehot, (tile_start * tm)[None, None, :], 0), axis=-1) + rank
    tidx = jnp.arange(n_tiles, dtype=I32)
    t_exp = jnp.sum((tidx[:, None] >= tile_end[None, :]).astype(I32), axis=1)
    used = tidx < tile_end[-1]
    last_exp = jnp.sum((tile_end[-1] - 1 >= tile_end).astype(I32))
    t_exp = jnp.where(used, t_exp, last_exp)
    n_valid = jnp.where(used, jnp.clip(counts[t_exp] - (tidx - tile_start[t_exp]) * tm, 0, tm), 0)
    return t_exp.astype(I32), n_valid.astype(I32), pos.reshape(2 * n)


def _final_body(x1_ref, y0_ref, y1_ref, route_ref, ln_ref, o_ref):
    z = x1_ref[...] + route_ref[:, 2:3] * y0_ref[...] + route_ref[:, 3:4] * y1_ref[...]
    o_ref[...] = _rms(z) * ln_ref[...]


def _final(x1, y2, route, ln, *, row0, rows):
    base = row0 // PROJ_TM
    slot1 = x1.shape[0] // PROJ_TM
    return pl.pallas_call(
        _final_body,
        out_shape=jax.ShapeDtypeStruct((rows, D_MODEL), F32),
        grid=(rows // PROJ_TM,),
        in_specs=[pl.BlockSpec((PROJ_TM, D_MODEL), lambda i: (base + i, 0)),
                  pl.BlockSpec((PROJ_TM, D_MODEL), lambda i: (base + i, 0)),
                  pl.BlockSpec((PROJ_TM, D_MODEL), lambda i: (slot1 + base + i, 0)),
                  pl.BlockSpec((PROJ_TM, LANES), lambda i: (base + i, 0)),
                  pl.BlockSpec((1, D_MODEL), lambda i: (0, 0))],
        out_specs=pl.BlockSpec((PROJ_TM, D_MODEL), lambda i: (i, 0)),
        compiler_params=pltpu.CompilerParams(dimension_semantics=("arbitrary",)),
        name="final_norm",
    )(x1, y2, y2, route, ln)


def kernel(x_prompt, x_sample, state_gla, state_ret, ln_attn, w_in, w_gk2, b_gk, gla_norm_w, w_out, ln_ffn, w_router_group, b_router_group, w_router_expert, b_router_expert, w_exp_gate, w_exp_up, w_exp_down, ln_final):
    bp, tp, d = x_prompt.shape
    bs, ts, _ = x_sample.shape
    assert d == D_MODEL and w_in.shape == (1, D_MODEL, IN_COLS_SRC)
    n_p, n_s = bp * tp, bs * ts
    n = n_p + n_s
    assert n_p % PROJ_TM == 0 and n_s % PROJ_TM == 0 and tp % GLA_CHUNK == 0 and tp % RET_CHUNK == 0

    xp = x_prompt.reshape(n_p, d)
    xs = x_sample.reshape(n_s, d)

    w_in_p = _wprep(w_in[0])
    w_lr = jnp.pad(w_in[0][:, C_GA:C_GA + GLA_LOWRANK], ((0, 0), (0, LANES - GLA_LOWRANK))).astype(BF16)
    w_gk2p = jnp.concatenate([w_gk2[0], jnp.zeros((LANES - GLA_LOWRANK, GLA_QK), F32)], axis=0)
    w_rt = jnp.concatenate([w_router_expert[0], w_router_group[0],
                            jnp.zeros((d, LANES - N_EXPERTS - N_GROUPS), F32)], axis=1)
    b_rt = jnp.concatenate([b_router_expert[0], b_router_group[0],
                            jnp.zeros((LANES - N_EXPERTS - N_GROUPS,), F32)])[None, :]

    proj = _inproj(xp, xs, ln_attn, w_in_p, w_lr)

    nw = gla_norm_w
    bgk = b_gk
    oa_p, sg_p = _gla(proj, w_gk2p, bgk, nw, None, row0=0, n_seq=bp, seq_len=tp, chunk=GLA_CHUNK, nseq=1)
    oa_s, sg_s = _gla(proj, w_gk2p, bgk, nw, state_gla[0], row0=n_p, n_seq=bs, seq_len=ts, chunk=ts, nseq=8)

    cos_p, sin_p = _rope_tables(jnp.arange(tp, dtype=F32), 1)
    ret_nseq_s = 4
    cos_s, sin_s = _rope_tables(jnp.arange(ts, dtype=F32) + float(PAST_LEN), ret_nseq_s)
    or_p, sr_p = _ret(proj, cos_p, sin_p, None, row0=0, n_seq=bp, seq_len=tp, chunk=RET_CHUNK, nseq=1)
    or_s, sr_s = _ret(proj, cos_s, sin_s, state_ret[0], row0=n_p, n_seq=bs, seq_len=ts, chunk=ts, nseq=ret_nseq_s)

    x1, hn, route, counts = _outproj_router(oa_p, oa_s, or_p, or_s, xp, xs, w_out[0].astype(BF16),
                                            ln_ffn, w_rt, b_rt)

    t_exp, n_valid, pos = _routing_tables(route, counts, n)
    y2 = _moe(t_exp, n_valid, pos, hn, w_exp_gate[0], w_exp_up[0], w_exp_down[0])

    ln_f = ln_final[None, :]
    y_p = _final(x1, y2, route, ln_f, row0=0, rows=n_p).reshape(bp, tp, d)
    y_s = _final(x1, y2, route, ln_f, row0=n_p, rows=n_s).reshape(bs, ts, d)
    return (y_p, y_s, sg_p[None], sr_p[None], sg_s[None], sr_s[None])
```

```python
import functools
import math

import jax
import jax.numpy as jnp
from jax import lax
from jax.experimental import pallas as pl
from jax.experimental.pallas import tpu as pltpu

F32 = jnp.float32
BF16 = jnp.bfloat16
I32 = jnp.int32

D_MODEL = 2048
PAST_LEN = 16384
H_GLA, DK_GLA, DV_GLA = 4, 128, 256
GLA_LOWRANK = 16
GLA_GATE_NORM = 16.0
GLA_SUBCHUNK = 16
H_RET, DK_RET, DV_RET = 4, 256, 256
ROPE_BASE = 10000.0
N_GROUPS, EXPERTS_PER_GROUP, D_EXPERT = 4, 8, 512
N_EXPERTS = N_GROUPS * EXPERTS_PER_GROUP
EPS = 1e-6

LANES = 128
SUBLANES = 8
V7X_SCOPED_VMEM_BYTES = 56 * 1024 * 1024

GLA_QK = H_GLA * DK_GLA
GLA_V = H_GLA * DV_GLA
RET_QK = H_RET * DK_RET
RET_V = H_RET * DV_RET
C_QA, C_KA, C_VA, C_GA = 0, GLA_QK, 2 * GLA_QK, 2 * GLA_QK + GLA_V
C_QR = C_GA + GLA_V
C_KR, C_VR, C_GR = C_QR + RET_QK, C_QR + 2 * RET_QK, C_QR + 2 * RET_QK + RET_V
C_LR = C_GR + RET_V
P_COLS = C_LR + LANES
IN_COLS_SRC = 2 * GLA_QK + GLA_V + GLA_LOWRANK + GLA_V + 2 * RET_QK + 2 * RET_V

PROJ_TM = 256
W_CHUNK = 1024
MOE_TM = 256
GLA_CHUNK = 128
RET_CHUNK = 128

def _dot(a, b):
    return jnp.dot(a, b, preferred_element_type=F32)


def _dot_nt(a, b):
    return lax.dot_general(a, b, (((1,), (1,)), ((), ())), preferred_element_type=F32)


def _dot_tn(a, b):
    return lax.dot_general(a, b, (((0,), (0,)), ((), ())), preferred_element_type=F32)


def _split_bf16(x, parts):
    out = []
    for _ in range(parts):
        p = x.astype(BF16)
        out.append(p)
        x = x - p.astype(F32)
    return out


def _dot_split(a, b_hi, b_lo):
    a_hi, a_lo = _split_bf16(a, 2)
    return _dot(a_hi, b_hi) + (_dot(a_lo, b_hi) + _dot(a_hi, b_lo))


def _rms(x):
    return x * lax.rsqrt(jnp.mean(x * x, axis=-1, keepdims=True) + EPS)


def _silu(x):
    return x * jax.nn.sigmoid(x)


def _wprep_body(a_ref, b_ref, o_ref, olr_ref):
    j = pl.program_id(0)
    a = a_ref[...]
    b = b_ref[...]
    shifted = jnp.concatenate([a[GLA_LOWRANK:, :], b], axis=0)
    o_ref[...] = jnp.where(j >= C_GA // W_CHUNK, shifted, a).T.astype(BF16)

    @pl.when(j == C_GA // W_CHUNK - 1)
    def _():
        padded = jnp.concatenate([b, jnp.zeros((LANES - GLA_LOWRANK, D_MODEL), F32)], axis=0)
        olr_ref[...] = padded.T.astype(BF16)


def _wprep(wt):
    per = W_CHUNK // GLA_LOWRANK
    return pl.pallas_call(
        _wprep_body,
        out_shape=(jax.ShapeDtypeStruct((D_MODEL, C_LR), BF16), jax.ShapeDtypeStruct((D_MODEL, LANES), BF16)),
        grid=(C_LR // W_CHUNK,),
        in_specs=[pl.BlockSpec((W_CHUNK, D_MODEL), lambda j: (j, 0)),
                  pl.BlockSpec((GLA_LOWRANK, D_MODEL), lambda j: (per * (j + 1), 0))],
        out_specs=(pl.BlockSpec((D_MODEL, W_CHUNK), lambda j: (0, j)),
                   pl.BlockSpec((D_MODEL, LANES), lambda j: (0, 0))),
        compiler_params=pltpu.CompilerParams(dimension_semantics=("arbitrary",),
                                             vmem_limit_bytes=V7X_SCOPED_VMEM_BYTES),
        name="wprep",
    )(wt, wt)


def _inproj_body(xp_ref, xs_ref, ln_ref, w_ref, wlr_ref, o_ref, *, n_prompt_blocks):
    x = jnp.where(pl.program_id(0) < n_prompt_blocks, xp_ref[...], xs_ref[...])
    h = (_rms(x) * ln_ref[...]).astype(BF16)
    for c in range(C_LR // W_CHUNK):
        o_ref[:, c * W_CHUNK:(c + 1) * W_CHUNK] = _dot(h, w_ref[:, c * W_CHUNK:(c + 1) * W_CHUNK])
    o_ref[:, C_LR:] = _dot(h, wlr_ref[...])


def _inproj(xp, xs, ln, w, w_lr):
    nbp = xp.shape[0] // PROJ_TM
    n = xp.shape[0] + xs.shape[0]
    fixed = lambda i: (0, 0)
    return pl.pallas_call(
        functools.partial(_inproj_body, n_prompt_blocks=nbp),
        out_shape=jax.ShapeDtypeStruct((n, P_COLS), F32),
        grid=(n // PROJ_TM,),
        in_specs=[pl.BlockSpec((PROJ_TM, D_MODEL), lambda i: (jnp.minimum(i, nbp - 1), 0)),
                  pl.BlockSpec((PROJ_TM, D_MODEL), lambda i: (jnp.maximum(i - nbp, 0), 0)),
                  pl.BlockSpec((1, D_MODEL), fixed),
                  pl.BlockSpec((D_MODEL, C_LR), fixed, pipeline_mode=pl.Buffered(1)),
                  pl.BlockSpec((D_MODEL, LANES), fixed)],
        out_specs=pl.BlockSpec((PROJ_TM, P_COLS), lambda i: (i, 0)),
        compiler_params=pltpu.CompilerParams(dimension_semantics=("arbitrary",),
                                             vmem_limit_bytes=V7X_SCOPED_VMEM_BYTES),
        name="inproj",
    )(xp, xs, ln, w, w_lr)


def _seg_cumsum(x, seg):
    pos = lax.broadcasted_iota(I32, x.shape, 0) & (seg - 1)
    s = 1
    while s < seg:
        x = x + jnp.where(pos >= s, pltpu.roll(x, s, axis=0), 0.0)
        s *= 2
    return x


def _block_ends(p, m):
    rows, w = p.shape
    return jnp.concatenate(
        [jnp.broadcast_to(p[j * m + m - 1:j * m + m, :], (m, w)) for j in range(rows // m)], axis=0)


def _block_starts(p, m, chunk):
    rows, w = p.shape
    pieces = []
    for j in range(rows // m):
        if (j * m) % chunk == 0:
            pieces.append(jnp.zeros((m, w), F32))
        else:
            pieces.append(jnp.broadcast_to(p[j * m - 1:j * m, :], (m, w)))
    return jnp.concatenate(pieces, axis=0)


def _gla_body(*refs, chunk, nseq, has_init):
    if has_init:
        (q_ref, k_ref, v_ref, g_ref, lr_ref, wgk_ref, bgk_ref, nw_ref, s0_ref,
         o_ref, sout_ref, s_scr) = refs
    else:
        (q_ref, k_ref, v_ref, g_ref, lr_ref, wgk_ref, bgk_ref, nw_ref,
         o_ref, sout_ref, s_scr) = refs
    t = pl.program_id(1)

    @pl.when(t == 0)
    def _():
        if has_init:
            s_scr[...] = s0_ref[...]
        else:
            s_scr[...] = jnp.zeros_like(s_scr)

    x = _dot_split(lr_ref[...], wgk_ref[0], wgk_ref[1]) + bgk_ref[...]
    log_g = -(jnp.maximum(-x, 0.0) + jnp.log1p(jnp.exp(-jnp.abs(x)))) / GLA_GATE_NORM
    p = _seg_cumsum(log_g, chunk)
    p_end = _block_ends(p, chunk)
    eq_c = jnp.exp(p)
    ek_c = jnp.exp(p_end - p)
    sub = min(GLA_SUBCHUNK, chunk)
    p_sub = p - _block_starts(p, sub, chunk)
    eq_d = jnp.exp(p_sub)
    ek_d = jnp.exp(-p_sub)
    levels = []
    m = sub
    while m < chunk:
        levels.append((m, jnp.exp(p - _block_starts(p, m, chunk)), jnp.exp(_block_ends(p, m) - p)))
        m *= 2

    ti = lax.broadcasted_iota(I32, (chunk, chunk), 0)
    si = lax.broadcasted_iota(I32, (chunk, chunk), 1)
    blk = lambda idx, size: idx >> (size.bit_length() - 1)
    mask_d = (blk(ti, sub) == blk(si, sub)) & (si <= ti)
    masks = [((blk(ti, m) & 1) == 1) & (blk(si, m) == blk(ti, m) - 1) for (m, _, _) in levels]
    sel8 = (lax.broadcasted_iota(I32, (8, DV_GLA), 0) == 0).astype(BF16)
    nw = nw_ref[...]

    outs = [[None] * nseq for _ in range(H_GLA)]
    for b in range(nseq):
        r0 = b * chunk
        rows = slice(r0, r0 + chunk)
        for h in range(H_GLA):
            kc = slice(h * DK_GLA, (h + 1) * DK_GLA)
            vc = slice(h * DV_GLA, (h + 1) * DV_GLA)
            q = q_ref[rows, kc] * (DK_GLA ** -0.5)
            k = k_ref[rows, kc]
            v = v_ref[rows, vc].astype(BF16)
            att = jnp.where(mask_d, _dot_nt((q * eq_d[rows, kc]).astype(BF16),
                                            (k * ek_d[rows, kc]).astype(BF16)), 0.0)
            for (m, eq_m, ek_m), mask in zip(levels, masks):
                att = jnp.where(mask, _dot_nt((q * eq_m[rows, kc]).astype(BF16),
                                              (k * ek_m[rows, kc]).astype(BF16)), att)
            s_old = s_scr[b, h]
            o = _dot(att.astype(BF16), v) + _dot((q * eq_c[rows, kc]).astype(BF16), s_old.astype(BF16))
            kv = _dot_tn((k * ek_c[rows, kc]).astype(BF16), v)
            tot = jnp.broadcast_to(p[r0 + chunk - 1:r0 + chunk, kc], (8, DK_GLA))
            t_hi, t_mid, t_lo = _split_bf16(tot, 3)
            decay = jnp.exp(_dot_tn(t_hi, sel8) + (_dot_tn(t_mid, sel8) + _dot_tn(t_lo, sel8)))
            s_scr[b, h] = decay * s_old + kv
            outs[h][b] = _rms(o) * nw * _silu(g_ref[rows, vc])
    for h in range(H_GLA):
        o_ref[:, h * DV_GLA:(h + 1) * DV_GLA] = jnp.concatenate(outs[h], axis=0).astype(BF16)

    @pl.when(t == pl.num_programs(1) - 1)
    def _():
        sout_ref[...] = s_scr[...]


def _gla(proj, w_gk2p, b_gk, nw, s0, *, row0, n_seq, seq_len, chunk, nseq):
    rows = nseq * chunk
    nt = seq_len // chunk
    base = row0 // rows

    def rmap(col):
        return lambda b, t: (base + b * nt + t, col)

    in_specs = [pl.BlockSpec((rows, GLA_QK), rmap(C_QA // GLA_QK)),
                pl.BlockSpec((rows, GLA_QK), rmap(C_KA // GLA_QK)),
                pl.BlockSpec((rows, GLA_V), rmap(C_VA // GLA_V)),
                pl.BlockSpec((rows, GLA_V), rmap(C_GA // GLA_V)),
                pl.BlockSpec((rows, LANES), rmap(C_LR // LANES)),
                pl.BlockSpec((2, LANES, GLA_QK), lambda b, t: (0, 0, 0)),
                pl.BlockSpec((1, GLA_QK), lambda b, t: (0, 0)),
                pl.BlockSpec((1, DV_GLA), lambda b, t: (0, 0))]
    args = [proj, proj, proj, proj, proj, w_gk2p, b_gk, nw]
    state_spec = pl.BlockSpec((nseq, H_GLA, DK_GLA, DV_GLA), lambda b, t: (b, 0, 0, 0))
    if s0 is not None:
        in_specs.append(state_spec)
        args.append(s0)
    return pl.pallas_call(
        functools.partial(_gla_body, chunk=chunk, nseq=nseq, has_init=s0 is not None),
        out_shape=(jax.ShapeDtypeStruct((n_seq * seq_len, GLA_V), BF16),
                   jax.ShapeDtypeStruct((n_seq, H_GLA, DK_GLA, DV_GLA), F32)),
        grid=(n_seq // nseq, nt),
        in_specs=in_specs,
        out_specs=(pl.BlockSpec((rows, GLA_V), lambda b, t: (b * nt + t, 0)), state_spec),
        scratch_shapes=[pltpu.VMEM((nseq, H_GLA, DK_GLA, DV_GLA), F32)],
        compiler_params=pltpu.CompilerParams(dimension_semantics=("arbitrary", "arbitrary"),
                                             vmem_limit_bytes=V7X_SCOPED_VMEM_BYTES),
        name="gla_init" if s0 is not None else "gla",
    )(*args)


def _ret_body(*refs, chunk, nseq, has_init):
    if has_init:
        (q_ref, k_ref, v_ref, g_ref, cos_ref, sin_ref, s0_ref, o_ref, sout_ref, s_scr) = refs
    else:
        (q_ref, k_ref, v_ref, g_ref, cos_ref, sin_ref, o_ref, sout_ref, s_scr) = refs
    t = pl.program_id(1)

    @pl.when(t == 0)
    def _():
        if has_init:
            s_scr[...] = s0_ref[...]
        else:
            s_scr[...] = jnp.zeros_like(s_scr)

    half = DK_RET // 2
    ti = lax.broadcasted_iota(I32, (chunk, chunk), 0)
    si = lax.broadcasted_iota(I32, (chunk, chunk), 1)
    diff = (ti - si).astype(F32)
    idx = lax.broadcasted_iota(I32, (chunk, 1), 0).astype(F32)

    outs = [[None] * nseq for _ in range(H_RET)]
    for h in range(H_RET):
        lg = math.log(1.0 - 2.0 ** (-5.0 - h))
        dmat = jnp.where(diff >= 0, jnp.exp(lg * jnp.maximum(diff, 0.0)), 0.0)
        q_dec = jnp.exp(lg * (idx + 1.0))
        k_dec = jnp.exp(lg * (chunk - 1.0 - idx))
        c_dec = math.exp(lg * chunk)
        for b in range(nseq):
            rows = slice(b * chunk, (b + 1) * chunk)
            cos = cos_ref[rows, :]
            sin = sin_ref[rows, :]
            c1 = slice(h * DK_RET, h * DK_RET + half)
            c2 = slice(h * DK_RET + half, (h + 1) * DK_RET)
            vc = slice(h * DV_RET, (h + 1) * DV_RET)
            q1, q2 = q_ref[rows, c1], q_ref[rows, c2]
            k1, k2 = k_ref[rows, c1], k_ref[rows, c2]
            q = jnp.concatenate([q1 * cos - q2 * sin, q1 * sin + q2 * cos], axis=-1)
            k = jnp.concatenate([k1 * cos - k2 * sin, k1 * sin + k2 * cos], axis=-1) * (DK_RET ** -0.5)
            v = v_ref[rows, vc].astype(BF16)
            qb = q.astype(BF16)
            s_old = s_scr[b, h]
            att = _dot_nt(qb, k.astype(BF16)) * dmat
            o = _dot(att.astype(BF16), v) + _dot(qb, s_old.astype(BF16)) * q_dec
            s_scr[b, h] = c_dec * s_old + _dot_tn((k * k_dec).astype(BF16), v)
            outs[h][b] = _rms(o) * _silu(g_ref[rows, vc])
    for h in range(H_RET):
        o_ref[:, h * DV_RET:(h + 1) * DV_RET] = jnp.concatenate(outs[h], axis=0).astype(BF16)

    @pl.when(t == pl.num_programs(1) - 1)
    def _():
        sout_ref[...] = s_scr[...]


def _ret(proj, cos, sin, s0, *, row0, n_seq, seq_len, chunk, nseq):
    rows = nseq * chunk
    nt = seq_len // chunk
    base = row0 // rows

    def rmap(col):
        return lambda b, t: (base + b * nt + t, col)

    tab_spec = pl.BlockSpec((rows, DK_RET // 2), lambda b, t: (t, 0))
    in_specs = [pl.BlockSpec((rows, RET_QK), rmap(C_QR // RET_QK)),
                pl.BlockSpec((rows, RET_QK), rmap(C_KR // RET_QK)),
                pl.BlockSpec((rows, RET_V), rmap(C_VR // RET_V)),
                pl.BlockSpec((rows, RET_V), rmap(C_GR // RET_V)),
                tab_spec, tab_spec]
    args = [proj, proj, proj, proj, cos, sin]
    state_spec = pl.BlockSpec((nseq, H_RET, DK_RET, DV_RET), lambda b, t: (b, 0, 0, 0))
    if s0 is not None:
        in_specs.append(state_spec)
        args.append(s0)
    return pl.pallas_call(
        functools.partial(_ret_body, chunk=chunk, nseq=nseq, has_init=s0 is not None),
        out_shape=(jax.ShapeDtypeStruct((n_seq * seq_len, RET_V), BF16),
                   jax.ShapeDtypeStruct((n_seq, H_RET, DK_RET, DV_RET), F32)),
        grid=(n_seq // nseq, nt),
        in_specs=in_specs,
        out_specs=(pl.BlockSpec((rows, RET_V), lambda b, t: (b * nt + t, 0)), state_spec),
        scratch_shapes=[pltpu.VMEM((nseq, H_RET, DK_RET, DV_RET), F32)],
        compiler_params=pltpu.CompilerParams(dimension_semantics=("arbitrary", "arbitrary"),
                                             vmem_limit_bytes=V7X_SCOPED_VMEM_BYTES),
        name="ret_init" if s0 is not None else "ret",
    )(*args)


def _rope_tables(pos, reps):
    half = DK_RET // 2
    inv = ROPE_BASE ** (-jnp.arange(half, dtype=F32) / half)
    ang = pos[:, None] * inv[None, :]
    return jnp.tile(jnp.cos(ang), (reps, 1)), jnp.tile(jnp.sin(ang), (reps, 1))


def _outproj_body(oap_ref, oas_ref, orp_ref, ors_ref, xp_ref, xs_ref, w_ref, ln_ref, wrt_ref, brt_ref,
                  x1_ref, hn_ref, route_ref, counts_ref, *, n_prompt_blocks):
    step = pl.program_id(0)
    is_prompt = step < n_prompt_blocks
    oa = jnp.where(is_prompt, oap_ref[...], oas_ref[...])
    orr = jnp.where(is_prompt, orp_ref[...], ors_ref[...])
    x = jnp.where(is_prompt, xp_ref[...], xs_ref[...])
    x1 = x + _dot(oa, w_ref[:GLA_V, :]) + _dot(orr, w_ref[GLA_V:, :])
    x1_ref[...] = x1
    hn = _rms(x1) * ln_ref[...]
    hn_ref[...] = hn

    logits = _dot_split(hn, wrt_ref[0], wrt_ref[1]) + brt_ref[...]
    lane_i = lax.broadcasted_iota(I32, logits.shape, 1)
    lane = lane_i.astype(F32)
    grp = (lane_i >> (EXPERTS_PER_GROUP.bit_length() - 1)).astype(F32)
    neg = -jnp.inf
    far = float(LANES)

    gl = jnp.where((lane_i >= N_EXPERTS) & (lane_i < N_EXPERTS + N_GROUPS), logits, neg)
    gmax = jnp.max(gl, axis=-1, keepdims=True)
    g_w = 1.0 / jnp.sum(jnp.exp(gl - gmax), axis=-1, keepdims=True)
    g_idx = jnp.min(jnp.where(gl == gmax, lane, far), axis=-1, keepdims=True) - float(N_EXPERTS)

    el = jnp.where((lane_i < N_EXPERTS) & (grp == g_idx), logits, neg)
    m1 = jnp.max(el, axis=-1, keepdims=True)
    esum = jnp.sum(jnp.exp(el - m1), axis=-1, keepdims=True)
    i1 = jnp.min(jnp.where(el == m1, lane, far), axis=-1, keepdims=True)
    el2 = jnp.where(lane == i1, neg, el)
    m2 = jnp.max(el2, axis=-1, keepdims=True)
    i2 = jnp.min(jnp.where(el2 == m2, lane, far), axis=-1, keepdims=True)
    p1 = 1.0 / esum
    p2 = jnp.exp(m2 - m1) / esum
    gate1 = g_w * (p1 / (p1 + p2))
    gate2 = g_w * (p2 / (p1 + p2))
    @pl.when(step == 0)
    def _():
        counts_ref[...] = jnp.zeros_like(counts_ref)

    tm = logits.shape[0]
    earlier = (lax.broadcasted_iota(I32, (tm, tm), 0) > lax.broadcasted_iota(I32, (tm, tm), 1)).astype(BF16)
    running = counts_ref[0:1, :]
    ranks = []
    for idx in (i1, i2):
        onehot = (lane == idx).astype(F32)
        before = _dot(earlier, onehot.astype(BF16)) + running
        ranks.append(jnp.sum(onehot * before, axis=-1, keepdims=True))
        running = running + jnp.sum(onehot, axis=0, keepdims=True)
    counts_ref[...] = jnp.broadcast_to(running, counts_ref.shape)

    route_ref[...] = jnp.where(lane_i == 0, i1,
                               jnp.where(lane_i == 1, i2,
                                         jnp.where(lane_i == 2, gate1,
                                                   jnp.where(lane_i == 3, gate2,
                                                             jnp.where(lane_i == 4, ranks[0],
                                                                       jnp.where(lane_i == 5, ranks[1], 0.0))))))


def _outproj_router(oa_p, oa_s, or_p, or_s, xp, xs, w_out, ln, w_rt, b_rt):
    n = xp.shape[0] + xs.shape[0]
    nbp = xp.shape[0] // PROJ_TM
    pmap = lambda i: (jnp.minimum(i, nbp - 1), 0)
    smap = lambda i: (jnp.maximum(i - nbp, 0), 0)
    row = lambda i: (i, 0)
    fixed = lambda i: (0, 0)
    return pl.pallas_call(
        functools.partial(_outproj_body, n_prompt_blocks=nbp),
        out_shape=(jax.ShapeDtypeStruct((n, D_MODEL), F32),
                   jax.ShapeDtypeStruct((n, D_MODEL), F32),
                   jax.ShapeDtypeStruct((n, LANES), F32),
                   jax.ShapeDtypeStruct((SUBLANES, LANES), F32)),
        grid=(n // PROJ_TM,),
        in_specs=[pl.BlockSpec((PROJ_TM, GLA_V), pmap), pl.BlockSpec((PROJ_TM, GLA_V), smap),
                  pl.BlockSpec((PROJ_TM, RET_V), pmap), pl.BlockSpec((PROJ_TM, RET_V), smap),
                  pl.BlockSpec((PROJ_TM, D_MODEL), pmap), pl.BlockSpec((PROJ_TM, D_MODEL), smap),
                  pl.BlockSpec((GLA_V + RET_V, D_MODEL), fixed, pipeline_mode=pl.Buffered(1)),
                  pl.BlockSpec((1, D_MODEL), fixed),
                  pl.BlockSpec((2, D_MODEL, LANES), lambda i: (0, 0, 0)),
                  pl.BlockSpec((1, LANES), fixed)],
        out_specs=(pl.BlockSpec((PROJ_TM, D_MODEL), row), pl.BlockSpec((PROJ_TM, D_MODEL), row),
                   pl.BlockSpec((PROJ_TM, LANES), row), pl.BlockSpec((SUBLANES, LANES), fixed)),
        compiler_params=pltpu.CompilerParams(dimension_semantics=("arbitrary",),
                                             vmem_limit_bytes=V7X_SCOPED_VMEM_BYTES),
        name="outproj_router",
    )(oa_p, oa_s, or_p, or_s, xp, xs, w_out, ln, w_rt, b_rt)


def _moe_body(texp_ref, nval_ref, pos_ref, hn_hbm, wg_ref, wu_ref, wd_ref, y_hbm,
              inv, xbuf, ybuf, wg_b, wu_b, wd_b, gsem, ssem, *, n_tokens):
    tm = MOE_TM
    i = pl.program_id(0)
    nt = pl.num_programs(0)
    slot = i % 2

    def for_rows(n_rows, fn):
        n_grp = n_rows // SUBLANES

        def group(g, c):
            for j in range(SUBLANES):
                fn(g * SUBLANES + j)
            return c
        lax.fori_loop(0, n_grp, group, 0)

        def single(r, c):
            fn(r)
            return c
        lax.fori_loop(n_grp * SUBLANES, n_rows, single, 0)

    def gather_copy(tok, r, slot_):
        return pltpu.make_async_copy(hn_hbm.at[pl.ds(tok, 1), :], xbuf.at[slot_, pl.ds(r, 1), :], gsem.at[slot_])

    def scatter_copy(r, dst, slot_):
        return pltpu.make_async_copy(ybuf.at[slot_, pl.ds(r, 1), :], y_hbm.at[pl.ds(dst, 1), :], ssem.at[slot_])

    def start_gather(tile, slot_):
        for_rows(nval_ref[tile], lambda r: gather_copy(inv[tile * tm + r] >> 1, r, slot_).start())

    def wait_rows(n_rows, group_copy, row_copy):
        n_grp = pl.multiple_of((n_rows // SUBLANES) * SUBLANES, SUBLANES)

        @pl.when(n_grp > 0)
        def _():
            group_copy(n_grp).wait()

        def single(r, c):
            row_copy().wait()
            return c
        lax.fori_loop(n_grp, n_rows, single, 0)

    def wait_gather(slot_, n_rows):
        wait_rows(n_rows,
                  lambda m: pltpu.make_async_copy(hn_hbm.at[pl.ds(0, m), :], xbuf.at[slot_, pl.ds(0, m), :],
                                                  gsem.at[slot_]),
                  lambda: gather_copy(0, 0, slot_))

    def wait_scatter(slot_, n_rows):
        wait_rows(n_rows,
                  lambda m: pltpu.make_async_copy(ybuf.at[slot_, pl.ds(0, m), :], y_hbm.at[pl.ds(0, m), :],
                                                  ssem.at[slot_]),
                  lambda: scatter_copy(0, 0, slot_))

    @pl.when(i == 0)
    def _():
        def body(a, c):
            inv[pos_ref[a]] = a
            return c
        lax.fori_loop(0, 2 * n_tokens, body, 0, unroll=8)
        xbuf[...] = jnp.zeros_like(xbuf)
        start_gather(0, 0)

    @pl.when(i + 1 < nt)
    def _():
        start_gather(jnp.minimum(i + 1, nt - 1), 1 - slot)

    @pl.when(i >= 2)
    def _():
        wait_scatter(slot, nval_ref[jnp.maximum(i - 2, 0)])

    nv = nval_ref[i]
    wait_gather(slot, nv)

    @pl.when(nv > 0)
    def _():
        @pl.when((i == 0) | (texp_ref[i] != texp_ref[jnp.maximum(i - 1, 0)]))
        def _():
            wg_b[...] = wg_ref[0].astype(BF16)
            wu_b[...] = wu_ref[0].astype(BF16)
            wd_b[...] = wd_ref[0].astype(BF16)

        h = xbuf[slot].astype(BF16)
        hid = _silu(_dot(h, wg_b[...])) * _dot(h, wu_b[...])
        ybuf[slot] = _dot(hid.astype(BF16), wd_b[...])

        def scatter_one(r):
            a = inv[i * tm + r]
            scatter_copy(r, (a & 1) * n_tokens + (a >> 1), slot).start()
        for_rows(nv, scatter_one)

    @pl.when(i == nt - 1)
    def _():
        @pl.when(i >= 1)
        def _():
            wait_scatter(1 - slot, nval_ref[jnp.maximum(i - 1, 0)])
        wait_scatter(slot, nv)


def _moe(tile_expert, n_valid, pos, hn, w_gate, w_up, w_down):
    n = hn.shape[0]
    n_tiles = tile_expert.shape[0]
    wmap = lambda i, te, nv, pos_: (te[i], 0, 0)
    return pl.pallas_call(
        functools.partial(_moe_body, n_tokens=n),
        out_shape=jax.ShapeDtypeStruct((2 * n, D_MODEL), F32),
        grid_spec=pltpu.PrefetchScalarGridSpec(
            num_scalar_prefetch=3,
            grid=(n_tiles,),
            in_specs=[pl.BlockSpec(memory_space=pl.ANY),
                      pl.BlockSpec((1, D_MODEL, D_EXPERT), wmap),
                      pl.BlockSpec((1, D_MODEL, D_EXPERT), wmap),
                      pl.BlockSpec((1, D_EXPERT, D_MODEL), wmap)],
            out_specs=pl.BlockSpec(memory_space=pl.ANY),
            scratch_shapes=[pltpu.SMEM((n_tiles * MOE_TM,), I32),
                            pltpu.VMEM((2, MOE_TM, D_MODEL), F32),
                            pltpu.VMEM((2, MOE_TM, D_MODEL), F32),
                            pltpu.VMEM((D_MODEL, D_EXPERT), BF16),
                            pltpu.VMEM((D_MODEL, D_EXPERT), BF16),
                            pltpu.VMEM((D_EXPERT, D_MODEL), BF16),
                            pltpu.SemaphoreType.DMA((2,)),
                            pltpu.SemaphoreType.DMA((2,))]),
        compiler_params=pltpu.CompilerParams(dimension_semantics=("arbitrary",),
                                             vmem_limit_bytes=V7X_SCOPED_VMEM_BYTES),
        name="moe",
    )(tile_expert, n_valid, pos, hn, w_gate, w_up, w_down)


def _routing_tables(route, counts, n):
    tm = MOE_TM
    n_tiles = 2 * n // tm + N_EXPERTS
    experts = route[:, 0:2].astype(I32)
    rank = route[:, 4:6].astype(I32)
    counts = counts[0, :N_EXPERTS].astype(I32)
    tiles = (counts + tm - 1) // tm
    tile_end = jnp.cumsum(tiles)
    tile_start = tile_end - tiles
    onehot = experts[:, :, None] == jnp.arange(N_EXPERTS, dtype=I32)[None, None, :]
    pos = jnp.sum(jnp.where(onehot, (tile_start * tm)[None, None, :], 0), axis=-1) + rank
    tidx = jnp.arange(n_tiles, dtype=I32)
    t_exp = jnp.sum((tidx[:, None] >= tile_end[None, :]).astype(I32), axis=1)
    used = tidx < tile_end[-1]
    last_exp = jnp.sum((tile_end[-1] - 1 >= tile_end).astype(I32))
    t_exp = jnp.where(used, t_exp, last_exp)
    n_valid = jnp.where(used, jnp.clip(counts[t_exp] - (tidx - tile_start[t_exp]) * tm, 0, tm), 0)
    return t_exp.astype(I32), n_valid.astype(I32), pos.reshape(2 * n)


def _final_body(x1_ref, y0_ref, y1_ref, route_ref, ln_ref, o_ref):
    z = x1_ref[...] + route_ref[:, 2:3] * y0_ref[...] + route_ref[:, 3:4] * y1_ref[...]
    o_ref[...] = _rms(z) * ln_ref[...]


def _final(x1, y2, route, ln, *, row0, rows):
    base = row0 // PROJ_TM
    slot1 = x1.shape[0] // PROJ_TM
    return pl.pallas_call(
        _final_body,
        out_shape=jax.ShapeDtypeStruct((rows, D_MODEL), F32),
        grid=(rows // PROJ_TM,),
        in_specs=[pl.BlockSpec((PROJ_TM, D_MODEL), lambda i: (base + i, 0)),
                  pl.BlockSpec((PROJ_TM, D_MODEL), lambda i: (base + i, 0)),
                  pl.BlockSpec((PROJ_TM, D_MODEL), lambda i: (slot1 + base + i, 0)),
                  pl.BlockSpec((PROJ_TM, LANES), lambda i: (base + i, 0)),
                  pl.BlockSpec((1, D_MODEL), lambda i: (0, 0))],
        out_specs=pl.BlockSpec((PROJ_TM, D_MODEL), lambda i: (i, 0)),
        compiler_params=pltpu.CompilerParams(dimension_semantics=("arbitrary",)),
        name="final_norm",
    )(x1, y2, y2, route, ln)


def kernel(x_prompt, x_sample, state_gla, state_ret, ln_attn, w_in, w_gk2, b_gk, gla_norm_w, w_out, ln_ffn, w_router_group, b_router_group, w_router_expert, b_router_expert, w_exp_gate, w_exp_up, w_exp_down, ln_final):
    bp, tp, d = x_prompt.shape
    bs, ts, _ = x_sample.shape
    assert d == D_MODEL and w_in.shape == (1, D_MODEL, IN_COLS_SRC)
    n_p, n_s = bp * tp, bs * ts
    n = n_p + n_s
    assert n_p % PROJ_TM == 0 and n_s % PROJ_TM == 0 and tp % GLA_CHUNK == 0 and tp % RET_CHUNK == 0

    xp = x_prompt.reshape(n_p, d)
    xs = x_sample.reshape(n_s, d)

    hi_lo = lambda a: jnp.stack(_split_bf16(a, 2))
    w_in_p, w_lr = _wprep(w_in[0].T)
    w_gk2p = hi_lo(jnp.concatenate([w_gk2[0], jnp.zeros((LANES - GLA_LOWRANK, GLA_QK), F32)], axis=0))
    w_rt = hi_lo(jnp.concatenate([w_router_expert[0], w_router_group[0],
                                  jnp.zeros((d, LANES - N_EXPERTS - N_GROUPS), F32)], axis=1))
    b_rt = jnp.concatenate([b_router_expert[0], b_router_group[0],
                            jnp.zeros((LANES - N_EXPERTS - N_GROUPS,), F32)])[None, :]

    proj = _inproj(xp, xs, ln_attn, w_in_p, w_lr)

    nw = gla_norm_w
    bgk = b_gk
    oa_p, sg_p = _gla(proj, w_gk2p, bgk, nw, None, row0=0, n_seq=bp, seq_len=tp, chunk=GLA_CHUNK, nseq=1)
    oa_s, sg_s = _gla(proj, w_gk2p, bgk, nw, state_gla[0], row0=n_p, n_seq=bs, seq_len=ts, chunk=ts, nseq=8)

    cos_p, sin_p = _rope_tables(jnp.arange(tp, dtype=F32), 1)
    ret_nseq_s = 4
    cos_s, sin_s = _rope_tables(jnp.arange(ts, dtype=F32) + float(PAST_LEN), ret_nseq_s)
    or_p, sr_p = _ret(proj, cos_p, sin_p, None, row0=0, n_seq=bp, seq_len=tp, chunk=RET_CHUNK, nseq=1)
    or_s, sr_s = _ret(proj, cos_s, sin_s, state_ret[0], row0=n_p, n_seq=bs, seq_len=ts, chunk=ts, nseq=ret_nseq_s)

    x1, hn, route, counts = _outproj_router(oa_p, oa_s, or_p, or_s, xp, xs, w_out[0].astype(BF16),
                                            ln_ffn, w_rt, b_rt)

    t_exp, n_valid, pos = _routing_tables(route, counts, n)
    y2 = _moe(t_exp, n_valid, pos, hn, w_exp_gate[0], w_exp_up[0], w_exp_down[0])

    ln_f = ln_final[None, :]
    y_p = _final(x1, y2, route, ln_f, row0=0, rows=n_p).reshape(bp, tp, d)
    y_s = _final(x1, y2, route, ln_f, row0=n_p, rows=n_s).reshape(bs, ts, d)
    return (y_p, y_s, sg_p[None], sr_p[None], sg_s[None], sr_s[None])
```

```python
import functools
import math

import jax
import jax.numpy as jnp
from jax import lax
from jax.experimental import pallas as pl
from jax.experimental.pallas import tpu as pltpu

F32 = jnp.float32
BF16 = jnp.bfloat16
I32 = jnp.int32

D_MODEL = 2048
PAST_LEN = 16384
H_GLA, DK_GLA, DV_GLA = 4, 128, 256
GLA_LOWRANK = 16
GLA_GATE_NORM = 16.0
GLA_SUBCHUNK = 16
H_RET, DK_RET, DV_RET = 4, 256, 256
ROPE_BASE = 10000.0
N_GROUPS, EXPERTS_PER_GROUP, D_EXPERT = 4, 8, 512
N_EXPERTS = N_GROUPS * EXPERTS_PER_GROUP
EPS = 1e-6

LANES = 128
SUBLANES = 8
V7X_SCOPED_VMEM_BYTES = 56 * 1024 * 1024

GLA_QK = H_GLA * DK_GLA
GLA_V = H_GLA * DV_GLA
RET_QK = H_RET * DK_RET
RET_V = H_RET * DV_RET
C_QA, C_KA, C_VA, C_GA = 0, GLA_QK, 2 * GLA_QK, 2 * GLA_QK + GLA_V
C_QR = C_GA + GLA_V
C_KR, C_VR, C_GR = C_QR + RET_QK, C_QR + 2 * RET_QK, C_QR + 2 * RET_QK + RET_V
C_LR = C_GR + RET_V
P_COLS = C_LR + LANES
IN_COLS_SRC = 2 * GLA_QK + GLA_V + GLA_LOWRANK + GLA_V + 2 * RET_QK + 2 * RET_V

PROJ_TM = 256
W_CHUNK = 1024
MOE_TM = 256
MOE_COL_CHUNK = 256
MOE_K_CHUNK = 256
GLA_CHUNK = 128
RET_CHUNK = 128

def _dot(a, b):
    return jnp.dot(a, b, preferred_element_type=F32)


def _dot_nt(a, b):
    return lax.dot_general(a, b, (((1,), (1,)), ((), ())), preferred_element_type=F32)


def _dot_tn(a, b):
    return lax.dot_general(a, b, (((0,), (0,)), ((), ())), preferred_element_type=F32)


def _split_bf16(x, parts):
    out = []
    for _ in range(parts):
        p = x.astype(BF16)
        out.append(p)
        x = x - p.astype(F32)
    return out


def _dot_split(a, b_hi, b_lo):
    a_hi, a_lo = _split_bf16(a, 2)
    return _dot(a_hi, b_hi) + (_dot(a_lo, b_hi) + _dot(a_hi, b_lo))


def _rms(x):
    return x * lax.rsqrt(jnp.mean(x * x, axis=-1, keepdims=True) + EPS)


def _silu(x):
    return x * jax.nn.sigmoid(x)


def _wprep_body(a_ref, b_ref, o_ref, olr_ref):
    j = pl.program_id(0)
    a = a_ref[...]
    b = b_ref[...]
    shifted = jnp.concatenate([a[GLA_LOWRANK:, :], b], axis=0)
    o_ref[...] = jnp.where(j >= C_GA // W_CHUNK, shifted, a).T.astype(BF16)

    @pl.when(j == C_GA // W_CHUNK - 1)
    def _():
        padded = jnp.concatenate([b, jnp.zeros((LANES - GLA_LOWRANK, D_MODEL), F32)], axis=0)
        olr_ref[...] = padded.T.astype(BF16)


def _wprep(wt):
    per = W_CHUNK // GLA_LOWRANK
    return pl.pallas_call(
        _wprep_body,
        out_shape=(jax.ShapeDtypeStruct((D_MODEL, C_LR), BF16), jax.ShapeDtypeStruct((D_MODEL, LANES), BF16)),
        grid=(C_LR // W_CHUNK,),
        in_specs=[pl.BlockSpec((W_CHUNK, D_MODEL), lambda j: (j, 0)),
                  pl.BlockSpec((GLA_LOWRANK, D_MODEL), lambda j: (per * (j + 1), 0))],
        out_specs=(pl.BlockSpec((D_MODEL, W_CHUNK), lambda j: (0, j)),
                   pl.BlockSpec((D_MODEL, LANES), lambda j: (0, 0))),
        compiler_params=pltpu.CompilerParams(dimension_semantics=("arbitrary",),
                                             vmem_limit_bytes=V7X_SCOPED_VMEM_BYTES),
        name="wprep",
    )(wt, wt)


def _inproj_body(xp_ref, xs_ref, ln_ref, w_ref, wlr_ref, o_ref, *, n_prompt_blocks):
    x = jnp.where(pl.program_id(0) < n_prompt_blocks, xp_ref[...], xs_ref[...])
    h = (_rms(x) * ln_ref[...]).astype(BF16)
    for c in range(C_LR // W_CHUNK):
        o_ref[:, c * W_CHUNK:(c + 1) * W_CHUNK] = _dot(h, w_ref[:, c * W_CHUNK:(c + 1) * W_CHUNK])
    o_ref[:, C_LR:] = _dot(h, wlr_ref[...])


def _inproj(xp, xs, ln, w, w_lr):
    nbp = xp.shape[0] // PROJ_TM
    n = xp.shape[0] + xs.shape[0]
    fixed = lambda i: (0, 0)
    return pl.pallas_call(
        functools.partial(_inproj_body, n_prompt_blocks=nbp),
        out_shape=jax.ShapeDtypeStruct((n, P_COLS), F32),
        grid=(n // PROJ_TM,),
        in_specs=[pl.BlockSpec((PROJ_TM, D_MODEL), lambda i: (jnp.minimum(i, nbp - 1), 0)),
                  pl.BlockSpec((PROJ_TM, D_MODEL), lambda i: (jnp.maximum(i - nbp, 0), 0)),
                  pl.BlockSpec((1, D_MODEL), fixed),
                  pl.BlockSpec((D_MODEL, C_LR), fixed, pipeline_mode=pl.Buffered(1)),
                  pl.BlockSpec((D_MODEL, LANES), fixed)],
        out_specs=pl.BlockSpec((PROJ_TM, P_COLS), lambda i: (i, 0)),
        compiler_params=pltpu.CompilerParams(dimension_semantics=("arbitrary",),
                                             vmem_limit_bytes=V7X_SCOPED_VMEM_BYTES),
        name="inproj",
    )(xp, xs, ln, w, w_lr)


def _seg_cumsum(x, seg):
    pos = lax.broadcasted_iota(I32, x.shape, 0) & (seg - 1)
    s = 1
    while s < seg:
        x = x + jnp.where(pos >= s, pltpu.roll(x, s, axis=0), 0.0)
        s *= 2
    return x


def _block_ends(p, m):
    rows, w = p.shape
    return jnp.concatenate(
        [jnp.broadcast_to(p[j * m + m - 1:j * m + m, :], (m, w)) for j in range(rows // m)], axis=0)


def _block_starts(p, m, chunk):
    rows, w = p.shape
    pieces = []
    for j in range(rows // m):
        if (j * m) % chunk == 0:
            pieces.append(jnp.zeros((m, w), F32))
        else:
            pieces.append(jnp.broadcast_to(p[j * m - 1:j * m, :], (m, w)))
    return jnp.concatenate(pieces, axis=0)


def _gla_body(*refs, chunk, nseq, has_init):
    if has_init:
        (q_ref, k_ref, v_ref, g_ref, lr_ref, wgk_ref, bgk_ref, nw_ref, s0_ref,
         o_ref, sout_ref, s_scr) = refs
    else:
        (q_ref, k_ref, v_ref, g_ref, lr_ref, wgk_ref, bgk_ref, nw_ref,
         o_ref, sout_ref, s_scr) = refs
    t = pl.program_id(1)

    @pl.when(t == 0)
    def _():
        if has_init:
            s_scr[...] = s0_ref[...]
        else:
            s_scr[...] = jnp.zeros_like(s_scr)

    x = _dot_split(lr_ref[...], wgk_ref[0], wgk_ref[1]) + bgk_ref[...]
    log_g = -(jnp.maximum(-x, 0.0) + jnp.log1p(jnp.exp(-jnp.abs(x)))) / GLA_GATE_NORM
    p = _seg_cumsum(log_g, chunk)
    p_end = _block_ends(p, chunk)
    eq_c = jnp.exp(p)
    ek_c = jnp.exp(p_end - p)
    sub = min(GLA_SUBCHUNK, chunk)
    p_sub = p - _block_starts(p, sub, chunk)
    eq_d = jnp.exp(p_sub)
    ek_d = jnp.exp(-p_sub)
    levels = []
    m = sub
    while m < chunk:
        levels.append((m, jnp.exp(p - _block_starts(p, m, chunk)), jnp.exp(_block_ends(p, m) - p)))
        m *= 2

    ti = lax.broadcasted_iota(I32, (chunk, chunk), 0)
    si = lax.broadcasted_iota(I32, (chunk, chunk), 1)
    blk = lambda idx, size: idx >> (size.bit_length() - 1)
    mask_d = (blk(ti, sub) == blk(si, sub)) & (si <= ti)
    masks = [((blk(ti, m) & 1) == 1) & (blk(si, m) == blk(ti, m) - 1) for (m, _, _) in levels]
    sel8 = (lax.broadcasted_iota(I32, (8, DV_GLA), 0) == 0).astype(BF16)
    nw = nw_ref[...]

    outs = [[None] * nseq for _ in range(H_GLA)]
    for b in range(nseq):
        r0 = b * chunk
        rows = slice(r0, r0 + chunk)
        for h in range(H_GLA):
            kc = slice(h * DK_GLA, (h + 1) * DK_GLA)
            vc = slice(h * DV_GLA, (h + 1) * DV_GLA)
            q = q_ref[rows, kc] * (DK_GLA ** -0.5)
            k = k_ref[rows, kc]
            v = v_ref[rows, vc].astype(BF16)
            att = jnp.where(mask_d, _dot_nt((q * eq_d[rows, kc]).astype(BF16),
                                            (k * ek_d[rows, kc]).astype(BF16)), 0.0)
            for (m, eq_m, ek_m), mask in zip(levels, masks):
                att = jnp.where(mask, _dot_nt((q * eq_m[rows, kc]).astype(BF16),
                                              (k * ek_m[rows, kc]).astype(BF16)), att)
            s_old = s_scr[b, h]
            o = _dot(att.astype(BF16), v) + _dot((q * eq_c[rows, kc]).astype(BF16), s_old.astype(BF16))
            kv = _dot_tn((k * ek_c[rows, kc]).astype(BF16), v)
            tot = jnp.broadcast_to(p[r0 + chunk - 1:r0 + chunk, kc], (8, DK_GLA))
            t_hi, t_mid, t_lo = _split_bf16(tot, 3)
            decay = jnp.exp(_dot_tn(t_hi, sel8) + (_dot_tn(t_mid, sel8) + _dot_tn(t_lo, sel8)))
            s_scr[b, h] = decay * s_old + kv
            outs[h][b] = _rms(o) * nw * _silu(g_ref[rows, vc])
    for h in range(H_GLA):
        o_ref[:, h * DV_GLA:(h + 1) * DV_GLA] = jnp.concatenate(outs[h], axis=0).astype(BF16)

    @pl.when(t == pl.num_programs(1) - 1)
    def _():
        sout_ref[...] = s_scr[...]


def _gla(proj, w_gk2p, b_gk, nw, s0, *, row0, n_seq, seq_len, chunk, nseq):
    rows = nseq * chunk
    nt = seq_len // chunk
    base = row0 // rows

    def rmap(col):
        return lambda b, t: (base + b * nt + t, col)

    in_specs = [pl.BlockSpec((rows, GLA_QK), rmap(C_QA // GLA_QK)),
                pl.BlockSpec((rows, GLA_QK), rmap(C_KA // GLA_QK)),
                pl.BlockSpec((rows, GLA_V), rmap(C_VA // GLA_V)),
                pl.BlockSpec((rows, GLA_V), rmap(C_GA // GLA_V)),
                pl.BlockSpec((rows, LANES), rmap(C_LR // LANES)),
                pl.BlockSpec((2, LANES, GLA_QK), lambda b, t: (0, 0, 0)),
                pl.BlockSpec((1, GLA_QK), lambda b, t: (0, 0)),
                pl.BlockSpec((1, DV_GLA), lambda b, t: (0, 0))]
    args = [proj, proj, proj, proj, proj, w_gk2p, b_gk, nw]
    state_spec = pl.BlockSpec((nseq, H_GLA, DK_GLA, DV_GLA), lambda b, t: (b, 0, 0, 0))
    if s0 is not None:
        in_specs.append(state_spec)
        args.append(s0)
    return pl.pallas_call(
        functools.partial(_gla_body, chunk=chunk, nseq=nseq, has_init=s0 is not None),
        out_shape=(jax.ShapeDtypeStruct((n_seq * seq_len, GLA_V), BF16),
                   jax.ShapeDtypeStruct((n_seq, H_GLA, DK_GLA, DV_GLA), F32)),
        grid=(n_seq // nseq, nt),
        in_specs=in_specs,
        out_specs=(pl.BlockSpec((rows, GLA_V), lambda b, t: (b * nt + t, 0)), state_spec),
        scratch_shapes=[pltpu.VMEM((nseq, H_GLA, DK_GLA, DV_GLA), F32)],
        compiler_params=pltpu.CompilerParams(dimension_semantics=("arbitrary", "arbitrary"),
                                             vmem_limit_bytes=V7X_SCOPED_VMEM_BYTES),
        name="gla_init" if s0 is not None else "gla",
    )(*args)


def _ret_body(*refs, chunk, nseq, has_init):
    if has_init:
        (q_ref, k_ref, v_ref, g_ref, cos_ref, sin_ref, s0_ref, o_ref, sout_ref, s_scr) = refs
    else:
        (q_ref, k_ref, v_ref, g_ref, cos_ref, sin_ref, o_ref, sout_ref, s_scr) = refs
    t = pl.program_id(1)

    @pl.when(t == 0)
    def _():
        if has_init:
            s_scr[...] = s0_ref[...]
        else:
            s_scr[...] = jnp.zeros_like(s_scr)

    half = DK_RET // 2
    ti = lax.broadcasted_iota(I32, (chunk, chunk), 0)
    si = lax.broadcasted_iota(I32, (chunk, chunk), 1)
    diff = (ti - si).astype(F32)
    idx = lax.broadcasted_iota(I32, (chunk, 1), 0).astype(F32)

    outs = [[None] * nseq for _ in range(H_RET)]
    for h in range(H_RET):
        lg = math.log(1.0 - 2.0 ** (-5.0 - h))
        dmat = jnp.where(diff >= 0, jnp.exp(lg * jnp.maximum(diff, 0.0)), 0.0)
        q_dec = jnp.exp(lg * (idx + 1.0))
        k_dec = jnp.exp(lg * (chunk - 1.0 - idx))
        c_dec = math.exp(lg * chunk)
        for b in range(nseq):
            rows = slice(b * chunk, (b + 1) * chunk)
            cos = cos_ref[rows, :]
            sin = sin_ref[rows, :]
            c1 = slice(h * DK_RET, h * DK_RET + half)
            c2 = slice(h * DK_RET + half, (h + 1) * DK_RET)
            vc = slice(h * DV_RET, (h + 1) * DV_RET)
            q1, q2 = q_ref[rows, c1], q_ref[rows, c2]
            k1, k2 = k_ref[rows, c1], k_ref[rows, c2]
            q = jnp.concatenate([q1 * cos - q2 * sin, q1 * sin + q2 * cos], axis=-1)
            k = jnp.concatenate([k1 * cos - k2 * sin, k1 * sin + k2 * cos], axis=-1) * (DK_RET ** -0.5)
            v = v_ref[rows, vc].astype(BF16)
            qb = q.astype(BF16)
            s_old = s_scr[b, h]
            att = _dot_nt(qb, k.astype(BF16)) * dmat
            o = _dot(att.astype(BF16), v) + _dot(qb, s_old.astype(BF16)) * q_dec
            s_scr[b, h] = c_dec * s_old + _dot_tn((k * k_dec).astype(BF16), v)
            outs[h][b] = _rms(o) * _silu(g_ref[rows, vc])
    for h in range(H_RET):
        o_ref[:, h * DV_RET:(h + 1) * DV_RET] = jnp.concatenate(outs[h], axis=0).astype(BF16)

    @pl.when(t == pl.num_programs(1) - 1)
    def _():
        sout_ref[...] = s_scr[...]


def _ret(proj, cos, sin, s0, *, row0, n_seq, seq_len, chunk, nseq):
    rows = nseq * chunk
    nt = seq_len // chunk
    base = row0 // rows

    def rmap(col):
        return lambda b, t: (base + b * nt + t, col)

    tab_spec = pl.BlockSpec((rows, DK_RET // 2), lambda b, t: (t, 0))
    in_specs = [pl.BlockSpec((rows, RET_QK), rmap(C_QR // RET_QK)),
                pl.BlockSpec((rows, RET_QK), rmap(C_KR // RET_QK)),
                pl.BlockSpec((rows, RET_V), rmap(C_VR // RET_V)),
                pl.BlockSpec((rows, RET_V), rmap(C_GR // RET_V)),
                tab_spec, tab_spec]
    args = [proj, proj, proj, proj, cos, sin]
    state_spec = pl.BlockSpec((nseq, H_RET, DK_RET, DV_RET), lambda b, t: (b, 0, 0, 0))
    if s0 is not None:
        in_specs.append(state_spec)
        args.append(s0)
    return pl.pallas_call(
        functools.partial(_ret_body, chunk=chunk, nseq=nseq, has_init=s0 is not None),
        out_shape=(jax.ShapeDtypeStruct((n_seq * seq_len, RET_V), BF16),
                   jax.ShapeDtypeStruct((n_seq, H_RET, DK_RET, DV_RET), F32)),
        grid=(n_seq // nseq, nt),
        in_specs=in_specs,
        out_specs=(pl.BlockSpec((rows, RET_V), lambda b, t: (b * nt + t, 0)), state_spec),
        scratch_shapes=[pltpu.VMEM((nseq, H_RET, DK_RET, DV_RET), F32)],
        compiler_params=pltpu.CompilerParams(dimension_semantics=("arbitrary", "arbitrary"),
                                             vmem_limit_bytes=V7X_SCOPED_VMEM_BYTES),
        name="ret_init" if s0 is not None else "ret",
    )(*args)


def _rope_tables(pos, reps):
    half = DK_RET // 2
    inv = ROPE_BASE ** (-jnp.arange(half, dtype=F32) / half)
    ang = pos[:, None] * inv[None, :]
    return jnp.tile(jnp.cos(ang), (reps, 1)), jnp.tile(jnp.sin(ang), (reps, 1))


def _outproj_body(oap_ref, oas_ref, orp_ref, ors_ref, xp_ref, xs_ref, w_ref, ln_ref, wrt_ref, brt_ref,
                  x1_ref, hn_ref, route_ref, counts_ref, *, n_prompt_blocks):
    step = pl.program_id(0)
    is_prompt = step < n_prompt_blocks
    oa = jnp.where(is_prompt, oap_ref[...], oas_ref[...])
    orr = jnp.where(is_prompt, orp_ref[...], ors_ref[...])
    x = jnp.where(is_prompt, xp_ref[...], xs_ref[...])
    x1 = x + _dot(oa, w_ref[:GLA_V, :]) + _dot(orr, w_ref[GLA_V:, :])
    x1_ref[...] = x1
    hn = _rms(x1) * ln_ref[...]
    hn_ref[...] = hn

    logits = _dot_split(hn, wrt_ref[0], wrt_ref[1]) + brt_ref[...]
    lane_i = lax.broadcasted_iota(I32, logits.shape, 1)
    lane = lane_i.astype(F32)
    grp = (lane_i >> (EXPERTS_PER_GROUP.bit_length() - 1)).astype(F32)
    neg = -jnp.inf
    far = float(LANES)

    gl = jnp.where((lane_i >= N_EXPERTS) & (lane_i < N_EXPERTS + N_GROUPS), logits, neg)
    gmax = jnp.max(gl, axis=-1, keepdims=True)
    g_w = 1.0 / jnp.sum(jnp.exp(gl - gmax), axis=-1, keepdims=True)
    g_idx = jnp.min(jnp.where(gl == gmax, lane, far), axis=-1, keepdims=True) - float(N_EXPERTS)

    el = jnp.where((lane_i < N_EXPERTS) & (grp == g_idx), logits, neg)
    m1 = jnp.max(el, axis=-1, keepdims=True)
    esum = jnp.sum(jnp.exp(el - m1), axis=-1, keepdims=True)
    i1 = jnp.min(jnp.where(el == m1, lane, far), axis=-1, keepdims=True)
    el2 = jnp.where(lane == i1, neg, el)
    m2 = jnp.max(el2, axis=-1, keepdims=True)
    i2 = jnp.min(jnp.where(el2 == m2, lane, far), axis=-1, keepdims=True)
    p1 = 1.0 / esum
    p2 = jnp.exp(m2 - m1) / esum
    gate1 = g_w * (p1 / (p1 + p2))
    gate2 = g_w * (p2 / (p1 + p2))
    @pl.when(step == 0)
    def _():
        counts_ref[...] = jnp.zeros_like(counts_ref)

    tm = logits.shape[0]
    earlier = (lax.broadcasted_iota(I32, (tm, tm), 0) > lax.broadcasted_iota(I32, (tm, tm), 1)).astype(BF16)
    running = counts_ref[0:1, :]
    ranks = []
    for idx in (i1, i2):
        onehot = (lane == idx).astype(F32)
        before = _dot(earlier, onehot.astype(BF16)) + running
        ranks.append(jnp.sum(onehot * before, axis=-1, keepdims=True))
        running = running + jnp.sum(onehot, axis=0, keepdims=True)
    counts_ref[...] = jnp.broadcast_to(running, counts_ref.shape)

    route_ref[...] = jnp.where(lane_i == 0, i1,
                               jnp.where(lane_i == 1, i2,
                                         jnp.where(lane_i == 2, gate1,
                                                   jnp.where(lane_i == 3, gate2,
                                                             jnp.where(lane_i == 4, ranks[0],
                                                                       jnp.where(lane_i == 5, ranks[1], 0.0))))))


def _outproj_router(oa_p, oa_s, or_p, or_s, xp, xs, w_out, ln, w_rt, b_rt):
    n = xp.shape[0] + xs.shape[0]
    nbp = xp.shape[0] // PROJ_TM
    pmap = lambda i: (jnp.minimum(i, nbp - 1), 0)
    smap = lambda i: (jnp.maximum(i - nbp, 0), 0)
    row = lambda i: (i, 0)
    fixed = lambda i: (0, 0)
    return pl.pallas_call(
        functools.partial(_outproj_body, n_prompt_blocks=nbp),
        out_shape=(jax.ShapeDtypeStruct((n, D_MODEL), F32),
                   jax.ShapeDtypeStruct((n, D_MODEL), F32),
                   jax.ShapeDtypeStruct((n, LANES), F32),
                   jax.ShapeDtypeStruct((SUBLANES, LANES), F32)),
        grid=(n // PROJ_TM,),
        in_specs=[pl.BlockSpec((PROJ_TM, GLA_V), pmap), pl.BlockSpec((PROJ_TM, GLA_V), smap),
                  pl.BlockSpec((PROJ_TM, RET_V), pmap), pl.BlockSpec((PROJ_TM, RET_V), smap),
                  pl.BlockSpec((PROJ_TM, D_MODEL), pmap), pl.BlockSpec((PROJ_TM, D_MODEL), smap),
                  pl.BlockSpec((GLA_V + RET_V, D_MODEL), fixed, pipeline_mode=pl.Buffered(1)),
                  pl.BlockSpec((1, D_MODEL), fixed),
                  pl.BlockSpec((2, D_MODEL, LANES), lambda i: (0, 0, 0)),
                  pl.BlockSpec((1, LANES), fixed)],
        out_specs=(pl.BlockSpec((PROJ_TM, D_MODEL), row), pl.BlockSpec((PROJ_TM, D_MODEL), row),
                   pl.BlockSpec((PROJ_TM, LANES), row), pl.BlockSpec((SUBLANES, LANES), fixed)),
        compiler_params=pltpu.CompilerParams(dimension_semantics=("arbitrary",),
                                             vmem_limit_bytes=V7X_SCOPED_VMEM_BYTES),
        name="outproj_router",
    )(oa_p, oa_s, or_p, or_s, xp, xs, w_out, ln, w_rt, b_rt)


def _row_dma_wait(buf, sem, s):
    pltpu.make_async_copy(buf.at[s], buf.at[s], sem.at[s]).wait()


def _moe_body(texp_ref, nval_ref, pos_ref, hn_hbm, wg_ref, wu_ref, wd_ref, o_ref,
              inv, xbuf, hb, au, hidb, wg_b, wu_b, wd_b, gsem):
    tm = MOE_TM
    i = pl.program_id(0)
    nt = pl.num_programs(0)
    nv = nval_ref[i]
    nv_prev = jnp.where(i >= 1, nval_ref[jnp.maximum(i - 1, 0)], 0)
    tile_next = jnp.minimum(i + 1, nt - 1)

    def gather_rows(tile, lo, hi, s):
        rows = range(lo, hi)
        ds = [inv[tile * tm + r] for r in rows]
        for d, r in zip(ds, rows):
            pltpu.make_async_copy(hn_hbm.at[pl.ds(d >> 1, 1), :],
                                  xbuf.at[s, r // SUBLANES, pl.ds(r % SUBLANES, 1), :], gsem.at[s]).start()

    def tile_compute(s):
        cw, kw = MOE_COL_CHUNK, MOE_K_CHUNK
        n_up = 2 * D_EXPERT // cw
        n_down = D_MODEL // cw
        n_slots = n_up * (D_MODEL // kw) + n_down * (D_EXPERT // kw)
        per = -(-tm // n_slots)
        slot_rows = iter([(min(k * per, tm), min((k + 1) * per, tm)) for k in range(n_slots)])

        def chunk_dot(lhs_ref, depth, w_ref, c0):
            acc = None
            for k0 in range(0, depth, kw):
                gather_rows(tile_next, *next(slot_rows), 1 - s)
                part = _dot(lhs_ref[:, k0:k0 + kw], w_ref[k0:k0 + kw, c0:c0 + cw])
                acc = part if acc is None else acc + part
            return acc

        hb[...] = xbuf[s].reshape(tm, D_MODEL).astype(BF16)
        for k in range(n_up):
            w_ref, c0 = (wg_b, k * cw) if k * cw < D_EXPERT else (wu_b, k * cw - D_EXPERT)
            au[:, k * cw:(k + 1) * cw] = chunk_dot(hb, D_MODEL, w_ref, c0)
        hidb[...] = (_silu(au[:, :D_EXPERT]) * au[:, D_EXPERT:]).astype(BF16)
        for k in range(n_down):
            o_ref[:, k * cw:(k + 1) * cw] = chunk_dot(hidb, D_EXPERT, wd_b, k * cw)

    @pl.when(i == 0)
    def _():
        def clear(p, c):
            inv[p] = 0
            return c
        lax.fori_loop(0, nt * tm, clear, 0, unroll=8)

        def body(a, c):
            inv[pos_ref[a]] = a
            return c
        lax.fori_loop(0, pos_ref.shape[0], body, 0, unroll=8)

        def group(g, c):
            ds = [inv[g * SUBLANES + j] for j in range(SUBLANES)]
            for j in range(SUBLANES):
                pltpu.make_async_copy(hn_hbm.at[pl.ds(ds[j] >> 1, 1), :], xbuf.at[0, g, pl.ds(j, 1), :],
                                      gsem.at[0]).start()
            return c
        lax.fori_loop(0, tm // SUBLANES, group, 0)

    def step(s):
        @pl.when(nv > 0)
        def _():
            _row_dma_wait(xbuf, gsem, s)

            @pl.when((i == 0) | (texp_ref[i] != texp_ref[jnp.maximum(i - 1, 0)]))
            def _():
                wg_b[...] = wg_ref[0].astype(BF16)
                wu_b[...] = wu_ref[0].astype(BF16)
                wd_b[...] = wd_ref[0].astype(BF16)

            tile_compute(s)

            @pl.when(i == nt - 1)
            def _():
                _row_dma_wait(xbuf, gsem, 1 - s)

        @pl.when(nv == 0)
        def _():
            o_ref[...] = jnp.zeros_like(o_ref)

            @pl.when(nv_prev > 0)
            def _():
                _row_dma_wait(xbuf, gsem, s)

    for s in range(2):
        pl.when(i % 2 == s)(functools.partial(step, s))


def _moe(tile_expert, n_valid, pos, hn, w_gate, w_up, w_down):
    n_tiles = tile_expert.shape[0]
    wmap = lambda i, te, nv, pos_: (te[i], 0, 0)
    grouped = (MOE_TM // SUBLANES, SUBLANES, D_MODEL)
    return pl.pallas_call(
        _moe_body,
        out_shape=jax.ShapeDtypeStruct((n_tiles * MOE_TM, D_MODEL), F32),
        grid_spec=pltpu.PrefetchScalarGridSpec(
            num_scalar_prefetch=3,
            grid=(n_tiles,),
            in_specs=[pl.BlockSpec(memory_space=pl.ANY),
                      pl.BlockSpec((1, D_MODEL, D_EXPERT), wmap),
                      pl.BlockSpec((1, D_MODEL, D_EXPERT), wmap),
                      pl.BlockSpec((1, D_EXPERT, D_MODEL), wmap)],
            out_specs=pl.BlockSpec((MOE_TM, D_MODEL), lambda i, te, nv, pos_: (i, 0)),
            scratch_shapes=[pltpu.SMEM((n_tiles * MOE_TM,), I32),
                            pltpu.VMEM((2,) + grouped, F32),
                            pltpu.VMEM((MOE_TM, D_MODEL), BF16),
                            pltpu.VMEM((MOE_TM, 2 * D_EXPERT), F32),
                            pltpu.VMEM((MOE_TM, D_EXPERT), BF16),
                            pltpu.VMEM((D_MODEL, D_EXPERT), BF16),
                            pltpu.VMEM((D_MODEL, D_EXPERT), BF16),
                            pltpu.VMEM((D_EXPERT, D_MODEL), BF16),
                            pltpu.SemaphoreType.DMA((2,))]),
        compiler_params=pltpu.CompilerParams(dimension_semantics=("arbitrary",),
                                             vmem_limit_bytes=V7X_SCOPED_VMEM_BYTES),
        name="moe",
    )(tile_expert, n_valid, pos, hn, w_gate, w_up, w_down)


def _routing_tables(route, counts, n):
    tm = MOE_TM
    n_tiles = 2 * n // tm + N_EXPERTS
    experts = route[:, 0:2].astype(I32)
    rank = route[:, 4:6].astype(I32)
    counts = counts[0, :N_EXPERTS].astype(I32)
    tiles = (counts + tm - 1) // tm
    tile_end = jnp.cumsum(tiles)
    tile_start = tile_end - tiles
    onehot = experts[:, :, None] == jnp.arange(N_EXPERTS, dtype=I32)[None, None, :]
    pos = jnp.sum(jnp.where(onehot, (tile_start * tm)[None, None, :], 0), axis=-1) + rank
    tidx = jnp.arange(n_tiles, dtype=I32)
    t_exp = jnp.sum((tidx[:, None] >= tile_end[None, :]).astype(I32), axis=1)
    used = tidx < tile_end[-1]
    last_exp = jnp.sum((tile_end[-1] - 1 >= tile_end).astype(I32))
    t_exp = jnp.where(used, t_exp, last_exp)
    n_valid = jnp.where(used, jnp.clip(counts[t_exp] - (tidx - tile_start[t_exp]) * tm, 0, tm), 0)
    return t_exp.astype(I32), n_valid.astype(I32), pos.reshape(2 * n)


def _final_body(pos_ref, x1_ref, route_ref, ln_ref, y_hbm, o_ref, ybuf, sem, *, row0):
    tm = PROJ_TM
    i = pl.program_id(0)
    n_steps = pl.num_programs(0)

    def fetch(step, b):
        a0 = 2 * (row0 + step * tm)
        for g in range(tm // SUBLANES):
            ps = [pos_ref[a0 + 2 * (g * SUBLANES + j) + k] for j in range(SUBLANES) for k in range(2)]
            for j in range(SUBLANES):
                for k in range(2):
                    pltpu.make_async_copy(y_hbm.at[pl.ds(ps[2 * j + k], 1), :],
                                          ybuf.at[b, k * (tm // SUBLANES) + g, pl.ds(j, 1), :], sem.at[b]).start()

    @pl.when(i == 0)
    def _():
        fetch(0, 0)

    def step(b):
        @pl.when(i + 1 < n_steps)
        def _():
            fetch(i + 1, 1 - b)
        _row_dma_wait(ybuf, sem, b)
        y = ybuf[b].reshape(2, tm, D_MODEL)
        z = x1_ref[...] + route_ref[:, 2:3] * y[0] + route_ref[:, 3:4] * y[1]
        o_ref[...] = _rms(z) * ln_ref[...]

    for b in range(2):
        pl.when(i % 2 == b)(functools.partial(step, b))


def _final(pos, x1, route, ln, y_sorted, *, row0, rows):
    base = row0 // PROJ_TM
    return pl.pallas_call(
        functools.partial(_final_body, row0=row0),
        out_shape=jax.ShapeDtypeStruct((rows, D_MODEL), F32),
        grid_spec=pltpu.PrefetchScalarGridSpec(
            num_scalar_prefetch=1,
            grid=(rows // PROJ_TM,),
            in_specs=[pl.BlockSpec((PROJ_TM, D_MODEL), lambda i, pos_: (base + i, 0)),
                      pl.BlockSpec((PROJ_TM, LANES), lambda i, pos_: (base + i, 0)),
                      pl.BlockSpec((1, D_MODEL), lambda i, pos_: (0, 0)),
                      pl.BlockSpec(memory_space=pl.ANY)],
            out_specs=pl.BlockSpec((PROJ_TM, D_MODEL), lambda i, pos_: (i, 0)),
            scratch_shapes=[pltpu.VMEM((2, 2 * PROJ_TM // SUBLANES, SUBLANES, D_MODEL), F32),
                            pltpu.SemaphoreType.DMA((2,))]),
        compiler_params=pltpu.CompilerParams(dimension_semantics=("arbitrary",),
                                             vmem_limit_bytes=V7X_SCOPED_VMEM_BYTES),
        name="final_norm",
    )(pos, x1, route, ln, y_sorted)


def kernel(x_prompt, x_sample, state_gla, state_ret, ln_attn, w_in, w_gk2, b_gk, gla_norm_w, w_out, ln_ffn, w_router_group, b_router_group, w_router_expert, b_router_expert, w_exp_gate, w_exp_up, w_exp_down, ln_final):
    bp, tp, d = x_prompt.shape
    bs, ts, _ = x_sample.shape
    assert d == D_MODEL and w_in.shape == (1, D_MODEL, IN_COLS_SRC)
    n_p, n_s = bp * tp, bs * ts
    n = n_p + n_s
    assert n_p % PROJ_TM == 0 and n_s % PROJ_TM == 0 and tp % GLA_CHUNK == 0 and tp % RET_CHUNK == 0

    xp = x_prompt.reshape(n_p, d)
    xs = x_sample.reshape(n_s, d)

    hi_lo = lambda a: jnp.stack(_split_bf16(a, 2))
    w_in_p, w_lr = _wprep(w_in[0].T)
    w_gk2p = hi_lo(jnp.concatenate([w_gk2[0], jnp.zeros((LANES - GLA_LOWRANK, GLA_QK), F32)], axis=0))
    w_rt = hi_lo(jnp.concatenate([w_router_expert[0], w_router_group[0],
                                  jnp.zeros((d, LANES - N_EXPERTS - N_GROUPS), F32)], axis=1))
    b_rt = jnp.concatenate([b_router_expert[0], b_router_group[0],
                            jnp.zeros((LANES - N_EXPERTS - N_GROUPS,), F32)])[None, :]

    proj = _inproj(xp, xs, ln_attn, w_in_p, w_lr)

    nw = gla_norm_w
    bgk = b_gk
    oa_p, sg_p = _gla(proj, w_gk2p, bgk, nw, None, row0=0, n_seq=bp, seq_len=tp, chunk=GLA_CHUNK, nseq=1)
    oa_s, sg_s = _gla(proj, w_gk2p, bgk, nw, state_gla[0], row0=n_p, n_seq=bs, seq_len=ts, chunk=ts, nseq=8)

    cos_p, sin_p = _rope_tables(jnp.arange(tp, dtype=F32), 1)
    ret_nseq_s = 4
    cos_s, sin_s = _rope_tables(jnp.arange(ts, dtype=F32) + float(PAST_LEN), ret_nseq_s)
    or_p, sr_p = _ret(proj, cos_p, sin_p, None, row0=0, n_seq=bp, seq_len=tp, chunk=RET_CHUNK, nseq=1)
    or_s, sr_s = _ret(proj, cos_s, sin_s, state_ret[0], row0=n_p, n_seq=bs, seq_len=ts, chunk=ts, nseq=ret_nseq_s)

    x1, hn, route, counts = _outproj_router(oa_p, oa_s, or_p, or_s, xp, xs, w_out[0].astype(BF16),
                                            ln_ffn, w_rt, b_rt)

    t_exp, n_valid, pos = _routing_tables(route, counts, n)
    y_sorted = _moe(t_exp, n_valid, pos, hn, w_exp_gate[0], w_exp_up[0], w_exp_down[0])

    ln_f = ln_final[None, :]
    y_p = _final(pos, x1, route, ln_f, y_sorted, row0=0, rows=n_p).reshape(bp, tp, d)
    y_s = _final(pos, x1, route, ln_f, y_sorted, row0=n_p, rows=n_s).reshape(bs, ts, d)
    return (y_p, y_s, sg_p[None], sr_p[None], sg_s[None], sr_s[None])
```

```python
import functools
import math

import jax
import jax.numpy as jnp
from jax import lax
from jax.experimental import pallas as pl
from jax.experimental.pallas import tpu as pltpu

F32 = jnp.float32
BF16 = jnp.bfloat16
I32 = jnp.int32

D_MODEL = 2048
PAST_LEN = 16384
H_GLA, DK_GLA, DV_GLA = 4, 128, 256
GLA_LOWRANK = 16
GLA_GATE_NORM = 16.0
GLA_SUBCHUNK = 16
H_RET, DK_RET, DV_RET = 4, 256, 256
ROPE_BASE = 10000.0
N_GROUPS, EXPERTS_PER_GROUP, D_EXPERT = 4, 8, 512
N_EXPERTS = N_GROUPS * EXPERTS_PER_GROUP
EPS = 1e-6

LANES = 128
SUBLANES = 8
V7X_SCOPED_VMEM_BYTES = 56 * 1024 * 1024

GLA_QK = H_GLA * DK_GLA
GLA_V = H_GLA * DV_GLA
RET_QK = H_RET * DK_RET
RET_V = H_RET * DV_RET
C_QA, C_KA, C_VA, C_GA = 0, GLA_QK, 2 * GLA_QK, 2 * GLA_QK + GLA_V
C_QR = C_GA + GLA_V
C_KR, C_VR, C_GR = C_QR + RET_QK, C_QR + 2 * RET_QK, C_QR + 2 * RET_QK + RET_V
C_LR = C_GR + RET_V
P_COLS = C_LR + LANES
IN_COLS_SRC = 2 * GLA_QK + GLA_V + GLA_LOWRANK + GLA_V + 2 * RET_QK + 2 * RET_V

PROJ_TM = 256
W_CHUNK = 1024
MOE_TM = 256
MOE_COL_CHUNK = 256
MOE_K_CHUNK = 256
MOE_DMA_BURSTS = 4
GLA_CHUNK = 128
RET_CHUNK = 128

def _dot(a, b):
    return jnp.dot(a, b, preferred_element_type=F32)


def _dot_nt(a, b):
    return lax.dot_general(a, b, (((1,), (1,)), ((), ())), preferred_element_type=F32)


def _dot_tn(a, b):
    return lax.dot_general(a, b, (((0,), (0,)), ((), ())), preferred_element_type=F32)


def _split_bf16(x, parts):
    out = []
    for _ in range(parts):
        p = x.astype(BF16)
        out.append(p)
        x = x - p.astype(F32)
    return out


def _dot_split(a, b_hi, b_lo):
    a_hi, a_lo = _split_bf16(a, 2)
    return _dot(a_hi, b_hi) + (_dot(a_lo, b_hi) + _dot(a_hi, b_lo))


def _rms(x):
    return x * lax.rsqrt(jnp.mean(x * x, axis=-1, keepdims=True) + EPS)


def _silu(x):
    return x * jax.nn.sigmoid(x)


def _wprep_body(a_ref, b_ref, o_ref, olr_ref):
    j = pl.program_id(0)
    a = a_ref[...]
    b = b_ref[...]
    shifted = jnp.concatenate([a[GLA_LOWRANK:, :], b], axis=0)
    o_ref[...] = jnp.where(j >= C_GA // W_CHUNK, shifted, a).T.astype(BF16)

    @pl.when(j == C_GA // W_CHUNK - 1)
    def _():
        padded = jnp.concatenate([b, jnp.zeros((LANES - GLA_LOWRANK, D_MODEL), F32)], axis=0)
        olr_ref[...] = padded.T.astype(BF16)


def _wprep(wt):
    per = W_CHUNK // GLA_LOWRANK
    return pl.pallas_call(
        _wprep_body,
        out_shape=(jax.ShapeDtypeStruct((D_MODEL, C_LR), BF16), jax.ShapeDtypeStruct((D_MODEL, LANES), BF16)),
        grid=(C_LR // W_CHUNK,),
        in_specs=[pl.BlockSpec((W_CHUNK, D_MODEL), lambda j: (j, 0)),
                  pl.BlockSpec((GLA_LOWRANK, D_MODEL), lambda j: (per * (j + 1), 0))],
        out_specs=(pl.BlockSpec((D_MODEL, W_CHUNK), lambda j: (0, j)),
                   pl.BlockSpec((D_MODEL, LANES), lambda j: (0, 0))),
        compiler_params=pltpu.CompilerParams(dimension_semantics=("arbitrary",),
                                             vmem_limit_bytes=V7X_SCOPED_VMEM_BYTES),
        name="wprep",
    )(wt, wt)


def _inproj_body(xp_ref, xs_ref, ln_ref, w_ref, wlr_ref, o_ref, *, n_prompt_blocks):
    x = jnp.where(pl.program_id(0) < n_prompt_blocks, xp_ref[...], xs_ref[...])
    h = (_rms(x) * ln_ref[...]).astype(BF16)
    for c in range(C_LR // W_CHUNK):
        o_ref[:, c * W_CHUNK:(c + 1) * W_CHUNK] = _dot(h, w_ref[:, c * W_CHUNK:(c + 1) * W_CHUNK])
    o_ref[:, C_LR:] = _dot(h, wlr_ref[...])


def _inproj(xp, xs, ln, w, w_lr):
    nbp = xp.shape[0] // PROJ_TM
    n = xp.shape[0] + xs.shape[0]
    fixed = lambda i: (0, 0)
    return pl.pallas_call(
        functools.partial(_inproj_body, n_prompt_blocks=nbp),
        out_shape=jax.ShapeDtypeStruct((n, P_COLS), F32),
        grid=(n // PROJ_TM,),
        in_specs=[pl.BlockSpec((PROJ_TM, D_MODEL), lambda i: (jnp.minimum(i, nbp - 1), 0)),
                  pl.BlockSpec((PROJ_TM, D_MODEL), lambda i: (jnp.maximum(i - nbp, 0), 0)),
                  pl.BlockSpec((1, D_MODEL), fixed),
                  pl.BlockSpec((D_MODEL, C_LR), fixed, pipeline_mode=pl.Buffered(1)),
                  pl.BlockSpec((D_MODEL, LANES), fixed)],
        out_specs=pl.BlockSpec((PROJ_TM, P_COLS), lambda i: (i, 0)),
        compiler_params=pltpu.CompilerParams(dimension_semantics=("arbitrary",),
                                             vmem_limit_bytes=V7X_SCOPED_VMEM_BYTES),
        name="inproj",
    )(xp, xs, ln, w, w_lr)


def _seg_cumsum(x, seg):
    pos = lax.broadcasted_iota(I32, x.shape, 0) & (seg - 1)
    s = 1
    while s < seg:
        x = x + jnp.where(pos >= s, pltpu.roll(x, s, axis=0), 0.0)
        s *= 2
    return x


def _block_ends(p, m):
    rows, w = p.shape
    return jnp.concatenate(
        [jnp.broadcast_to(p[j * m + m - 1:j * m + m, :], (m, w)) for j in range(rows // m)], axis=0)


def _block_starts(p, m, chunk):
    rows, w = p.shape
    pieces = []
    for j in range(rows // m):
        if (j * m) % chunk == 0:
            pieces.append(jnp.zeros((m, w), F32))
        else:
            pieces.append(jnp.broadcast_to(p[j * m - 1:j * m, :], (m, w)))
    return jnp.concatenate(pieces, axis=0)


def _gla_body(*refs, chunk, nseq, has_init):
    if has_init:
        (q_ref, k_ref, v_ref, g_ref, lr_ref, wgk_ref, bgk_ref, nw_ref, s0_ref,
         o_ref, sout_ref, s_scr) = refs
    else:
        (q_ref, k_ref, v_ref, g_ref, lr_ref, wgk_ref, bgk_ref, nw_ref,
         o_ref, sout_ref, s_scr) = refs
    t = pl.program_id(1)

    @pl.when(t == 0)
    def _():
        if has_init:
            s_scr[...] = s0_ref[...]
        else:
            s_scr[...] = jnp.zeros_like(s_scr)

    x = _dot_split(lr_ref[...], wgk_ref[0], wgk_ref[1]) + bgk_ref[...]
    log_g = -(jnp.maximum(-x, 0.0) + jnp.log1p(jnp.exp(-jnp.abs(x)))) / GLA_GATE_NORM
    p = _seg_cumsum(log_g, chunk)
    p_end = _block_ends(p, chunk)
    eq_c = jnp.exp(p)
    ek_c = jnp.exp(p_end - p)
    sub = min(GLA_SUBCHUNK, chunk)
    p_sub = p - _block_starts(p, sub, chunk)
    eq_d = jnp.exp(p_sub)
    ek_d = jnp.exp(-p_sub)
    levels = []
    m = sub
    while m < chunk:
        levels.append((m, jnp.exp(p - _block_starts(p, m, chunk)), jnp.exp(_block_ends(p, m) - p)))
        m *= 2

    ti = lax.broadcasted_iota(I32, (chunk, chunk), 0)
    si = lax.broadcasted_iota(I32, (chunk, chunk), 1)
    blk = lambda idx, size: idx >> (size.bit_length() - 1)
    mask_d = (blk(ti, sub) == blk(si, sub)) & (si <= ti)
    masks = [((blk(ti, m) & 1) == 1) & (blk(si, m) == blk(ti, m) - 1) for (m, _, _) in levels]
    sel8 = (lax.broadcasted_iota(I32, (8, DV_GLA), 0) == 0).astype(BF16)
    nw = nw_ref[...]

    outs = [[None] * nseq for _ in range(H_GLA)]
    for b in range(nseq):
        r0 = b * chunk
        rows = slice(r0, r0 + chunk)
        for h in range(H_GLA):
            kc = slice(h * DK_GLA, (h + 1) * DK_GLA)
            vc = slice(h * DV_GLA, (h + 1) * DV_GLA)
            q = q_ref[rows, kc] * (DK_GLA ** -0.5)
            k = k_ref[rows, kc]
            v = v_ref[rows, vc].astype(BF16)
            att = jnp.where(mask_d, _dot_nt((q * eq_d[rows, kc]).astype(BF16),
                                            (k * ek_d[rows, kc]).astype(BF16)), 0.0)
            for (m, eq_m, ek_m), mask in zip(levels, masks):
                att = jnp.where(mask, _dot_nt((q * eq_m[rows, kc]).astype(BF16),
                                              (k * ek_m[rows, kc]).astype(BF16)), att)
            s_old = s_scr[b, h]
            o = _dot(att.astype(BF16), v) + _dot((q * eq_c[rows, kc]).astype(BF16), s_old.astype(BF16))
            kv = _dot_tn((k * ek_c[rows, kc]).astype(BF16), v)
            tot = jnp.broadcast_to(p[r0 + chunk - 1:r0 + chunk, kc], (8, DK_GLA))
            t_hi, t_mid, t_lo = _split_bf16(tot, 3)
            decay = jnp.exp(_dot_tn(t_hi, sel8) + (_dot_tn(t_mid, sel8) + _dot_tn(t_lo, sel8)))
            s_scr[b, h] = decay * s_old + kv
            outs[h][b] = _rms(o) * nw * _silu(g_ref[rows, vc])
    for h in range(H_GLA):
        o_ref[:, h * DV_GLA:(h + 1) * DV_GLA] = jnp.concatenate(outs[h], axis=0).astype(BF16)

    @pl.when(t == pl.num_programs(1) - 1)
    def _():
        sout_ref[...] = s_scr[...]


def _gla(proj, w_gk2p, b_gk, nw, s0, *, row0, n_seq, seq_len, chunk, nseq):
    rows = nseq * chunk
    nt = seq_len // chunk
    base = row0 // rows

    def rmap(col):
        return lambda b, t: (base + b * nt + t, col)

    in_specs = [pl.BlockSpec((rows, GLA_QK), rmap(C_QA // GLA_QK)),
                pl.BlockSpec((rows, GLA_QK), rmap(C_KA // GLA_QK)),
                pl.BlockSpec((rows, GLA_V), rmap(C_VA // GLA_V)),
                pl.BlockSpec((rows, GLA_V), rmap(C_GA // GLA_V)),
                pl.BlockSpec((rows, LANES), rmap(C_LR // LANES)),
                pl.BlockSpec((2, LANES, GLA_QK), lambda b, t: (0, 0, 0)),
                pl.BlockSpec((1, GLA_QK), lambda b, t: (0, 0)),
                pl.BlockSpec((1, DV_GLA), lambda b, t: (0, 0))]
    args = [proj, proj, proj, proj, proj, w_gk2p, b_gk, nw]
    state_spec = pl.BlockSpec((nseq, H_GLA, DK_GLA, DV_GLA), lambda b, t: (b, 0, 0, 0))
    if s0 is not None:
        in_specs.append(state_spec)
        args.append(s0)
    return pl.pallas_call(
        functools.partial(_gla_body, chunk=chunk, nseq=nseq, has_init=s0 is not None),
        out_shape=(jax.ShapeDtypeStruct((n_seq * seq_len, GLA_V), BF16),
                   jax.ShapeDtypeStruct((n_seq, H_GLA, DK_GLA, DV_GLA), F32)),
        grid=(n_seq // nseq, nt),
        in_specs=in_specs,
        out_specs=(pl.BlockSpec((rows, GLA_V), lambda b, t: (b * nt + t, 0)), state_spec),
        scratch_shapes=[pltpu.VMEM((nseq, H_GLA, DK_GLA, DV_GLA), F32)],
        compiler_params=pltpu.CompilerParams(dimension_semantics=("arbitrary", "arbitrary"),
                                             vmem_limit_bytes=V7X_SCOPED_VMEM_BYTES),
        name="gla_init" if s0 is not None else "gla",
    )(*args)


def _ret_body(*refs, chunk, nseq, has_init):
    if has_init:
        (q_ref, k_ref, v_ref, g_ref, cos_ref, sin_ref, s0_ref, o_ref, sout_ref, s_scr) = refs
    else:
        (q_ref, k_ref, v_ref, g_ref, cos_ref, sin_ref, o_ref, sout_ref, s_scr) = refs
    t = pl.program_id(1)

    @pl.when(t == 0)
    def _():
        if has_init:
            s_scr[...] = s0_ref[...]
        else:
            s_scr[...] = jnp.zeros_like(s_scr)

    half = DK_RET // 2
    ti = lax.broadcasted_iota(I32, (chunk, chunk), 0)
    si = lax.broadcasted_iota(I32, (chunk, chunk), 1)
    diff = (ti - si).astype(F32)
    idx = lax.broadcasted_iota(I32, (chunk, 1), 0).astype(F32)

    outs = [[None] * nseq for _ in range(H_RET)]
    for h in range(H_RET):
        lg = math.log(1.0 - 2.0 ** (-5.0 - h))
        dmat = jnp.where(diff >= 0, jnp.exp(lg * jnp.maximum(diff, 0.0)), 0.0)
        q_dec = jnp.exp(lg * (idx + 1.0))
        k_dec = jnp.exp(lg * (chunk - 1.0 - idx))
        c_dec = math.exp(lg * chunk)
        for b in range(nseq):
            rows = slice(b * chunk, (b + 1) * chunk)
            cos = cos_ref[rows, :]
            sin = sin_ref[rows, :]
            c1 = slice(h * DK_RET, h * DK_RET + half)
            c2 = slice(h * DK_RET + half, (h + 1) * DK_RET)
            vc = slice(h * DV_RET, (h + 1) * DV_RET)
            q1, q2 = q_ref[rows, c1], q_ref[rows, c2]
            k1, k2 = k_ref[rows, c1], k_ref[rows, c2]
            q = jnp.concatenate([q1 * cos - q2 * sin, q1 * sin + q2 * cos], axis=-1)
            k = jnp.concatenate([k1 * cos - k2 * sin, k1 * sin + k2 * cos], axis=-1) * (DK_RET ** -0.5)
            v = v_ref[rows, vc].astype(BF16)
            qb = q.astype(BF16)
            s_old = s_scr[b, h]
            att = _dot_nt(qb, k.astype(BF16)) * dmat
            o = _dot(att.astype(BF16), v) + _dot(qb, s_old.astype(BF16)) * q_dec
            s_scr[b, h] = c_dec * s_old + _dot_tn((k * k_dec).astype(BF16), v)
            outs[h][b] = _rms(o) * _silu(g_ref[rows, vc])
    for h in range(H_RET):
        o_ref[:, h * DV_RET:(h + 1) * DV_RET] = jnp.concatenate(outs[h], axis=0).astype(BF16)

    @pl.when(t == pl.num_programs(1) - 1)
    def _():
        sout_ref[...] = s_scr[...]


def _ret(proj, cos, sin, s0, *, row0, n_seq, seq_len, chunk, nseq):
    rows = nseq * chunk
    nt = seq_len // chunk
    base = row0 // rows

    def rmap(col):
        return lambda b, t: (base + b * nt + t, col)

    tab_spec = pl.BlockSpec((rows, DK_RET // 2), lambda b, t: (t, 0))
    in_specs = [pl.BlockSpec((rows, RET_QK), rmap(C_QR // RET_QK)),
                pl.BlockSpec((rows, RET_QK), rmap(C_KR // RET_QK)),
                pl.BlockSpec((rows, RET_V), rmap(C_VR // RET_V)),
                pl.BlockSpec((rows, RET_V), rmap(C_GR // RET_V)),
                tab_spec, tab_spec]
    args = [proj, proj, proj, proj, cos, sin]
    state_spec = pl.BlockSpec((nseq, H_RET, DK_RET, DV_RET), lambda b, t: (b, 0, 0, 0))
    if s0 is not None:
        in_specs.append(state_spec)
        args.append(s0)
    return pl.pallas_call(
        functools.partial(_ret_body, chunk=chunk, nseq=nseq, has_init=s0 is not None),
        out_shape=(jax.ShapeDtypeStruct((n_seq * seq_len, RET_V), BF16),
                   jax.ShapeDtypeStruct((n_seq, H_RET, DK_RET, DV_RET), F32)),
        grid=(n_seq // nseq, nt),
        in_specs=in_specs,
        out_specs=(pl.BlockSpec((rows, RET_V), lambda b, t: (b * nt + t, 0)), state_spec),
        scratch_shapes=[pltpu.VMEM((nseq, H_RET, DK_RET, DV_RET), F32)],
        compiler_params=pltpu.CompilerParams(dimension_semantics=("arbitrary", "arbitrary"),
                                             vmem_limit_bytes=V7X_SCOPED_VMEM_BYTES),
        name="ret_init" if s0 is not None else "ret",
    )(*args)


def _rope_tables(pos, reps):
    half = DK_RET // 2
    inv = ROPE_BASE ** (-jnp.arange(half, dtype=F32) / half)
    ang = pos[:, None] * inv[None, :]
    return jnp.tile(jnp.cos(ang), (reps, 1)), jnp.tile(jnp.sin(ang), (reps, 1))


def _outproj_body(oap_ref, oas_ref, orp_ref, ors_ref, xp_ref, xs_ref, w_ref, ln_ref, wrt_ref, brt_ref,
                  x1_ref, hn_ref, route_ref, counts_ref, *, n_prompt_blocks):
    step = pl.program_id(0)
    is_prompt = step < n_prompt_blocks
    oa = jnp.where(is_prompt, oap_ref[...], oas_ref[...])
    orr = jnp.where(is_prompt, orp_ref[...], ors_ref[...])
    x = jnp.where(is_prompt, xp_ref[...], xs_ref[...])
    x1 = x + _dot(oa, w_ref[:GLA_V, :]) + _dot(orr, w_ref[GLA_V:, :])
    x1_ref[...] = x1
    hn = _rms(x1) * ln_ref[...]
    hn_ref[...] = hn

    logits = _dot_split(hn, wrt_ref[0], wrt_ref[1]) + brt_ref[...]
    lane_i = lax.broadcasted_iota(I32, logits.shape, 1)
    lane = lane_i.astype(F32)
    grp = (lane_i >> (EXPERTS_PER_GROUP.bit_length() - 1)).astype(F32)
    neg = -jnp.inf
    far = float(LANES)

    gl = jnp.where((lane_i >= N_EXPERTS) & (lane_i < N_EXPERTS + N_GROUPS), logits, neg)
    gmax = jnp.max(gl, axis=-1, keepdims=True)
    g_w = 1.0 / jnp.sum(jnp.exp(gl - gmax), axis=-1, keepdims=True)
    g_idx = jnp.min(jnp.where(gl == gmax, lane, far), axis=-1, keepdims=True) - float(N_EXPERTS)

    el = jnp.where((lane_i < N_EXPERTS) & (grp == g_idx), logits, neg)
    m1 = jnp.max(el, axis=-1, keepdims=True)
    esum = jnp.sum(jnp.exp(el - m1), axis=-1, keepdims=True)
    i1 = jnp.min(jnp.where(el == m1, lane, far), axis=-1, keepdims=True)
    el2 = jnp.where(lane == i1, neg, el)
    m2 = jnp.max(el2, axis=-1, keepdims=True)
    i2 = jnp.min(jnp.where(el2 == m2, lane, far), axis=-1, keepdims=True)
    p1 = 1.0 / esum
    p2 = jnp.exp(m2 - m1) / esum
    gate1 = g_w * (p1 / (p1 + p2))
    gate2 = g_w * (p2 / (p1 + p2))
    @pl.when(step == 0)
    def _():
        counts_ref[...] = jnp.zeros_like(counts_ref)

    tm = logits.shape[0]
    earlier = (lax.broadcasted_iota(I32, (tm, tm), 0) > lax.broadcasted_iota(I32, (tm, tm), 1)).astype(BF16)
    running = counts_ref[0:1, :]
    ranks = []
    for idx in (i1, i2):
        onehot = (lane == idx).astype(F32)
        before = _dot(earlier, onehot.astype(BF16)) + running
        ranks.append(jnp.sum(onehot * before, axis=-1, keepdims=True))
        running = running + jnp.sum(onehot, axis=0, keepdims=True)
    counts_ref[...] = jnp.broadcast_to(running, counts_ref.shape)

    route_ref[...] = jnp.where(lane_i == 0, i1,
                               jnp.where(lane_i == 1, i2,
                                         jnp.where(lane_i == 2, gate1,
                                                   jnp.where(lane_i == 3, gate2,
                                                             jnp.where(lane_i == 4, ranks[0],
                                                                       jnp.where(lane_i == 5, ranks[1], 0.0))))))


def _outproj_router(oa_p, oa_s, or_p, or_s, xp, xs, w_out, ln, w_rt, b_rt):
    n = xp.shape[0] + xs.shape[0]
    nbp = xp.shape[0] // PROJ_TM
    pmap = lambda i: (jnp.minimum(i, nbp - 1), 0)
    smap = lambda i: (jnp.maximum(i - nbp, 0), 0)
    row = lambda i: (i, 0)
    fixed = lambda i: (0, 0)
    return pl.pallas_call(
        functools.partial(_outproj_body, n_prompt_blocks=nbp),
        out_shape=(jax.ShapeDtypeStruct((n, D_MODEL), F32),
                   jax.ShapeDtypeStruct((n, D_MODEL), F32),
                   jax.ShapeDtypeStruct((n, LANES), F32),
                   jax.ShapeDtypeStruct((SUBLANES, LANES), F32)),
        grid=(n // PROJ_TM,),
        in_specs=[pl.BlockSpec((PROJ_TM, GLA_V), pmap), pl.BlockSpec((PROJ_TM, GLA_V), smap),
                  pl.BlockSpec((PROJ_TM, RET_V), pmap), pl.BlockSpec((PROJ_TM, RET_V), smap),
                  pl.BlockSpec((PROJ_TM, D_MODEL), pmap), pl.BlockSpec((PROJ_TM, D_MODEL), smap),
                  pl.BlockSpec((GLA_V + RET_V, D_MODEL), fixed, pipeline_mode=pl.Buffered(1)),
                  pl.BlockSpec((1, D_MODEL), fixed),
                  pl.BlockSpec((2, D_MODEL, LANES), lambda i: (0, 0, 0)),
                  pl.BlockSpec((1, LANES), fixed)],
        out_specs=(pl.BlockSpec((PROJ_TM, D_MODEL), row), pl.BlockSpec((PROJ_TM, D_MODEL), row),
                   pl.BlockSpec((PROJ_TM, LANES), row), pl.BlockSpec((SUBLANES, LANES), fixed)),
        compiler_params=pltpu.CompilerParams(dimension_semantics=("arbitrary",),
                                             vmem_limit_bytes=V7X_SCOPED_VMEM_BYTES),
        name="outproj_router",
    )(oa_p, oa_s, or_p, or_s, xp, xs, w_out, ln, w_rt, b_rt)


def _row_dma_wait(buf, sem, s):
    pltpu.make_async_copy(buf.at[s], buf.at[s], sem.at[s]).wait()


def _moe_body(texp_ref, nval_ref, pos_ref, hn_hbm, wg_ref, wu_ref, wd_ref, o_ref,
              inv, zeros_v, xbuf, hb, au, hidb, wg_b, wu_b, wd_b, gsem):
    tm = MOE_TM
    i = pl.program_id(0)
    nt = pl.num_programs(0)
    nv = nval_ref[i]
    nv_prev = jnp.where(i >= 1, nval_ref[jnp.maximum(i - 1, 0)], 0)
    tile_next = jnp.minimum(i + 1, nt - 1)

    def gather_rows(tile, lo, hi, s):
        rows = range(lo, hi)
        ds = [inv[tile * tm + r] for r in rows]
        for d, r in zip(ds, rows):
            pltpu.make_async_copy(hn_hbm.at[pl.ds(d >> 1, 1), :],
                                  xbuf.at[s, r // SUBLANES, pl.ds(r % SUBLANES, 1), :], gsem.at[s]).start()

    def tile_compute(s):
        cw, kw = MOE_COL_CHUNK, MOE_K_CHUNK
        n_up = 2 * D_EXPERT // cw
        n_down = D_MODEL // cw
        per = tm // MOE_DMA_BURSTS

        def chunk_dot(lhs_ref, depth, w_ref, c0, burst):
            acc = None
            for k0 in range(0, depth, kw):
                if burst is not None and k0 == (depth // kw // 2) * kw:
                    gather_rows(tile_next, burst * per, (burst + 1) * per, 1 - s)
                part = _dot(lhs_ref[:, k0:k0 + kw], w_ref[k0:k0 + kw, c0:c0 + cw])
                acc = part if acc is None else acc + part
            return acc

        hb[...] = xbuf[s].reshape(tm, D_MODEL).astype(BF16)
        for k in range(n_up):
            w_ref, c0 = (wg_b, k * cw) if k * cw < D_EXPERT else (wu_b, k * cw - D_EXPERT)
            au[:, k * cw:(k + 1) * cw] = chunk_dot(hb, D_MODEL, w_ref, c0, k if k < MOE_DMA_BURSTS else None)
        hidb[...] = (_silu(au[:, :D_EXPERT]) * au[:, D_EXPERT:]).astype(BF16)
        for k in range(n_down):
            o_ref[:, k * cw:(k + 1) * cw] = chunk_dot(hidb, D_EXPERT, wd_b, k * cw, None)

    @pl.when(i == 0)
    def _():
        zeros_v[...] = jnp.zeros_like(zeros_v)
        clear = pltpu.make_async_copy(zeros_v, inv, gsem.at[1])
        clear.start()
        clear.wait()

        def body(a, c):
            inv[pos_ref[a]] = a
            return c
        lax.fori_loop(0, pos_ref.shape[0], body, 0, unroll=8)

        def group(g, c):
            ds = [inv[g * SUBLANES + j] for j in range(SUBLANES)]
            for j in range(SUBLANES):
                pltpu.make_async_copy(hn_hbm.at[pl.ds(ds[j] >> 1, 1), :], xbuf.at[0, g, pl.ds(j, 1), :],
                                      gsem.at[0]).start()
            return c
        lax.fori_loop(0, tm // SUBLANES, group, 0)

    def step(s):
        @pl.when(nv > 0)
        def _():
            _row_dma_wait(xbuf, gsem, s)

            @pl.when((i == 0) | (texp_ref[i] != texp_ref[jnp.maximum(i - 1, 0)]))
            def _():
                wg_b[...] = wg_ref[0].astype(BF16)
                wu_b[...] = wu_ref[0].astype(BF16)
                wd_b[...] = wd_ref[0].astype(BF16)

            tile_compute(s)

            @pl.when(i == nt - 1)
            def _():
                _row_dma_wait(xbuf, gsem, 1 - s)

        @pl.when(nv == 0)
        def _():
            o_ref[...] = jnp.zeros_like(o_ref)

            @pl.when(nv_prev > 0)
            def _():
                _row_dma_wait(xbuf, gsem, s)

    for s in range(2):
        pl.when(i % 2 == s)(functools.partial(step, s))


def _moe(tile_expert, n_valid, pos, hn, w_gate, w_up, w_down):
    n_tiles = tile_expert.shape[0]
    wmap = lambda i, te, nv, pos_: (te[i], 0, 0)
    grouped = (MOE_TM // SUBLANES, SUBLANES, D_MODEL)
    return pl.pallas_call(
        _moe_body,
        out_shape=jax.ShapeDtypeStruct((n_tiles * MOE_TM, D_MODEL), F32),
        grid_spec=pltpu.PrefetchScalarGridSpec(
            num_scalar_prefetch=3,
            grid=(n_tiles,),
            in_specs=[pl.BlockSpec(memory_space=pl.ANY),
                      pl.BlockSpec((1, D_MODEL, D_EXPERT), wmap),
                      pl.BlockSpec((1, D_MODEL, D_EXPERT), wmap),
                      pl.BlockSpec((1, D_EXPERT, D_MODEL), wmap)],
            out_specs=pl.BlockSpec((MOE_TM, D_MODEL), lambda i, te, nv, pos_: (i, 0)),
            scratch_shapes=[pltpu.SMEM((n_tiles * MOE_TM,), I32),
                            pltpu.VMEM((n_tiles * MOE_TM,), I32),
                            pltpu.VMEM((2,) + grouped, F32),
                            pltpu.VMEM((MOE_TM, D_MODEL), BF16),
                            pltpu.VMEM((MOE_TM, 2 * D_EXPERT), F32),
                            pltpu.VMEM((MOE_TM, D_EXPERT), BF16),
                            pltpu.VMEM((D_MODEL, D_EXPERT), BF16),
                            pltpu.VMEM((D_MODEL, D_EXPERT), BF16),
                            pltpu.VMEM((D_EXPERT, D_MODEL), BF16),
                            pltpu.SemaphoreType.DMA((2,))]),
        compiler_params=pltpu.CompilerParams(dimension_semantics=("arbitrary",),
                                             vmem_limit_bytes=V7X_SCOPED_VMEM_BYTES),
        name="moe",
    )(tile_expert, n_valid, pos, hn, w_gate, w_up, w_down)


def _routing_tables(route, counts, n):
    tm = MOE_TM
    n_tiles = 2 * n // tm + N_EXPERTS
    experts = route[:, 0:2].astype(I32)
    rank = route[:, 4:6].astype(I32)
    counts = counts[0, :N_EXPERTS].astype(I32)
    tiles = (counts + tm - 1) // tm
    tile_end = jnp.cumsum(tiles)
    tile_start = tile_end - tiles
    onehot = experts[:, :, None] == jnp.arange(N_EXPERTS, dtype=I32)[None, None, :]
    pos = jnp.sum(jnp.where(onehot, (tile_start * tm)[None, None, :], 0), axis=-1) + rank
    tidx = jnp.arange(n_tiles, dtype=I32)
    t_exp = jnp.sum((tidx[:, None] >= tile_end[None, :]).astype(I32), axis=1)
    used = tidx < tile_end[-1]
    last_exp = jnp.sum((tile_end[-1] - 1 >= tile_end).astype(I32))
    t_exp = jnp.where(used, t_exp, last_exp)
    n_valid = jnp.where(used, jnp.clip(counts[t_exp] - (tidx - tile_start[t_exp]) * tm, 0, tm), 0)
    return t_exp.astype(I32), n_valid.astype(I32), pos.reshape(2 * n)


def _final_body(pos_ref, x1_ref, route_ref, ln_ref, y_hbm, o_ref, ybuf, sem, *, row0):
    tm = PROJ_TM
    i = pl.program_id(0)
    n_steps = pl.num_programs(0)

    def fetch(step, b):
        a0 = 2 * (row0 + step * tm)
        for g in range(tm // SUBLANES):
            ps = [pos_ref[a0 + 2 * (g * SUBLANES + j) + k] for j in range(SUBLANES) for k in range(2)]
            for j in range(SUBLANES):
                for k in range(2):
                    pltpu.make_async_copy(y_hbm.at[pl.ds(ps[2 * j + k], 1), :],
                                          ybuf.at[b, k * (tm // SUBLANES) + g, pl.ds(j, 1), :], sem.at[b]).start()

    @pl.when(i == 0)
    def _():
        fetch(0, 0)

    def step(b):
        @pl.when(i + 1 < n_steps)
        def _():
            fetch(i + 1, 1 - b)
        _row_dma_wait(ybuf, sem, b)
        y = ybuf[b].reshape(2, tm, D_MODEL)
        z = x1_ref[...] + route_ref[:, 2:3] * y[0] + route_ref[:, 3:4] * y[1]
        o_ref[...] = _rms(z) * ln_ref[...]

    for b in range(2):
        pl.when(i % 2 == b)(functools.partial(step, b))


def _final(pos, x1, route, ln, y_sorted, *, row0, rows):
    base = row0 // PROJ_TM
    return pl.pallas_call(
        functools.partial(_final_body, row0=row0),
        out_shape=jax.ShapeDtypeStruct((rows, D_MODEL), F32),
        grid_spec=pltpu.PrefetchScalarGridSpec(
            num_scalar_prefetch=1,
            grid=(rows // PROJ_TM,),
            in_specs=[pl.BlockSpec((PROJ_TM, D_MODEL), lambda i, pos_: (base + i, 0)),
                      pl.BlockSpec((PROJ_TM, LANES), lambda i, pos_: (base + i, 0)),
                      pl.BlockSpec((1, D_MODEL), lambda i, pos_: (0, 0)),
                      pl.BlockSpec(memory_space=pl.ANY)],
            out_specs=pl.BlockSpec((PROJ_TM, D_MODEL), lambda i, pos_: (i, 0)),
            scratch_shapes=[pltpu.VMEM((2, 2 * PROJ_TM // SUBLANES, SUBLANES, D_MODEL), F32),
                            pltpu.SemaphoreType.DMA((2,))]),
        compiler_params=pltpu.CompilerParams(dimension_semantics=("arbitrary",),
                                             vmem_limit_bytes=V7X_SCOPED_VMEM_BYTES),
        name="final_norm",
    )(pos, x1, route, ln, y_sorted)


def kernel(x_prompt, x_sample, state_gla, state_ret, ln_attn, w_in, w_gk2, b_gk, gla_norm_w, w_out, ln_ffn, w_router_group, b_router_group, w_router_expert, b_router_expert, w_exp_gate, w_exp_up, w_exp_down, ln_final):
    bp, tp, d = x_prompt.shape
    bs, ts, _ = x_sample.shape
    assert d == D_MODEL and w_in.shape == (1, D_MODEL, IN_COLS_SRC)
    n_p, n_s = bp * tp, bs * ts
    n = n_p + n_s
    assert n_p % PROJ_TM == 0 and n_s % PROJ_TM == 0 and tp % GLA_CHUNK == 0 and tp % RET_CHUNK == 0

    xp = x_prompt.reshape(n_p, d)
    xs = x_sample.reshape(n_s, d)

    hi_lo = lambda a: jnp.stack(_split_bf16(a, 2))
    w_in_p, w_lr = _wprep(w_in[0].T)
    w_gk2p = hi_lo(jnp.concatenate([w_gk2[0], jnp.zeros((LANES - GLA_LOWRANK, GLA_QK), F32)], axis=0))
    w_rt = hi_lo(jnp.concatenate([w_router_expert[0], w_router_group[0],
                                  jnp.zeros((d, LANES - N_EXPERTS - N_GROUPS), F32)], axis=1))
    b_rt = jnp.concatenate([b_router_expert[0], b_router_group[0],
                            jnp.zeros((LANES - N_EXPERTS - N_GROUPS,), F32)])[None, :]

    proj = _inproj(xp, xs, ln_attn, w_in_p, w_lr)

    nw = gla_norm_w
    bgk = b_gk
    oa_p, sg_p = _gla(proj, w_gk2p, bgk, nw, None, row0=0, n_seq=bp, seq_len=tp, chunk=GLA_CHUNK, nseq=1)
    oa_s, sg_s = _gla(proj, w_gk2p, bgk, nw, state_gla[0], row0=n_p, n_seq=bs, seq_len=ts, chunk=ts, nseq=8)

    cos_p, sin_p = _rope_tables(jnp.arange(tp, dtype=F32), 1)
    ret_nseq_s = 4
    cos_s, sin_s = _rope_tables(jnp.arange(ts, dtype=F32) + float(PAST_LEN), ret_nseq_s)
    or_p, sr_p = _ret(proj, cos_p, sin_p, None, row0=0, n_seq=bp, seq_len=tp, chunk=RET_CHUNK, nseq=1)
    or_s, sr_s = _ret(proj, cos_s, sin_s, state_ret[0], row0=n_p, n_seq=bs, seq_len=ts, chunk=ts, nseq=ret_nseq_s)

    x1, hn, route, counts = _outproj_router(oa_p, oa_s, or_p, or_s, xp, xs, w_out[0].astype(BF16),
                                            ln_ffn, w_rt, b_rt)

    t_exp, n_valid, pos = _routing_tables(route, counts, n)
    y_sorted = _moe(t_exp, n_valid, pos, hn, w_exp_gate[0], w_exp_up[0], w_exp_down[0])

    ln_f = ln_final[None, :]
    y_p = _final(pos, x1, route, ln_f, y_sorted, row0=0, rows=n_p).reshape(bp, tp, d)
    y_s = _final(pos, x1, route, ln_f, y_sorted, row0=n_p, rows=n_s).reshape(bs, ts, d)
    return (y_p, y_s, sg_p[None], sr_p[None], sg_s[None], sr_s[None])
```

```python
import functools
import math

import jax
import jax.numpy as jnp
from jax import lax
from jax.experimental import pallas as pl
from jax.experimental.pallas import tpu as pltpu

F32 = jnp.float32
BF16 = jnp.bfloat16
I32 = jnp.int32

D_MODEL = 2048
PAST_LEN = 16384
H_GLA, DK_GLA, DV_GLA = 4, 128, 256
GLA_LOWRANK = 16
GLA_GATE_NORM = 16.0
GLA_SUBCHUNK = 16
H_RET, DK_RET, DV_RET = 4, 256, 256
ROPE_BASE = 10000.0
N_GROUPS, EXPERTS_PER_GROUP, D_EXPERT = 4, 8, 512
N_EXPERTS = N_GROUPS * EXPERTS_PER_GROUP
EPS = 1e-6

LANES = 128
SUBLANES = 8
V7X_SCOPED_VMEM_BYTES = 56 * 1024 * 1024

GLA_QK = H_GLA * DK_GLA
GLA_V = H_GLA * DV_GLA
RET_QK = H_RET * DK_RET
RET_V = H_RET * DV_RET
C_QA, C_KA, C_VA, C_GA = 0, GLA_QK, 2 * GLA_QK, 2 * GLA_QK + GLA_V
C_QR = C_GA + GLA_V
C_KR, C_VR, C_GR = C_QR + RET_QK, C_QR + 2 * RET_QK, C_QR + 2 * RET_QK + RET_V
C_LR = C_GR + RET_V
P_COLS = C_LR + LANES
IN_COLS_SRC = 2 * GLA_QK + GLA_V + GLA_LOWRANK + GLA_V + 2 * RET_QK + 2 * RET_V

PROJ_TM = 256
W_CHUNK = 1024
MOE_TM = 256
GLA_CHUNK = 128
RET_CHUNK = 128

def _dot(a, b):
    return jnp.dot(a, b, preferred_element_type=F32)


def _dot_nt(a, b):
    return lax.dot_general(a, b, (((1,), (1,)), ((), ())), preferred_element_type=F32)


def _dot_tn(a, b):
    return lax.dot_general(a, b, (((0,), (0,)), ((), ())), preferred_element_type=F32)


def _split_bf16(x, parts):
    out = []
    for _ in range(parts):
        p = x.astype(BF16)
        out.append(p)
        x = x - p.astype(F32)
    return out


def _dot_split(a, b_hi, b_lo):
    a_hi, a_lo = _split_bf16(a, 2)
    return _dot(a_hi, b_hi) + (_dot(a_lo, b_hi) + _dot(a_hi, b_lo))


def _rms(x):
    return x * lax.rsqrt(jnp.mean(x * x, axis=-1, keepdims=True) + EPS)


def _silu(x):
    return x * jax.nn.sigmoid(x)


def _wprep_body(a_ref, b_ref, o_ref, olr_ref):
    j = pl.program_id(0)
    a = a_ref[...]
    b = b_ref[...]
    shifted = jnp.concatenate([a[GLA_LOWRANK:, :], b], axis=0)
    o_ref[...] = jnp.where(j >= C_GA // W_CHUNK, shifted, a).T.astype(BF16)

    @pl.when(j == C_GA // W_CHUNK - 1)
    def _():
        padded = jnp.concatenate([b, jnp.zeros((LANES - GLA_LOWRANK, D_MODEL), F32)], axis=0)
        olr_ref[...] = padded.T.astype(BF16)


def _wprep(wt):
    per = W_CHUNK // GLA_LOWRANK
    return pl.pallas_call(
        _wprep_body,
        out_shape=(jax.ShapeDtypeStruct((D_MODEL, C_LR), BF16), jax.ShapeDtypeStruct((D_MODEL, LANES), BF16)),
        grid=(C_LR // W_CHUNK,),
        in_specs=[pl.BlockSpec((W_CHUNK, D_MODEL), lambda j: (j, 0)),
                  pl.BlockSpec((GLA_LOWRANK, D_MODEL), lambda j: (per * (j + 1), 0))],
        out_specs=(pl.BlockSpec((D_MODEL, W_CHUNK), lambda j: (0, j)),
                   pl.BlockSpec((D_MODEL, LANES), lambda j: (0, 0))),
        compiler_params=pltpu.CompilerParams(dimension_semantics=("arbitrary",),
                                             vmem_limit_bytes=V7X_SCOPED_VMEM_BYTES),
        name="wprep",
    )(wt, wt)


def _inproj_body(xp_ref, xs_ref, ln_ref, w_ref, wlr_ref, o_ref, *, n_prompt_blocks):
    x = jnp.where(pl.program_id(0) < n_prompt_blocks, xp_ref[...], xs_ref[...])
    h = (_rms(x) * ln_ref[...]).astype(BF16)
    for c in range(C_LR // W_CHUNK):
        o_ref[:, c * W_CHUNK:(c + 1) * W_CHUNK] = _dot(h, w_ref[:, c * W_CHUNK:(c + 1) * W_CHUNK])
    o_ref[:, C_LR:] = _dot(h, wlr_ref[...])


def _inproj(xp, xs, ln, w, w_lr):
    nbp = xp.shape[0] // PROJ_TM
    n = xp.shape[0] + xs.shape[0]
    fixed = lambda i: (0, 0)
    return pl.pallas_call(
        functools.partial(_inproj_body, n_prompt_blocks=nbp),
        out_shape=jax.ShapeDtypeStruct((n, P_COLS), F32),
        grid=(n // PROJ_TM,),
        in_specs=[pl.BlockSpec((PROJ_TM, D_MODEL), lambda i: (jnp.minimum(i, nbp - 1), 0)),
                  pl.BlockSpec((PROJ_TM, D_MODEL), lambda i: (jnp.maximum(i - nbp, 0), 0)),
                  pl.BlockSpec((1, D_MODEL), fixed),
                  pl.BlockSpec((D_MODEL, C_LR), fixed, pipeline_mode=pl.Buffered(1)),
                  pl.BlockSpec((D_MODEL, LANES), fixed)],
        out_specs=pl.BlockSpec((PROJ_TM, P_COLS), lambda i: (i, 0)),
        compiler_params=pltpu.CompilerParams(dimension_semantics=("arbitrary",),
                                             vmem_limit_bytes=V7X_SCOPED_VMEM_BYTES),
        name="inproj",
    )(xp, xs, ln, w, w_lr)


def _seg_cumsum(x, seg):
    pos = lax.broadcasted_iota(I32, x.shape, 0) & (seg - 1)
    s = 1
    while s < seg:
        x = x + jnp.where(pos >= s, pltpu.roll(x, s, axis=0), 0.0)
        s *= 2
    return x


def _block_ends(p, m):
    rows, w = p.shape
    return jnp.concatenate(
        [jnp.broadcast_to(p[j * m + m - 1:j * m + m, :], (m, w)) for j in range(rows // m)], axis=0)


def _block_starts(p, m, chunk):
    rows, w = p.shape
    pieces = []
    for j in range(rows // m):
        if (j * m) % chunk == 0:
            pieces.append(jnp.zeros((m, w), F32))
        else:
            pieces.append(jnp.broadcast_to(p[j * m - 1:j * m, :], (m, w)))
    return jnp.concatenate(pieces, axis=0)


def _gla_body(*refs, chunk, nseq, has_init):
    if has_init:
        (q_ref, k_ref, v_ref, g_ref, lr_ref, wgk_ref, bgk_ref, nw_ref, s0_ref,
         o_ref, sout_ref, s_scr) = refs
    else:
        (q_ref, k_ref, v_ref, g_ref, lr_ref, wgk_ref, bgk_ref, nw_ref,
         o_ref, sout_ref, s_scr) = refs
    t = pl.program_id(1)

    @pl.when(t == 0)
    def _():
        if has_init:
            s_scr[...] = s0_ref[...]
        else:
            s_scr[...] = jnp.zeros_like(s_scr)

    x = _dot_split(lr_ref[...], wgk_ref[0], wgk_ref[1]) + bgk_ref[...]
    log_g = -(jnp.maximum(-x, 0.0) + jnp.log1p(jnp.exp(-jnp.abs(x)))) / GLA_GATE_NORM
    p = _seg_cumsum(log_g, chunk)
    p_end = _block_ends(p, chunk)
    eq_c = jnp.exp(p)
    ek_c = jnp.exp(p_end - p)
    sub = min(GLA_SUBCHUNK, chunk)
    p_sub = p - _block_starts(p, sub, chunk)
    eq_d = jnp.exp(p_sub)
    ek_d = jnp.exp(-p_sub)
    levels = []
    m = sub
    while m < chunk:
        levels.append((m, jnp.exp(p - _block_starts(p, m, chunk)), jnp.exp(_block_ends(p, m) - p)))
        m *= 2

    ti = lax.broadcasted_iota(I32, (chunk, chunk), 0)
    si = lax.broadcasted_iota(I32, (chunk, chunk), 1)
    blk = lambda idx, size: idx >> (size.bit_length() - 1)
    mask_d = (blk(ti, sub) == blk(si, sub)) & (si <= ti)
    masks = [((blk(ti, m) & 1) == 1) & (blk(si, m) == blk(ti, m) - 1) for (m, _, _) in levels]
    sel8 = (lax.broadcasted_iota(I32, (8, DV_GLA), 0) == 0).astype(BF16)
    nw = nw_ref[...]

    outs = [[None] * nseq for _ in range(H_GLA)]
    for b in range(nseq):
        r0 = b * chunk
        rows = slice(r0, r0 + chunk)
        for h in range(H_GLA):
            kc = slice(h * DK_GLA, (h + 1) * DK_GLA)
            vc = slice(h * DV_GLA, (h + 1) * DV_GLA)
            q = q_ref[rows, kc] * (DK_GLA ** -0.5)
            k = k_ref[rows, kc]
            v = v_ref[rows, vc].astype(BF16)
            att = jnp.where(mask_d, _dot_nt((q * eq_d[rows, kc]).astype(BF16),
                                            (k * ek_d[rows, kc]).astype(BF16)), 0.0)
            for (m, eq_m, ek_m), mask in zip(levels, masks):
                att = jnp.where(mask, _dot_nt((q * eq_m[rows, kc]).astype(BF16),
                                              (k * ek_m[rows, kc]).astype(BF16)), att)
            s_old = s_scr[b, h]
            o = _dot(att.astype(BF16), v) + _dot((q * eq_c[rows, kc]).astype(BF16), s_old.astype(BF16))
            kv = _dot_tn((k * ek_c[rows, kc]).astype(BF16), v)
            tot = jnp.broadcast_to(p[r0 + chunk - 1:r0 + chunk, kc], (8, DK_GLA))
            t_hi, t_mid, t_lo = _split_bf16(tot, 3)
            decay = jnp.exp(_dot_tn(t_hi, sel8) + (_dot_tn(t_mid, sel8) + _dot_tn(t_lo, sel8)))
            s_scr[b, h] = decay * s_old + kv
            outs[h][b] = _rms(o) * nw * _silu(g_ref[rows, vc])
    for h in range(H_GLA):
        o_ref[:, h * DV_GLA:(h + 1) * DV_GLA] = jnp.concatenate(outs[h], axis=0).astype(BF16)

    @pl.when(t == pl.num_programs(1) - 1)
    def _():
        sout_ref[...] = s_scr[...]


def _gla(proj, w_gk2p, b_gk, nw, s0, *, row0, n_seq, seq_len, chunk, nseq):
    rows = nseq * chunk
    nt = seq_len // chunk
    base = row0 // rows

    def rmap(col):
        return lambda b, t: (base + b * nt + t, col)

    in_specs = [pl.BlockSpec((rows, GLA_QK), rmap(C_QA // GLA_QK)),
                pl.BlockSpec((rows, GLA_QK), rmap(C_KA // GLA_QK)),
                pl.BlockSpec((rows, GLA_V), rmap(C_VA // GLA_V)),
                pl.BlockSpec((rows, GLA_V), rmap(C_GA // GLA_V)),
                pl.BlockSpec((rows, LANES), rmap(C_LR // LANES)),
                pl.BlockSpec((2, LANES, GLA_QK), lambda b, t: (0, 0, 0)),
                pl.BlockSpec((1, GLA_QK), lambda b, t: (0, 0)),
                pl.BlockSpec((1, DV_GLA), lambda b, t: (0, 0))]
    args = [proj, proj, proj, proj, proj, w_gk2p, b_gk, nw]
    state_spec = pl.BlockSpec((nseq, H_GLA, DK_GLA, DV_GLA), lambda b, t: (b, 0, 0, 0))
    if s0 is not None:
        in_specs.append(state_spec)
        args.append(s0)
    return pl.pallas_call(
        functools.partial(_gla_body, chunk=chunk, nseq=nseq, has_init=s0 is not None),
        out_shape=(jax.ShapeDtypeStruct((n_seq * seq_len, GLA_V), BF16),
                   jax.ShapeDtypeStruct((n_seq, H_GLA, DK_GLA, DV_GLA), F32)),
        grid=(n_seq // nseq, nt),
        in_specs=in_specs,
        out_specs=(pl.BlockSpec((rows, GLA_V), lambda b, t: (b * nt + t, 0)), state_spec),
        scratch_shapes=[pltpu.VMEM((nseq, H_GLA, DK_GLA, DV_GLA), F32)],
        compiler_params=pltpu.CompilerParams(dimension_semantics=("arbitrary", "arbitrary"),
                                             vmem_limit_bytes=V7X_SCOPED_VMEM_BYTES),
        name="gla_init" if s0 is not None else "gla",
    )(*args)


def _ret_body(*refs, chunk, nseq, has_init):
    if has_init:
        (q_ref, k_ref, v_ref, g_ref, cos_ref, sin_ref, s0_ref, o_ref, sout_ref, s_scr) = refs
    else:
        (q_ref, k_ref, v_ref, g_ref, cos_ref, sin_ref, o_ref, sout_ref, s_scr) = refs
    t = pl.program_id(1)

    @pl.when(t == 0)
    def _():
        if has_init:
            s_scr[...] = s0_ref[...]
        else:
            s_scr[...] = jnp.zeros_like(s_scr)

    half = DK_RET // 2
    ti = lax.broadcasted_iota(I32, (chunk, chunk), 0)
    si = lax.broadcasted_iota(I32, (chunk, chunk), 1)
    diff = (ti - si).astype(F32)
    idx = lax.broadcasted_iota(I32, (chunk, 1), 0).astype(F32)

    outs = [[None] * nseq for _ in range(H_RET)]
    for h in range(H_RET):
        lg = math.log(1.0 - 2.0 ** (-5.0 - h))
        dmat = jnp.where(diff >= 0, jnp.exp(lg * jnp.maximum(diff, 0.0)), 0.0)
        q_dec = jnp.exp(lg * (idx + 1.0))
        k_dec = jnp.exp(lg * (chunk - 1.0 - idx))
        c_dec = math.exp(lg * chunk)
        for b in range(nseq):
            rows = slice(b * chunk, (b + 1) * chunk)
            cos = cos_ref[rows, :]
            sin = sin_ref[rows, :]
            c1 = slice(h * DK_RET, h * DK_RET + half)
            c2 = slice(h * DK_RET + half, (h + 1) * DK_RET)
            vc = slice(h * DV_RET, (h + 1) * DV_RET)
            q1, q2 = q_ref[rows, c1], q_ref[rows, c2]
            k1, k2 = k_ref[rows, c1], k_ref[rows, c2]
            q = jnp.concatenate([q1 * cos - q2 * sin, q1 * sin + q2 * cos], axis=-1)
            k = jnp.concatenate([k1 * cos - k2 * sin, k1 * sin + k2 * cos], axis=-1) * (DK_RET ** -0.5)
            v = v_ref[rows, vc].astype(BF16)
            qb = q.astype(BF16)
            s_old = s_scr[b, h]
            att = _dot_nt(qb, k.astype(BF16)) * dmat
            o = _dot(att.astype(BF16), v) + _dot(qb, s_old.astype(BF16)) * q_dec
            s_scr[b, h] = c_dec * s_old + _dot_tn((k * k_dec).astype(BF16), v)
            outs[h][b] = _rms(o) * _silu(g_ref[rows, vc])
    for h in range(H_RET):
        o_ref[:, h * DV_RET:(h + 1) * DV_RET] = jnp.concatenate(outs[h], axis=0).astype(BF16)

    @pl.when(t == pl.num_programs(1) - 1)
    def _():
        sout_ref[...] = s_scr[...]


def _ret(proj, cos, sin, s0, *, row0, n_seq, seq_len, chunk, nseq):
    rows = nseq * chunk
    nt = seq_len // chunk
    base = row0 // rows

    def rmap(col):
        return lambda b, t: (base + b * nt + t, col)

    tab_spec = pl.BlockSpec((rows, DK_RET // 2), lambda b, t: (t, 0))
    in_specs = [pl.BlockSpec((rows, RET_QK), rmap(C_QR // RET_QK)),
                pl.BlockSpec((rows, RET_QK), rmap(C_KR // RET_QK)),
                pl.BlockSpec((rows, RET_V), rmap(C_VR // RET_V)),
                pl.BlockSpec((rows, RET_V), rmap(C_GR // RET_V)),
                tab_spec, tab_spec]
    args = [proj, proj, proj, proj, cos, sin]
    state_spec = pl.BlockSpec((nseq, H_RET, DK_RET, DV_RET), lambda b, t: (b, 0, 0, 0))
    if s0 is not None:
        in_specs.append(state_spec)
        args.append(s0)
    return pl.pallas_call(
        functools.partial(_ret_body, chunk=chunk, nseq=nseq, has_init=s0 is not None),
        out_shape=(jax.ShapeDtypeStruct((n_seq * seq_len, RET_V), BF16),
                   jax.ShapeDtypeStruct((n_seq, H_RET, DK_RET, DV_RET), F32)),
        grid=(n_seq // nseq, nt),
        in_specs=in_specs,
        out_specs=(pl.BlockSpec((rows, RET_V), lambda b, t: (b * nt + t, 0)), state_spec),
        scratch_shapes=[pltpu.VMEM((nseq, H_RET, DK_RET, DV_RET), F32)],
        compiler_params=pltpu.CompilerParams(dimension_semantics=("arbitrary", "arbitrary"),
                                             vmem_limit_bytes=V7X_SCOPED_VMEM_BYTES),
        name="ret_init" if s0 is not None else "ret",
    )(*args)


def _rope_tables(pos, reps):
    half = DK_RET // 2
    inv = ROPE_BASE ** (-jnp.arange(half, dtype=F32) / half)
    ang = pos[:, None] * inv[None, :]
    return jnp.tile(jnp.cos(ang), (reps, 1)), jnp.tile(jnp.sin(ang), (reps, 1))


def _outproj_body(oap_ref, oas_ref, orp_ref, ors_ref, xp_ref, xs_ref, w_ref, ln_ref, wrt_ref, brt_ref,
                  x1_ref, hn_ref, route_ref, counts_ref, *, n_prompt_blocks):
    step = pl.program_id(0)
    is_prompt = step < n_prompt_blocks
    oa = jnp.where(is_prompt, oap_ref[...], oas_ref[...])
    orr = jnp.where(is_prompt, orp_ref[...], ors_ref[...])
    x = jnp.where(is_prompt, xp_ref[...], xs_ref[...])
    x1 = x + _dot(oa, w_ref[:GLA_V, :]) + _dot(orr, w_ref[GLA_V:, :])
    x1_ref[...] = x1
    hn = _rms(x1) * ln_ref[...]
    hn_ref[...] = hn

    logits = _dot_split(hn, wrt_ref[0], wrt_ref[1]) + brt_ref[...]
    lane_i = lax.broadcasted_iota(I32, logits.shape, 1)
    lane = lane_i.astype(F32)
    grp = (lane_i >> (EXPERTS_PER_GROUP.bit_length() - 1)).astype(F32)
    neg = -jnp.inf
    far = float(LANES)

    gl = jnp.where((lane_i >= N_EXPERTS) & (lane_i < N_EXPERTS + N_GROUPS), logits, neg)
    gmax = jnp.max(gl, axis=-1, keepdims=True)
    g_w = 1.0 / jnp.sum(jnp.exp(gl - gmax), axis=-1, keepdims=True)
    g_idx = jnp.min(jnp.where(gl == gmax, lane, far), axis=-1, keepdims=True) - float(N_EXPERTS)

    el = jnp.where((lane_i < N_EXPERTS) & (grp == g_idx), logits, neg)
    m1 = jnp.max(el, axis=-1, keepdims=True)
    esum = jnp.sum(jnp.exp(el - m1), axis=-1, keepdims=True)
    i1 = jnp.min(jnp.where(el == m1, lane, far), axis=-1, keepdims=True)
    el2 = jnp.where(lane == i1, neg, el)
    m2 = jnp.max(el2, axis=-1, keepdims=True)
    i2 = jnp.min(jnp.where(el2 == m2, lane, far), axis=-1, keepdims=True)
    p1 = 1.0 / esum
    p2 = jnp.exp(m2 - m1) / esum
    gate1 = g_w * (p1 / (p1 + p2))
    gate2 = g_w * (p2 / (p1 + p2))
    @pl.when(step == 0)
    def _():
        counts_ref[...] = jnp.zeros_like(counts_ref)

    tm = logits.shape[0]
    earlier = (lax.broadcasted_iota(I32, (tm, tm), 0) > lax.broadcasted_iota(I32, (tm, tm), 1)).astype(BF16)
    running = counts_ref[0:1, :]
    ranks = []
    for idx in (i1, i2):
        onehot = (lane == idx).astype(F32)
        before = _dot(earlier, onehot.astype(BF16)) + running
        ranks.append(jnp.sum(onehot * before, axis=-1, keepdims=True))
        running = running + jnp.sum(onehot, axis=0, keepdims=True)
    counts_ref[...] = jnp.broadcast_to(running, counts_ref.shape)

    route_ref[...] = jnp.where(lane_i == 0, i1,
                               jnp.where(lane_i == 1, i2,
                                         jnp.where(lane_i == 2, gate1,
                                                   jnp.where(lane_i == 3, gate2,
                                                             jnp.where(lane_i == 4, ranks[0],
                                                                       jnp.where(lane_i == 5, ranks[1], 0.0))))))


def _outproj_router(oa_p, oa_s, or_p, or_s, xp, xs, w_out, ln, w_rt, b_rt):
    n = xp.shape[0] + xs.shape[0]
    nbp = xp.shape[0] // PROJ_TM
    pmap = lambda i: (jnp.minimum(i, nbp - 1), 0)
    smap = lambda i: (jnp.maximum(i - nbp, 0), 0)
    row = lambda i: (i, 0)
    fixed = lambda i: (0, 0)
    return pl.pallas_call(
        functools.partial(_outproj_body, n_prompt_blocks=nbp),
        out_shape=(jax.ShapeDtypeStruct((n, D_MODEL), F32),
                   jax.ShapeDtypeStruct((n, D_MODEL), F32),
                   jax.ShapeDtypeStruct((n, LANES), F32),
                   jax.ShapeDtypeStruct((SUBLANES, LANES), F32)),
        grid=(n // PROJ_TM,),
        in_specs=[pl.BlockSpec((PROJ_TM, GLA_V), pmap), pl.BlockSpec((PROJ_TM, GLA_V), smap),
                  pl.BlockSpec((PROJ_TM, RET_V), pmap), pl.BlockSpec((PROJ_TM, RET_V), smap),
                  pl.BlockSpec((PROJ_TM, D_MODEL), pmap), pl.BlockSpec((PROJ_TM, D_MODEL), smap),
                  pl.BlockSpec((GLA_V + RET_V, D_MODEL), fixed, pipeline_mode=pl.Buffered(1)),
                  pl.BlockSpec((1, D_MODEL), fixed),
                  pl.BlockSpec((2, D_MODEL, LANES), lambda i: (0, 0, 0)),
                  pl.BlockSpec((1, LANES), fixed)],
        out_specs=(pl.BlockSpec((PROJ_TM, D_MODEL), row), pl.BlockSpec((PROJ_TM, D_MODEL), row),
                   pl.BlockSpec((PROJ_TM, LANES), row), pl.BlockSpec((SUBLANES, LANES), fixed)),
        compiler_params=pltpu.CompilerParams(dimension_semantics=("arbitrary",),
                                             vmem_limit_bytes=V7X_SCOPED_VMEM_BYTES),
        name="outproj_router",
    )(oa_p, oa_s, or_p, or_s, xp, xs, w_out, ln, w_rt, b_rt)


def _row_dma_wait(buf, sem, s):
    pltpu.make_async_copy(buf.at[s], buf.at[s], sem.at[s]).wait()


def _moe_body(texp_ref, nval_ref, pos_ref, hn_hbm, wg_ref, wu_ref, wd_ref, o_ref,
              inv, zeros_v, xbuf, wg_b, wu_b, wd_b, gsem):
    tm = MOE_TM
    i = pl.program_id(0)
    nt = pl.num_programs(0)
    nv = nval_ref[i]
    nv_prev = jnp.where(i >= 1, nval_ref[jnp.maximum(i - 1, 0)], 0)
    tile_next = jnp.minimum(i + 1, nt - 1)

    def gather_rows(tile, lo, hi, s):
        rows = range(lo, hi)
        ds = [inv[tile * tm + r] for r in rows]
        for d, r in zip(ds, rows):
            pltpu.make_async_copy(hn_hbm.at[pl.ds(d >> 1, 1), :],
                                  xbuf.at[s, r // SUBLANES, pl.ds(r % SUBLANES, 1), :], gsem.at[s]).start()

    def tile_compute(s):
        for lo in range(0, tm, SUBLANES):
            gather_rows(tile_next, lo, lo + SUBLANES, 1 - s)
        h = xbuf[s].reshape(tm, D_MODEL).astype(BF16)
        hid = _silu(_dot(h, wg_b[...])) * _dot(h, wu_b[...])
        o_ref[...] = _dot(hid.astype(BF16), wd_b[...])

    @pl.when(i == 0)
    def _():
        zeros_v[...] = jnp.zeros_like(zeros_v)
        clear = pltpu.make_async_copy(zeros_v, inv, gsem.at[1])
        clear.start()
        clear.wait()

        def body(a, c):
            inv[pos_ref[a]] = a
            return c
        lax.fori_loop(0, pos_ref.shape[0], body, 0, unroll=8)

        def group(g, c):
            ds = [inv[g * SUBLANES + j] for j in range(SUBLANES)]
            for j in range(SUBLANES):
                pltpu.make_async_copy(hn_hbm.at[pl.ds(ds[j] >> 1, 1), :], xbuf.at[0, g, pl.ds(j, 1), :],
                                      gsem.at[0]).start()
            return c
        lax.fori_loop(0, tm // SUBLANES, group, 0)

    def step(s):
        @pl.when(nv > 0)
        def _():
            _row_dma_wait(xbuf, gsem, s)

            @pl.when((i == 0) | (texp_ref[i] != texp_ref[jnp.maximum(i - 1, 0)]))
            def _():
                wg_b[...] = wg_ref[0].astype(BF16)
                wu_b[...] = wu_ref[0].astype(BF16)
                wd_b[...] = wd_ref[0].astype(BF16)

            tile_compute(s)

            @pl.when(i == nt - 1)
            def _():
                _row_dma_wait(xbuf, gsem, 1 - s)

        @pl.when(nv == 0)
        def _():
            o_ref[...] = jnp.zeros_like(o_ref)

            @pl.when(nv_prev > 0)
            def _():
                _row_dma_wait(xbuf, gsem, s)

    for s in range(2):
        pl.when(i % 2 == s)(functools.partial(step, s))


def _moe(tile_expert, n_valid, pos, hn, w_gate, w_up, w_down):
    n_tiles = tile_expert.shape[0]
    wmap = lambda i, te, nv, pos_: (te[i], 0, 0)
    grouped = (MOE_TM // SUBLANES, SUBLANES, D_MODEL)
    return pl.pallas_call(
        _moe_body,
        out_shape=jax.ShapeDtypeStruct((n_tiles * MOE_TM, D_MODEL), F32),
        grid_spec=pltpu.PrefetchScalarGridSpec(
            num_scalar_prefetch=3,
            grid=(n_tiles,),
            in_specs=[pl.BlockSpec(memory_space=pl.ANY),
                      pl.BlockSpec((1, D_MODEL, D_EXPERT), wmap),
                      pl.BlockSpec((1, D_MODEL, D_EXPERT), wmap),
                      pl.BlockSpec((1, D_EXPERT, D_MODEL), wmap)],
            out_specs=pl.BlockSpec((MOE_TM, D_MODEL), lambda i, te, nv, pos_: (i, 0)),
            scratch_shapes=[pltpu.SMEM((n_tiles * MOE_TM,), I32),
                            pltpu.VMEM((n_tiles * MOE_TM,), I32),
                            pltpu.VMEM((2,) + grouped, F32),
                            pltpu.VMEM((D_MODEL, D_EXPERT), BF16),
                            pltpu.VMEM((D_MODEL, D_EXPERT), BF16),
                            pltpu.VMEM((D_EXPERT, D_MODEL), BF16),
                            pltpu.SemaphoreType.DMA((2,))]),
        compiler_params=pltpu.CompilerParams(dimension_semantics=("arbitrary",),
                                             vmem_limit_bytes=V7X_SCOPED_VMEM_BYTES),
        name="moe",
    )(tile_expert, n_valid, pos, hn, w_gate, w_up, w_down)


def _routing_tables(route, counts, n):
    tm = MOE_TM
    n_tiles = 2 * n // tm + N_EXPERTS
    experts = route[:, 0:2].astype(I32)
    rank = route[:, 4:6].astype(I32)
    counts = counts[0, :N_EXPERTS].astype(I32)
    tiles = (counts + tm - 1) // tm
    tile_end = jnp.cumsum(tiles)
    tile_start = tile_end - tiles
    onehot = experts[:, :, None] == jnp.arange(N_EXPERTS, dtype=I32)[None, None, :]
    pos = jnp.sum(jnp.where(onehot, (tile_start * tm)[None, None, :], 0), axis=-1) + rank
    tidx = jnp.arange(n_tiles, dtype=I32)
    t_exp = jnp.sum((tidx[:, None] >= tile_end[None, :]).astype(I32), axis=1)
    used = tidx < tile_end[-1]
    last_exp = jnp.sum((tile_end[-1] - 1 >= tile_end).astype(I32))
    t_exp = jnp.where(used, t_exp, last_exp)
    n_valid = jnp.where(used, jnp.clip(counts[t_exp] - (tidx - tile_start[t_exp]) * tm, 0, tm), 0)
    return t_exp.astype(I32), n_valid.astype(I32), pos.reshape(2 * n)


def _final_body(pos_ref, x1_ref, route_ref, ln_ref, y_hbm, o_ref, ybuf, sem, *, row0):
    tm = PROJ_TM
    i = pl.program_id(0)
    n_steps = pl.num_programs(0)

    def fetch(step, b):
        a0 = 2 * (row0 + step * tm)
        for g in range(tm // SUBLANES):
            ps = [pos_ref[a0 + 2 * (g * SUBLANES + j) + k] for j in range(SUBLANES) for k in range(2)]
            for j in range(SUBLANES):
                for k in range(2):
                    pltpu.make_async_copy(y_hbm.at[pl.ds(ps[2 * j + k], 1), :],
                                          ybuf.at[b, k * (tm // SUBLANES) + g, pl.ds(j, 1), :], sem.at[b]).start()

    @pl.when(i == 0)
    def _():
        fetch(0, 0)

    def step(b):
        @pl.when(i + 1 < n_steps)
        def _():
            fetch(i + 1, 1 - b)
        _row_dma_wait(ybuf, sem, b)
        y = ybuf[b].reshape(2, tm, D_MODEL)
        z = x1_ref[...] + route_ref[:, 2:3] * y[0] + route_ref[:, 3:4] * y[1]
        o_ref[...] = _rms(z) * ln_ref[...]

    for b in range(2):
        pl.when(i % 2 == b)(functools.partial(step, b))


def _final(pos, x1, route, ln, y_sorted, *, row0, rows):
    base = row0 // PROJ_TM
    return pl.pallas_call(
        functools.partial(_final_body, row0=row0),
        out_shape=jax.ShapeDtypeStruct((rows, D_MODEL), F32),
        grid_spec=pltpu.PrefetchScalarGridSpec(
            num_scalar_prefetch=1,
            grid=(rows // PROJ_TM,),
            in_specs=[pl.BlockSpec((PROJ_TM, D_MODEL), lambda i, pos_: (base + i, 0)),
                      pl.BlockSpec((PROJ_TM, LANES), lambda i, pos_: (base + i, 0)),
                      pl.BlockSpec((1, D_MODEL), lambda i, pos_: (0, 0)),
                      pl.BlockSpec(memory_space=pl.ANY)],
            out_specs=pl.BlockSpec((PROJ_TM, D_MODEL), lambda i, pos_: (i, 0)),
            scratch_shapes=[pltpu.VMEM((2, 2 * PROJ_TM // SUBLANES, SUBLANES, D_MODEL), F32),
                            pltpu.SemaphoreType.DMA((2,))]),
        compiler_params=pltpu.CompilerParams(dimension_semantics=("arbitrary",),
                                             vmem_limit_bytes=V7X_SCOPED_VMEM_BYTES),
        name="final_norm",
    )(pos, x1, route, ln, y_sorted)


def kernel(x_prompt, x_sample, state_gla, state_ret, ln_attn, w_in, w_gk2, b_gk, gla_norm_w, w_out, ln_ffn, w_router_group, b_router_group, w_router_expert, b_router_expert, w_exp_gate, w_exp_up, w_exp_down, ln_final):
    bp, tp, d = x_prompt.shape
    bs, ts, _ = x_sample.shape
    assert d == D_MODEL and w_in.shape == (1, D_MODEL, IN_COLS_SRC)
    n_p, n_s = bp * tp, bs * ts
    n = n_p + n_s
    assert n_p % PROJ_TM == 0 and n_s % PROJ_TM == 0 and tp % GLA_CHUNK == 0 and tp % RET_CHUNK == 0

    xp = x_prompt.reshape(n_p, d)
    xs = x_sample.reshape(n_s, d)

    hi_lo = lambda a: jnp.stack(_split_bf16(a, 2))
    w_in_p, w_lr = _wprep(w_in[0].T)
    w_gk2p = hi_lo(jnp.concatenate([w_gk2[0], jnp.zeros((LANES - GLA_LOWRANK, GLA_QK), F32)], axis=0))
    w_rt = hi_lo(jnp.concatenate([w_router_expert[0], w_router_group[0],
                                  jnp.zeros((d, LANES - N_EXPERTS - N_GROUPS), F32)], axis=1))
    b_rt = jnp.concatenate([b_router_expert[0], b_router_group[0],
                            jnp.zeros((LANES - N_EXPERTS - N_GROUPS,), F32)])[None, :]

    proj = _inproj(xp, xs, ln_attn, w_in_p, w_lr)

    nw = gla_norm_w
    bgk = b_gk
    oa_p, sg_p = _gla(proj, w_gk2p, bgk, nw, None, row0=0, n_seq=bp, seq_len=tp, chunk=GLA_CHUNK, nseq=1)
    oa_s, sg_s = _gla(proj, w_gk2p, bgk, nw, state_gla[0], row0=n_p, n_seq=bs, seq_len=ts, chunk=ts, nseq=8)

    cos_p, sin_p = _rope_tables(jnp.arange(tp, dtype=F32), 1)
    ret_nseq_s = 4
    cos_s, sin_s = _rope_tables(jnp.arange(ts, dtype=F32) + float(PAST_LEN), ret_nseq_s)
    or_p, sr_p = _ret(proj, cos_p, sin_p, None, row0=0, n_seq=bp, seq_len=tp, chunk=RET_CHUNK, nseq=1)
    or_s, sr_s = _ret(proj, cos_s, sin_s, state_ret[0], row0=n_p, n_seq=bs, seq_len=ts, chunk=ts, nseq=ret_nseq_s)

    x1, hn, route, counts = _outproj_router(oa_p, oa_s, or_p, or_s, xp, xs, w_out[0].astype(BF16),
                                            ln_ffn, w_rt, b_rt)

    t_exp, n_valid, pos = _routing_tables(route, counts, n)
    y_sorted = _moe(t_exp, n_valid, pos, hn, w_exp_gate[0], w_exp_up[0], w_exp_down[0])

    ln_f = ln_final[None, :]
    y_p = _final(pos, x1, route, ln_f, y_sorted, row0=0, rows=n_p).reshape(bp, tp, d)
    y_s = _final(pos, x1, route, ln_f, y_sorted, row0=n_p, rows=n_s).reshape(bs, ts, d)
    return (y_p, y_s, sg_p[None], sr_p[None], sg_s[None], sr_s[None])
```

```python
import functools
import math

import jax
import jax.numpy as jnp
from jax import lax
from jax.experimental import pallas as pl
from jax.experimental.pallas import tpu as pltpu

F32 = jnp.float32
BF16 = jnp.bfloat16
I32 = jnp.int32

D_MODEL = 2048
PAST_LEN = 16384
H_GLA, DK_GLA, DV_GLA = 4, 128, 256
GLA_LOWRANK = 16
GLA_GATE_NORM = 16.0
GLA_SUBCHUNK = 16
H_RET, DK_RET, DV_RET = 4, 256, 256
ROPE_BASE = 10000.0
N_GROUPS, EXPERTS_PER_GROUP, D_EXPERT = 4, 8, 512
N_EXPERTS = N_GROUPS * EXPERTS_PER_GROUP
EPS = 1e-6

LANES = 128
SUBLANES = 8
V7X_SCOPED_VMEM_BYTES = 56 * 1024 * 1024

GLA_QK = H_GLA * DK_GLA
GLA_V = H_GLA * DV_GLA
RET_QK = H_RET * DK_RET
RET_V = H_RET * DV_RET
C_QA, C_KA, C_VA, C_GA = 0, GLA_QK, 2 * GLA_QK, 2 * GLA_QK + GLA_V
C_QR = C_GA + GLA_V
C_KR, C_VR, C_GR = C_QR + RET_QK, C_QR + 2 * RET_QK, C_QR + 2 * RET_QK + RET_V
C_LR = C_GR + RET_V
P_COLS = C_LR + LANES
IN_COLS_SRC = 2 * GLA_QK + GLA_V + GLA_LOWRANK + GLA_V + 2 * RET_QK + 2 * RET_V

PROJ_TM = 256
W_CHUNK = 1024
MOE_TM = 256
GLA_CHUNK = 128
RET_CHUNK = 128

def _dot(a, b):
    return jnp.dot(a, b, preferred_element_type=F32)


def _dot_nt(a, b):
    return lax.dot_general(a, b, (((1,), (1,)), ((), ())), preferred_element_type=F32)


def _dot_tn(a, b):
    return lax.dot_general(a, b, (((0,), (0,)), ((), ())), preferred_element_type=F32)


def _split_bf16(x, parts):
    out = []
    for _ in range(parts):
        p = x.astype(BF16)
        out.append(p)
        x = x - p.astype(F32)
    return out


def _dot_split(a, b_hi, b_lo):
    a_hi, a_lo = _split_bf16(a, 2)
    return _dot(a_hi, b_hi) + (_dot(a_lo, b_hi) + _dot(a_hi, b_lo))


def _rms(x):
    return x * lax.rsqrt(jnp.mean(x * x, axis=-1, keepdims=True) + EPS)


def _silu(x):
    return x * jax.nn.sigmoid(x)


def _wprep_body(a_ref, b_ref, o_ref, olr_ref):
    j = pl.program_id(0)
    a = a_ref[...]
    b = b_ref[...]
    shifted = jnp.concatenate([a[GLA_LOWRANK:, :], b], axis=0)
    o_ref[...] = jnp.where(j >= C_GA // W_CHUNK, shifted, a).T.astype(BF16)

    @pl.when(j == C_GA // W_CHUNK - 1)
    def _():
        padded = jnp.concatenate([b, jnp.zeros((LANES - GLA_LOWRANK, D_MODEL), F32)], axis=0)
        olr_ref[...] = padded.T.astype(BF16)


def _wprep(wt):
    per = W_CHUNK // GLA_LOWRANK
    return pl.pallas_call(
        _wprep_body,
        out_shape=(jax.ShapeDtypeStruct((D_MODEL, C_LR), BF16), jax.ShapeDtypeStruct((D_MODEL, LANES), BF16)),
        grid=(C_LR // W_CHUNK,),
        in_specs=[pl.BlockSpec((W_CHUNK, D_MODEL), lambda j: (j, 0)),
                  pl.BlockSpec((GLA_LOWRANK, D_MODEL), lambda j: (per * (j + 1), 0))],
        out_specs=(pl.BlockSpec((D_MODEL, W_CHUNK), lambda j: (0, j)),
                   pl.BlockSpec((D_MODEL, LANES), lambda j: (0, 0))),
        compiler_params=pltpu.CompilerParams(dimension_semantics=("arbitrary",),
                                             vmem_limit_bytes=V7X_SCOPED_VMEM_BYTES),
        name="wprep",
    )(wt, wt)


def _inproj_body(xp_ref, xs_ref, ln_ref, w_ref, wlr_ref, o_ref, *, n_prompt_blocks):
    x = jnp.where(pl.program_id(0) < n_prompt_blocks, xp_ref[...], xs_ref[...])
    h = (_rms(x) * ln_ref[...]).astype(BF16)
    for c in range(C_LR // W_CHUNK):
        o_ref[:, c * W_CHUNK:(c + 1) * W_CHUNK] = _dot(h, w_ref[:, c * W_CHUNK:(c + 1) * W_CHUNK])
    o_ref[:, C_LR:] = _dot(h, wlr_ref[...])


def _inproj(xp, xs, ln, w, w_lr):
    nbp = xp.shape[0] // PROJ_TM
    n = xp.shape[0] + xs.shape[0]
    fixed = lambda i: (0, 0)
    return pl.pallas_call(
        functools.partial(_inproj_body, n_prompt_blocks=nbp),
        out_shape=jax.ShapeDtypeStruct((n, P_COLS), F32),
        grid=(n // PROJ_TM,),
        in_specs=[pl.BlockSpec((PROJ_TM, D_MODEL), lambda i: (jnp.minimum(i, nbp - 1), 0)),
                  pl.BlockSpec((PROJ_TM, D_MODEL), lambda i: (jnp.maximum(i - nbp, 0), 0)),
                  pl.BlockSpec((1, D_MODEL), fixed),
                  pl.BlockSpec((D_MODEL, C_LR), fixed, pipeline_mode=pl.Buffered(1)),
                  pl.BlockSpec((D_MODEL, LANES), fixed)],
        out_specs=pl.BlockSpec((PROJ_TM, P_COLS), lambda i: (i, 0)),
        compiler_params=pltpu.CompilerParams(dimension_semantics=("arbitrary",),
                                             vmem_limit_bytes=V7X_SCOPED_VMEM_BYTES),
        name="inproj",
    )(xp, xs, ln, w, w_lr)


def _seg_cumsum(x, seg):
    pos = lax.broadcasted_iota(I32, x.shape, 0) & (seg - 1)
    s = 1
    while s < seg:
        x = x + jnp.where(pos >= s, pltpu.roll(x, s, axis=0), 0.0)
        s *= 2
    return x


def _block_ends(p, m):
    rows, w = p.shape
    return jnp.concatenate(
        [jnp.broadcast_to(p[j * m + m - 1:j * m + m, :], (m, w)) for j in range(rows // m)], axis=0)


def _block_starts(p, m, chunk):
    rows, w = p.shape
    pieces = []
    for j in range(rows // m):
        if (j * m) % chunk == 0:
            pieces.append(jnp.zeros((m, w), F32))
        else:
            pieces.append(jnp.broadcast_to(p[j * m - 1:j * m, :], (m, w)))
    return jnp.concatenate(pieces, axis=0)


def _gla_body(*refs, chunk, nseq, has_init):
    if has_init:
        (q_ref, k_ref, v_ref, g_ref, lr_ref, wgk_ref, bgk_ref, nw_ref, s0_ref,
         o_ref, sout_ref, s_scr) = refs
    else:
        (q_ref, k_ref, v_ref, g_ref, lr_ref, wgk_ref, bgk_ref, nw_ref,
         o_ref, sout_ref, s_scr) = refs
    t = pl.program_id(1)

    @pl.when(t == 0)
    def _():
        if has_init:
            s_scr[...] = s0_ref[...]
        else:
            s_scr[...] = jnp.zeros_like(s_scr)

    x = _dot_split(lr_ref[...], wgk_ref[0], wgk_ref[1]) + bgk_ref[...]
    log_g = -(jnp.maximum(-x, 0.0) + jnp.log1p(jnp.exp(-jnp.abs(x)))) / GLA_GATE_NORM
    p = _seg_cumsum(log_g, chunk)
    p_end = _block_ends(p, chunk)
    eq_c = jnp.exp(p)
    ek_c = jnp.exp(p_end - p)
    sub = min(GLA_SUBCHUNK, chunk)
    p_sub = p - _block_starts(p, sub, chunk)
    eq_d = jnp.exp(p_sub)
    ek_d = jnp.exp(-p_sub)
    levels = []
    m = sub
    while m < chunk:
        levels.append((m, jnp.exp(p - _block_starts(p, m, chunk)), jnp.exp(_block_ends(p, m) - p)))
        m *= 2

    ti = lax.broadcasted_iota(I32, (chunk, chunk), 0)
    si = lax.broadcasted_iota(I32, (chunk, chunk), 1)
    blk = lambda idx, size: idx >> (size.bit_length() - 1)
    mask_d = (blk(ti, sub) == blk(si, sub)) & (si <= ti)
    masks = [((blk(ti, m) & 1) == 1) & (blk(si, m) == blk(ti, m) - 1) for (m, _, _) in levels]
    sel8 = (lax.broadcasted_iota(I32, (8, DV_GLA), 0) == 0).astype(BF16)
    nw = nw_ref[...]

    outs = [[None] * nseq for _ in range(H_GLA)]
    for b in range(nseq):
        r0 = b * chunk
        rows = slice(r0, r0 + chunk)
        for h in range(H_GLA):
            kc = slice(h * DK_GLA, (h + 1) * DK_GLA)
            vc = slice(h * DV_GLA, (h + 1) * DV_GLA)
            q = q_ref[rows, kc] * (DK_GLA ** -0.5)
            k = k_ref[rows, kc]
            v = v_ref[rows, vc].astype(BF16)
            att = jnp.where(mask_d, _dot_nt((q * eq_d[rows, kc]).astype(BF16),
                                            (k * ek_d[rows, kc]).astype(BF16)), 0.0)
            for (m, eq_m, ek_m), mask in zip(levels, masks):
                att = jnp.where(mask, _dot_nt((q * eq_m[rows, kc]).astype(BF16),
                                              (k * ek_m[rows, kc]).astype(BF16)), att)
            s_old = s_scr[b, h]
            o = _dot(att.astype(BF16), v) + _dot((q * eq_c[rows, kc]).astype(BF16), s_old.astype(BF16))
            kv = _dot_tn((k * ek_c[rows, kc]).astype(BF16), v)
            tot = jnp.broadcast_to(p[r0 + chunk - 1:r0 + chunk, kc], (8, DK_GLA))
            t_hi, t_mid, t_lo = _split_bf16(tot, 3)
            decay = jnp.exp(_dot_tn(t_hi, sel8) + (_dot_tn(t_mid, sel8) + _dot_tn(t_lo, sel8)))
            s_scr[b, h] = decay * s_old + kv
            outs[h][b] = _rms(o) * nw * _silu(g_ref[rows, vc])
    for h in range(H_GLA):
        o_ref[:, h * DV_GLA:(h + 1) * DV_GLA] = jnp.concatenate(outs[h], axis=0).astype(BF16)

    @pl.when(t == pl.num_programs(1) - 1)
    def _():
        sout_ref[...] = s_scr[...]


def _gla(proj, w_gk2p, b_gk, nw, s0, *, row0, n_seq, seq_len, chunk, nseq):
    rows = nseq * chunk
    nt = seq_len // chunk
    base = row0 // rows

    def rmap(col):
        return lambda b, t: (base + b * nt + t, col)

    in_specs = [pl.BlockSpec((rows, GLA_QK), rmap(C_QA // GLA_QK)),
                pl.BlockSpec((rows, GLA_QK), rmap(C_KA // GLA_QK)),
                pl.BlockSpec((rows, GLA_V), rmap(C_VA // GLA_V)),
                pl.BlockSpec((rows, GLA_V), rmap(C_GA // GLA_V)),
                pl.BlockSpec((rows, LANES), rmap(C_LR // LANES)),
                pl.BlockSpec((2, LANES, GLA_QK), lambda b, t: (0, 0, 0)),
                pl.BlockSpec((1, GLA_QK), lambda b, t: (0, 0)),
                pl.BlockSpec((1, DV_GLA), lambda b, t: (0, 0))]
    args = [proj, proj, proj, proj, proj, w_gk2p, b_gk, nw]
    state_spec = pl.BlockSpec((nseq, H_GLA, DK_GLA, DV_GLA), lambda b, t: (b, 0, 0, 0))
    if s0 is not None:
        in_specs.append(state_spec)
        args.append(s0)
    return pl.pallas_call(
        functools.partial(_gla_body, chunk=chunk, nseq=nseq, has_init=s0 is not None),
        out_shape=(jax.ShapeDtypeStruct((n_seq * seq_len, GLA_V), BF16),
                   jax.ShapeDtypeStruct((n_seq, H_GLA, DK_GLA, DV_GLA), F32)),
        grid=(n_seq // nseq, nt),
        in_specs=in_specs,
        out_specs=(pl.BlockSpec((rows, GLA_V), lambda b, t: (b * nt + t, 0)), state_spec),
        scratch_shapes=[pltpu.VMEM((nseq, H_GLA, DK_GLA, DV_GLA), F32)],
        compiler_params=pltpu.CompilerParams(dimension_semantics=("arbitrary", "arbitrary"),
                                             vmem_limit_bytes=V7X_SCOPED_VMEM_BYTES),
        name="gla_init" if s0 is not None else "gla",
    )(*args)


def _ret_body(*refs, chunk, nseq, has_init):
    if has_init:
        (q_ref, k_ref, v_ref, g_ref, cos_ref, sin_ref, s0_ref, o_ref, sout_ref, s_scr) = refs
    else:
        (q_ref, k_ref, v_ref, g_ref, cos_ref, sin_ref, o_ref, sout_ref, s_scr) = refs
    t = pl.program_id(1)

    @pl.when(t == 0)
    def _():
        if has_init:
            s_scr[...] = s0_ref[...]
        else:
            s_scr[...] = jnp.zeros_like(s_scr)

    half = DK_RET // 2
    ti = lax.broadcasted_iota(I32, (chunk, chunk), 0)
    si = lax.broadcasted_iota(I32, (chunk, chunk), 1)
    diff = (ti - si).astype(F32)
    idx = lax.broadcasted_iota(I32, (chunk, 1), 0).astype(F32)

    outs = [[None] * nseq for _ in range(H_RET)]
    for h in range(H_RET):
        lg = math.log(1.0 - 2.0 ** (-5.0 - h))
        dmat = jnp.where(diff >= 0, jnp.exp(lg * jnp.maximum(diff, 0.0)), 0.0)
        q_dec = jnp.exp(lg * (idx + 1.0))
        k_dec = jnp.exp(lg * (chunk - 1.0 - idx))
        c_dec = math.exp(lg * chunk)
        for b in range(nseq):
            rows = slice(b * chunk, (b + 1) * chunk)
            cos = cos_ref[rows, :]
            sin = sin_ref[rows, :]
            c1 = slice(h * DK_RET, h * DK_RET + half)
            c2 = slice(h * DK_RET + half, (h + 1) * DK_RET)
            vc = slice(h * DV_RET, (h + 1) * DV_RET)
            q1, q2 = q_ref[rows, c1], q_ref[rows, c2]
            k1, k2 = k_ref[rows, c1], k_ref[rows, c2]
            q = jnp.concatenate([q1 * cos - q2 * sin, q1 * sin + q2 * cos], axis=-1)
            k = jnp.concatenate([k1 * cos - k2 * sin, k1 * sin + k2 * cos], axis=-1) * (DK_RET ** -0.5)
            v = v_ref[rows, vc].astype(BF16)
            qb = q.astype(BF16)
            s_old = s_scr[b, h]
            att = _dot_nt(qb, k.astype(BF16)) * dmat
            o = _dot(att.astype(BF16), v) + _dot(qb, s_old.astype(BF16)) * q_dec
            s_scr[b, h] = c_dec * s_old + _dot_tn((k * k_dec).astype(BF16), v)
            outs[h][b] = _rms(o) * _silu(g_ref[rows, vc])
    for h in range(H_RET):
        o_ref[:, h * DV_RET:(h + 1) * DV_RET] = jnp.concatenate(outs[h], axis=0).astype(BF16)

    @pl.when(t == pl.num_programs(1) - 1)
    def _():
        sout_ref[...] = s_scr[...]


def _ret(proj, cos, sin, s0, *, row0, n_seq, seq_len, chunk, nseq):
    rows = nseq * chunk
    nt = seq_len // chunk
    base = row0 // rows

    def rmap(col):
        return lambda b, t: (base + b * nt + t, col)

    tab_spec = pl.BlockSpec((rows, DK_RET // 2), lambda b, t: (t, 0))
    in_specs = [pl.BlockSpec((rows, RET_QK), rmap(C_QR // RET_QK)),
                pl.BlockSpec((rows, RET_QK), rmap(C_KR // RET_QK)),
                pl.BlockSpec((rows, RET_V), rmap(C_VR // RET_V)),
                pl.BlockSpec((rows, RET_V), rmap(C_GR // RET_V)),
                tab_spec, tab_spec]
    args = [proj, proj, proj, proj, cos, sin]
    state_spec = pl.BlockSpec((nseq, H_RET, DK_RET, DV_RET), lambda b, t: (b, 0, 0, 0))
    if s0 is not None:
        in_specs.append(state_spec)
        args.append(s0)
    return pl.pallas_call(
        functools.partial(_ret_body, chunk=chunk, nseq=nseq, has_init=s0 is not None),
        out_shape=(jax.ShapeDtypeStruct((n_seq * seq_len, RET_V), BF16),
                   jax.ShapeDtypeStruct((n_seq, H_RET, DK_RET, DV_RET), F32)),
        grid=(n_seq // nseq, nt),
        in_specs=in_specs,
        out_specs=(pl.BlockSpec((rows, RET_V), lambda b, t: (b * nt + t, 0)), state_spec),
        scratch_shapes=[pltpu.VMEM((nseq, H_RET, DK_RET, DV_RET), F32)],
        compiler_params=pltpu.CompilerParams(dimension_semantics=("arbitrary", "arbitrary"),
                                             vmem_limit_bytes=V7X_SCOPED_VMEM_BYTES),
        name="ret_init" if s0 is not None else "ret",
    )(*args)


def _rope_tables(pos, reps):
    half = DK_RET // 2
    inv = ROPE_BASE ** (-jnp.arange(half, dtype=F32) / half)
    ang = pos[:, None] * inv[None, :]
    return jnp.tile(jnp.cos(ang), (reps, 1)), jnp.tile(jnp.sin(ang), (reps, 1))


def _outproj_body(oap_ref, oas_ref, orp_ref, ors_ref, xp_ref, xs_ref, w_ref, ln_ref, wrt_ref, brt_ref,
                  x1_ref, hn_ref, route_ref, counts_ref, *, n_prompt_blocks):
    step = pl.program_id(0)
    is_prompt = step < n_prompt_blocks
    oa = jnp.where(is_prompt, oap_ref[...], oas_ref[...])
    orr = jnp.where(is_prompt, orp_ref[...], ors_ref[...])
    x = jnp.where(is_prompt, xp_ref[...], xs_ref[...])
    x1 = x + _dot(oa, w_ref[:GLA_V, :]) + _dot(orr, w_ref[GLA_V:, :])
    x1_ref[...] = x1
    hn = _rms(x1) * ln_ref[...]
    hn_ref[...] = hn

    logits = _dot_split(hn, wrt_ref[0], wrt_ref[1]) + brt_ref[...]
    lane_i = lax.broadcasted_iota(I32, logits.shape, 1)
    lane = lane_i.astype(F32)
    grp = (lane_i >> (EXPERTS_PER_GROUP.bit_length() - 1)).astype(F32)
    neg = -jnp.inf
    far = float(LANES)

    gl = jnp.where((lane_i >= N_EXPERTS) & (lane_i < N_EXPERTS + N_GROUPS), logits, neg)
    gmax = jnp.max(gl, axis=-1, keepdims=True)
    g_w = 1.0 / jnp.sum(jnp.exp(gl - gmax), axis=-1, keepdims=True)
    g_idx = jnp.min(jnp.where(gl == gmax, lane, far), axis=-1, keepdims=True) - float(N_EXPERTS)

    el = jnp.where((lane_i < N_EXPERTS) & (grp == g_idx), logits, neg)
    m1 = jnp.max(el, axis=-1, keepdims=True)
    esum = jnp.sum(jnp.exp(el - m1), axis=-1, keepdims=True)
    i1 = jnp.min(jnp.where(el == m1, lane, far), axis=-1, keepdims=True)
    el2 = jnp.where(lane == i1, neg, el)
    m2 = jnp.max(el2, axis=-1, keepdims=True)
    i2 = jnp.min(jnp.where(el2 == m2, lane, far), axis=-1, keepdims=True)
    p1 = 1.0 / esum
    p2 = jnp.exp(m2 - m1) / esum
    gate1 = g_w * (p1 / (p1 + p2))
    gate2 = g_w * (p2 / (p1 + p2))
    @pl.when(step == 0)
    def _():
        counts_ref[...] = jnp.zeros_like(counts_ref)

    tm = logits.shape[0]
    earlier = (lax.broadcasted_iota(I32, (tm, tm), 0) > lax.broadcasted_iota(I32, (tm, tm), 1)).astype(BF16)
    running = counts_ref[0:1, :]
    ranks = []
    for idx in (i1, i2):
        onehot = (lane == idx).astype(F32)
        before = _dot(earlier, onehot.astype(BF16)) + running
        ranks.append(jnp.sum(onehot * before, axis=-1, keepdims=True))
        running = running + jnp.sum(onehot, axis=0, keepdims=True)
    counts_ref[...] = jnp.broadcast_to(running, counts_ref.shape)

    route_ref[...] = jnp.where(lane_i == 0, i1,
                               jnp.where(lane_i == 1, i2,
                                         jnp.where(lane_i == 2, gate1,
                                                   jnp.where(lane_i == 3, gate2,
                                                             jnp.where(lane_i == 4, ranks[0],
                                                                       jnp.where(lane_i == 5, ranks[1], 0.0))))))


def _outproj_router(oa_p, oa_s, or_p, or_s, xp, xs, w_out, ln, w_rt, b_rt):
    n = xp.shape[0] + xs.shape[0]
    nbp = xp.shape[0] // PROJ_TM
    pmap = lambda i: (jnp.minimum(i, nbp - 1), 0)
    smap = lambda i: (jnp.maximum(i - nbp, 0), 0)
    row = lambda i: (i, 0)
    fixed = lambda i: (0, 0)
    return pl.pallas_call(
        functools.partial(_outproj_body, n_prompt_blocks=nbp),
        out_shape=(jax.ShapeDtypeStruct((n, D_MODEL), F32),
                   jax.ShapeDtypeStruct((n, D_MODEL), F32),
                   jax.ShapeDtypeStruct((n, LANES), F32),
                   jax.ShapeDtypeStruct((SUBLANES, LANES), F32)),
        grid=(n // PROJ_TM,),
        in_specs=[pl.BlockSpec((PROJ_TM, GLA_V), pmap), pl.BlockSpec((PROJ_TM, GLA_V), smap),
                  pl.BlockSpec((PROJ_TM, RET_V), pmap), pl.BlockSpec((PROJ_TM, RET_V), smap),
                  pl.BlockSpec((PROJ_TM, D_MODEL), pmap), pl.BlockSpec((PROJ_TM, D_MODEL), smap),
                  pl.BlockSpec((GLA_V + RET_V, D_MODEL), fixed, pipeline_mode=pl.Buffered(1)),
                  pl.BlockSpec((1, D_MODEL), fixed),
                  pl.BlockSpec((2, D_MODEL, LANES), lambda i: (0, 0, 0)),
                  pl.BlockSpec((1, LANES), fixed)],
        out_specs=(pl.BlockSpec((PROJ_TM, D_MODEL), row), pl.BlockSpec((PROJ_TM, D_MODEL), row),
                   pl.BlockSpec((PROJ_TM, LANES), row), pl.BlockSpec((SUBLANES, LANES), fixed)),
        compiler_params=pltpu.CompilerParams(dimension_semantics=("arbitrary",),
                                             vmem_limit_bytes=V7X_SCOPED_VMEM_BYTES),
        name="outproj_router",
    )(oa_p, oa_s, or_p, or_s, xp, xs, w_out, ln, w_rt, b_rt)


def _row_dma_wait(buf, sem, s):
    pltpu.make_async_copy(buf.at[s], buf.at[s], sem.at[s]).wait()


def _moe_body(texp_ref, nval_ref, pos_ref, hn_hbm, wg_ref, wu_ref, wd_ref, o_ref,
              inv, zeros_v, xbuf, wg_b, wu_b, wd_b, gsem):
    tm = MOE_TM
    i = pl.program_id(0)
    nt = pl.num_programs(0)
    nv = nval_ref[i]
    nv_prev = jnp.where(i >= 1, nval_ref[jnp.maximum(i - 1, 0)], 0)
    tile_next = jnp.minimum(i + 1, nt - 1)

    def gather_rows(tile, rows, s):
        ds = [inv[tile * tm + r] for r in rows]
        for d, r in zip(ds, rows):
            pltpu.make_async_copy(hn_hbm.at[pl.ds(d >> 1, 1), :],
                                  xbuf.at[s, r // SUBLANES, pl.ds(r % SUBLANES, 1), :], gsem.at[s]).start()

    def tile_compute(s):
        bits = (tm - 1).bit_length()
        order = [int(format(r, "0%db" % bits)[::-1], 2) for r in range(tm)]
        for lo in range(0, tm, SUBLANES):
            gather_rows(tile_next, order[lo:lo + SUBLANES], 1 - s)
        h = xbuf[s].reshape(tm, D_MODEL).astype(BF16)
        hid = _silu(_dot(h, wg_b[...])) * _dot(h, wu_b[...])
        o_ref[...] = _dot(hid.astype(BF16), wd_b[...])

    @pl.when(i == 0)
    def _():
        zeros_v[...] = jnp.zeros_like(zeros_v)
        clear = pltpu.make_async_copy(zeros_v, inv, gsem.at[1])
        clear.start()
        clear.wait()

        def body(a, c):
            inv[pos_ref[a]] = a
            return c
        lax.fori_loop(0, pos_ref.shape[0], body, 0, unroll=8)

        def group(g, c):
            ds = [inv[g * SUBLANES + j] for j in range(SUBLANES)]
            for j in range(SUBLANES):
                pltpu.make_async_copy(hn_hbm.at[pl.ds(ds[j] >> 1, 1), :], xbuf.at[0, g, pl.ds(j, 1), :],
                                      gsem.at[0]).start()
            return c
        lax.fori_loop(0, tm // SUBLANES, group, 0)

    def step(s):
        @pl.when(nv > 0)
        def _():
            _row_dma_wait(xbuf, gsem, s)

            @pl.when((i == 0) | (texp_ref[i] != texp_ref[jnp.maximum(i - 1, 0)]))
            def _():
                wg_b[...] = wg_ref[0].astype(BF16)
                wu_b[...] = wu_ref[0].astype(BF16)
                wd_b[...] = wd_ref[0].astype(BF16)

            tile_compute(s)

            @pl.when(i == nt - 1)
            def _():
                _row_dma_wait(xbuf, gsem, 1 - s)

        @pl.when(nv == 0)
        def _():
            o_ref[...] = jnp.zeros_like(o_ref)

            @pl.when(nv_prev > 0)
            def _():
                _row_dma_wait(xbuf, gsem, s)

    for s in range(2):
        pl.when(i % 2 == s)(functools.partial(step, s))


def _moe(tile_expert, n_valid, pos, hn, w_gate, w_up, w_down):
    n_tiles = tile_expert.shape[0]
    wmap = lambda i, te, nv, pos_: (te[i], 0, 0)
    grouped = (MOE_TM // SUBLANES, SUBLANES, D_MODEL)
    return pl.pallas_call(
        _moe_body,
        out_shape=jax.ShapeDtypeStruct((n_tiles * MOE_TM, D_MODEL), F32),
        grid_spec=pltpu.PrefetchScalarGridSpec(
            num_scalar_prefetch=3,
            grid=(n_tiles,),
            in_specs=[pl.BlockSpec(memory_space=pl.ANY),
                      pl.BlockSpec((1, D_MODEL, D_EXPERT), wmap),
                      pl.BlockSpec((1, D_MODEL, D_EXPERT), wmap),
                      pl.BlockSpec((1, D_EXPERT, D_MODEL), wmap)],
            out_specs=pl.BlockSpec((MOE_TM, D_MODEL), lambda i, te, nv, pos_: (i, 0)),
            scratch_shapes=[pltpu.SMEM((n_tiles * MOE_TM,), I32),
                            pltpu.VMEM((n_tiles * MOE_TM,), I32),
                            pltpu.VMEM((2,) + grouped, F32),
                            pltpu.VMEM((D_MODEL, D_EXPERT), BF16),
                            pltpu.VMEM((D_MODEL, D_EXPERT), BF16),
                            pltpu.VMEM((D_EXPERT, D_MODEL), BF16),
                            pltpu.SemaphoreType.DMA((2,))]),
        compiler_params=pltpu.CompilerParams(dimension_semantics=("arbitrary",),
                                             vmem_limit_bytes=V7X_SCOPED_VMEM_BYTES),
        name="moe",
    )(tile_expert, n_valid, pos, hn, w_gate, w_up, w_down)


def _routing_tables(route, counts, n):
    tm = MOE_TM
    n_tiles = 2 * n // tm + N_EXPERTS
    experts = route[:, 0:2].astype(I32)
    rank = route[:, 4:6].astype(I32)
    counts = counts[0, :N_EXPERTS].astype(I32)
    tiles = (counts + tm - 1) // tm
    tile_end = jnp.cumsum(tiles)
    tile_start = tile_end - tiles
    onehot = experts[:, :, None] == jnp.arange(N_EXPERTS, dtype=I32)[None, None, :]
    pos = jnp.sum(jnp.where(onehot, (tile_start * tm)[None, None, :], 0), axis=-1) + rank
    tidx = jnp.arange(n_tiles, dtype=I32)
    t_exp = jnp.sum((tidx[:, None] >= tile_end[None, :]).astype(I32), axis=1)
    used = tidx < tile_end[-1]
    last_exp = jnp.sum((tile_end[-1] - 1 >= tile_end).astype(I32))
    t_exp = jnp.where(used, t_exp, last_exp)
    n_valid = jnp.where(used, jnp.clip(counts[t_exp] - (tidx - tile_start[t_exp]) * tm, 0, tm), 0)
    return t_exp.astype(I32), n_valid.astype(I32), pos.reshape(2 * n)


def _final_body(pos_ref, x1_ref, route_ref, ln_ref, y_hbm, o_ref, ybuf, sem, *, row0):
    tm = PROJ_TM
    i = pl.program_id(0)
    n_steps = pl.num_programs(0)

    def fetch(step, b):
        a0 = 2 * (row0 + step * tm)
        for g in range(tm // SUBLANES):
            ps = [pos_ref[a0 + 2 * (g * SUBLANES + j) + k] for j in range(SUBLANES) for k in range(2)]
            for j in range(SUBLANES):
                for k in range(2):
                    pltpu.make_async_copy(y_hbm.at[pl.ds(ps[2 * j + k], 1), :],
                                          ybuf.at[b, k * (tm // SUBLANES) + g, pl.ds(j, 1), :], sem.at[b]).start()

    @pl.when(i == 0)
    def _():
        fetch(0, 0)

    def step(b):
        @pl.when(i + 1 < n_steps)
        def _():
            fetch(i + 1, 1 - b)
        _row_dma_wait(ybuf, sem, b)
        y = ybuf[b].reshape(2, tm, D_MODEL)
        z = x1_ref[...] + route_ref[:, 2:3] * y[0] + route_ref[:, 3:4] * y[1]
        o_ref[...] = _rms(z) * ln_ref[...]

    for b in range(2):
        pl.when(i % 2 == b)(functools.partial(step, b))


def _final(pos, x1, route, ln, y_sorted, *, row0, rows):
    base = row0 // PROJ_TM
    return pl.pallas_call(
        functools.partial(_final_body, row0=row0),
        out_shape=jax.ShapeDtypeStruct((rows, D_MODEL), F32),
        grid_spec=pltpu.PrefetchScalarGridSpec(
            num_scalar_prefetch=1,
            grid=(rows // PROJ_TM,),
            in_specs=[pl.BlockSpec((PROJ_TM, D_MODEL), lambda i, pos_: (base + i, 0)),
                      pl.BlockSpec((PROJ_TM, LANES), lambda i, pos_: (base + i, 0)),
                      pl.BlockSpec((1, D_MODEL), lambda i, pos_: (0, 0)),
                      pl.BlockSpec(memory_space=pl.ANY)],
            out_specs=pl.BlockSpec((PROJ_TM, D_MODEL), lambda i, pos_: (i, 0)),
            scratch_shapes=[pltpu.VMEM((2, 2 * PROJ_TM // SUBLANES, SUBLANES, D_MODEL), F32),
                            pltpu.SemaphoreType.DMA((2,))]),
        compiler_params=pltpu.CompilerParams(dimension_semantics=("arbitrary",),
                                             vmem_limit_bytes=V7X_SCOPED_VMEM_BYTES),
        name="final_norm",
    )(pos, x1, route, ln, y_sorted)


def kernel(x_prompt, x_sample, state_gla, state_ret, ln_attn, w_in, w_gk2, b_gk, gla_norm_w, w_out, ln_ffn, w_router_group, b_router_group, w_router_expert, b_router_expert, w_exp_gate, w_exp_up, w_exp_down, ln_final):
    bp, tp, d = x_prompt.shape
    bs, ts, _ = x_sample.shape
    assert d == D_MODEL and w_in.shape == (1, D_MODEL, IN_COLS_SRC)
    n_p, n_s = bp * tp, bs * ts
    n = n_p + n_s
    assert n_p % PROJ_TM == 0 and n_s % PROJ_TM == 0 and tp % GLA_CHUNK == 0 and tp % RET_CHUNK == 0

    xp = x_prompt.reshape(n_p, d)
    xs = x_sample.reshape(n_s, d)

    hi_lo = lambda a: jnp.stack(_split_bf16(a, 2))
    w_in_p, w_lr = _wprep(w_in[0].T)
    w_gk2p = hi_lo(jnp.concatenate([w_gk2[0], jnp.zeros((LANES - GLA_LOWRANK, GLA_QK), F32)], axis=0))
    w_rt = hi_lo(jnp.concatenate([w_router_expert[0], w_router_group[0],
                                  jnp.zeros((d, LANES - N_EXPERTS - N_GROUPS), F32)], axis=1))
    b_rt = jnp.concatenate([b_router_expert[0], b_router_group[0],
                            jnp.zeros((LANES - N_EXPERTS - N_GROUPS,), F32)])[None, :]

    proj = _inproj(xp, xs, ln_attn, w_in_p, w_lr)

    nw = gla_norm_w
    bgk = b_gk
    oa_p, sg_p = _gla(proj, w_gk2p, bgk, nw, None, row0=0, n_seq=bp, seq_len=tp, chunk=GLA_CHUNK, nseq=1)
    oa_s, sg_s = _gla(proj, w_gk2p, bgk, nw, state_gla[0], row0=n_p, n_seq=bs, seq_len=ts, chunk=ts, nseq=8)

    cos_p, sin_p = _rope_tables(jnp.arange(tp, dtype=F32), 1)
    ret_nseq_s = 4
    cos_s, sin_s = _rope_tables(jnp.arange(ts, dtype=F32) + float(PAST_LEN), ret_nseq_s)
    or_p, sr_p = _ret(proj, cos_p, sin_p, None, row0=0, n_seq=bp, seq_len=tp, chunk=RET_CHUNK, nseq=1)
    or_s, sr_s = _ret(proj, cos_s, sin_s, state_ret[0], row0=n_p, n_seq=bs, seq_len=ts, chunk=ts, nseq=ret_nseq_s)

    x1, hn, route, counts = _outproj_router(oa_p, oa_s, or_p, or_s, xp, xs, w_out[0].astype(BF16),
                                            ln_ffn, w_rt, b_rt)

    t_exp, n_valid, pos = _routing_tables(route, counts, n)
    y_sorted = _moe(t_exp, n_valid, pos, hn, w_exp_gate[0], w_exp_up[0], w_exp_down[0])

    ln_f = ln_final[None, :]
    y_p = _final(pos, x1, route, ln_f, y_sorted, row0=0, rows=n_p).reshape(bp, tp, d)
    y_s = _final(pos, x1, route, ln_f, y_sorted, row0=n_p, rows=n_s).reshape(bs, ts, d)
    return (y_p, y_s, sg_p[None], sr_p[None], sg_s[None], sr_s[None])
```

```python
import functools
import math

import jax
import jax.numpy as jnp
from jax import lax
from jax.experimental import pallas as pl
from jax.experimental.pallas import tpu as pltpu

F32 = jnp.float32
BF16 = jnp.bfloat16
I32 = jnp.int32

D_MODEL = 2048
PAST_LEN = 16384
H_GLA, DK_GLA, DV_GLA = 4, 128, 256
GLA_LOWRANK = 16
GLA_GATE_NORM = 16.0
GLA_SUBCHUNK = 16
H_RET, DK_RET, DV_RET = 4, 256, 256
ROPE_BASE = 10000.0
N_GROUPS, EXPERTS_PER_GROUP, D_EXPERT = 4, 8, 512
N_EXPERTS = N_GROUPS * EXPERTS_PER_GROUP
EPS = 1e-6

LANES = 128
SUBLANES = 8
V7X_SCOPED_VMEM_BYTES = 56 * 1024 * 1024

GLA_QK = H_GLA * DK_GLA
GLA_V = H_GLA * DV_GLA
RET_QK = H_RET * DK_RET
RET_V = H_RET * DV_RET
C_QA, C_KA, C_VA, C_GA = 0, GLA_QK, 2 * GLA_QK, 2 * GLA_QK + GLA_V
C_QR = C_GA + GLA_V
C_KR, C_VR, C_GR = C_QR + RET_QK, C_QR + 2 * RET_QK, C_QR + 2 * RET_QK + RET_V
C_LR = C_GR + RET_V
P_COLS = C_LR + LANES
IN_COLS_SRC = 2 * GLA_QK + GLA_V + GLA_LOWRANK + GLA_V + 2 * RET_QK + 2 * RET_V

PROJ_TM = 256
W_CHUNK = 1024
MOE_TM = 256
GLA_CHUNK = 128
RET_CHUNK = 128

def _dot(a, b):
    return jnp.dot(a, b, preferred_element_type=F32)


def _dot_nt(a, b):
    return lax.dot_general(a, b, (((1,), (1,)), ((), ())), preferred_element_type=F32)


def _dot_tn(a, b):
    return lax.dot_general(a, b, (((0,), (0,)), ((), ())), preferred_element_type=F32)


def _split_bf16(x, parts):
    out = []
    for _ in range(parts):
        p = x.astype(BF16)
        out.append(p)
        x = x - p.astype(F32)
    return out


def _dot_split(a, b_hi, b_lo):
    a_hi, a_lo = _split_bf16(a, 2)
    return _dot(a_hi, b_hi) + (_dot(a_lo, b_hi) + _dot(a_hi, b_lo))


def _rms(x):
    return x * lax.rsqrt(jnp.mean(x * x, axis=-1, keepdims=True) + EPS)


def _silu(x):
    return x * jax.nn.sigmoid(x)


def _wprep_body(a_ref, b_ref, o_ref, olr_ref):
    j = pl.program_id(0)
    a = a_ref[...]
    b = b_ref[...]
    shifted = jnp.concatenate([a[GLA_LOWRANK:, :], b], axis=0)
    o_ref[...] = jnp.where(j >= C_GA // W_CHUNK, shifted, a).T.astype(BF16)

    @pl.when(j == C_GA // W_CHUNK - 1)
    def _():
        padded = jnp.concatenate([b, jnp.zeros((LANES - GLA_LOWRANK, D_MODEL), F32)], axis=0)
        olr_ref[...] = padded.T.astype(BF16)


def _wprep(wt):
    per = W_CHUNK // GLA_LOWRANK
    return pl.pallas_call(
        _wprep_body,
        out_shape=(jax.ShapeDtypeStruct((D_MODEL, C_LR), BF16), jax.ShapeDtypeStruct((D_MODEL, LANES), BF16)),
        grid=(C_LR // W_CHUNK,),
        in_specs=[pl.BlockSpec((W_CHUNK, D_MODEL), lambda j: (j, 0)),
                  pl.BlockSpec((GLA_LOWRANK, D_MODEL), lambda j: (per * (j + 1), 0))],
        out_specs=(pl.BlockSpec((D_MODEL, W_CHUNK), lambda j: (0, j)),
                   pl.BlockSpec((D_MODEL, LANES), lambda j: (0, 0))),
        compiler_params=pltpu.CompilerParams(dimension_semantics=("arbitrary",),
                                             vmem_limit_bytes=V7X_SCOPED_VMEM_BYTES),
        name="wprep",
    )(wt, wt)


def _inproj_body(xp_ref, xs_ref, ln_ref, w_ref, wlr_ref, o_ref, *, n_prompt_blocks):
    x = jnp.where(pl.program_id(0) < n_prompt_blocks, xp_ref[...], xs_ref[...])
    h = (_rms(x) * ln_ref[...]).astype(BF16)
    for c in range(C_LR // W_CHUNK):
        o_ref[:, c * W_CHUNK:(c + 1) * W_CHUNK] = _dot(h, w_ref[:, c * W_CHUNK:(c + 1) * W_CHUNK])
    o_ref[:, C_LR:] = _dot(h, wlr_ref[...])


def _inproj(xp, xs, ln, w, w_lr):
    nbp = xp.shape[0] // PROJ_TM
    n = xp.shape[0] + xs.shape[0]
    fixed = lambda i: (0, 0)
    return pl.pallas_call(
        functools.partial(_inproj_body, n_prompt_blocks=nbp),
        out_shape=jax.ShapeDtypeStruct((n, P_COLS), F32),
        grid=(n // PROJ_TM,),
        in_specs=[pl.BlockSpec((PROJ_TM, D_MODEL), lambda i: (jnp.minimum(i, nbp - 1), 0)),
                  pl.BlockSpec((PROJ_TM, D_MODEL), lambda i: (jnp.maximum(i - nbp, 0), 0)),
                  pl.BlockSpec((1, D_MODEL), fixed),
                  pl.BlockSpec((D_MODEL, C_LR), fixed, pipeline_mode=pl.Buffered(1)),
                  pl.BlockSpec((D_MODEL, LANES), fixed)],
        out_specs=pl.BlockSpec((PROJ_TM, P_COLS), lambda i: (i, 0)),
        compiler_params=pltpu.CompilerParams(dimension_semantics=("arbitrary",),
                                             vmem_limit_bytes=V7X_SCOPED_VMEM_BYTES),
        name="inproj",
    )(xp, xs, ln, w, w_lr)


def _seg_cumsum(x, seg):
    pos = lax.broadcasted_iota(I32, x.shape, 0) & (seg - 1)
    s = 1
    while s < seg:
        x = x + jnp.where(pos >= s, pltpu.roll(x, s, axis=0), 0.0)
        s *= 2
    return x


def _block_ends(p, m):
    rows, w = p.shape
    return jnp.concatenate(
        [jnp.broadcast_to(p[j * m + m - 1:j * m + m, :], (m, w)) for j in range(rows // m)], axis=0)


def _block_starts(p, m, chunk):
    rows, w = p.shape
    pieces = []
    for j in range(rows // m):
        if (j * m) % chunk == 0:
            pieces.append(jnp.zeros((m, w), F32))
        else:
            pieces.append(jnp.broadcast_to(p[j * m - 1:j * m, :], (m, w)))
    return jnp.concatenate(pieces, axis=0)


def _gla_body(*refs, chunk, nseq, has_init):
    if has_init:
        (q_ref, k_ref, v_ref, g_ref, lr_ref, wgk_ref, bgk_ref, nw_ref, s0_ref,
         o_ref, sout_ref, s_scr) = refs
    else:
        (q_ref, k_ref, v_ref, g_ref, lr_ref, wgk_ref, bgk_ref, nw_ref,
         o_ref, sout_ref, s_scr) = refs
    t = pl.program_id(1)

    @pl.when(t == 0)
    def _():
        if has_init:
            s_scr[...] = s0_ref[...]
        else:
            s_scr[...] = jnp.zeros_like(s_scr)

    x = _dot_split(lr_ref[...], wgk_ref[0], wgk_ref[1]) + bgk_ref[...]
    log_g = -(jnp.maximum(-x, 0.0) + jnp.log1p(jnp.exp(-jnp.abs(x)))) / GLA_GATE_NORM
    p = _seg_cumsum(log_g, chunk)
    p_end = _block_ends(p, chunk)
    eq_c = jnp.exp(p)
    ek_c = jnp.exp(p_end - p)
    sub = min(GLA_SUBCHUNK, chunk)
    p_sub = p - _block_starts(p, sub, chunk)
    eq_d = jnp.exp(p_sub)
    ek_d = jnp.exp(-p_sub)
    levels = []
    m = sub
    while m < chunk:
        levels.append((m, jnp.exp(p - _block_starts(p, m, chunk)), jnp.exp(_block_ends(p, m) - p)))
        m *= 2

    ti = lax.broadcasted_iota(I32, (chunk, chunk), 0)
    si = lax.broadcasted_iota(I32, (chunk, chunk), 1)
    blk = lambda idx, size: idx >> (size.bit_length() - 1)
    mask_d = (blk(ti, sub) == blk(si, sub)) & (si <= ti)
    masks = [((blk(ti, m) & 1) == 1) & (blk(si, m) == blk(ti, m) - 1) for (m, _, _) in levels]
    sel8 = (lax.broadcasted_iota(I32, (8, DV_GLA), 0) == 0).astype(BF16)
    nw = nw_ref[...]

    outs = [[None] * nseq for _ in range(H_GLA)]
    for b in range(nseq):
        r0 = b * chunk
        rows = slice(r0, r0 + chunk)
        for h in range(H_GLA):
            kc = slice(h * DK_GLA, (h + 1) * DK_GLA)
            vc = slice(h * DV_GLA, (h + 1) * DV_GLA)
            q = q_ref[rows, kc] * (DK_GLA ** -0.5)
            k = k_ref[rows, kc]
            v = v_ref[rows, vc].astype(BF16)
            att = jnp.where(mask_d, _dot_nt((q * eq_d[rows, kc]).astype(BF16),
                                            (k * ek_d[rows, kc]).astype(BF16)), 0.0)
            for (m, eq_m, ek_m), mask in zip(levels, masks):
                att = jnp.where(mask, _dot_nt((q * eq_m[rows, kc]).astype(BF16),
                                              (k * ek_m[rows, kc]).astype(BF16)), att)
            s_old = s_scr[b, h]
            o = _dot(att.astype(BF16), v) + _dot((q * eq_c[rows, kc]).astype(BF16), s_old.astype(BF16))
            kv = _dot_tn((k * ek_c[rows, kc]).astype(BF16), v)
            tot = jnp.broadcast_to(p[r0 + chunk - 1:r0 + chunk, kc], (8, DK_GLA))
            t_hi, t_mid, t_lo = _split_bf16(tot, 3)
            decay = jnp.exp(_dot_tn(t_hi, sel8) + (_dot_tn(t_mid, sel8) + _dot_tn(t_lo, sel8)))
            s_scr[b, h] = decay * s_old + kv
            outs[h][b] = _rms(o) * nw * _silu(g_ref[rows, vc])
    for h in range(H_GLA):
        o_ref[:, h * DV_GLA:(h + 1) * DV_GLA] = jnp.concatenate(outs[h], axis=0).astype(BF16)

    @pl.when(t == pl.num_programs(1) - 1)
    def _():
        sout_ref[...] = s_scr[...]


def _gla(proj, w_gk2p, b_gk, nw, s0, *, row0, n_seq, seq_len, chunk, nseq):
    rows = nseq * chunk
    nt = seq_len // chunk
    base = row0 // rows

    def rmap(col):
        return lambda b, t: (base + b * nt + t, col)

    in_specs = [pl.BlockSpec((rows, GLA_QK), rmap(C_QA // GLA_QK)),
                pl.BlockSpec((rows, GLA_QK), rmap(C_KA // GLA_QK)),
                pl.BlockSpec((rows, GLA_V), rmap(C_VA // GLA_V)),
                pl.BlockSpec((rows, GLA_V), rmap(C_GA // GLA_V)),
                pl.BlockSpec((rows, LANES), rmap(C_LR // LANES)),
                pl.BlockSpec((2, LANES, GLA_QK), lambda b, t: (0, 0, 0)),
                pl.BlockSpec((1, GLA_QK), lambda b, t: (0, 0)),
                pl.BlockSpec((1, DV_GLA), lambda b, t: (0, 0))]
    args = [proj, proj, proj, proj, proj, w_gk2p, b_gk, nw]
    state_spec = pl.BlockSpec((nseq, H_GLA, DK_GLA, DV_GLA), lambda b, t: (b, 0, 0, 0))
    if s0 is not None:
        in_specs.append(state_spec)
        args.append(s0)
    return pl.pallas_call(
        functools.partial(_gla_body, chunk=chunk, nseq=nseq, has_init=s0 is not None),
        out_shape=(jax.ShapeDtypeStruct((n_seq * seq_len, GLA_V), BF16),
                   jax.ShapeDtypeStruct((n_seq, H_GLA, DK_GLA, DV_GLA), F32)),
        grid=(n_seq // nseq, nt),
        in_specs=in_specs,
        out_specs=(pl.BlockSpec((rows, GLA_V), lambda b, t: (b * nt + t, 0)), state_spec),
        scratch_shapes=[pltpu.VMEM((nseq, H_GLA, DK_GLA, DV_GLA), F32)],
        compiler_params=pltpu.CompilerParams(dimension_semantics=("arbitrary", "arbitrary"),
                                             vmem_limit_bytes=V7X_SCOPED_VMEM_BYTES),
        name="gla_init" if s0 is not None else "gla",
    )(*args)


def _ret_body(*refs, chunk, nseq, has_init):
    if has_init:
        (q_ref, k_ref, v_ref, g_ref, cos_ref, sin_ref, s0_ref, o_ref, sout_ref, s_scr) = refs
    else:
        (q_ref, k_ref, v_ref, g_ref, cos_ref, sin_ref, o_ref, sout_ref, s_scr) = refs
    t = pl.program_id(1)

    @pl.when(t == 0)
    def _():
        if has_init:
            s_scr[...] = s0_ref[...]
        else:
            s_scr[...] = jnp.zeros_like(s_scr)

    half = DK_RET // 2
    ti = lax.broadcasted_iota(I32, (chunk, chunk), 0)
    si = lax.broadcasted_iota(I32, (chunk, chunk), 1)
    diff = (ti - si).astype(F32)
    idx = lax.broadcasted_iota(I32, (chunk, 1), 0).astype(F32)

    outs = [[None] * nseq for _ in range(H_RET)]
    for h in range(H_RET):
        lg = math.log(1.0 - 2.0 ** (-5.0 - h))
        dmat = jnp.where(diff >= 0, jnp.exp(lg * jnp.maximum(diff, 0.0)), 0.0)
        q_dec = jnp.exp(lg * (idx + 1.0))
        k_dec = jnp.exp(lg * (chunk - 1.0 - idx))
        c_dec = math.exp(lg * chunk)
        for b in range(nseq):
            rows = slice(b * chunk, (b + 1) * chunk)
            cos = cos_ref[rows, :]
            sin = sin_ref[rows, :]
            c1 = slice(h * DK_RET, h * DK_RET + half)
            c2 = slice(h * DK_RET + half, (h + 1) * DK_RET)
            vc = slice(h * DV_RET, (h + 1) * DV_RET)
            q1, q2 = q_ref[rows, c1], q_ref[rows, c2]
            k1, k2 = k_ref[rows, c1], k_ref[rows, c2]
            q = jnp.concatenate([q1 * cos - q2 * sin, q1 * sin + q2 * cos], axis=-1)
            k = jnp.concatenate([k1 * cos - k2 * sin, k1 * sin + k2 * cos], axis=-1) * (DK_RET ** -0.5)
            v = v_ref[rows, vc].astype(BF16)
            qb = q.astype(BF16)
            s_old = s_scr[b, h]
            att = _dot_nt(qb, k.astype(BF16)) * dmat
            o = _dot(att.astype(BF16), v) + _dot(qb, s_old.astype(BF16)) * q_dec
            s_scr[b, h] = c_dec * s_old + _dot_tn((k * k_dec).astype(BF16), v)
            outs[h][b] = _rms(o) * _silu(g_ref[rows, vc])
    for h in range(H_RET):
        o_ref[:, h * DV_RET:(h + 1) * DV_RET] = jnp.concatenate(outs[h], axis=0).astype(BF16)

    @pl.when(t == pl.num_programs(1) - 1)
    def _():
        sout_ref[...] = s_scr[...]


def _ret(proj, cos, sin, s0, *, row0, n_seq, seq_len, chunk, nseq):
    rows = nseq * chunk
    nt = seq_len // chunk
    base = row0 // rows

    def rmap(col):
        return lambda b, t: (base + b * nt + t, col)

    tab_spec = pl.BlockSpec((rows, DK_RET // 2), lambda b, t: (t, 0))
    in_specs = [pl.BlockSpec((rows, RET_QK), rmap(C_QR // RET_QK)),
                pl.BlockSpec((rows, RET_QK), rmap(C_KR // RET_QK)),
                pl.BlockSpec((rows, RET_V), rmap(C_VR // RET_V)),
                pl.BlockSpec((rows, RET_V), rmap(C_GR // RET_V)),
                tab_spec, tab_spec]
    args = [proj, proj, proj, proj, cos, sin]
    state_spec = pl.BlockSpec((nseq, H_RET, DK_RET, DV_RET), lambda b, t: (b, 0, 0, 0))
    if s0 is not None:
        in_specs.append(state_spec)
        args.append(s0)
    return pl.pallas_call(
        functools.partial(_ret_body, chunk=chunk, nseq=nseq, has_init=s0 is not None),
        out_shape=(jax.ShapeDtypeStruct((n_seq * seq_len, RET_V), BF16),
                   jax.ShapeDtypeStruct((n_seq, H_RET, DK_RET, DV_RET), F32)),
        grid=(n_seq // nseq, nt),
        in_specs=in_specs,
        out_specs=(pl.BlockSpec((rows, RET_V), lambda b, t: (b * nt + t, 0)), state_spec),
        scratch_shapes=[pltpu.VMEM((nseq, H_RET, DK_RET, DV_RET), F32)],
        compiler_params=pltpu.CompilerParams(dimension_semantics=("arbitrary", "arbitrary"),
                                             vmem_limit_bytes=V7X_SCOPED_VMEM_BYTES),
        name="ret_init" if s0 is not None else "ret",
    )(*args)


def _rope_tables(pos, reps):
    half = DK_RET // 2
    inv = ROPE_BASE ** (-jnp.arange(half, dtype=F32) / half)
    ang = pos[:, None] * inv[None, :]
    return jnp.tile(jnp.cos(ang), (reps, 1)), jnp.tile(jnp.sin(ang), (reps, 1))


def _outproj_body(oap_ref, oas_ref, orp_ref, ors_ref, xp_ref, xs_ref, w_ref, ln_ref, wrt_ref, brt_ref,
                  x1_ref, hn_ref, route_ref, counts_ref, *, n_prompt_blocks):
    step = pl.program_id(0)
    is_prompt = step < n_prompt_blocks
    oa = jnp.where(is_prompt, oap_ref[...], oas_ref[...])
    orr = jnp.where(is_prompt, orp_ref[...], ors_ref[...])
    x = jnp.where(is_prompt, xp_ref[...], xs_ref[...])
    x1 = x + _dot(oa, w_ref[:GLA_V, :]) + _dot(orr, w_ref[GLA_V:, :])
    x1_ref[...] = x1
    hn = _rms(x1) * ln_ref[...]
    hn_ref[...] = hn

    logits = _dot_split(hn, wrt_ref[0], wrt_ref[1]) + brt_ref[...]
    lane_i = lax.broadcasted_iota(I32, logits.shape, 1)
    lane = lane_i.astype(F32)
    grp = (lane_i >> (EXPERTS_PER_GROUP.bit_length() - 1)).astype(F32)
    neg = -jnp.inf
    far = float(LANES)

    gl = jnp.where((lane_i >= N_EXPERTS) & (lane_i < N_EXPERTS + N_GROUPS), logits, neg)
    gmax = jnp.max(gl, axis=-1, keepdims=True)
    g_w = 1.0 / jnp.sum(jnp.exp(gl - gmax), axis=-1, keepdims=True)
    g_idx = jnp.min(jnp.where(gl == gmax, lane, far), axis=-1, keepdims=True) - float(N_EXPERTS)

    el = jnp.where((lane_i < N_EXPERTS) & (grp == g_idx), logits, neg)
    m1 = jnp.max(el, axis=-1, keepdims=True)
    esum = jnp.sum(jnp.exp(el - m1), axis=-1, keepdims=True)
    i1 = jnp.min(jnp.where(el == m1, lane, far), axis=-1, keepdims=True)
    el2 = jnp.where(lane == i1, neg, el)
    m2 = jnp.max(el2, axis=-1, keepdims=True)
    i2 = jnp.min(jnp.where(el2 == m2, lane, far), axis=-1, keepdims=True)
    p1 = 1.0 / esum
    p2 = jnp.exp(m2 - m1) / esum
    gate1 = g_w * (p1 / (p1 + p2))
    gate2 = g_w * (p2 / (p1 + p2))
    @pl.when(step == 0)
    def _():
        counts_ref[...] = jnp.zeros_like(counts_ref)

    tm = logits.shape[0]
    earlier = (lax.broadcasted_iota(I32, (tm, tm), 0) > lax.broadcasted_iota(I32, (tm, tm), 1)).astype(BF16)
    running = counts_ref[0:1, :]
    ranks = []
    for idx in (i1, i2):
        onehot = (lane == idx).astype(F32)
        before = _dot(earlier, onehot.astype(BF16)) + running
        ranks.append(jnp.sum(onehot * before, axis=-1, keepdims=True))
        running = running + jnp.sum(onehot, axis=0, keepdims=True)
    counts_ref[...] = jnp.broadcast_to(running, counts_ref.shape)

    route_ref[...] = jnp.where(lane_i == 0, i1,
                               jnp.where(lane_i == 1, i2,
                                         jnp.where(lane_i == 2, gate1,
                                                   jnp.where(lane_i == 3, gate2,
                                                             jnp.where(lane_i == 4, ranks[0],
                                                                       jnp.where(lane_i == 5, ranks[1], 0.0))))))


def _outproj_router(oa_p, oa_s, or_p, or_s, xp, xs, w_out, ln, w_rt, b_rt):
    n = xp.shape[0] + xs.shape[0]
    nbp = xp.shape[0] // PROJ_TM
    pmap = lambda i: (jnp.minimum(i, nbp - 1), 0)
    smap = lambda i: (jnp.maximum(i - nbp, 0), 0)
    row = lambda i: (i, 0)
    fixed = lambda i: (0, 0)
    return pl.pallas_call(
        functools.partial(_outproj_body, n_prompt_blocks=nbp),
        out_shape=(jax.ShapeDtypeStruct((n, D_MODEL), F32),
                   jax.ShapeDtypeStruct((n, D_MODEL), F32),
                   jax.ShapeDtypeStruct((n, LANES), F32),
                   jax.ShapeDtypeStruct((SUBLANES, LANES), F32)),
        grid=(n // PROJ_TM,),
        in_specs=[pl.BlockSpec((PROJ_TM, GLA_V), pmap), pl.BlockSpec((PROJ_TM, GLA_V), smap),
                  pl.BlockSpec((PROJ_TM, RET_V), pmap), pl.BlockSpec((PROJ_TM, RET_V), smap),
                  pl.BlockSpec((PROJ_TM, D_MODEL), pmap), pl.BlockSpec((PROJ_TM, D_MODEL), smap),
                  pl.BlockSpec((GLA_V + RET_V, D_MODEL), fixed, pipeline_mode=pl.Buffered(1)),
                  pl.BlockSpec((1, D_MODEL), fixed),
                  pl.BlockSpec((2, D_MODEL, LANES), lambda i: (0, 0, 0)),
                  pl.BlockSpec((1, LANES), fixed)],
        out_specs=(pl.BlockSpec((PROJ_TM, D_MODEL), row), pl.BlockSpec((PROJ_TM, D_MODEL), row),
                   pl.BlockSpec((PROJ_TM, LANES), row), pl.BlockSpec((SUBLANES, LANES), fixed)),
        compiler_params=pltpu.CompilerParams(dimension_semantics=("arbitrary",),
                                             vmem_limit_bytes=V7X_SCOPED_VMEM_BYTES),
        name="outproj_router",
    )(oa_p, oa_s, or_p, or_s, xp, xs, w_out, ln, w_rt, b_rt)


def _row_dma_wait(buf, sem, s):
    pltpu.make_async_copy(buf.at[s], buf.at[s], sem.at[s]).wait()


MOE_XBUFS = 3


def _moe_body(texp_ref, nval_ref, wslot_ref, nexp_ref, pos_ref, hn_hbm, wg_hbm, wu_hbm, wd_hbm, o_ref,
              inv, zeros_v, xbuf, wg_f, wu_f, wd_f, wg_b, wu_b, wd_b, gsem, wsem):
    tm = MOE_TM
    i = pl.program_id(0)
    nt = pl.num_programs(0)
    nv = nval_ref[i]
    nv_prev = jnp.where(i >= 1, nval_ref[jnp.maximum(i - 1, 0)], 0)

    def gather_tile(tile, s):
        for lo in range(0, tm, SUBLANES):
            ds = [inv[tile * tm + lo + j] for j in range(SUBLANES)]
            for j in range(SUBLANES):
                pltpu.make_async_copy(hn_hbm.at[pl.ds(ds[j] >> 1, 1), :],
                                      xbuf.at[s, lo // SUBLANES, pl.ds(j, 1), :], gsem.at[s]).start()

    def weight_copies(e, p):
        return [pltpu.make_async_copy(src.at[e], dst.at[p], wsem.at[p])
                for src, dst in ((wg_hbm, wg_f), (wu_hbm, wu_f), (wd_hbm, wd_f))]

    @pl.when(i == 0)
    def _():
        for c in weight_copies(texp_ref[0], wslot_ref[0]):
            c.start()
        zeros_v[...] = jnp.zeros_like(zeros_v)
        clear = pltpu.make_async_copy(zeros_v, inv, gsem.at[0])
        clear.start()
        clear.wait()

        def body(a, c):
            inv[pos_ref[a]] = a
            return c
        lax.fori_loop(0, pos_ref.shape[0], body, 0, unroll=8)
        gather_tile(0, 0)
        gather_tile(1, 1)

    def step(s):
        @pl.when(nv > 0)
        def _():
            @pl.when(i + 2 < nt)
            def _():
                gather_tile(i + 2, (s + 2) % MOE_XBUFS)

            @pl.when((i == 0) | (texp_ref[i] != texp_ref[jnp.maximum(i - 1, 0)]))
            def _():
                p = wslot_ref[i]
                for c in weight_copies(0, p):
                    c.wait()
                wg_b[...] = wg_f[p].astype(BF16)
                wu_b[...] = wu_f[p].astype(BF16)
                wd_b[...] = wd_f[p].astype(BF16)

                @pl.when(nexp_ref[i] >= 0)
                def _():
                    for c in weight_copies(nexp_ref[i], 1 - p):
                        c.start()

            _row_dma_wait(xbuf, gsem, s)
            h = xbuf[s].reshape(tm, D_MODEL).astype(BF16)
            hid = _silu(_dot(h, wg_b[...])) * _dot(h, wu_b[...])
            o_ref[...] = _dot(hid.astype(BF16), wd_b[...])

        @pl.when(nv == 0)
        def _():
            o_ref[...] = jnp.zeros_like(o_ref)

            @pl.when(nv_prev > 0)
            def _():
                _row_dma_wait(xbuf, gsem, s)

                @pl.when(i + 1 < nt)
                def _():
                    _row_dma_wait(xbuf, gsem, (s + 1) % MOE_XBUFS)

    for s in range(MOE_XBUFS):
        pl.when(i % MOE_XBUFS == s)(functools.partial(step, s))


def _moe(tile_expert, n_valid, w_slot, next_expert, pos, hn, w_gate, w_up, w_down):
    n_tiles = tile_expert.shape[0]
    assert n_tiles >= 2
    grouped = (MOE_TM // SUBLANES, SUBLANES, D_MODEL)
    any_spec = pl.BlockSpec(memory_space=pl.ANY)
    return pl.pallas_call(
        _moe_body,
        out_shape=jax.ShapeDtypeStruct((n_tiles * MOE_TM, D_MODEL), F32),
        grid_spec=pltpu.PrefetchScalarGridSpec(
            num_scalar_prefetch=5,
            grid=(n_tiles,),
            in_specs=[any_spec, any_spec, any_spec, any_spec],
            out_specs=pl.BlockSpec((MOE_TM, D_MODEL), lambda i, *_: (i, 0)),
            scratch_shapes=[pltpu.SMEM((n_tiles * MOE_TM,), I32),
                            pltpu.VMEM((n_tiles * MOE_TM,), I32),
                            pltpu.VMEM((MOE_XBUFS,) + grouped, F32),
                            pltpu.VMEM((2, D_MODEL, D_EXPERT), F32),
                            pltpu.VMEM((2, D_MODEL, D_EXPERT), F32),
                            pltpu.VMEM((2, D_EXPERT, D_MODEL), F32),
                            pltpu.VMEM((D_MODEL, D_EXPERT), BF16),
                            pltpu.VMEM((D_MODEL, D_EXPERT), BF16),
                            pltpu.VMEM((D_EXPERT, D_MODEL), BF16),
                            pltpu.SemaphoreType.DMA((MOE_XBUFS,)),
                            pltpu.SemaphoreType.DMA((2,))]),
        compiler_params=pltpu.CompilerParams(dimension_semantics=("arbitrary",),
                                             vmem_limit_bytes=V7X_SCOPED_VMEM_BYTES),
        name="moe",
    )(tile_expert, n_valid, w_slot, next_expert, pos, hn, w_gate, w_up, w_down)


def _routing_tables(route, counts, n):
    tm = MOE_TM
    n_tiles = 2 * n // tm + N_EXPERTS
    experts = route[:, 0:2].astype(I32)
    rank = route[:, 4:6].astype(I32)
    counts = counts[0, :N_EXPERTS].astype(I32)
    tiles = (counts + tm - 1) // tm
    tile_end = jnp.cumsum(tiles)
    tile_start = tile_end - tiles
    onehot = experts[:, :, None] == jnp.arange(N_EXPERTS, dtype=I32)[None, None, :]
    pos = jnp.sum(jnp.where(onehot, (tile_start * tm)[None, None, :], 0), axis=-1) + rank
    tidx = jnp.arange(n_tiles, dtype=I32)
    t_exp = jnp.sum((tidx[:, None] >= tile_end[None, :]).astype(I32), axis=1)
    used = tidx < tile_end[-1]
    last_exp = jnp.sum((tile_end[-1] - 1 >= tile_end).astype(I32))
    t_exp = jnp.where(used, t_exp, last_exp)
    n_valid = jnp.where(used, jnp.clip(counts[t_exp] - (tidx - tile_start[t_exp]) * tm, 0, tm), 0)
    eidx = jnp.arange(N_EXPERTS, dtype=I32)
    owns = tiles > 0
    slot_e = (jnp.cumsum(owns.astype(I32)) - 1) & 1
    later = jnp.where(owns[None, :] & (eidx[None, :] > eidx[:, None]), eidx[None, :], N_EXPERTS)
    next_e = jnp.min(later, axis=1)
    next_e = jnp.where(next_e < N_EXPERTS, next_e, -1)
    return (t_exp.astype(I32), n_valid.astype(I32), slot_e[t_exp].astype(I32), next_e[t_exp].astype(I32),
            pos.reshape(2 * n))


def _final_body(pos_ref, x1_ref, route_ref, ln_ref, y_hbm, o_ref, ybuf, sem, *, row0):
    tm = PROJ_TM
    i = pl.program_id(0)
    n_steps = pl.num_programs(0)

    def fetch(step, b):
        a0 = 2 * (row0 + step * tm)
        for g in range(tm // SUBLANES):
            ps = [pos_ref[a0 + 2 * (g * SUBLANES + j) + k] for j in range(SUBLANES) for k in range(2)]
            for j in range(SUBLANES):
                for k in range(2):
                    pltpu.make_async_copy(y_hbm.at[pl.ds(ps[2 * j + k], 1), :],
                                          ybuf.at[b, k * (tm // SUBLANES) + g, pl.ds(j, 1), :], sem.at[b]).start()

    @pl.when(i == 0)
    def _():
        fetch(0, 0)

    def step(b):
        @pl.when(i + 1 < n_steps)
        def _():
            fetch(i + 1, 1 - b)
        _row_dma_wait(ybuf, sem, b)
        y = ybuf[b].reshape(2, tm, D_MODEL)
        z = x1_ref[...] + route_ref[:, 2:3] * y[0] + route_ref[:, 3:4] * y[1]
        o_ref[...] = _rms(z) * ln_ref[...]

    for b in range(2):
        pl.when(i % 2 == b)(functools.partial(step, b))


def _final(pos, x1, route, ln, y_sorted, *, row0, rows):
    base = row0 // PROJ_TM
    return pl.pallas_call(
        functools.partial(_final_body, row0=row0),
        out_shape=jax.ShapeDtypeStruct((rows, D_MODEL), F32),
        grid_spec=pltpu.PrefetchScalarGridSpec(
            num_scalar_prefetch=1,
            grid=(rows // PROJ_TM,),
            in_specs=[pl.BlockSpec((PROJ_TM, D_MODEL), lambda i, pos_: (base + i, 0)),
                      pl.BlockSpec((PROJ_TM, LANES), lambda i, pos_: (base + i, 0)),
                      pl.BlockSpec((1, D_MODEL), lambda i, pos_: (0, 0)),
                      pl.BlockSpec(memory_space=pl.ANY)],
            out_specs=pl.BlockSpec((PROJ_TM, D_MODEL), lambda i, pos_: (i, 0)),
            scratch_shapes=[pltpu.VMEM((2, 2 * PROJ_TM // SUBLANES, SUBLANES, D_MODEL), F32),
                            pltpu.SemaphoreType.DMA((2,))]),
        compiler_params=pltpu.CompilerParams(dimension_semantics=("arbitrary",),
                                             vmem_limit_bytes=V7X_SCOPED_VMEM_BYTES),
        name="final_norm",
    )(pos, x1, route, ln, y_sorted)


def kernel(x_prompt, x_sample, state_gla, state_ret, ln_attn, w_in, w_gk2, b_gk, gla_norm_w, w_out, ln_ffn, w_router_group, b_router_group, w_router_expert, b_router_expert, w_exp_gate, w_exp_up, w_exp_down, ln_final):
    bp, tp, d = x_prompt.shape
    bs, ts, _ = x_sample.shape
    assert d == D_MODEL and w_in.shape == (1, D_MODEL, IN_COLS_SRC)
    n_p, n_s = bp * tp, bs * ts
    n = n_p + n_s
    assert n_p % PROJ_TM == 0 and n_s % PROJ_TM == 0 and tp % GLA_CHUNK == 0 and tp % RET_CHUNK == 0

    xp = x_prompt.reshape(n_p, d)
    xs = x_sample.reshape(n_s, d)

    hi_lo = lambda a: jnp.stack(_split_bf16(a, 2))
    w_in_p, w_lr = _wprep(w_in[0].T)
    w_gk2p = hi_lo(jnp.concatenate([w_gk2[0], jnp.zeros((LANES - GLA_LOWRANK, GLA_QK), F32)], axis=0))
    w_rt = hi_lo(jnp.concatenate([w_router_expert[0], w_router_group[0],
                                  jnp.zeros((d, LANES - N_EXPERTS - N_GROUPS), F32)], axis=1))
    b_rt = jnp.concatenate([b_router_expert[0], b_router_group[0],
                            jnp.zeros((LANES - N_EXPERTS - N_GROUPS,), F32)])[None, :]

    proj = _inproj(xp, xs, ln_attn, w_in_p, w_lr)

    nw = gla_norm_w
    bgk = b_gk
    oa_p, sg_p = _gla(proj, w_gk2p, bgk, nw, None, row0=0, n_seq=bp, seq_len=tp, chunk=GLA_CHUNK, nseq=1)
    oa_s, sg_s = _gla(proj, w_gk2p, bgk, nw, state_gla[0], row0=n_p, n_seq=bs, seq_len=ts, chunk=ts, nseq=8)

    cos_p, sin_p = _rope_tables(jnp.arange(tp, dtype=F32), 1)
    ret_nseq_s = 4
    cos_s, sin_s = _rope_tables(jnp.arange(ts, dtype=F32) + float(PAST_LEN), ret_nseq_s)
    or_p, sr_p = _ret(proj, cos_p, sin_p, None, row0=0, n_seq=bp, seq_len=tp, chunk=RET_CHUNK, nseq=1)
    or_s, sr_s = _ret(proj, cos_s, sin_s, state_ret[0], row0=n_p, n_seq=bs, seq_len=ts, chunk=ts, nseq=ret_nseq_s)

    x1, hn, route, counts = _outproj_router(oa_p, oa_s, or_p, or_s, xp, xs, w_out[0].astype(BF16),
                                            ln_ffn, w_rt, b_rt)

    t_exp, n_valid, w_slot, next_exp, pos = _routing_tables(route, counts, n)
    y_sorted = _moe(t_exp, n_valid, w_slot, next_exp, pos, hn, w_exp_gate[0], w_exp_up[0], w_exp_down[0])

    ln_f = ln_final[None, :]
    y_p = _final(pos, x1, route, ln_f, y_sorted, row0=0, rows=n_p).reshape(bp, tp, d)
    y_s = _final(pos, x1, route, ln_f, y_sorted, row0=n_p, rows=n_s).reshape(bs, ts, d)
    return (y_p, y_s, sg_p[None], sr_p[None], sg_s[None], sr_s[None])
```

```python
import functools
import math

import jax
import jax.numpy as jnp
from jax import lax
from jax.experimental import pallas as pl
from jax.experimental.pallas import tpu as pltpu

F32 = jnp.float32
BF16 = jnp.bfloat16
I32 = jnp.int32

D_MODEL = 2048
PAST_LEN = 16384
H_GLA, DK_GLA, DV_GLA = 4, 128, 256
GLA_LOWRANK = 16
GLA_GATE_NORM = 16.0
GLA_SUBCHUNK = 16
H_RET, DK_RET, DV_RET = 4, 256, 256
ROPE_BASE = 10000.0
N_GROUPS, EXPERTS_PER_GROUP, D_EXPERT = 4, 8, 512
N_EXPERTS = N_GROUPS * EXPERTS_PER_GROUP
EPS = 1e-6

LANES = 128
SUBLANES = 8
V7X_SCOPED_VMEM_BYTES = 56 * 1024 * 1024

GLA_QK = H_GLA * DK_GLA
GLA_V = H_GLA * DV_GLA
RET_QK = H_RET * DK_RET
RET_V = H_RET * DV_RET
C_QA, C_KA, C_VA, C_GA = 0, GLA_QK, 2 * GLA_QK, 2 * GLA_QK + GLA_V
C_QR = C_GA + GLA_V
C_KR, C_VR, C_GR = C_QR + RET_QK, C_QR + 2 * RET_QK, C_QR + 2 * RET_QK + RET_V
C_LR = C_GR + RET_V
P_COLS = C_LR + LANES
IN_COLS_SRC = 2 * GLA_QK + GLA_V + GLA_LOWRANK + GLA_V + 2 * RET_QK + 2 * RET_V

PROJ_TM = 256
W_CHUNK = 1024
MOE_TM = 256
GLA_CHUNK = 128
RET_CHUNK = 128

def _dot(a, b):
    return jnp.dot(a, b, preferred_element_type=F32)


def _dot_nt(a, b):
    return lax.dot_general(a, b, (((1,), (1,)), ((), ())), preferred_element_type=F32)


def _dot_tn(a, b):
    return lax.dot_general(a, b, (((0,), (0,)), ((), ())), preferred_element_type=F32)


def _split_bf16(x, parts):
    out = []
    for _ in range(parts):
        p = x.astype(BF16)
        out.append(p)
        x = x - p.astype(F32)
    return out


def _dot_split(a, b_hi, b_lo):
    a_hi, a_lo = _split_bf16(a, 2)
    return _dot(a_hi, b_hi) + (_dot(a_lo, b_hi) + _dot(a_hi, b_lo))


def _rms(x):
    return x * lax.rsqrt(jnp.mean(x * x, axis=-1, keepdims=True) + EPS)


def _silu(x):
    return x * jax.nn.sigmoid(x)


def _wprep_body(a_ref, b_ref, o_ref, olr_ref):
    j = pl.program_id(0)
    a = a_ref[...]
    b = b_ref[...]
    shifted = jnp.concatenate([a[GLA_LOWRANK:, :], b], axis=0)
    o_ref[...] = jnp.where(j >= C_GA // W_CHUNK, shifted, a).T.astype(BF16)

    @pl.when(j == C_GA // W_CHUNK - 1)
    def _():
        padded = jnp.concatenate([b, jnp.zeros((LANES - GLA_LOWRANK, D_MODEL), F32)], axis=0)
        olr_ref[...] = padded.T.astype(BF16)


def _wprep(wt):
    per = W_CHUNK // GLA_LOWRANK
    return pl.pallas_call(
        _wprep_body,
        out_shape=(jax.ShapeDtypeStruct((D_MODEL, C_LR), BF16), jax.ShapeDtypeStruct((D_MODEL, LANES), BF16)),
        grid=(C_LR // W_CHUNK,),
        in_specs=[pl.BlockSpec((W_CHUNK, D_MODEL), lambda j: (j, 0)),
                  pl.BlockSpec((GLA_LOWRANK, D_MODEL), lambda j: (per * (j + 1), 0))],
        out_specs=(pl.BlockSpec((D_MODEL, W_CHUNK), lambda j: (0, j)),
                   pl.BlockSpec((D_MODEL, LANES), lambda j: (0, 0))),
        compiler_params=pltpu.CompilerParams(dimension_semantics=("arbitrary",),
                                             vmem_limit_bytes=V7X_SCOPED_VMEM_BYTES),
        name="wprep",
    )(wt, wt)


def _inproj_body(xp_ref, xs_ref, ln_ref, w_ref, wlr_ref, o_ref, *, n_prompt_blocks):
    x = jnp.where(pl.program_id(0) < n_prompt_blocks, xp_ref[...], xs_ref[...])
    h = (_rms(x) * ln_ref[...]).astype(BF16)
    for c in range(C_LR // W_CHUNK):
        o_ref[:, c * W_CHUNK:(c + 1) * W_CHUNK] = _dot(h, w_ref[:, c * W_CHUNK:(c + 1) * W_CHUNK])
    o_ref[:, C_LR:] = _dot(h, wlr_ref[...])


def _inproj(xp, xs, ln, w, w_lr):
    nbp = xp.shape[0] // PROJ_TM
    n = xp.shape[0] + xs.shape[0]
    fixed = lambda i: (0, 0)
    return pl.pallas_call(
        functools.partial(_inproj_body, n_prompt_blocks=nbp),
        out_shape=jax.ShapeDtypeStruct((n, P_COLS), F32),
        grid=(n // PROJ_TM,),
        in_specs=[pl.BlockSpec((PROJ_TM, D_MODEL), lambda i: (jnp.minimum(i, nbp - 1), 0)),
                  pl.BlockSpec((PROJ_TM, D_MODEL), lambda i: (jnp.maximum(i - nbp, 0), 0)),
                  pl.BlockSpec((1, D_MODEL), fixed),
                  pl.BlockSpec((D_MODEL, C_LR), fixed, pipeline_mode=pl.Buffered(1)),
                  pl.BlockSpec((D_MODEL, LANES), fixed)],
        out_specs=pl.BlockSpec((PROJ_TM, P_COLS), lambda i: (i, 0)),
        compiler_params=pltpu.CompilerParams(dimension_semantics=("arbitrary",),
                                             vmem_limit_bytes=V7X_SCOPED_VMEM_BYTES),
        name="inproj",
    )(xp, xs, ln, w, w_lr)


def _seg_cumsum(x, seg):
    pos = lax.broadcasted_iota(I32, x.shape, 0) & (seg - 1)
    s = 1
    while s < seg:
        x = x + jnp.where(pos >= s, pltpu.roll(x, s, axis=0), 0.0)
        s *= 2
    return x


def _block_ends(p, m):
    rows, w = p.shape
    return jnp.concatenate(
        [jnp.broadcast_to(p[j * m + m - 1:j * m + m, :], (m, w)) for j in range(rows // m)], axis=0)


def _block_starts(p, m, chunk):
    rows, w = p.shape
    pieces = []
    for j in range(rows // m):
        if (j * m) % chunk == 0:
            pieces.append(jnp.zeros((m, w), F32))
        else:
            pieces.append(jnp.broadcast_to(p[j * m - 1:j * m, :], (m, w)))
    return jnp.concatenate(pieces, axis=0)


def _gla_body(*refs, chunk, nseq, has_init):
    if has_init:
        (q_ref, k_ref, v_ref, g_ref, lr_ref, wgk_ref, bgk_ref, nw_ref, s0_ref,
         o_ref, sout_ref, s_scr) = refs
    else:
        (q_ref, k_ref, v_ref, g_ref, lr_ref, wgk_ref, bgk_ref, nw_ref,
         o_ref, sout_ref, s_scr) = refs
    t = pl.program_id(1)

    @pl.when(t == 0)
    def _():
        if has_init:
            s_scr[...] = s0_ref[...]
        else:
            s_scr[...] = jnp.zeros_like(s_scr)

    x = _dot_split(lr_ref[...], wgk_ref[0], wgk_ref[1]) + bgk_ref[...]
    log_g = -(jnp.maximum(-x, 0.0) + jnp.log1p(jnp.exp(-jnp.abs(x)))) / GLA_GATE_NORM
    p = _seg_cumsum(log_g, chunk)
    p_end = _block_ends(p, chunk)
    eq_c = jnp.exp(p)
    ek_c = jnp.exp(p_end - p)
    sub = min(GLA_SUBCHUNK, chunk)
    p_sub = p - _block_starts(p, sub, chunk)
    eq_d = jnp.exp(p_sub)
    ek_d = jnp.exp(-p_sub)
    levels = []
    m = sub
    while m < chunk:
        levels.append((m, jnp.exp(p - _block_starts(p, m, chunk)), jnp.exp(_block_ends(p, m) - p)))
        m *= 2

    ti = lax.broadcasted_iota(I32, (chunk, chunk), 0)
    si = lax.broadcasted_iota(I32, (chunk, chunk), 1)
    blk = lambda idx, size: idx >> (size.bit_length() - 1)
    mask_d = (blk(ti, sub) == blk(si, sub)) & (si <= ti)
    masks = [((blk(ti, m) & 1) == 1) & (blk(si, m) == blk(ti, m) - 1) for (m, _, _) in levels]
    sel8 = (lax.broadcasted_iota(I32, (8, DV_GLA), 0) == 0).astype(BF16)
    nw = nw_ref[...]

    outs = [[None] * nseq for _ in range(H_GLA)]
    for b in range(nseq):
        r0 = b * chunk
        rows = slice(r0, r0 + chunk)
        for h in range(H_GLA):
            kc = slice(h * DK_GLA, (h + 1) * DK_GLA)
            vc = slice(h * DV_GLA, (h + 1) * DV_GLA)
            q = q_ref[rows, kc] * (DK_GLA ** -0.5)
            k = k_ref[rows, kc]
            v = v_ref[rows, vc].astype(BF16)
            att = jnp.where(mask_d, _dot_nt((q * eq_d[rows, kc]).astype(BF16),
                                            (k * ek_d[rows, kc]).astype(BF16)), 0.0)
            for (m, eq_m, ek_m), mask in zip(levels, masks):
                att = jnp.where(mask, _dot_nt((q * eq_m[rows, kc]).astype(BF16),
                                              (k * ek_m[rows, kc]).astype(BF16)), att)
            s_old = s_scr[b, h]
            o = _dot(att.astype(BF16), v) + _dot((q * eq_c[rows, kc]).astype(BF16), s_old.astype(BF16))
            kv = _dot_tn((k * ek_c[rows, kc]).astype(BF16), v)
            tot = jnp.broadcast_to(p[r0 + chunk - 1:r0 + chunk, kc], (8, DK_GLA))
            t_hi, t_mid, t_lo = _split_bf16(tot, 3)
            decay = jnp.exp(_dot_tn(t_hi, sel8) + (_dot_tn(t_mid, sel8) + _dot_tn(t_lo, sel8)))
            s_scr[b, h] = decay * s_old + kv
            outs[h][b] = _rms(o) * nw * _silu(g_ref[rows, vc])
    for h in range(H_GLA):
        o_ref[:, h * DV_GLA:(h + 1) * DV_GLA] = jnp.concatenate(outs[h], axis=0).astype(BF16)

    @pl.when(t == pl.num_programs(1) - 1)
    def _():
        sout_ref[...] = s_scr[...]


def _gla(proj, w_gk2p, b_gk, nw, s0, *, row0, n_seq, seq_len, chunk, nseq):
    rows = nseq * chunk
    nt = seq_len // chunk
    base = row0 // rows

    def rmap(col):
        return lambda b, t: (base + b * nt + t, col)

    in_specs = [pl.BlockSpec((rows, GLA_QK), rmap(C_QA // GLA_QK)),
                pl.BlockSpec((rows, GLA_QK), rmap(C_KA // GLA_QK)),
                pl.BlockSpec((rows, GLA_V), rmap(C_VA // GLA_V)),
                pl.BlockSpec((rows, GLA_V), rmap(C_GA // GLA_V)),
                pl.BlockSpec((rows, LANES), rmap(C_LR // LANES)),
                pl.BlockSpec((2, LANES, GLA_QK), lambda b, t: (0, 0, 0)),
                pl.BlockSpec((1, GLA_QK), lambda b, t: (0, 0)),
                pl.BlockSpec((1, DV_GLA), lambda b, t: (0, 0))]
    args = [proj, proj, proj, proj, proj, w_gk2p, b_gk, nw]
    state_spec = pl.BlockSpec((nseq, H_GLA, DK_GLA, DV_GLA), lambda b, t: (b, 0, 0, 0))
    if s0 is not None:
        in_specs.append(state_spec)
        args.append(s0)
    return pl.pallas_call(
        functools.partial(_gla_body, chunk=chunk, nseq=nseq, has_init=s0 is not None),
        out_shape=(jax.ShapeDtypeStruct((n_seq * seq_len, GLA_V), BF16),
                   jax.ShapeDtypeStruct((n_seq, H_GLA, DK_GLA, DV_GLA), F32)),
        grid=(n_seq // nseq, nt),
        in_specs=in_specs,
        out_specs=(pl.BlockSpec((rows, GLA_V), lambda b, t: (b * nt + t, 0)), state_spec),
        scratch_shapes=[pltpu.VMEM((nseq, H_GLA, DK_GLA, DV_GLA), F32)],
        compiler_params=pltpu.CompilerParams(dimension_semantics=("arbitrary", "arbitrary"),
                                             vmem_limit_bytes=V7X_SCOPED_VMEM_BYTES),
        name="gla_init" if s0 is not None else "gla",
    )(*args)


def _ret_body(*refs, chunk, nseq, has_init):
    if has_init:
        (q_ref, k_ref, v_ref, g_ref, cos_ref, sin_ref, s0_ref, o_ref, sout_ref, s_scr) = refs
    else:
        (q_ref, k_ref, v_ref, g_ref, cos_ref, sin_ref, o_ref, sout_ref, s_scr) = refs
    t = pl.program_id(1)

    @pl.when(t == 0)
    def _():
        if has_init:
            s_scr[...] = s0_ref[...]
        else:
            s_scr[...] = jnp.zeros_like(s_scr)

    half = DK_RET // 2
    ti = lax.broadcasted_iota(I32, (chunk, chunk), 0)
    si = lax.broadcasted_iota(I32, (chunk, chunk), 1)
    diff = (ti - si).astype(F32)
    idx = lax.broadcasted_iota(I32, (chunk, 1), 0).astype(F32)

    outs = [[None] * nseq for _ in range(H_RET)]
    for h in range(H_RET):
        lg = math.log(1.0 - 2.0 ** (-5.0 - h))
        dmat = jnp.where(diff >= 0, jnp.exp(lg * jnp.maximum(diff, 0.0)), 0.0)
        q_dec = jnp.exp(lg * (idx + 1.0))
        k_dec = jnp.exp(lg * (chunk - 1.0 - idx))
        c_dec = math.exp(lg * chunk)
        for b in range(nseq):
            rows = slice(b * chunk, (b + 1) * chunk)
            cos = cos_ref[rows, :]
            sin = sin_ref[rows, :]
            c1 = slice(h * DK_RET, h * DK_RET + half)
            c2 = slice(h * DK_RET + half, (h + 1) * DK_RET)
            vc = slice(h * DV_RET, (h + 1) * DV_RET)
            q1, q2 = q_ref[rows, c1], q_ref[rows, c2]
            k1, k2 = k_ref[rows, c1], k_ref[rows, c2]
            q = jnp.concatenate([q1 * cos - q2 * sin, q1 * sin + q2 * cos], axis=-1)
            k = jnp.concatenate([k1 * cos - k2 * sin, k1 * sin + k2 * cos], axis=-1) * (DK_RET ** -0.5)
            v = v_ref[rows, vc].astype(BF16)
            qb = q.astype(BF16)
            s_old = s_scr[b, h]
            att = _dot_nt(qb, k.astype(BF16)) * dmat
            o = _dot(att.astype(BF16), v) + _dot(qb, s_old.astype(BF16)) * q_dec
            s_scr[b, h] = c_dec * s_old + _dot_tn((k * k_dec).astype(BF16), v)
            outs[h][b] = _rms(o) * _silu(g_ref[rows, vc])
    for h in range(H_RET):
        o_ref[:, h * DV_RET:(h + 1) * DV_RET] = jnp.concatenate(outs[h], axis=0).astype(BF16)

    @pl.when(t == pl.num_programs(1) - 1)
    def _():
        sout_ref[...] = s_scr[...]


def _ret(proj, cos, sin, s0, *, row0, n_seq, seq_len, chunk, nseq):
    rows = nseq * chunk
    nt = seq_len // chunk
    base = row0 // rows

    def rmap(col):
        return lambda b, t: (base + b * nt + t, col)

    tab_spec = pl.BlockSpec((rows, DK_RET // 2), lambda b, t: (t, 0))
    in_specs = [pl.BlockSpec((rows, RET_QK), rmap(C_QR // RET_QK)),
                pl.BlockSpec((rows, RET_QK), rmap(C_KR // RET_QK)),
                pl.BlockSpec((rows, RET_V), rmap(C_VR // RET_V)),
                pl.BlockSpec((rows, RET_V), rmap(C_GR // RET_V)),
                tab_spec, tab_spec]
    args = [proj, proj, proj, proj, cos, sin]
    state_spec = pl.BlockSpec((nseq, H_RET, DK_RET, DV_RET), lambda b, t: (b, 0, 0, 0))
    if s0 is not None:
        in_specs.append(state_spec)
        args.append(s0)
    return pl.pallas_call(
        functools.partial(_ret_body, chunk=chunk, nseq=nseq, has_init=s0 is not None),
        out_shape=(jax.ShapeDtypeStruct((n_seq * seq_len, RET_V), BF16),
                   jax.ShapeDtypeStruct((n_seq, H_RET, DK_RET, DV_RET), F32)),
        grid=(n_seq // nseq, nt),
        in_specs=in_specs,
        out_specs=(pl.BlockSpec((rows, RET_V), lambda b, t: (b * nt + t, 0)), state_spec),
        scratch_shapes=[pltpu.VMEM((nseq, H_RET, DK_RET, DV_RET), F32)],
        compiler_params=pltpu.CompilerParams(dimension_semantics=("arbitrary", "arbitrary"),
                                             vmem_limit_bytes=V7X_SCOPED_VMEM_BYTES),
        name="ret_init" if s0 is not None else "ret",
    )(*args)


def _rope_tables(pos, reps):
    half = DK_RET // 2
    inv = ROPE_BASE ** (-jnp.arange(half, dtype=F32) / half)
    ang = pos[:, None] * inv[None, :]
    return jnp.tile(jnp.cos(ang), (reps, 1)), jnp.tile(jnp.sin(ang), (reps, 1))


def _outproj_body(oap_ref, oas_ref, orp_ref, ors_ref, xp_ref, xs_ref, w_ref, ln_ref, wrt_ref, brt_ref,
                  x1_ref, hn_ref, route_ref, counts_ref, *, n_prompt_blocks):
    step = pl.program_id(0)
    is_prompt = step < n_prompt_blocks
    oa = jnp.where(is_prompt, oap_ref[...], oas_ref[...])
    orr = jnp.where(is_prompt, orp_ref[...], ors_ref[...])
    x = jnp.where(is_prompt, xp_ref[...], xs_ref[...])
    x1 = x + _dot(oa, w_ref[:GLA_V, :]) + _dot(orr, w_ref[GLA_V:, :])
    x1_ref[...] = x1
    hn = _rms(x1) * ln_ref[...]
    for c in range(D_MODEL // LANES):
        hn_ref[pl.ds(c, x1.shape[0], stride=D_MODEL // LANES), :] = hn[:, c * LANES:(c + 1) * LANES]

    logits = _dot_split(hn, wrt_ref[0], wrt_ref[1]) + brt_ref[...]
    lane_i = lax.broadcasted_iota(I32, logits.shape, 1)
    lane = lane_i.astype(F32)
    grp = (lane_i >> (EXPERTS_PER_GROUP.bit_length() - 1)).astype(F32)
    neg = -jnp.inf
    far = float(LANES)

    gl = jnp.where((lane_i >= N_EXPERTS) & (lane_i < N_EXPERTS + N_GROUPS), logits, neg)
    gmax = jnp.max(gl, axis=-1, keepdims=True)
    g_w = 1.0 / jnp.sum(jnp.exp(gl - gmax), axis=-1, keepdims=True)
    g_idx = jnp.min(jnp.where(gl == gmax, lane, far), axis=-1, keepdims=True) - float(N_EXPERTS)

    el = jnp.where((lane_i < N_EXPERTS) & (grp == g_idx), logits, neg)
    m1 = jnp.max(el, axis=-1, keepdims=True)
    esum = jnp.sum(jnp.exp(el - m1), axis=-1, keepdims=True)
    i1 = jnp.min(jnp.where(el == m1, lane, far), axis=-1, keepdims=True)
    el2 = jnp.where(lane == i1, neg, el)
    m2 = jnp.max(el2, axis=-1, keepdims=True)
    i2 = jnp.min(jnp.where(el2 == m2, lane, far), axis=-1, keepdims=True)
    p1 = 1.0 / esum
    p2 = jnp.exp(m2 - m1) / esum
    gate1 = g_w * (p1 / (p1 + p2))
    gate2 = g_w * (p2 / (p1 + p2))
    @pl.when(step == 0)
    def _():
        counts_ref[...] = jnp.zeros_like(counts_ref)

    tm = logits.shape[0]
    earlier = (lax.broadcasted_iota(I32, (tm, tm), 0) > lax.broadcasted_iota(I32, (tm, tm), 1)).astype(BF16)
    running = counts_ref[0:1, :]
    ranks = []
    for idx in (i1, i2):
        onehot = (lane == idx).astype(F32)
        before = _dot(earlier, onehot.astype(BF16)) + running
        ranks.append(jnp.sum(onehot * before, axis=-1, keepdims=True))
        running = running + jnp.sum(onehot, axis=0, keepdims=True)
    counts_ref[...] = jnp.broadcast_to(running, counts_ref.shape)

    route_ref[...] = jnp.where(lane_i == 0, i1,
                               jnp.where(lane_i == 1, i2,
                                         jnp.where(lane_i == 2, gate1,
                                                   jnp.where(lane_i == 3, gate2,
                                                             jnp.where(lane_i == 4, ranks[0],
                                                                       jnp.where(lane_i == 5, ranks[1], 0.0))))))


def _outproj_router(oa_p, oa_s, or_p, or_s, xp, xs, w_out, ln, w_rt, b_rt):
    n = xp.shape[0] + xs.shape[0]
    nbp = xp.shape[0] // PROJ_TM
    pmap = lambda i: (jnp.minimum(i, nbp - 1), 0)
    smap = lambda i: (jnp.maximum(i - nbp, 0), 0)
    row = lambda i: (i, 0)
    fixed = lambda i: (0, 0)
    return pl.pallas_call(
        functools.partial(_outproj_body, n_prompt_blocks=nbp),
        out_shape=(jax.ShapeDtypeStruct((n, D_MODEL), F32),
                   jax.ShapeDtypeStruct((n * (D_MODEL // LANES), LANES), F32),
                   jax.ShapeDtypeStruct((n, LANES), F32),
                   jax.ShapeDtypeStruct((SUBLANES, LANES), F32)),
        grid=(n // PROJ_TM,),
        in_specs=[pl.BlockSpec((PROJ_TM, GLA_V), pmap), pl.BlockSpec((PROJ_TM, GLA_V), smap),
                  pl.BlockSpec((PROJ_TM, RET_V), pmap), pl.BlockSpec((PROJ_TM, RET_V), smap),
                  pl.BlockSpec((PROJ_TM, D_MODEL), pmap), pl.BlockSpec((PROJ_TM, D_MODEL), smap),
                  pl.BlockSpec((GLA_V + RET_V, D_MODEL), fixed, pipeline_mode=pl.Buffered(1)),
                  pl.BlockSpec((1, D_MODEL), fixed),
                  pl.BlockSpec((2, D_MODEL, LANES), lambda i: (0, 0, 0)),
                  pl.BlockSpec((1, LANES), fixed)],
        out_specs=(pl.BlockSpec((PROJ_TM, D_MODEL), row), pl.BlockSpec((PROJ_TM * (D_MODEL // LANES), LANES), row),
                   pl.BlockSpec((PROJ_TM, LANES), row), pl.BlockSpec((SUBLANES, LANES), fixed)),
        compiler_params=pltpu.CompilerParams(dimension_semantics=("arbitrary",),
                                             vmem_limit_bytes=V7X_SCOPED_VMEM_BYTES),
        name="outproj_router",
    )(oa_p, oa_s, or_p, or_s, xp, xs, w_out, ln, w_rt, b_rt)


def _row_dma_wait(buf, sem, s):
    pltpu.make_async_copy(buf.at[s], buf.at[s], sem.at[s]).wait()


MOE_XBUFS = 3


def _moe_body(texp_ref, nval_ref, wslot_ref, nexp_ref, pos_ref, hn_hbm, wg_hbm, wu_hbm, wd_hbm, o_ref,
              inv, zeros_v, xbuf, wg_f, wu_f, wd_f, wg_b, wu_b, wd_b, gsem, wsem):
    tm = MOE_TM
    i = pl.program_id(0)
    nt = pl.num_programs(0)
    nv = nval_ref[i]
    nv_prev = jnp.where(i >= 1, nval_ref[jnp.maximum(i - 1, 0)], 0)

    tok_rows = D_MODEL // LANES

    def gather_tile(tile, s):
        for lo in range(0, tm, SUBLANES):
            ds = [inv[tile * tm + lo + j] for j in range(SUBLANES)]
            for j in range(SUBLANES):
                src = pl.multiple_of((ds[j] >> 1) * tok_rows, tok_rows)
                pltpu.make_async_copy(hn_hbm.at[pl.ds(src, tok_rows), :],
                                      xbuf.at[s, pl.ds((lo + j) * tok_rows, tok_rows), :], gsem.at[s]).start()

    def weight_copies(e, p):
        return [pltpu.make_async_copy(src.at[e], dst.at[p], wsem.at[p])
                for src, dst in ((wg_hbm, wg_f), (wu_hbm, wu_f), (wd_hbm, wd_f))]

    @pl.when(i == 0)
    def _():
        for c in weight_copies(texp_ref[0], wslot_ref[0]):
            c.start()
        zeros_v[...] = jnp.zeros_like(zeros_v)
        clear = pltpu.make_async_copy(zeros_v, inv, gsem.at[0])
        clear.start()
        clear.wait()

        def body(a, c):
            inv[pos_ref[a]] = a
            return c
        lax.fori_loop(0, pos_ref.shape[0], body, 0, unroll=8)
        gather_tile(0, 0)
        gather_tile(1, 1)

    def step(s):
        @pl.when(nv > 0)
        def _():
            @pl.when(i + 2 < nt)
            def _():
                gather_tile(i + 2, (s + 2) % MOE_XBUFS)

            @pl.when((i == 0) | (texp_ref[i] != texp_ref[jnp.maximum(i - 1, 0)]))
            def _():
                p = wslot_ref[i]
                for c in weight_copies(0, p):
                    c.wait()
                wg_b[...] = wg_f[p].astype(BF16)
                wu_b[...] = wu_f[p].astype(BF16)
                wd_b[...] = wd_f[p].astype(BF16)

                @pl.when(nexp_ref[i] >= 0)
                def _():
                    for c in weight_copies(nexp_ref[i], 1 - p):
                        c.start()

            _row_dma_wait(xbuf, gsem, s)
            h = jnp.concatenate([xbuf[s, pl.ds(c, tm, stride=tok_rows), :].astype(BF16)
                                 for c in range(tok_rows)], axis=1)
            hid = _silu(_dot(h, wg_b[...])) * _dot(h, wu_b[...])
            o_ref[...] = _dot(hid.astype(BF16), wd_b[...])

        @pl.when(nv == 0)
        def _():
            o_ref[...] = jnp.zeros_like(o_ref)

            @pl.when(nv_prev > 0)
            def _():
                _row_dma_wait(xbuf, gsem, s)

                @pl.when(i + 1 < nt)
                def _():
                    _row_dma_wait(xbuf, gsem, (s + 1) % MOE_XBUFS)

    for s in range(MOE_XBUFS):
        pl.when(i % MOE_XBUFS == s)(functools.partial(step, s))


def _moe(tile_expert, n_valid, w_slot, next_expert, pos, hn, w_gate, w_up, w_down):
    n_tiles = tile_expert.shape[0]
    assert n_tiles >= 2
    any_spec = pl.BlockSpec(memory_space=pl.ANY)
    return pl.pallas_call(
        _moe_body,
        out_shape=jax.ShapeDtypeStruct((n_tiles * MOE_TM, D_MODEL), F32),
        grid_spec=pltpu.PrefetchScalarGridSpec(
            num_scalar_prefetch=5,
            grid=(n_tiles,),
            in_specs=[any_spec, any_spec, any_spec, any_spec],
            out_specs=pl.BlockSpec((MOE_TM, D_MODEL), lambda i, *_: (i, 0)),
            scratch_shapes=[pltpu.SMEM((n_tiles * MOE_TM,), I32),
                            pltpu.VMEM((n_tiles * MOE_TM,), I32),
                            pltpu.VMEM((MOE_XBUFS, MOE_TM * (D_MODEL // LANES), LANES), F32),
                            pltpu.VMEM((2, D_MODEL, D_EXPERT), F32),
                            pltpu.VMEM((2, D_MODEL, D_EXPERT), F32),
                            pltpu.VMEM((2, D_EXPERT, D_MODEL), F32),
                            pltpu.VMEM((D_MODEL, D_EXPERT), BF16),
                            pltpu.VMEM((D_MODEL, D_EXPERT), BF16),
                            pltpu.VMEM((D_EXPERT, D_MODEL), BF16),
                            pltpu.SemaphoreType.DMA((MOE_XBUFS,)),
                            pltpu.SemaphoreType.DMA((2,))]),
        compiler_params=pltpu.CompilerParams(dimension_semantics=("arbitrary",),
                                             vmem_limit_bytes=V7X_SCOPED_VMEM_BYTES),
        name="moe",
    )(tile_expert, n_valid, w_slot, next_expert, pos, hn, w_gate, w_up, w_down)


def _routing_tables(route, counts, n):
    tm = MOE_TM
    n_tiles = 2 * n // tm + N_EXPERTS
    experts = route[:, 0:2].astype(I32)
    rank = route[:, 4:6].astype(I32)
    counts = counts[0, :N_EXPERTS].astype(I32)
    tiles = (counts + tm - 1) // tm
    tile_end = jnp.cumsum(tiles)
    tile_start = tile_end - tiles
    onehot = experts[:, :, None] == jnp.arange(N_EXPERTS, dtype=I32)[None, None, :]
    pos = jnp.sum(jnp.where(onehot, (tile_start * tm)[None, None, :], 0), axis=-1) + rank
    tidx = jnp.arange(n_tiles, dtype=I32)
    t_exp = jnp.sum((tidx[:, None] >= tile_end[None, :]).astype(I32), axis=1)
    used = tidx < tile_end[-1]
    last_exp = jnp.sum((tile_end[-1] - 1 >= tile_end).astype(I32))
    t_exp = jnp.where(used, t_exp, last_exp)
    n_valid = jnp.where(used, jnp.clip(counts[t_exp] - (tidx - tile_start[t_exp]) * tm, 0, tm), 0)
    eidx = jnp.arange(N_EXPERTS, dtype=I32)
    owns = tiles > 0
    slot_e = (jnp.cumsum(owns.astype(I32)) - 1) & 1
    later = jnp.where(owns[None, :] & (eidx[None, :] > eidx[:, None]), eidx[None, :], N_EXPERTS)
    next_e = jnp.min(later, axis=1)
    next_e = jnp.where(next_e < N_EXPERTS, next_e, -1)
    return (t_exp.astype(I32), n_valid.astype(I32), slot_e[t_exp].astype(I32), next_e[t_exp].astype(I32),
            pos.reshape(2 * n))


def _final_body(pos_ref, x1_ref, route_ref, ln_ref, y_hbm, o_ref, ybuf, sem, *, row0):
    tm = PROJ_TM
    i = pl.program_id(0)
    n_steps = pl.num_programs(0)

    def fetch(step, b):
        a0 = 2 * (row0 + step * tm)
        for g in range(tm // SUBLANES):
            ps = [pos_ref[a0 + 2 * (g * SUBLANES + j) + k] for j in range(SUBLANES) for k in range(2)]
            for j in range(SUBLANES):
                for k in range(2):
                    pltpu.make_async_copy(y_hbm.at[pl.ds(ps[2 * j + k], 1), :],
                                          ybuf.at[b, k * (tm // SUBLANES) + g, pl.ds(j, 1), :], sem.at[b]).start()

    @pl.when(i == 0)
    def _():
        fetch(0, 0)

    def step(b):
        @pl.when(i + 1 < n_steps)
        def _():
            fetch(i + 1, 1 - b)
        _row_dma_wait(ybuf, sem, b)
        y = ybuf[b].reshape(2, tm, D_MODEL)
        z = x1_ref[...] + route_ref[:, 2:3] * y[0] + route_ref[:, 3:4] * y[1]
        o_ref[...] = _rms(z) * ln_ref[...]

    for b in range(2):
        pl.when(i % 2 == b)(functools.partial(step, b))


def _final(pos, x1, route, ln, y_sorted, *, row0, rows):
    base = row0 // PROJ_TM
    return pl.pallas_call(
        functools.partial(_final_body, row0=row0),
        out_shape=jax.ShapeDtypeStruct((rows, D_MODEL), F32),
        grid_spec=pltpu.PrefetchScalarGridSpec(
            num_scalar_prefetch=1,
            grid=(rows // PROJ_TM,),
            in_specs=[pl.BlockSpec((PROJ_TM, D_MODEL), lambda i, pos_: (base + i, 0)),
                      pl.BlockSpec((PROJ_TM, LANES), lambda i, pos_: (base + i, 0)),
                      pl.BlockSpec((1, D_MODEL), lambda i, pos_: (0, 0)),
                      pl.BlockSpec(memory_space=pl.ANY)],
            out_specs=pl.BlockSpec((PROJ_TM, D_MODEL), lambda i, pos_: (i, 0)),
            scratch_shapes=[pltpu.VMEM((2, 2 * PROJ_TM // SUBLANES, SUBLANES, D_MODEL), F32),
                            pltpu.SemaphoreType.DMA((2,))]),
        compiler_params=pltpu.CompilerParams(dimension_semantics=("arbitrary",),
                                             vmem_limit_bytes=V7X_SCOPED_VMEM_BYTES),
        name="final_norm",
    )(pos, x1, route, ln, y_sorted)


def kernel(x_prompt, x_sample, state_gla, state_ret, ln_attn, w_in, w_gk2, b_gk, gla_norm_w, w_out, ln_ffn, w_router_group, b_router_group, w_router_expert, b_router_expert, w_exp_gate, w_exp_up, w_exp_down, ln_final):
    bp, tp, d = x_prompt.shape
    bs, ts, _ = x_sample.shape
    assert d == D_MODEL and w_in.shape == (1, D_MODEL, IN_COLS_SRC)
    n_p, n_s = bp * tp, bs * ts
    n = n_p + n_s
    assert n_p % PROJ_TM == 0 and n_s % PROJ_TM == 0 and tp % GLA_CHUNK == 0 and tp % RET_CHUNK == 0

    xp = x_prompt.reshape(n_p, d)
    xs = x_sample.reshape(n_s, d)

    hi_lo = lambda a: jnp.stack(_split_bf16(a, 2))
    w_in_p, w_lr = _wprep(w_in[0].T)
    w_gk2p = hi_lo(jnp.concatenate([w_gk2[0], jnp.zeros((LANES - GLA_LOWRANK, GLA_QK), F32)], axis=0))
    w_rt = hi_lo(jnp.concatenate([w_router_expert[0], w_router_group[0],
                                  jnp.zeros((d, LANES - N_EXPERTS - N_GROUPS), F32)], axis=1))
    b_rt = jnp.concatenate([b_router_expert[0], b_router_group[0],
                            jnp.zeros((LANES - N_EXPERTS - N_GROUPS,), F32)])[None, :]

    proj = _inproj(xp, xs, ln_attn, w_in_p, w_lr)

    nw = gla_norm_w
    bgk = b_gk
    oa_p, sg_p = _gla(proj, w_gk2p, bgk, nw, None, row0=0, n_seq=bp, seq_len=tp, chunk=GLA_CHUNK, nseq=1)
    oa_s, sg_s = _gla(proj, w_gk2p, bgk, nw, state_gla[0], row0=n_p, n_seq=bs, seq_len=ts, chunk=ts, nseq=8)

    cos_p, sin_p = _rope_tables(jnp.arange(tp, dtype=F32), 1)
    ret_nseq_s = 4
    cos_s, sin_s = _rope_tables(jnp.arange(ts, dtype=F32) + float(PAST_LEN), ret_nseq_s)
    or_p, sr_p = _ret(proj, cos_p, sin_p, None, row0=0, n_seq=bp, seq_len=tp, chunk=RET_CHUNK, nseq=1)
    or_s, sr_s = _ret(proj, cos_s, sin_s, state_ret[0], row0=n_p, n_seq=bs, seq_len=ts, chunk=ts, nseq=ret_nseq_s)

    x1, hn, route, counts = _outproj_router(oa_p, oa_s, or_p, or_s, xp, xs, w_out[0].astype(BF16),
                                            ln_ffn, w_rt, b_rt)

    t_exp, n_valid, w_slot, next_exp, pos = _routing_tables(route, counts, n)
    y_sorted = _moe(t_exp, n_valid, w_slot, next_exp, pos, hn, w_exp_gate[0], w_exp_up[0], w_exp_down[0])

    ln_f = ln_final[None, :]
    y_p = _final(pos, x1, route, ln_f, y_sorted, row0=0, rows=n_p).reshape(bp, tp, d)
    y_s = _final(pos, x1, route, ln_f, y_sorted, row0=n_p, rows=n_s).reshape(bs, ts, d)
    return (y_p, y_s, sg_p[None], sr_p[None], sg_s[None], sr_s[None])
```

```python
import functools
import math

import jax
import jax.numpy as jnp
from jax import lax
from jax.experimental import pallas as pl
from jax.experimental.pallas import tpu as pltpu

F32 = jnp.float32
BF16 = jnp.bfloat16
I32 = jnp.int32

D_MODEL = 2048
PAST_LEN = 16384
H_GLA, DK_GLA, DV_GLA = 4, 128, 256
GLA_LOWRANK = 16
GLA_GATE_NORM = 16.0
GLA_SUBCHUNK = 16
H_RET, DK_RET, DV_RET = 4, 256, 256
ROPE_BASE = 10000.0
N_GROUPS, EXPERTS_PER_GROUP, D_EXPERT = 4, 8, 512
N_EXPERTS = N_GROUPS * EXPERTS_PER_GROUP
EPS = 1e-6

LANES = 128
SUBLANES = 8
V7X_SCOPED_VMEM_BYTES = 56 * 1024 * 1024

GLA_QK = H_GLA * DK_GLA
GLA_V = H_GLA * DV_GLA
RET_QK = H_RET * DK_RET
RET_V = H_RET * DV_RET
C_QA, C_KA, C_VA, C_GA = 0, GLA_QK, 2 * GLA_QK, 2 * GLA_QK + GLA_V
C_QR = C_GA + GLA_V
C_KR, C_VR, C_GR = C_QR + RET_QK, C_QR + 2 * RET_QK, C_QR + 2 * RET_QK + RET_V
C_LR = C_GR + RET_V
P_COLS = C_LR + LANES
IN_COLS_SRC = 2 * GLA_QK + GLA_V + GLA_LOWRANK + GLA_V + 2 * RET_QK + 2 * RET_V

PROJ_TM = 256
W_CHUNK = 1024
MOE_TM = 256
GLA_CHUNK = 128
RET_CHUNK = 128

def _dot(a, b):
    return jnp.dot(a, b, preferred_element_type=F32)


def _dot_nt(a, b):
    return lax.dot_general(a, b, (((1,), (1,)), ((), ())), preferred_element_type=F32)


def _dot_tn(a, b):
    return lax.dot_general(a, b, (((0,), (0,)), ((), ())), preferred_element_type=F32)


def _split_bf16(x, parts):
    out = []
    for _ in range(parts):
        p = x.astype(BF16)
        out.append(p)
        x = x - p.astype(F32)
    return out


def _dot_split(a, b_hi, b_lo):
    a_hi, a_lo = _split_bf16(a, 2)
    return _dot(a_hi, b_hi) + (_dot(a_lo, b_hi) + _dot(a_hi, b_lo))


def _rms(x):
    return x * lax.rsqrt(jnp.mean(x * x, axis=-1, keepdims=True) + EPS)


def _silu(x):
    return x * jax.nn.sigmoid(x)


def _wprep_body(a_ref, b_ref, o_ref, olr_ref):
    j = pl.program_id(0)
    a = a_ref[...]
    b = b_ref[...]
    shifted = jnp.concatenate([a[GLA_LOWRANK:, :], b], axis=0)
    o_ref[...] = jnp.where(j >= C_GA // W_CHUNK, shifted, a).T.astype(BF16)

    @pl.when(j == C_GA // W_CHUNK - 1)
    def _():
        padded = jnp.concatenate([b, jnp.zeros((LANES - GLA_LOWRANK, D_MODEL), F32)], axis=0)
        olr_ref[...] = padded.T.astype(BF16)


def _wprep(wt):
    per = W_CHUNK // GLA_LOWRANK
    return pl.pallas_call(
        _wprep_body,
        out_shape=(jax.ShapeDtypeStruct((D_MODEL, C_LR), BF16), jax.ShapeDtypeStruct((D_MODEL, LANES), BF16)),
        grid=(C_LR // W_CHUNK,),
        in_specs=[pl.BlockSpec((W_CHUNK, D_MODEL), lambda j: (j, 0)),
                  pl.BlockSpec((GLA_LOWRANK, D_MODEL), lambda j: (per * (j + 1), 0))],
        out_specs=(pl.BlockSpec((D_MODEL, W_CHUNK), lambda j: (0, j)),
                   pl.BlockSpec((D_MODEL, LANES), lambda j: (0, 0))),
        compiler_params=pltpu.CompilerParams(dimension_semantics=("arbitrary",),
                                             vmem_limit_bytes=V7X_SCOPED_VMEM_BYTES),
        name="wprep",
    )(wt, wt)


def _inproj_body(xp_ref, xs_ref, ln_ref, w_ref, wlr_ref, o_ref, *, n_prompt_blocks):
    x = jnp.where(pl.program_id(0) < n_prompt_blocks, xp_ref[...], xs_ref[...])
    h = (_rms(x) * ln_ref[...]).astype(BF16)
    for c in range(C_LR // W_CHUNK):
        o_ref[:, c * W_CHUNK:(c + 1) * W_CHUNK] = _dot(h, w_ref[:, c * W_CHUNK:(c + 1) * W_CHUNK])
    o_ref[:, C_LR:] = _dot(h, wlr_ref[...])


def _inproj(xp, xs, ln, w, w_lr):
    nbp = xp.shape[0] // PROJ_TM
    n = xp.shape[0] + xs.shape[0]
    fixed = lambda i: (0, 0)
    return pl.pallas_call(
        functools.partial(_inproj_body, n_prompt_blocks=nbp),
        out_shape=jax.ShapeDtypeStruct((n, P_COLS), F32),
        grid=(n // PROJ_TM,),
        in_specs=[pl.BlockSpec((PROJ_TM, D_MODEL), lambda i: (jnp.minimum(i, nbp - 1), 0)),
                  pl.BlockSpec((PROJ_TM, D_MODEL), lambda i: (jnp.maximum(i - nbp, 0), 0)),
                  pl.BlockSpec((1, D_MODEL), fixed),
                  pl.BlockSpec((D_MODEL, C_LR), fixed, pipeline_mode=pl.Buffered(1)),
                  pl.BlockSpec((D_MODEL, LANES), fixed)],
        out_specs=pl.BlockSpec((PROJ_TM, P_COLS), lambda i: (i, 0)),
        compiler_params=pltpu.CompilerParams(dimension_semantics=("arbitrary",),
                                             vmem_limit_bytes=V7X_SCOPED_VMEM_BYTES),
        name="inproj",
    )(xp, xs, ln, w, w_lr)


def _seg_cumsum(x, seg):
    pos = lax.broadcasted_iota(I32, x.shape, 0) & (seg - 1)
    s = 1
    while s < seg:
        x = x + jnp.where(pos >= s, pltpu.roll(x, s, axis=0), 0.0)
        s *= 2
    return x


def _block_ends(p, m):
    rows, w = p.shape
    return jnp.concatenate(
        [jnp.broadcast_to(p[j * m + m - 1:j * m + m, :], (m, w)) for j in range(rows // m)], axis=0)


def _block_starts(p, m, chunk):
    rows, w = p.shape
    pieces = []
    for j in range(rows // m):
        if (j * m) % chunk == 0:
            pieces.append(jnp.zeros((m, w), F32))
        else:
            pieces.append(jnp.broadcast_to(p[j * m - 1:j * m, :], (m, w)))
    return jnp.concatenate(pieces, axis=0)


def _gla_body(*refs, chunk, nseq, has_init):
    if has_init:
        (q_ref, k_ref, v_ref, g_ref, lr_ref, wgk_ref, bgk_ref, nw_ref, s0_ref,
         o_ref, sout_ref, s_scr) = refs
    else:
        (q_ref, k_ref, v_ref, g_ref, lr_ref, wgk_ref, bgk_ref, nw_ref,
         o_ref, sout_ref, s_scr) = refs
    t = pl.program_id(1)

    @pl.when(t == 0)
    def _():
        if has_init:
            s_scr[...] = s0_ref[...]
        else:
            s_scr[...] = jnp.zeros_like(s_scr)

    x = _dot_split(lr_ref[...], wgk_ref[0], wgk_ref[1]) + bgk_ref[...]
    log_g = -(jnp.maximum(-x, 0.0) + jnp.log1p(jnp.exp(-jnp.abs(x)))) / GLA_GATE_NORM
    p = _seg_cumsum(log_g, chunk)
    p_end = _block_ends(p, chunk)
    eq_c = jnp.exp(p)
    ek_c = jnp.exp(p_end - p)
    sub = min(GLA_SUBCHUNK, chunk)
    p_sub = p - _block_starts(p, sub, chunk)
    eq_d = jnp.exp(p_sub)
    ek_d = jnp.exp(-p_sub)
    levels = []
    m = sub
    while m < chunk:
        levels.append((m, jnp.exp(p - _block_starts(p, m, chunk)), jnp.exp(_block_ends(p, m) - p)))
        m *= 2

    ti = lax.broadcasted_iota(I32, (chunk, chunk), 0)
    si = lax.broadcasted_iota(I32, (chunk, chunk), 1)
    blk = lambda idx, size: idx >> (size.bit_length() - 1)
    mask_d = (blk(ti, sub) == blk(si, sub)) & (si <= ti)
    masks = [((blk(ti, m) & 1) == 1) & (blk(si, m) == blk(ti, m) - 1) for (m, _, _) in levels]
    sel8 = (lax.broadcasted_iota(I32, (8, DV_GLA), 0) == 0).astype(BF16)
    nw = nw_ref[...]

    outs = [[None] * nseq for _ in range(H_GLA)]
    for b in range(nseq):
        r0 = b * chunk
        rows = slice(r0, r0 + chunk)
        for h in range(H_GLA):
            kc = slice(h * DK_GLA, (h + 1) * DK_GLA)
            vc = slice(h * DV_GLA, (h + 1) * DV_GLA)
            q = q_ref[rows, kc] * (DK_GLA ** -0.5)
            k = k_ref[rows, kc]
            v = v_ref[rows, vc].astype(BF16)
            att = jnp.where(mask_d, _dot_nt((q * eq_d[rows, kc]).astype(BF16),
                                            (k * ek_d[rows, kc]).astype(BF16)), 0.0)
            for (m, eq_m, ek_m), mask in zip(levels, masks):
                att = jnp.where(mask, _dot_nt((q * eq_m[rows, kc]).astype(BF16),
                                              (k * ek_m[rows, kc]).astype(BF16)), att)
            s_old = s_scr[b, h]
            o = _dot(att.astype(BF16), v) + _dot((q * eq_c[rows, kc]).astype(BF16), s_old.astype(BF16))
            kv = _dot_tn((k * ek_c[rows, kc]).astype(BF16), v)
            tot = jnp.broadcast_to(p[r0 + chunk - 1:r0 + chunk, kc], (8, DK_GLA))
            t_hi, t_mid, t_lo = _split_bf16(tot, 3)
            decay = jnp.exp(_dot_tn(t_hi, sel8) + (_dot_tn(t_mid, sel8) + _dot_tn(t_lo, sel8)))
            s_scr[b, h] = decay * s_old + kv
            outs[h][b] = _rms(o) * nw * _silu(g_ref[rows, vc])
    for h in range(H_GLA):
        o_ref[:, h * DV_GLA:(h + 1) * DV_GLA] = jnp.concatenate(outs[h], axis=0).astype(BF16)

    @pl.when(t == pl.num_programs(1) - 1)
    def _():
        sout_ref[...] = s_scr[...]


def _gla(proj, w_gk2p, b_gk, nw, s0, *, row0, n_seq, seq_len, chunk, nseq):
    rows = nseq * chunk
    nt = seq_len // chunk
    base = row0 // rows

    def rmap(col):
        return lambda b, t: (base + b * nt + t, col)

    in_specs = [pl.BlockSpec((rows, GLA_QK), rmap(C_QA // GLA_QK)),
                pl.BlockSpec((rows, GLA_QK), rmap(C_KA // GLA_QK)),
                pl.BlockSpec((rows, GLA_V), rmap(C_VA // GLA_V)),
                pl.BlockSpec((rows, GLA_V), rmap(C_GA // GLA_V)),
                pl.BlockSpec((rows, LANES), rmap(C_LR // LANES)),
                pl.BlockSpec((2, LANES, GLA_QK), lambda b, t: (0, 0, 0)),
                pl.BlockSpec((1, GLA_QK), lambda b, t: (0, 0)),
                pl.BlockSpec((1, DV_GLA), lambda b, t: (0, 0))]
    args = [proj, proj, proj, proj, proj, w_gk2p, b_gk, nw]
    state_spec = pl.BlockSpec((nseq, H_GLA, DK_GLA, DV_GLA), lambda b, t: (b, 0, 0, 0))
    if s0 is not None:
        in_specs.append(state_spec)
        args.append(s0)
    return pl.pallas_call(
        functools.partial(_gla_body, chunk=chunk, nseq=nseq, has_init=s0 is not None),
        out_shape=(jax.ShapeDtypeStruct((n_seq * seq_len, GLA_V), BF16),
                   jax.ShapeDtypeStruct((n_seq, H_GLA, DK_GLA, DV_GLA), F32)),
        grid=(n_seq // nseq, nt),
        in_specs=in_specs,
        out_specs=(pl.BlockSpec((rows, GLA_V), lambda b, t: (b * nt + t, 0)), state_spec),
        scratch_shapes=[pltpu.VMEM((nseq, H_GLA, DK_GLA, DV_GLA), F32)],
        compiler_params=pltpu.CompilerParams(dimension_semantics=("arbitrary", "arbitrary"),
                                             vmem_limit_bytes=V7X_SCOPED_VMEM_BYTES),
        name="gla_init" if s0 is not None else "gla",
    )(*args)


def _ret_body(*refs, chunk, nseq, has_init):
    if has_init:
        (q_ref, k_ref, v_ref, g_ref, cos_ref, sin_ref, s0_ref, o_ref, sout_ref, s_scr) = refs
    else:
        (q_ref, k_ref, v_ref, g_ref, cos_ref, sin_ref, o_ref, sout_ref, s_scr) = refs
    t = pl.program_id(1)

    @pl.when(t == 0)
    def _():
        if has_init:
            s_scr[...] = s0_ref[...]
        else:
            s_scr[...] = jnp.zeros_like(s_scr)

    half = DK_RET // 2
    ti = lax.broadcasted_iota(I32, (chunk, chunk), 0)
    si = lax.broadcasted_iota(I32, (chunk, chunk), 1)
    diff = (ti - si).astype(F32)
    idx = lax.broadcasted_iota(I32, (chunk, 1), 0).astype(F32)

    outs = [[None] * nseq for _ in range(H_RET)]
    for h in range(H_RET):
        lg = math.log(1.0 - 2.0 ** (-5.0 - h))
        dmat = jnp.where(diff >= 0, jnp.exp(lg * jnp.maximum(diff, 0.0)), 0.0)
        q_dec = jnp.exp(lg * (idx + 1.0))
        k_dec = jnp.exp(lg * (chunk - 1.0 - idx))
        c_dec = math.exp(lg * chunk)
        for b in range(nseq):
            rows = slice(b * chunk, (b + 1) * chunk)
            cos = cos_ref[rows, :]
            sin = sin_ref[rows, :]
            c1 = slice(h * DK_RET, h * DK_RET + half)
            c2 = slice(h * DK_RET + half, (h + 1) * DK_RET)
            vc = slice(h * DV_RET, (h + 1) * DV_RET)
            q1, q2 = q_ref[rows, c1], q_ref[rows, c2]
            k1, k2 = k_ref[rows, c1], k_ref[rows, c2]
            q = jnp.concatenate([q1 * cos - q2 * sin, q1 * sin + q2 * cos], axis=-1)
            k = jnp.concatenate([k1 * cos - k2 * sin, k1 * sin + k2 * cos], axis=-1) * (DK_RET ** -0.5)
            v = v_ref[rows, vc].astype(BF16)
            qb = q.astype(BF16)
            s_old = s_scr[b, h]
            att = _dot_nt(qb, k.astype(BF16)) * dmat
            o = _dot(att.astype(BF16), v) + _dot(qb, s_old.astype(BF16)) * q_dec
            s_scr[b, h] = c_dec * s_old + _dot_tn((k * k_dec).astype(BF16), v)
            outs[h][b] = _rms(o) * _silu(g_ref[rows, vc])
    for h in range(H_RET):
        o_ref[:, h * DV_RET:(h + 1) * DV_RET] = jnp.concatenate(outs[h], axis=0).astype(BF16)

    @pl.when(t == pl.num_programs(1) - 1)
    def _():
        sout_ref[...] = s_scr[...]


def _ret(proj, cos, sin, s0, *, row0, n_seq, seq_len, chunk, nseq):
    rows = nseq * chunk
    nt = seq_len // chunk
    base = row0 // rows

    def rmap(col):
        return lambda b, t: (base + b * nt + t, col)

    tab_spec = pl.BlockSpec((rows, DK_RET // 2), lambda b, t: (t, 0))
    in_specs = [pl.BlockSpec((rows, RET_QK), rmap(C_QR // RET_QK)),
                pl.BlockSpec((rows, RET_QK), rmap(C_KR // RET_QK)),
                pl.BlockSpec((rows, RET_V), rmap(C_VR // RET_V)),
                pl.BlockSpec((rows, RET_V), rmap(C_GR // RET_V)),
                tab_spec, tab_spec]
    args = [proj, proj, proj, proj, cos, sin]
    state_spec = pl.BlockSpec((nseq, H_RET, DK_RET, DV_RET), lambda b, t: (b, 0, 0, 0))
    if s0 is not None:
        in_specs.append(state_spec)
        args.append(s0)
    return pl.pallas_call(
        functools.partial(_ret_body, chunk=chunk, nseq=nseq, has_init=s0 is not None),
        out_shape=(jax.ShapeDtypeStruct((n_seq * seq_len, RET_V), BF16),
                   jax.ShapeDtypeStruct((n_seq, H_RET, DK_RET, DV_RET), F32)),
        grid=(n_seq // nseq, nt),
        in_specs=in_specs,
        out_specs=(pl.BlockSpec((rows, RET_V), lambda b, t: (b * nt + t, 0)), state_spec),
        scratch_shapes=[pltpu.VMEM((nseq, H_RET, DK_RET, DV_RET), F32)],
        compiler_params=pltpu.CompilerParams(dimension_semantics=("arbitrary", "arbitrary"),
                                             vmem_limit_bytes=V7X_SCOPED_VMEM_BYTES),
        name="ret_init" if s0 is not None else "ret",
    )(*args)


def _rope_tables(pos, reps):
    half = DK_RET // 2
    inv = ROPE_BASE ** (-jnp.arange(half, dtype=F32) / half)
    ang = pos[:, None] * inv[None, :]
    return jnp.tile(jnp.cos(ang), (reps, 1)), jnp.tile(jnp.sin(ang), (reps, 1))


def _outproj_body(oap_ref, oas_ref, orp_ref, ors_ref, xp_ref, xs_ref, w_ref, ln_ref, wrt_ref, brt_ref,
                  x1_ref, hn_ref, route_ref, counts_ref, *, n_prompt_blocks):
    step = pl.program_id(0)
    is_prompt = step < n_prompt_blocks
    oa = jnp.where(is_prompt, oap_ref[...], oas_ref[...])
    orr = jnp.where(is_prompt, orp_ref[...], ors_ref[...])
    x = jnp.where(is_prompt, xp_ref[...], xs_ref[...])
    x1 = x + _dot(oa, w_ref[:GLA_V, :]) + _dot(orr, w_ref[GLA_V:, :])
    x1_ref[...] = x1
    hn = _rms(x1) * ln_ref[...]
    hn_ref[...] = hn

    logits = _dot_split(hn, wrt_ref[0], wrt_ref[1]) + brt_ref[...]
    lane_i = lax.broadcasted_iota(I32, logits.shape, 1)
    lane = lane_i.astype(F32)
    grp = (lane_i >> (EXPERTS_PER_GROUP.bit_length() - 1)).astype(F32)
    neg = -jnp.inf
    far = float(LANES)

    gl = jnp.where((lane_i >= N_EXPERTS) & (lane_i < N_EXPERTS + N_GROUPS), logits, neg)
    gmax = jnp.max(gl, axis=-1, keepdims=True)
    g_w = 1.0 / jnp.sum(jnp.exp(gl - gmax), axis=-1, keepdims=True)
    g_idx = jnp.min(jnp.where(gl == gmax, lane, far), axis=-1, keepdims=True) - float(N_EXPERTS)

    el = jnp.where((lane_i < N_EXPERTS) & (grp == g_idx), logits, neg)
    m1 = jnp.max(el, axis=-1, keepdims=True)
    esum = jnp.sum(jnp.exp(el - m1), axis=-1, keepdims=True)
    i1 = jnp.min(jnp.where(el == m1, lane, far), axis=-1, keepdims=True)
    el2 = jnp.where(lane == i1, neg, el)
    m2 = jnp.max(el2, axis=-1, keepdims=True)
    i2 = jnp.min(jnp.where(el2 == m2, lane, far), axis=-1, keepdims=True)
    p1 = 1.0 / esum
    p2 = jnp.exp(m2 - m1) / esum
    gate1 = g_w * (p1 / (p1 + p2))
    gate2 = g_w * (p2 / (p1 + p2))
    @pl.when(step == 0)
    def _():
        counts_ref[...] = jnp.zeros_like(counts_ref)

    tm = logits.shape[0]
    earlier = (lax.broadcasted_iota(I32, (tm, tm), 0) > lax.broadcasted_iota(I32, (tm, tm), 1)).astype(BF16)
    running = counts_ref[0:1, :]
    ranks = []
    for idx in (i1, i2):
        onehot = (lane == idx).astype(F32)
        before = _dot(earlier, onehot.astype(BF16)) + running
        ranks.append(jnp.sum(onehot * before, axis=-1, keepdims=True))
        running = running + jnp.sum(onehot, axis=0, keepdims=True)
    counts_ref[...] = jnp.broadcast_to(running, counts_ref.shape)

    route_ref[...] = jnp.where(lane_i == 0, i1,
                               jnp.where(lane_i == 1, i2,
                                         jnp.where(lane_i == 2, gate1,
                                                   jnp.where(lane_i == 3, gate2,
                                                             jnp.where(lane_i == 4, ranks[0],
                                                                       jnp.where(lane_i == 5, ranks[1], 0.0))))))


def _outproj_router(oa_p, oa_s, or_p, or_s, xp, xs, w_out, ln, w_rt, b_rt):
    n = xp.shape[0] + xs.shape[0]
    nbp = xp.shape[0] // PROJ_TM
    pmap = lambda i: (jnp.minimum(i, nbp - 1), 0)
    smap = lambda i: (jnp.maximum(i - nbp, 0), 0)
    row = lambda i: (i, 0)
    fixed = lambda i: (0, 0)
    return pl.pallas_call(
        functools.partial(_outproj_body, n_prompt_blocks=nbp),
        out_shape=(jax.ShapeDtypeStruct((n, D_MODEL), F32),
                   jax.ShapeDtypeStruct((n, D_MODEL), F32),
                   jax.ShapeDtypeStruct((n, LANES), F32),
                   jax.ShapeDtypeStruct((SUBLANES, LANES), F32)),
        grid=(n // PROJ_TM,),
        in_specs=[pl.BlockSpec((PROJ_TM, GLA_V), pmap), pl.BlockSpec((PROJ_TM, GLA_V), smap),
                  pl.BlockSpec((PROJ_TM, RET_V), pmap), pl.BlockSpec((PROJ_TM, RET_V), smap),
                  pl.BlockSpec((PROJ_TM, D_MODEL), pmap), pl.BlockSpec((PROJ_TM, D_MODEL), smap),
                  pl.BlockSpec((GLA_V + RET_V, D_MODEL), fixed, pipeline_mode=pl.Buffered(1)),
                  pl.BlockSpec((1, D_MODEL), fixed),
                  pl.BlockSpec((2, D_MODEL, LANES), lambda i: (0, 0, 0)),
                  pl.BlockSpec((1, LANES), fixed)],
        out_specs=(pl.BlockSpec((PROJ_TM, D_MODEL), row), pl.BlockSpec((PROJ_TM, D_MODEL), row),
                   pl.BlockSpec((PROJ_TM, LANES), row), pl.BlockSpec((SUBLANES, LANES), fixed)),
        compiler_params=pltpu.CompilerParams(dimension_semantics=("arbitrary",),
                                             vmem_limit_bytes=V7X_SCOPED_VMEM_BYTES),
        name="outproj_router",
    )(oa_p, oa_s, or_p, or_s, xp, xs, w_out, ln, w_rt, b_rt)


def _row_dma_wait(buf, sem, s):
    pltpu.make_async_copy(buf.at[s], buf.at[s], sem.at[s]).wait()


MOE_XBUFS = 3
WEIGHT_DMA_PRIORITY = 1


def _moe_body(texp_ref, nval_ref, wslot_ref, nexp_ref, pos_ref, hn_hbm, wg_hbm, wu_hbm, wd_hbm, o_ref,
              inv, zeros_v, xbuf, wg_f, wu_f, wd_f, wg_b, wu_b, wd_b, gsem, wsem):
    tm = MOE_TM
    i = pl.program_id(0)
    nt = pl.num_programs(0)
    nv = nval_ref[i]
    nv_prev = jnp.where(i >= 1, nval_ref[jnp.maximum(i - 1, 0)], 0)

    def gather_tile(tile, s):
        for lo in range(0, tm, SUBLANES):
            ds = [inv[tile * tm + lo + j] for j in range(SUBLANES)]
            for j in range(SUBLANES):
                pltpu.make_async_copy(hn_hbm.at[pl.ds(ds[j] >> 1, 1), :],
                                      xbuf.at[s, lo // SUBLANES, pl.ds(j, 1), :], gsem.at[s]).start()

    def weight_copies(e, p):
        return [pltpu.make_async_copy(src.at[e], dst.at[p], wsem.at[p])
                for src, dst in ((wg_hbm, wg_f), (wu_hbm, wu_f), (wd_hbm, wd_f))]

    @pl.when(i == 0)
    def _():
        for c in weight_copies(texp_ref[0], wslot_ref[0]):
            c.start(priority=WEIGHT_DMA_PRIORITY)
        zeros_v[...] = jnp.zeros_like(zeros_v)
        clear = pltpu.make_async_copy(zeros_v, inv, gsem.at[0])
        clear.start()
        clear.wait()

        def body(a, c):
            inv[pos_ref[a]] = a
            return c
        lax.fori_loop(0, pos_ref.shape[0], body, 0, unroll=8)
        gather_tile(0, 0)
        gather_tile(1, 1)

    def step(s):
        @pl.when(nv > 0)
        def _():
            @pl.when(i + 2 < nt)
            def _():
                gather_tile(i + 2, (s + 2) % MOE_XBUFS)

            @pl.when((i == 0) | (texp_ref[i] != texp_ref[jnp.maximum(i - 1, 0)]))
            def _():
                p = wslot_ref[i]
                for c in weight_copies(0, p):
                    c.wait()
                wg_b[...] = wg_f[p].astype(BF16)
                wu_b[...] = wu_f[p].astype(BF16)
                wd_b[...] = wd_f[p].astype(BF16)

                @pl.when(nexp_ref[i] >= 0)
                def _():
                    for c in weight_copies(nexp_ref[i], 1 - p):
                        c.start(priority=WEIGHT_DMA_PRIORITY)

            _row_dma_wait(xbuf, gsem, s)
            h = xbuf[s].reshape(tm, D_MODEL).astype(BF16)
            hid = _silu(_dot(h, wg_b[...])) * _dot(h, wu_b[...])
            o_ref[...] = _dot(hid.astype(BF16), wd_b[...])

        @pl.when(nv == 0)
        def _():
            o_ref[...] = jnp.zeros_like(o_ref)

            @pl.when(nv_prev > 0)
            def _():
                _row_dma_wait(xbuf, gsem, s)

                @pl.when(i + 1 < nt)
                def _():
                    _row_dma_wait(xbuf, gsem, (s + 1) % MOE_XBUFS)

    for s in range(MOE_XBUFS):
        pl.when(i % MOE_XBUFS == s)(functools.partial(step, s))


def _moe(tile_expert, n_valid, w_slot, next_expert, pos, hn, w_gate, w_up, w_down):
    n_tiles = tile_expert.shape[0]
    assert n_tiles >= 2
    any_spec = pl.BlockSpec(memory_space=pl.ANY)
    return pl.pallas_call(
        _moe_body,
        out_shape=jax.ShapeDtypeStruct((n_tiles * MOE_TM, D_MODEL), F32),
        grid_spec=pltpu.PrefetchScalarGridSpec(
            num_scalar_prefetch=5,
            grid=(n_tiles,),
            in_specs=[any_spec, any_spec, any_spec, any_spec],
            out_specs=pl.BlockSpec((MOE_TM, D_MODEL), lambda i, *_: (i, 0)),
            scratch_shapes=[pltpu.SMEM((n_tiles * MOE_TM,), I32),
                            pltpu.VMEM((n_tiles * MOE_TM,), I32),
                            pltpu.VMEM((MOE_XBUFS, MOE_TM // SUBLANES, SUBLANES, D_MODEL), F32),
                            pltpu.VMEM((2, D_MODEL, D_EXPERT), F32),
                            pltpu.VMEM((2, D_MODEL, D_EXPERT), F32),
                            pltpu.VMEM((2, D_EXPERT, D_MODEL), F32),
                            pltpu.VMEM((D_MODEL, D_EXPERT), BF16),
                            pltpu.VMEM((D_MODEL, D_EXPERT), BF16),
                            pltpu.VMEM((D_EXPERT, D_MODEL), BF16),
                            pltpu.SemaphoreType.DMA((MOE_XBUFS,)),
                            pltpu.SemaphoreType.DMA((2,))]),
        compiler_params=pltpu.CompilerParams(dimension_semantics=("arbitrary",),
                                             vmem_limit_bytes=V7X_SCOPED_VMEM_BYTES),
        name="moe",
    )(tile_expert, n_valid, w_slot, next_expert, pos, hn, w_gate, w_up, w_down)


def _routing_tables(route, counts, n):
    tm = MOE_TM
    n_tiles = 2 * n // tm + N_EXPERTS
    experts = route[:, 0:2].astype(I32)
    rank = route[:, 4:6].astype(I32)
    counts = counts[0, :N_EXPERTS].astype(I32)
    tiles = (counts + tm - 1) // tm
    tile_end = jnp.cumsum(tiles)
    tile_start = tile_end - tiles
    onehot = experts[:, :, None] == jnp.arange(N_EXPERTS, dtype=I32)[None, None, :]
    pos = jnp.sum(jnp.where(onehot, (tile_start * tm)[None, None, :], 0), axis=-1) + rank
    tidx = jnp.arange(n_tiles, dtype=I32)
    t_exp = jnp.sum((tidx[:, None] >= tile_end[None, :]).astype(I32), axis=1)
    used = tidx < tile_end[-1]
    last_exp = jnp.sum((tile_end[-1] - 1 >= tile_end).astype(I32))
    t_exp = jnp.where(used, t_exp, last_exp)
    n_valid = jnp.where(used, jnp.clip(counts[t_exp] - (tidx - tile_start[t_exp]) * tm, 0, tm), 0)
    eidx = jnp.arange(N_EXPERTS, dtype=I32)
    owns = tiles > 0
    slot_e = (jnp.cumsum(owns.astype(I32)) - 1) & 1
    later = jnp.where(owns[None, :] & (eidx[None, :] > eidx[:, None]), eidx[None, :], N_EXPERTS)
    next_e = jnp.min(later, axis=1)
    next_e = jnp.where(next_e < N_EXPERTS, next_e, -1)
    return (t_exp.astype(I32), n_valid.astype(I32), slot_e[t_exp].astype(I32), next_e[t_exp].astype(I32),
            pos.reshape(2 * n))


def _final_body(pos_ref, x1_ref, route_ref, ln_ref, y_hbm, o_ref, ybuf, sem, *, row0):
    tm = PROJ_TM
    i = pl.program_id(0)
    n_steps = pl.num_programs(0)

    def fetch(step, b):
        a0 = 2 * (row0 + step * tm)
        for g in range(tm // SUBLANES):
            ps = [pos_ref[a0 + 2 * (g * SUBLANES + j) + k] for j in range(SUBLANES) for k in range(2)]
            for j in range(SUBLANES):
                for k in range(2):
                    pltpu.make_async_copy(y_hbm.at[pl.ds(ps[2 * j + k], 1), :],
                                          ybuf.at[b, k * (tm // SUBLANES) + g, pl.ds(j, 1), :], sem.at[b]).start()

    @pl.when(i == 0)
    def _():
        fetch(0, 0)

    def step(b):
        @pl.when(i + 1 < n_steps)
        def _():
            fetch(i + 1, 1 - b)
        _row_dma_wait(ybuf, sem, b)
        y = ybuf[b].reshape(2, tm, D_MODEL)
        z = x1_ref[...] + route_ref[:, 2:3] * y[0] + route_ref[:, 3:4] * y[1]
        o_ref[...] = _rms(z) * ln_ref[...]

    for b in range(2):
        pl.when(i % 2 == b)(functools.partial(step, b))


def _final(pos, x1, route, ln, y_sorted, *, row0, rows):
    base = row0 // PROJ_TM
    return pl.pallas_call(
        functools.partial(_final_body, row0=row0),
        out_shape=jax.ShapeDtypeStruct((rows, D_MODEL), F32),
        grid_spec=pltpu.PrefetchScalarGridSpec(
            num_scalar_prefetch=1,
            grid=(rows // PROJ_TM,),
            in_specs=[pl.BlockSpec((PROJ_TM, D_MODEL), lambda i, pos_: (base + i, 0)),
                      pl.BlockSpec((PROJ_TM, LANES), lambda i, pos_: (base + i, 0)),
                      pl.BlockSpec((1, D_MODEL), lambda i, pos_: (0, 0)),
                      pl.BlockSpec(memory_space=pl.ANY)],
            out_specs=pl.BlockSpec((PROJ_TM, D_MODEL), lambda i, pos_: (i, 0)),
            scratch_shapes=[pltpu.VMEM((2, 2 * PROJ_TM // SUBLANES, SUBLANES, D_MODEL), F32),
                            pltpu.SemaphoreType.DMA((2,))]),
        compiler_params=pltpu.CompilerParams(dimension_semantics=("arbitrary",),
                                             vmem_limit_bytes=V7X_SCOPED_VMEM_BYTES),
        name="final_norm",
    )(pos, x1, route, ln, y_sorted)


def kernel(x_prompt, x_sample, state_gla, state_ret, ln_attn, w_in, w_gk2, b_gk, gla_norm_w, w_out, ln_ffn, w_router_group, b_router_group, w_router_expert, b_router_expert, w_exp_gate, w_exp_up, w_exp_down, ln_final):
    bp, tp, d = x_prompt.shape
    bs, ts, _ = x_sample.shape
    assert d == D_MODEL and w_in.shape == (1, D_MODEL, IN_COLS_SRC)
    n_p, n_s = bp * tp, bs * ts
    n = n_p + n_s
    assert n_p % PROJ_TM == 0 and n_s % PROJ_TM == 0 and tp % GLA_CHUNK == 0 and tp % RET_CHUNK == 0

    xp = x_prompt.reshape(n_p, d)
    xs = x_sample.reshape(n_s, d)

    hi_lo = lambda a: jnp.stack(_split_bf16(a, 2))
    w_in_p, w_lr = _wprep(w_in[0].T)
    w_gk2p = hi_lo(jnp.concatenate([w_gk2[0], jnp.zeros((LANES - GLA_LOWRANK, GLA_QK), F32)], axis=0))
    w_rt = hi_lo(jnp.concatenate([w_router_expert[0], w_router_group[0],
                                  jnp.zeros((d, LANES - N_EXPERTS - N_GROUPS), F32)], axis=1))
    b_rt = jnp.concatenate([b_router_expert[0], b_router_group[0],
                            jnp.zeros((LANES - N_EXPERTS - N_GROUPS,), F32)])[None, :]

    proj = _inproj(xp, xs, ln_attn, w_in_p, w_lr)

    nw = gla_norm_w
    bgk = b_gk
    oa_p, sg_p = _gla(proj, w_gk2p, bgk, nw, None, row0=0, n_seq=bp, seq_len=tp, chunk=GLA_CHUNK, nseq=1)
    oa_s, sg_s = _gla(proj, w_gk2p, bgk, nw, state_gla[0], row0=n_p, n_seq=bs, seq_len=ts, chunk=ts, nseq=8)

    cos_p, sin_p = _rope_tables(jnp.arange(tp, dtype=F32), 1)
    ret_nseq_s = 4
    cos_s, sin_s = _rope_tables(jnp.arange(ts, dtype=F32) + float(PAST_LEN), ret_nseq_s)
    or_p, sr_p = _ret(proj, cos_p, sin_p, None, row0=0, n_seq=bp, seq_len=tp, chunk=RET_CHUNK, nseq=1)
    or_s, sr_s = _ret(proj, cos_s, sin_s, state_ret[0], row0=n_p, n_seq=bs, seq_len=ts, chunk=ts, nseq=ret_nseq_s)

    x1, hn, route, counts = _outproj_router(oa_p, oa_s, or_p, or_s, xp, xs, w_out[0].astype(BF16),
                                            ln_ffn, w_rt, b_rt)

    t_exp, n_valid, w_slot, next_exp, pos = _routing_tables(route, counts, n)
    y_sorted = _moe(t_exp, n_valid, w_slot, next_exp, pos, hn, w_exp_gate[0], w_exp_up[0], w_exp_down[0])

    ln_f = ln_final[None, :]
    y_p = _final(pos, x1, route, ln_f, y_sorted, row0=0, rows=n_p).reshape(bp, tp, d)
    y_s = _final(pos, x1, route, ln_f, y_sorted, row0=n_p, rows=n_s).reshape(bs, ts, d)
    return (y_p, y_s, sg_p[None], sr_p[None], sg_s[None], sr_s[None])
```

```python
import functools
import math

import jax
import jax.numpy as jnp
from jax import lax
from jax.experimental import pallas as pl
from jax.experimental.pallas import tpu as pltpu

F32 = jnp.float32
BF16 = jnp.bfloat16
I32 = jnp.int32

D_MODEL = 2048
PAST_LEN = 16384
H_GLA, DK_GLA, DV_GLA = 4, 128, 256
GLA_LOWRANK = 16
GLA_GATE_NORM = 16.0
GLA_SUBCHUNK = 16
H_RET, DK_RET, DV_RET = 4, 256, 256
ROPE_BASE = 10000.0
N_GROUPS, EXPERTS_PER_GROUP, D_EXPERT = 4, 8, 512
N_EXPERTS = N_GROUPS * EXPERTS_PER_GROUP
EPS = 1e-6

LANES = 128
SUBLANES = 8
V7X_SCOPED_VMEM_BYTES = 56 * 1024 * 1024

GLA_QK = H_GLA * DK_GLA
GLA_V = H_GLA * DV_GLA
RET_QK = H_RET * DK_RET
RET_V = H_RET * DV_RET
C_QA, C_KA, C_VA, C_GA = 0, GLA_QK, 2 * GLA_QK, 2 * GLA_QK + GLA_V
C_QR = C_GA + GLA_V
C_KR, C_VR, C_GR = C_QR + RET_QK, C_QR + 2 * RET_QK, C_QR + 2 * RET_QK + RET_V
C_LR = C_GR + RET_V
P_COLS = C_LR + LANES
IN_COLS_SRC = 2 * GLA_QK + GLA_V + GLA_LOWRANK + GLA_V + 2 * RET_QK + 2 * RET_V

PROJ_TM = 256
W_CHUNK = 1024
MOE_TM = 256
GLA_CHUNK = 128
RET_CHUNK = 128

def _dot(a, b):
    return jnp.dot(a, b, preferred_element_type=F32)


def _dot_nt(a, b):
    return lax.dot_general(a, b, (((1,), (1,)), ((), ())), preferred_element_type=F32)


def _dot_tn(a, b):
    return lax.dot_general(a, b, (((0,), (0,)), ((), ())), preferred_element_type=F32)


def _split_bf16(x, parts):
    out = []
    for _ in range(parts):
        p = x.astype(BF16)
        out.append(p)
        x = x - p.astype(F32)
    return out


def _dot_split(a, b_hi, b_lo):
    a_hi, a_lo = _split_bf16(a, 2)
    return _dot(a_hi, b_hi) + (_dot(a_lo, b_hi) + _dot(a_hi, b_lo))


def _rms(x):
    return x * lax.rsqrt(jnp.mean(x * x, axis=-1, keepdims=True) + EPS)


def _silu(x):
    return x * jax.nn.sigmoid(x)


def _wprep_body(a_ref, b_ref, o_ref, olr_ref):
    j = pl.program_id(0)
    a = a_ref[...]
    b = b_ref[...]
    shifted = jnp.concatenate([a[GLA_LOWRANK:, :], b], axis=0)
    o_ref[...] = jnp.where(j >= C_GA // W_CHUNK, shifted, a).T.astype(BF16)

    @pl.when(j == C_GA // W_CHUNK - 1)
    def _():
        padded = jnp.concatenate([b, jnp.zeros((LANES - GLA_LOWRANK, D_MODEL), F32)], axis=0)
        olr_ref[...] = padded.T.astype(BF16)


def _wprep(wt):
    per = W_CHUNK // GLA_LOWRANK
    return pl.pallas_call(
        _wprep_body,
        out_shape=(jax.ShapeDtypeStruct((D_MODEL, C_LR), BF16), jax.ShapeDtypeStruct((D_MODEL, LANES), BF16)),
        grid=(C_LR // W_CHUNK,),
        in_specs=[pl.BlockSpec((W_CHUNK, D_MODEL), lambda j: (j, 0)),
                  pl.BlockSpec((GLA_LOWRANK, D_MODEL), lambda j: (per * (j + 1), 0))],
        out_specs=(pl.BlockSpec((D_MODEL, W_CHUNK), lambda j: (0, j)),
                   pl.BlockSpec((D_MODEL, LANES), lambda j: (0, 0))),
        compiler_params=pltpu.CompilerParams(dimension_semantics=("arbitrary",),
                                             vmem_limit_bytes=V7X_SCOPED_VMEM_BYTES),
        name="wprep",
    )(wt, wt)


def _inproj_body(xp_ref, xs_ref, ln_ref, w_ref, wlr_ref, o_ref, *, n_prompt_blocks):
    x = jnp.where(pl.program_id(0) < n_prompt_blocks, xp_ref[...], xs_ref[...])
    h = (_rms(x) * ln_ref[...]).astype(BF16)
    for c in range(C_LR // W_CHUNK):
        o_ref[:, c * W_CHUNK:(c + 1) * W_CHUNK] = _dot(h, w_ref[:, c * W_CHUNK:(c + 1) * W_CHUNK])
    o_ref[:, C_LR:] = _dot(h, wlr_ref[...])


def _inproj(xp, xs, ln, w, w_lr):
    nbp = xp.shape[0] // PROJ_TM
    n = xp.shape[0] + xs.shape[0]
    fixed = lambda i: (0, 0)
    return pl.pallas_call(
        functools.partial(_inproj_body, n_prompt_blocks=nbp),
        out_shape=jax.ShapeDtypeStruct((n, P_COLS), F32),
        grid=(n // PROJ_TM,),
        in_specs=[pl.BlockSpec((PROJ_TM, D_MODEL), lambda i: (jnp.minimum(i, nbp - 1), 0)),
                  pl.BlockSpec((PROJ_TM, D_MODEL), lambda i: (jnp.maximum(i - nbp, 0), 0)),
                  pl.BlockSpec((1, D_MODEL), fixed),
                  pl.BlockSpec((D_MODEL, C_LR), fixed, pipeline_mode=pl.Buffered(1)),
                  pl.BlockSpec((D_MODEL, LANES), fixed)],
        out_specs=pl.BlockSpec((PROJ_TM, P_COLS), lambda i: (i, 0)),
        compiler_params=pltpu.CompilerParams(dimension_semantics=("arbitrary",),
                                             vmem_limit_bytes=V7X_SCOPED_VMEM_BYTES),
        name="inproj",
    )(xp, xs, ln, w, w_lr)


def _seg_cumsum(x, seg):
    pos = lax.broadcasted_iota(I32, x.shape, 0) & (seg - 1)
    s = 1
    while s < seg:
        x = x + jnp.where(pos >= s, pltpu.roll(x, s, axis=0), 0.0)
        s *= 2
    return x


def _block_ends(p, m):
    rows, w = p.shape
    return jnp.concatenate(
        [jnp.broadcast_to(p[j * m + m - 1:j * m + m, :], (m, w)) for j in range(rows // m)], axis=0)


def _block_starts(p, m, chunk):
    rows, w = p.shape
    pieces = []
    for j in range(rows // m):
        if (j * m) % chunk == 0:
            pieces.append(jnp.zeros((m, w), F32))
        else:
            pieces.append(jnp.broadcast_to(p[j * m - 1:j * m, :], (m, w)))
    return jnp.concatenate(pieces, axis=0)


def _gla_body(*refs, chunk, nseq, has_init):
    if has_init:
        (q_ref, k_ref, v_ref, g_ref, lr_ref, wgk_ref, bgk_ref, nw_ref, s0_ref,
         o_ref, sout_ref, s_scr) = refs
    else:
        (q_ref, k_ref, v_ref, g_ref, lr_ref, wgk_ref, bgk_ref, nw_ref,
         o_ref, sout_ref, s_scr) = refs
    t = pl.program_id(1)

    @pl.when(t == 0)
    def _():
        if has_init:
            s_scr[...] = s0_ref[...]
        else:
            s_scr[...] = jnp.zeros_like(s_scr)

    x = _dot_split(lr_ref[...], wgk_ref[0], wgk_ref[1]) + bgk_ref[...]
    log_g = -(jnp.maximum(-x, 0.0) + jnp.log1p(jnp.exp(-jnp.abs(x)))) / GLA_GATE_NORM
    p = _seg_cumsum(log_g, chunk)
    p_end = _block_ends(p, chunk)
    eq_c = jnp.exp(p)
    ek_c = jnp.exp(p_end - p)
    sub = min(GLA_SUBCHUNK, chunk)
    p_sub = p - _block_starts(p, sub, chunk)
    eq_d = jnp.exp(p_sub)
    ek_d = jnp.exp(-p_sub)
    levels = []
    m = sub
    while m < chunk:
        levels.append((m, jnp.exp(p - _block_starts(p, m, chunk)), jnp.exp(_block_ends(p, m) - p)))
        m *= 2

    ti = lax.broadcasted_iota(I32, (chunk, chunk), 0)
    si = lax.broadcasted_iota(I32, (chunk, chunk), 1)
    blk = lambda idx, size: idx >> (size.bit_length() - 1)
    mask_d = (blk(ti, sub) == blk(si, sub)) & (si <= ti)
    masks = [((blk(ti, m) & 1) == 1) & (blk(si, m) == blk(ti, m) - 1) for (m, _, _) in levels]
    sel8 = (lax.broadcasted_iota(I32, (8, DV_GLA), 0) == 0).astype(BF16)
    nw = nw_ref[...]

    outs = [[None] * nseq for _ in range(H_GLA)]
    for b in range(nseq):
        r0 = b * chunk
        rows = slice(r0, r0 + chunk)
        for h in range(H_GLA):
            kc = slice(h * DK_GLA, (h + 1) * DK_GLA)
            vc = slice(h * DV_GLA, (h + 1) * DV_GLA)
            q = q_ref[rows, kc] * (DK_GLA ** -0.5)
            k = k_ref[rows, kc]
            v = v_ref[rows, vc].astype(BF16)
            att = jnp.where(mask_d, _dot_nt((q * eq_d[rows, kc]).astype(BF16),
                                            (k * ek_d[rows, kc]).astype(BF16)), 0.0)
            for (m, eq_m, ek_m), mask in zip(levels, masks):
                att = jnp.where(mask, _dot_nt((q * eq_m[rows, kc]).astype(BF16),
                                              (k * ek_m[rows, kc]).astype(BF16)), att)
            s_old = s_scr[b, h]
            o = _dot(att.astype(BF16), v) + _dot((q * eq_c[rows, kc]).astype(BF16), s_old.astype(BF16))
            kv = _dot_tn((k * ek_c[rows, kc]).astype(BF16), v)
            tot = jnp.broadcast_to(p[r0 + chunk - 1:r0 + chunk, kc], (8, DK_GLA))
            t_hi, t_mid, t_lo = _split_bf16(tot, 3)
            decay = jnp.exp(_dot_tn(t_hi, sel8) + (_dot_tn(t_mid, sel8) + _dot_tn(t_lo, sel8)))
            s_scr[b, h] = decay * s_old + kv
            outs[h][b] = _rms(o) * nw * _silu(g_ref[rows, vc])
    for h in range(H_GLA):
        o_ref[:, h * DV_GLA:(h + 1) * DV_GLA] = jnp.concatenate(outs[h], axis=0).astype(BF16)

    @pl.when(t == pl.num_programs(1) - 1)
    def _():
        sout_ref[...] = s_scr[...]


def _gla(proj, w_gk2p, b_gk, nw, s0, *, row0, n_seq, seq_len, chunk, nseq):
    rows = nseq * chunk
    nt = seq_len // chunk
    base = row0 // rows

    def rmap(col):
        return lambda b, t: (base + b * nt + t, col)

    in_specs = [pl.BlockSpec((rows, GLA_QK), rmap(C_QA // GLA_QK)),
                pl.BlockSpec((rows, GLA_QK), rmap(C_KA // GLA_QK)),
                pl.BlockSpec((rows, GLA_V), rmap(C_VA // GLA_V)),
                pl.BlockSpec((rows, GLA_V), rmap(C_GA // GLA_V)),
                pl.BlockSpec((rows, LANES), rmap(C_LR // LANES)),
                pl.BlockSpec((2, LANES, GLA_QK), lambda b, t: (0, 0, 0)),
                pl.BlockSpec((1, GLA_QK), lambda b, t: (0, 0)),
                pl.BlockSpec((1, DV_GLA), lambda b, t: (0, 0))]
    args = [proj, proj, proj, proj, proj, w_gk2p, b_gk, nw]
    state_spec = pl.BlockSpec((nseq, H_GLA, DK_GLA, DV_GLA), lambda b, t: (b, 0, 0, 0))
    if s0 is not None:
        in_specs.append(state_spec)
        args.append(s0)
    return pl.pallas_call(
        functools.partial(_gla_body, chunk=chunk, nseq=nseq, has_init=s0 is not None),
        out_shape=(jax.ShapeDtypeStruct((n_seq * seq_len, GLA_V), BF16),
                   jax.ShapeDtypeStruct((n_seq, H_GLA, DK_GLA, DV_GLA), F32)),
        grid=(n_seq // nseq, nt),
        in_specs=in_specs,
        out_specs=(pl.BlockSpec((rows, GLA_V), lambda b, t: (b * nt + t, 0)), state_spec),
        scratch_shapes=[pltpu.VMEM((nseq, H_GLA, DK_GLA, DV_GLA), F32)],
        compiler_params=pltpu.CompilerParams(dimension_semantics=("arbitrary", "arbitrary"),
                                             vmem_limit_bytes=V7X_SCOPED_VMEM_BYTES),
        name="gla_init" if s0 is not None else "gla",
    )(*args)


def _ret_body(*refs, chunk, nseq, has_init):
    if has_init:
        (q_ref, k_ref, v_ref, g_ref, cos_ref, sin_ref, s0_ref, o_ref, sout_ref, s_scr) = refs
    else:
        (q_ref, k_ref, v_ref, g_ref, cos_ref, sin_ref, o_ref, sout_ref, s_scr) = refs
    t = pl.program_id(1)

    @pl.when(t == 0)
    def _():
        if has_init:
            s_scr[...] = s0_ref[...]
        else:
            s_scr[...] = jnp.zeros_like(s_scr)

    half = DK_RET // 2
    ti = lax.broadcasted_iota(I32, (chunk, chunk), 0)
    si = lax.broadcasted_iota(I32, (chunk, chunk), 1)
    diff = (ti - si).astype(F32)
    idx = lax.broadcasted_iota(I32, (chunk, 1), 0).astype(F32)

    outs = [[None] * nseq for _ in range(H_RET)]
    for h in range(H_RET):
        lg = math.log(1.0 - 2.0 ** (-5.0 - h))
        dmat = jnp.where(diff >= 0, jnp.exp(lg * jnp.maximum(diff, 0.0)), 0.0)
        q_dec = jnp.exp(lg * (idx + 1.0))
        k_dec = jnp.exp(lg * (chunk - 1.0 - idx))
        c_dec = math.exp(lg * chunk)
        for b in range(nseq):
            rows = slice(b * chunk, (b + 1) * chunk)
            cos = cos_ref[rows, :]
            sin = sin_ref[rows, :]
            c1 = slice(h * DK_RET, h * DK_RET + half)
            c2 = slice(h * DK_RET + half, (h + 1) * DK_RET)
            vc = slice(h * DV_RET, (h + 1) * DV_RET)
            q1, q2 = q_ref[rows, c1], q_ref[rows, c2]
            k1, k2 = k_ref[rows, c1], k_ref[rows, c2]
            q = jnp.concatenate([q1 * cos - q2 * sin, q1 * sin + q2 * cos], axis=-1)
            k = jnp.concatenate([k1 * cos - k2 * sin, k1 * sin + k2 * cos], axis=-1) * (DK_RET ** -0.5)
            v = v_ref[rows, vc].astype(BF16)
            qb = q.astype(BF16)
            s_old = s_scr[b, h]
            att = _dot_nt(qb, k.astype(BF16)) * dmat
            o = _dot(att.astype(BF16), v) + _dot(qb, s_old.astype(BF16)) * q_dec
            s_scr[b, h] = c_dec * s_old + _dot_tn((k * k_dec).astype(BF16), v)
            outs[h][b] = _rms(o) * _silu(g_ref[rows, vc])
    for h in range(H_RET):
        o_ref[:, h * DV_RET:(h + 1) * DV_RET] = jnp.concatenate(outs[h], axis=0).astype(BF16)

    @pl.when(t == pl.num_programs(1) - 1)
    def _():
        sout_ref[...] = s_scr[...]


def _ret(proj, cos, sin, s0, *, row0, n_seq, seq_len, chunk, nseq):
    rows = nseq * chunk
    nt = seq_len // chunk
    base = row0 // rows

    def rmap(col):
        return lambda b, t: (base + b * nt + t, col)

    tab_spec = pl.BlockSpec((rows, DK_RET // 2), lambda b, t: (t, 0))
    in_specs = [pl.BlockSpec((rows, RET_QK), rmap(C_QR // RET_QK)),
                pl.BlockSpec((rows, RET_QK), rmap(C_KR // RET_QK)),
                pl.BlockSpec((rows, RET_V), rmap(C_VR // RET_V)),
                pl.BlockSpec((rows, RET_V), rmap(C_GR // RET_V)),
                tab_spec, tab_spec]
    args = [proj, proj, proj, proj, cos, sin]
    state_spec = pl.BlockSpec((nseq, H_RET, DK_RET, DV_RET), lambda b, t: (b, 0, 0, 0))
    if s0 is not None:
        in_specs.append(state_spec)
        args.append(s0)
    return pl.pallas_call(
        functools.partial(_ret_body, chunk=chunk, nseq=nseq, has_init=s0 is not None),
        out_shape=(jax.ShapeDtypeStruct((n_seq * seq_len, RET_V), BF16),
                   jax.ShapeDtypeStruct((n_seq, H_RET, DK_RET, DV_RET), F32)),
        grid=(n_seq // nseq, nt),
        in_specs=in_specs,
        out_specs=(pl.BlockSpec((rows, RET_V), lambda b, t: (b * nt + t, 0)), state_spec),
        scratch_shapes=[pltpu.VMEM((nseq, H_RET, DK_RET, DV_RET), F32)],
        compiler_params=pltpu.CompilerParams(dimension_semantics=("arbitrary", "arbitrary"),
                                             vmem_limit_bytes=V7X_SCOPED_VMEM_BYTES),
        name="ret_init" if s0 is not None else "ret",
    )(*args)


def _rope_tables(pos, reps):
    half = DK_RET // 2
    inv = ROPE_BASE ** (-jnp.arange(half, dtype=F32) / half)
    ang = pos[:, None] * inv[None, :]
    return jnp.tile(jnp.cos(ang), (reps, 1)), jnp.tile(jnp.sin(ang), (reps, 1))


def _outproj_body(oap_ref, oas_ref, orp_ref, ors_ref, xp_ref, xs_ref, w_ref, ln_ref, wrt_ref, brt_ref,
                  x1_ref, hn_ref, route_ref, counts_ref, *, n_prompt_blocks):
    step = pl.program_id(0)
    is_prompt = step < n_prompt_blocks
    oa = jnp.where(is_prompt, oap_ref[...], oas_ref[...])
    orr = jnp.where(is_prompt, orp_ref[...], ors_ref[...])
    x = jnp.where(is_prompt, xp_ref[...], xs_ref[...])
    x1 = x + _dot(oa, w_ref[:GLA_V, :]) + _dot(orr, w_ref[GLA_V:, :])
    x1_ref[...] = x1
    hn = _rms(x1) * ln_ref[...]
    hn_ref[...] = hn

    logits = _dot_split(hn, wrt_ref[0], wrt_ref[1]) + brt_ref[...]
    lane_i = lax.broadcasted_iota(I32, logits.shape, 1)
    lane = lane_i.astype(F32)
    grp = (lane_i >> (EXPERTS_PER_GROUP.bit_length() - 1)).astype(F32)
    neg = -jnp.inf
    far = float(LANES)

    gl = jnp.where((lane_i >= N_EXPERTS) & (lane_i < N_EXPERTS + N_GROUPS), logits, neg)
    gmax = jnp.max(gl, axis=-1, keepdims=True)
    g_w = 1.0 / jnp.sum(jnp.exp(gl - gmax), axis=-1, keepdims=True)
    g_idx = jnp.min(jnp.where(gl == gmax, lane, far), axis=-1, keepdims=True) - float(N_EXPERTS)

    el = jnp.where((lane_i < N_EXPERTS) & (grp == g_idx), logits, neg)
    m1 = jnp.max(el, axis=-1, keepdims=True)
    esum = jnp.sum(jnp.exp(el - m1), axis=-1, keepdims=True)
    i1 = jnp.min(jnp.where(el == m1, lane, far), axis=-1, keepdims=True)
    el2 = jnp.where(lane == i1, neg, el)
    m2 = jnp.max(el2, axis=-1, keepdims=True)
    i2 = jnp.min(jnp.where(el2 == m2, lane, far), axis=-1, keepdims=True)
    p1 = 1.0 / esum
    p2 = jnp.exp(m2 - m1) / esum
    gate1 = g_w * (p1 / (p1 + p2))
    gate2 = g_w * (p2 / (p1 + p2))
    @pl.when(step == 0)
    def _():
        counts_ref[...] = jnp.zeros_like(counts_ref)

    tm = logits.shape[0]
    earlier = (lax.broadcasted_iota(I32, (tm, tm), 0) > lax.broadcasted_iota(I32, (tm, tm), 1)).astype(BF16)
    running = counts_ref[0:1, :]
    ranks = []
    for idx in (i1, i2):
        onehot = (lane == idx).astype(F32)
        before = _dot(earlier, onehot.astype(BF16)) + running
        ranks.append(jnp.sum(onehot * before, axis=-1, keepdims=True))
        running = running + jnp.sum(onehot, axis=0, keepdims=True)
    counts_ref[...] = jnp.broadcast_to(running, counts_ref.shape)

    route_ref[...] = jnp.where(lane_i == 0, i1,
                               jnp.where(lane_i == 1, i2,
                                         jnp.where(lane_i == 2, gate1,
                                                   jnp.where(lane_i == 3, gate2,
                                                             jnp.where(lane_i == 4, ranks[0],
                                                                       jnp.where(lane_i == 5, ranks[1], 0.0))))))


def _outproj_router(oa_p, oa_s, or_p, or_s, xp, xs, w_out, ln, w_rt, b_rt):
    n = xp.shape[0] + xs.shape[0]
    nbp = xp.shape[0] // PROJ_TM
    pmap = lambda i: (jnp.minimum(i, nbp - 1), 0)
    smap = lambda i: (jnp.maximum(i - nbp, 0), 0)
    row = lambda i: (i, 0)
    fixed = lambda i: (0, 0)
    return pl.pallas_call(
        functools.partial(_outproj_body, n_prompt_blocks=nbp),
        out_shape=(jax.ShapeDtypeStruct((n, D_MODEL), F32),
                   jax.ShapeDtypeStruct((n, D_MODEL), F32),
                   jax.ShapeDtypeStruct((n, LANES), F32),
                   jax.ShapeDtypeStruct((SUBLANES, LANES), F32)),
        grid=(n // PROJ_TM,),
        in_specs=[pl.BlockSpec((PROJ_TM, GLA_V), pmap), pl.BlockSpec((PROJ_TM, GLA_V), smap),
                  pl.BlockSpec((PROJ_TM, RET_V), pmap), pl.BlockSpec((PROJ_TM, RET_V), smap),
                  pl.BlockSpec((PROJ_TM, D_MODEL), pmap), pl.BlockSpec((PROJ_TM, D_MODEL), smap),
                  pl.BlockSpec((GLA_V + RET_V, D_MODEL), fixed, pipeline_mode=pl.Buffered(1)),
                  pl.BlockSpec((1, D_MODEL), fixed),
                  pl.BlockSpec((2, D_MODEL, LANES), lambda i: (0, 0, 0)),
                  pl.BlockSpec((1, LANES), fixed)],
        out_specs=(pl.BlockSpec((PROJ_TM, D_MODEL), row), pl.BlockSpec((PROJ_TM, D_MODEL), row),
                   pl.BlockSpec((PROJ_TM, LANES), row), pl.BlockSpec((SUBLANES, LANES), fixed)),
        compiler_params=pltpu.CompilerParams(dimension_semantics=("arbitrary",),
                                             vmem_limit_bytes=V7X_SCOPED_VMEM_BYTES),
        name="outproj_router",
    )(oa_p, oa_s, or_p, or_s, xp, xs, w_out, ln, w_rt, b_rt)


def _row_dma_wait(buf, sem, s):
    pltpu.make_async_copy(buf.at[s], buf.at[s], sem.at[s]).wait()


MOE_XBUFS = 3


def _dispatch_body(nval_ref, pos_ref, hn_hbm, o_ref, inv, zeros_v, xbuf, gsem):
    tm = MOE_TM
    i = pl.program_id(0)
    nt = pl.num_programs(0)
    nv = nval_ref[i]
    nv_prev = jnp.where(i >= 1, nval_ref[jnp.maximum(i - 1, 0)], 0)

    def gather_tile(tile, s):
        for lo in range(0, tm, SUBLANES):
            ds = [inv[tile * tm + lo + j] for j in range(SUBLANES)]
            for j in range(SUBLANES):
                pltpu.make_async_copy(hn_hbm.at[pl.ds(ds[j] >> 1, 1), :],
                                      xbuf.at[s, lo // SUBLANES, pl.ds(j, 1), :], gsem.at[s]).start()

    @pl.when(i == 0)
    def _():
        zeros_v[...] = jnp.zeros_like(zeros_v)
        clear = pltpu.make_async_copy(zeros_v, inv, gsem.at[0])
        clear.start()
        clear.wait()

        def body(a, c):
            inv[pos_ref[a]] = a
            return c
        lax.fori_loop(0, pos_ref.shape[0], body, 0, unroll=8)
        gather_tile(0, 0)
        gather_tile(1, 1)

    def step(s):
        @pl.when(nv > 0)
        def _():
            @pl.when(i + 2 < nt)
            def _():
                gather_tile(i + 2, (s + 2) % MOE_XBUFS)
            _row_dma_wait(xbuf, gsem, s)
            o_ref[...] = xbuf[s].reshape(tm, D_MODEL).astype(BF16)

        @pl.when(nv == 0)
        def _():
            o_ref[...] = jnp.zeros_like(o_ref)

            @pl.when(nv_prev > 0)
            def _():
                _row_dma_wait(xbuf, gsem, s)

                @pl.when(i + 1 < nt)
                def _():
                    _row_dma_wait(xbuf, gsem, (s + 1) % MOE_XBUFS)

    for s in range(MOE_XBUFS):
        pl.when(i % MOE_XBUFS == s)(functools.partial(step, s))


def _dispatch(n_valid, pos, hn):
    n_tiles = n_valid.shape[0]
    assert n_tiles >= 2
    return pl.pallas_call(
        _dispatch_body,
        out_shape=jax.ShapeDtypeStruct((n_tiles * MOE_TM, D_MODEL), BF16),
        grid_spec=pltpu.PrefetchScalarGridSpec(
            num_scalar_prefetch=2,
            grid=(n_tiles,),
            in_specs=[pl.BlockSpec(memory_space=pl.ANY)],
            out_specs=pl.BlockSpec((MOE_TM, D_MODEL), lambda i, *_: (i, 0)),
            scratch_shapes=[pltpu.SMEM((n_tiles * MOE_TM,), I32),
                            pltpu.VMEM((n_tiles * MOE_TM,), I32),
                            pltpu.VMEM((MOE_XBUFS, MOE_TM // SUBLANES, SUBLANES, D_MODEL), F32),
                            pltpu.SemaphoreType.DMA((MOE_XBUFS,))]),
        compiler_params=pltpu.CompilerParams(dimension_semantics=("arbitrary",)),
        name="dispatch",
    )(n_valid, pos, hn)


def _moe_body(texp_ref, nval_ref, wslot_ref, nexp_ref, x_ref, wg_hbm, wu_hbm, wd_hbm, o_ref,
              wg_f, wu_f, wd_f, wg_b, wu_b, wd_b, wsem):
    i = pl.program_id(0)
    nv = nval_ref[i]

    def weight_copies(e, p):
        return [pltpu.make_async_copy(src.at[e], dst.at[p], wsem.at[p])
                for src, dst in ((wg_hbm, wg_f), (wu_hbm, wu_f), (wd_hbm, wd_f))]

    @pl.when(i == 0)
    def _():
        for c in weight_copies(texp_ref[0], wslot_ref[0]):
            c.start()

    @pl.when(nv > 0)
    def _():
        @pl.when((i == 0) | (texp_ref[i] != texp_ref[jnp.maximum(i - 1, 0)]))
        def _():
            p = wslot_ref[i]
            for c in weight_copies(0, p):
                c.wait()
            wg_b[...] = wg_f[p].astype(BF16)
            wu_b[...] = wu_f[p].astype(BF16)
            wd_b[...] = wd_f[p].astype(BF16)

            @pl.when(nexp_ref[i] >= 0)
            def _():
                for c in weight_copies(nexp_ref[i], 1 - p):
                    c.start()

        h = x_ref[...]
        hid = _silu(_dot(h, wg_b[...])) * _dot(h, wu_b[...])
        o_ref[...] = _dot(hid.astype(BF16), wd_b[...])

    @pl.when(nv == 0)
    def _():
        o_ref[...] = jnp.zeros_like(o_ref)


def _moe(tile_expert, n_valid, w_slot, next_expert, x_sorted, w_gate, w_up, w_down):
    n_tiles = tile_expert.shape[0]
    any_spec = pl.BlockSpec(memory_space=pl.ANY)
    tile = lambda i, *_: (i, 0)
    return pl.pallas_call(
        _moe_body,
        out_shape=jax.ShapeDtypeStruct((n_tiles * MOE_TM, D_MODEL), F32),
        grid_spec=pltpu.PrefetchScalarGridSpec(
            num_scalar_prefetch=4,
            grid=(n_tiles,),
            in_specs=[pl.BlockSpec((MOE_TM, D_MODEL), tile), any_spec, any_spec, any_spec],
            out_specs=pl.BlockSpec((MOE_TM, D_MODEL), tile),
            scratch_shapes=[pltpu.VMEM((2, D_MODEL, D_EXPERT), F32),
                            pltpu.VMEM((2, D_MODEL, D_EXPERT), F32),
                            pltpu.VMEM((2, D_EXPERT, D_MODEL), F32),
                            pltpu.VMEM((D_MODEL, D_EXPERT), BF16),
                            pltpu.VMEM((D_MODEL, D_EXPERT), BF16),
                            pltpu.VMEM((D_EXPERT, D_MODEL), BF16),
                            pltpu.SemaphoreType.DMA((2,))]),
        compiler_params=pltpu.CompilerParams(dimension_semantics=("arbitrary",),
                                             vmem_limit_bytes=V7X_SCOPED_VMEM_BYTES),
        name="moe",
    )(tile_expert, n_valid, w_slot, next_expert, x_sorted, w_gate, w_up, w_down)


def _routing_tables(route, counts, n):
    tm = MOE_TM
    n_tiles = 2 * n // tm + N_EXPERTS
    experts = route[:, 0:2].astype(I32)
    rank = route[:, 4:6].astype(I32)
    counts = counts[0, :N_EXPERTS].astype(I32)
    tiles = (counts + tm - 1) // tm
    tile_end = jnp.cumsum(tiles)
    tile_start = tile_end - tiles
    onehot = experts[:, :, None] == jnp.arange(N_EXPERTS, dtype=I32)[None, None, :]
    pos = jnp.sum(jnp.where(onehot, (tile_start * tm)[None, None, :], 0), axis=-1) + rank
    tidx = jnp.arange(n_tiles, dtype=I32)
    t_exp = jnp.sum((tidx[:, None] >= tile_end[None, :]).astype(I32), axis=1)
    used = tidx < tile_end[-1]
    last_exp = jnp.sum((tile_end[-1] - 1 >= tile_end).astype(I32))
    t_exp = jnp.where(used, t_exp, last_exp)
    n_valid = jnp.where(used, jnp.clip(counts[t_exp] - (tidx - tile_start[t_exp]) * tm, 0, tm), 0)
    eidx = jnp.arange(N_EXPERTS, dtype=I32)
    owns = tiles > 0
    slot_e = (jnp.cumsum(owns.astype(I32)) - 1) & 1
    later = jnp.where(owns[None, :] & (eidx[None, :] > eidx[:, None]), eidx[None, :], N_EXPERTS)
    next_e = jnp.min(later, axis=1)
    next_e = jnp.where(next_e < N_EXPERTS, next_e, -1)
    return (t_exp.astype(I32), n_valid.astype(I32), slot_e[t_exp].astype(I32), next_e[t_exp].astype(I32),
            pos.reshape(2 * n))


def _final_body(pos_ref, x1_ref, route_ref, ln_ref, y_hbm, o_ref, ybuf, sem, *, row0):
    tm = PROJ_TM
    i = pl.program_id(0)
    n_steps = pl.num_programs(0)

    def fetch(step, b):
        a0 = 2 * (row0 + step * tm)
        for g in range(tm // SUBLANES):
            ps = [pos_ref[a0 + 2 * (g * SUBLANES + j) + k] for j in range(SUBLANES) for k in range(2)]
            for j in range(SUBLANES):
                for k in range(2):
                    pltpu.make_async_copy(y_hbm.at[pl.ds(ps[2 * j + k], 1), :],
                                          ybuf.at[b, k * (tm // SUBLANES) + g, pl.ds(j, 1), :], sem.at[b]).start()

    @pl.when(i == 0)
    def _():
        fetch(0, 0)

    def step(b):
        @pl.when(i + 1 < n_steps)
        def _():
            fetch(i + 1, 1 - b)
        _row_dma_wait(ybuf, sem, b)
        y = ybuf[b].reshape(2, tm, D_MODEL)
        z = x1_ref[...] + route_ref[:, 2:3] * y[0] + route_ref[:, 3:4] * y[1]
        o_ref[...] = _rms(z) * ln_ref[...]

    for b in range(2):
        pl.when(i % 2 == b)(functools.partial(step, b))


def _final(pos, x1, route, ln, y_sorted, *, row0, rows):
    base = row0 // PROJ_TM
    return pl.pallas_call(
        functools.partial(_final_body, row0=row0),
        out_shape=jax.ShapeDtypeStruct((rows, D_MODEL), F32),
        grid_spec=pltpu.PrefetchScalarGridSpec(
            num_scalar_prefetch=1,
            grid=(rows // PROJ_TM,),
            in_specs=[pl.BlockSpec((PROJ_TM, D_MODEL), lambda i, pos_: (base + i, 0)),
                      pl.BlockSpec((PROJ_TM, LANES), lambda i, pos_: (base + i, 0)),
                      pl.BlockSpec((1, D_MODEL), lambda i, pos_: (0, 0)),
                      pl.BlockSpec(memory_space=pl.ANY)],
            out_specs=pl.BlockSpec((PROJ_TM, D_MODEL), lambda i, pos_: (i, 0)),
            scratch_shapes=[pltpu.VMEM((2, 2 * PROJ_TM // SUBLANES, SUBLANES, D_MODEL), F32),
                            pltpu.SemaphoreType.DMA((2,))]),
        compiler_params=pltpu.CompilerParams(dimension_semantics=("arbitrary",),
                                             vmem_limit_bytes=V7X_SCOPED_VMEM_BYTES),
        name="final_norm",
    )(pos, x1, route, ln, y_sorted)


def kernel(x_prompt, x_sample, state_gla, state_ret, ln_attn, w_in, w_gk2, b_gk, gla_norm_w, w_out, ln_ffn, w_router_group, b_router_group, w_router_expert, b_router_expert, w_exp_gate, w_exp_up, w_exp_down, ln_final):
    bp, tp, d = x_prompt.shape
    bs, ts, _ = x_sample.shape
    assert d == D_MODEL and w_in.shape == (1, D_MODEL, IN_COLS_SRC)
    n_p, n_s = bp * tp, bs * ts
    n = n_p + n_s
    assert n_p % PROJ_TM == 0 and n_s % PROJ_TM == 0 and tp % GLA_CHUNK == 0 and tp % RET_CHUNK == 0

    xp = x_prompt.reshape(n_p, d)
    xs = x_sample.reshape(n_s, d)

    hi_lo = lambda a: jnp.stack(_split_bf16(a, 2))
    w_in_p, w_lr = _wprep(w_in[0].T)
    w_gk2p = hi_lo(jnp.concatenate([w_gk2[0], jnp.zeros((LANES - GLA_LOWRANK, GLA_QK), F32)], axis=0))
    w_rt = hi_lo(jnp.concatenate([w_router_expert[0], w_router_group[0],
                                  jnp.zeros((d, LANES - N_EXPERTS - N_GROUPS), F32)], axis=1))
    b_rt = jnp.concatenate([b_router_expert[0], b_router_group[0],
                            jnp.zeros((LANES - N_EXPERTS - N_GROUPS,), F32)])[None, :]

    proj = _inproj(xp, xs, ln_attn, w_in_p, w_lr)

    nw = gla_norm_w
    bgk = b_gk
    oa_p, sg_p = _gla(proj, w_gk2p, bgk, nw, None, row0=0, n_seq=bp, seq_len=tp, chunk=GLA_CHUNK, nseq=1)
    oa_s, sg_s = _gla(proj, w_gk2p, bgk, nw, state_gla[0], row0=n_p, n_seq=bs, seq_len=ts, chunk=ts, nseq=8)

    cos_p, sin_p = _rope_tables(jnp.arange(tp, dtype=F32), 1)
    ret_nseq_s = 4
    cos_s, sin_s = _rope_tables(jnp.arange(ts, dtype=F32) + float(PAST_LEN), ret_nseq_s)
    or_p, sr_p = _ret(proj, cos_p, sin_p, None, row0=0, n_seq=bp, seq_len=tp, chunk=RET_CHUNK, nseq=1)
    or_s, sr_s = _ret(proj, cos_s, sin_s, state_ret[0], row0=n_p, n_seq=bs, seq_len=ts, chunk=ts, nseq=ret_nseq_s)

    x1, hn, route, counts = _outproj_router(oa_p, oa_s, or_p, or_s, xp, xs, w_out[0].astype(BF16),
                                            ln_ffn, w_rt, b_rt)

    t_exp, n_valid, w_slot, next_exp, pos = _routing_tables(route, counts, n)
    x_sorted = _dispatch(n_valid, pos, hn)
    y_sorted = _moe(t_exp, n_valid, w_slot, next_exp, x_sorted, w_exp_gate[0], w_exp_up[0], w_exp_down[0])

    ln_f = ln_final[None, :]
    y_p = _final(pos, x1, route, ln_f, y_sorted, row0=0, rows=n_p).reshape(bp, tp, d)
    y_s = _final(pos, x1, route, ln_f, y_sorted, row0=n_p, rows=n_s).reshape(bs, ts, d)
    return (y_p, y_s, sg_p[None], sr_p[None], sg_s[None], sr_s[None])
```

```python
import functools
import math

import jax
import jax.numpy as jnp
from jax import lax
from jax.experimental import pallas as pl
from jax.experimental.pallas import tpu as pltpu

F32 = jnp.float32
BF16 = jnp.bfloat16
I32 = jnp.int32

D_MODEL = 2048
PAST_LEN = 16384
H_GLA, DK_GLA, DV_GLA = 4, 128, 256
GLA_LOWRANK = 16
GLA_GATE_NORM = 16.0
GLA_SUBCHUNK = 16
H_RET, DK_RET, DV_RET = 4, 256, 256
ROPE_BASE = 10000.0
N_GROUPS, EXPERTS_PER_GROUP, D_EXPERT = 4, 8, 512
N_EXPERTS = N_GROUPS * EXPERTS_PER_GROUP
EPS = 1e-6

LANES = 128
SUBLANES = 8
V7X_SCOPED_VMEM_BYTES = 56 * 1024 * 1024

GLA_QK = H_GLA * DK_GLA
GLA_V = H_GLA * DV_GLA
RET_QK = H_RET * DK_RET
RET_V = H_RET * DV_RET
C_QA, C_KA, C_VA, C_GA = 0, GLA_QK, 2 * GLA_QK, 2 * GLA_QK + GLA_V
C_QR = C_GA + GLA_V
C_KR, C_VR, C_GR = C_QR + RET_QK, C_QR + 2 * RET_QK, C_QR + 2 * RET_QK + RET_V
C_LR = C_GR + RET_V
P_COLS = C_LR + LANES
IN_COLS_SRC = 2 * GLA_QK + GLA_V + GLA_LOWRANK + GLA_V + 2 * RET_QK + 2 * RET_V

PROJ_TM = 256
W_CHUNK = 1024
MOE_TM = 256
HN_ROW_STRIDE = 7919
GLA_CHUNK = 128
RET_CHUNK = 128

def _dot(a, b):
    return jnp.dot(a, b, preferred_element_type=F32)


def _dot_nt(a, b):
    return lax.dot_general(a, b, (((1,), (1,)), ((), ())), preferred_element_type=F32)


def _dot_tn(a, b):
    return lax.dot_general(a, b, (((0,), (0,)), ((), ())), preferred_element_type=F32)


def _split_bf16(x, parts):
    out = []
    for _ in range(parts):
        p = x.astype(BF16)
        out.append(p)
        x = x - p.astype(F32)
    return out


def _dot_split(a, b_hi, b_lo):
    a_hi, a_lo = _split_bf16(a, 2)
    return _dot(a_hi, b_hi) + (_dot(a_lo, b_hi) + _dot(a_hi, b_lo))


def _rms(x):
    return x * lax.rsqrt(jnp.mean(x * x, axis=-1, keepdims=True) + EPS)


def _silu(x):
    return x * jax.nn.sigmoid(x)


def _wprep_body(a_ref, b_ref, o_ref, olr_ref):
    j = pl.program_id(0)
    a = a_ref[...]
    b = b_ref[...]
    shifted = jnp.concatenate([a[GLA_LOWRANK:, :], b], axis=0)
    o_ref[...] = jnp.where(j >= C_GA // W_CHUNK, shifted, a).T.astype(BF16)

    @pl.when(j == C_GA // W_CHUNK - 1)
    def _():
        padded = jnp.concatenate([b, jnp.zeros((LANES - GLA_LOWRANK, D_MODEL), F32)], axis=0)
        olr_ref[...] = padded.T.astype(BF16)


def _wprep(wt):
    per = W_CHUNK // GLA_LOWRANK
    return pl.pallas_call(
        _wprep_body,
        out_shape=(jax.ShapeDtypeStruct((D_MODEL, C_LR), BF16), jax.ShapeDtypeStruct((D_MODEL, LANES), BF16)),
        grid=(C_LR // W_CHUNK,),
        in_specs=[pl.BlockSpec((W_CHUNK, D_MODEL), lambda j: (j, 0)),
                  pl.BlockSpec((GLA_LOWRANK, D_MODEL), lambda j: (per * (j + 1), 0))],
        out_specs=(pl.BlockSpec((D_MODEL, W_CHUNK), lambda j: (0, j)),
                   pl.BlockSpec((D_MODEL, LANES), lambda j: (0, 0))),
        compiler_params=pltpu.CompilerParams(dimension_semantics=("arbitrary",),
                                             vmem_limit_bytes=V7X_SCOPED_VMEM_BYTES),
        name="wprep",
    )(wt, wt)


def _inproj_body(xp_ref, xs_ref, ln_ref, w_ref, wlr_ref, o_ref, *, n_prompt_blocks):
    x = jnp.where(pl.program_id(0) < n_prompt_blocks, xp_ref[...], xs_ref[...])
    h = (_rms(x) * ln_ref[...]).astype(BF16)
    for c in range(C_LR // W_CHUNK):
        o_ref[:, c * W_CHUNK:(c + 1) * W_CHUNK] = _dot(h, w_ref[:, c * W_CHUNK:(c + 1) * W_CHUNK])
    o_ref[:, C_LR:] = _dot(h, wlr_ref[...])


def _inproj(xp, xs, ln, w, w_lr):
    nbp = xp.shape[0] // PROJ_TM
    n = xp.shape[0] + xs.shape[0]
    fixed = lambda i: (0, 0)
    return pl.pallas_call(
        functools.partial(_inproj_body, n_prompt_blocks=nbp),
        out_shape=jax.ShapeDtypeStruct((n, P_COLS), F32),
        grid=(n // PROJ_TM,),
        in_specs=[pl.BlockSpec((PROJ_TM, D_MODEL), lambda i: (jnp.minimum(i, nbp - 1), 0)),
                  pl.BlockSpec((PROJ_TM, D_MODEL), lambda i: (jnp.maximum(i - nbp, 0), 0)),
                  pl.BlockSpec((1, D_MODEL), fixed),
                  pl.BlockSpec((D_MODEL, C_LR), fixed, pipeline_mode=pl.Buffered(1)),
                  pl.BlockSpec((D_MODEL, LANES), fixed)],
        out_specs=pl.BlockSpec((PROJ_TM, P_COLS), lambda i: (i, 0)),
        compiler_params=pltpu.CompilerParams(dimension_semantics=("arbitrary",),
                                             vmem_limit_bytes=V7X_SCOPED_VMEM_BYTES),
        name="inproj",
    )(xp, xs, ln, w, w_lr)


def _seg_cumsum(x, seg):
    pos = lax.broadcasted_iota(I32, x.shape, 0) & (seg - 1)
    s = 1
    while s < seg:
        x = x + jnp.where(pos >= s, pltpu.roll(x, s, axis=0), 0.0)
        s *= 2
    return x


def _block_ends(p, m):
    rows, w = p.shape
    return jnp.concatenate(
        [jnp.broadcast_to(p[j * m + m - 1:j * m + m, :], (m, w)) for j in range(rows // m)], axis=0)


def _block_starts(p, m, chunk):
    rows, w = p.shape
    pieces = []
    for j in range(rows // m):
        if (j * m) % chunk == 0:
            pieces.append(jnp.zeros((m, w), F32))
        else:
            pieces.append(jnp.broadcast_to(p[j * m - 1:j * m, :], (m, w)))
    return jnp.concatenate(pieces, axis=0)


def _gla_body(*refs, chunk, nseq, has_init):
    if has_init:
        (q_ref, k_ref, v_ref, g_ref, lr_ref, wgk_ref, bgk_ref, nw_ref, s0_ref,
         o_ref, sout_ref, s_scr) = refs
    else:
        (q_ref, k_ref, v_ref, g_ref, lr_ref, wgk_ref, bgk_ref, nw_ref,
         o_ref, sout_ref, s_scr) = refs
    t = pl.program_id(1)

    @pl.when(t == 0)
    def _():
        if has_init:
            s_scr[...] = s0_ref[...]
        else:
            s_scr[...] = jnp.zeros_like(s_scr)

    x = _dot_split(lr_ref[...], wgk_ref[0], wgk_ref[1]) + bgk_ref[...]
    log_g = -(jnp.maximum(-x, 0.0) + jnp.log1p(jnp.exp(-jnp.abs(x)))) / GLA_GATE_NORM
    p = _seg_cumsum(log_g, chunk)
    p_end = _block_ends(p, chunk)
    eq_c = jnp.exp(p)
    ek_c = jnp.exp(p_end - p)
    sub = min(GLA_SUBCHUNK, chunk)
    p_sub = p - _block_starts(p, sub, chunk)
    eq_d = jnp.exp(p_sub)
    ek_d = jnp.exp(-p_sub)
    levels = []
    m = sub
    while m < chunk:
        levels.append((m, jnp.exp(p - _block_starts(p, m, chunk)), jnp.exp(_block_ends(p, m) - p)))
        m *= 2

    ti = lax.broadcasted_iota(I32, (chunk, chunk), 0)
    si = lax.broadcasted_iota(I32, (chunk, chunk), 1)
    blk = lambda idx, size: idx >> (size.bit_length() - 1)
    mask_d = (blk(ti, sub) == blk(si, sub)) & (si <= ti)
    masks = [((blk(ti, m) & 1) == 1) & (blk(si, m) == blk(ti, m) - 1) for (m, _, _) in levels]
    sel8 = (lax.broadcasted_iota(I32, (8, DV_GLA), 0) == 0).astype(BF16)
    nw = nw_ref[...]

    outs = [[None] * nseq for _ in range(H_GLA)]
    for b in range(nseq):
        r0 = b * chunk
        rows = slice(r0, r0 + chunk)
        for h in range(H_GLA):
            kc = slice(h * DK_GLA, (h + 1) * DK_GLA)
            vc = slice(h * DV_GLA, (h + 1) * DV_GLA)
            q = q_ref[rows, kc] * (DK_GLA ** -0.5)
            k = k_ref[rows, kc]
            v = v_ref[rows, vc].astype(BF16)
            att = jnp.where(mask_d, _dot_nt((q * eq_d[rows, kc]).astype(BF16),
                                            (k * ek_d[rows, kc]).astype(BF16)), 0.0)
            for (m, eq_m, ek_m), mask in zip(levels, masks):
                att = jnp.where(mask, _dot_nt((q * eq_m[rows, kc]).astype(BF16),
                                              (k * ek_m[rows, kc]).astype(BF16)), att)
            s_old = s_scr[b, h]
            o = _dot(att.astype(BF16), v) + _dot((q * eq_c[rows, kc]).astype(BF16), s_old.astype(BF16))
            kv = _dot_tn((k * ek_c[rows, kc]).astype(BF16), v)
            tot = jnp.broadcast_to(p[r0 + chunk - 1:r0 + chunk, kc], (8, DK_GLA))
            t_hi, t_mid, t_lo = _split_bf16(tot, 3)
            decay = jnp.exp(_dot_tn(t_hi, sel8) + (_dot_tn(t_mid, sel8) + _dot_tn(t_lo, sel8)))
            s_scr[b, h] = decay * s_old + kv
            outs[h][b] = _rms(o) * nw * _silu(g_ref[rows, vc])
    for h in range(H_GLA):
        o_ref[:, h * DV_GLA:(h + 1) * DV_GLA] = jnp.concatenate(outs[h], axis=0).astype(BF16)

    @pl.when(t == pl.num_programs(1) - 1)
    def _():
        sout_ref[...] = s_scr[...]


def _gla(proj, w_gk2p, b_gk, nw, s0, *, row0, n_seq, seq_len, chunk, nseq):
    rows = nseq * chunk
    nt = seq_len // chunk
    base = row0 // rows

    def rmap(col):
        return lambda b, t: (base + b * nt + t, col)

    in_specs = [pl.BlockSpec((rows, GLA_QK), rmap(C_QA // GLA_QK)),
                pl.BlockSpec((rows, GLA_QK), rmap(C_KA // GLA_QK)),
                pl.BlockSpec((rows, GLA_V), rmap(C_VA // GLA_V)),
                pl.BlockSpec((rows, GLA_V), rmap(C_GA // GLA_V)),
                pl.BlockSpec((rows, LANES), rmap(C_LR // LANES)),
                pl.BlockSpec((2, LANES, GLA_QK), lambda b, t: (0, 0, 0)),
                pl.BlockSpec((1, GLA_QK), lambda b, t: (0, 0)),
                pl.BlockSpec((1, DV_GLA), lambda b, t: (0, 0))]
    args = [proj, proj, proj, proj, proj, w_gk2p, b_gk, nw]
    state_spec = pl.BlockSpec((nseq, H_GLA, DK_GLA, DV_GLA), lambda b, t: (b, 0, 0, 0))
    if s0 is not None:
        in_specs.append(state_spec)
        args.append(s0)
    return pl.pallas_call(
        functools.partial(_gla_body, chunk=chunk, nseq=nseq, has_init=s0 is not None),
        out_shape=(jax.ShapeDtypeStruct((n_seq * seq_len, GLA_V), BF16),
                   jax.ShapeDtypeStruct((n_seq, H_GLA, DK_GLA, DV_GLA), F32)),
        grid=(n_seq // nseq, nt),
        in_specs=in_specs,
        out_specs=(pl.BlockSpec((rows, GLA_V), lambda b, t: (b * nt + t, 0)), state_spec),
        scratch_shapes=[pltpu.VMEM((nseq, H_GLA, DK_GLA, DV_GLA), F32)],
        compiler_params=pltpu.CompilerParams(dimension_semantics=("arbitrary", "arbitrary"),
                                             vmem_limit_bytes=V7X_SCOPED_VMEM_BYTES),
        name="gla_init" if s0 is not None else "gla",
    )(*args)


def _ret_body(*refs, chunk, nseq, has_init):
    if has_init:
        (q_ref, k_ref, v_ref, g_ref, cos_ref, sin_ref, s0_ref, o_ref, sout_ref, s_scr) = refs
    else:
        (q_ref, k_ref, v_ref, g_ref, cos_ref, sin_ref, o_ref, sout_ref, s_scr) = refs
    t = pl.program_id(1)

    @pl.when(t == 0)
    def _():
        if has_init:
            s_scr[...] = s0_ref[...]
        else:
            s_scr[...] = jnp.zeros_like(s_scr)

    half = DK_RET // 2
    ti = lax.broadcasted_iota(I32, (chunk, chunk), 0)
    si = lax.broadcasted_iota(I32, (chunk, chunk), 1)
    diff = (ti - si).astype(F32)
    idx = lax.broadcasted_iota(I32, (chunk, 1), 0).astype(F32)

    outs = [[None] * nseq for _ in range(H_RET)]
    for h in range(H_RET):
        lg = math.log(1.0 - 2.0 ** (-5.0 - h))
        dmat = jnp.where(diff >= 0, jnp.exp(lg * jnp.maximum(diff, 0.0)), 0.0)
        q_dec = jnp.exp(lg * (idx + 1.0))
        k_dec = jnp.exp(lg * (chunk - 1.0 - idx))
        c_dec = math.exp(lg * chunk)
        for b in range(nseq):
            rows = slice(b * chunk, (b + 1) * chunk)
            cos = cos_ref[rows, :]
            sin = sin_ref[rows, :]
            c1 = slice(h * DK_RET, h * DK_RET + half)
            c2 = slice(h * DK_RET + half, (h + 1) * DK_RET)
            vc = slice(h * DV_RET, (h + 1) * DV_RET)
            q1, q2 = q_ref[rows, c1], q_ref[rows, c2]
            k1, k2 = k_ref[rows, c1], k_ref[rows, c2]
            q = jnp.concatenate([q1 * cos - q2 * sin, q1 * sin + q2 * cos], axis=-1)
            k = jnp.concatenate([k1 * cos - k2 * sin, k1 * sin + k2 * cos], axis=-1) * (DK_RET ** -0.5)
            v = v_ref[rows, vc].astype(BF16)
            qb = q.astype(BF16)
            s_old = s_scr[b, h]
            att = _dot_nt(qb, k.astype(BF16)) * dmat
            o = _dot(att.astype(BF16), v) + _dot(qb, s_old.astype(BF16)) * q_dec
            s_scr[b, h] = c_dec * s_old + _dot_tn((k * k_dec).astype(BF16), v)
            outs[h][b] = _rms(o) * _silu(g_ref[rows, vc])
    for h in range(H_RET):
        o_ref[:, h * DV_RET:(h + 1) * DV_RET] = jnp.concatenate(outs[h], axis=0).astype(BF16)

    @pl.when(t == pl.num_programs(1) - 1)
    def _():
        sout_ref[...] = s_scr[...]


def _ret(proj, cos, sin, s0, *, row0, n_seq, seq_len, chunk, nseq):
    rows = nseq * chunk
    nt = seq_len // chunk
    base = row0 // rows

    def rmap(col):
        return lambda b, t: (base + b * nt + t, col)

    tab_spec = pl.BlockSpec((rows, DK_RET // 2), lambda b, t: (t, 0))
    in_specs = [pl.BlockSpec((rows, RET_QK), rmap(C_QR // RET_QK)),
                pl.BlockSpec((rows, RET_QK), rmap(C_KR // RET_QK)),
                pl.BlockSpec((rows, RET_V), rmap(C_VR // RET_V)),
                pl.BlockSpec((rows, RET_V), rmap(C_GR // RET_V)),
                tab_spec, tab_spec]
    args = [proj, proj, proj, proj, cos, sin]
    state_spec = pl.BlockSpec((nseq, H_RET, DK_RET, DV_RET), lambda b, t: (b, 0, 0, 0))
    if s0 is not None:
        in_specs.append(state_spec)
        args.append(s0)
    return pl.pallas_call(
        functools.partial(_ret_body, chunk=chunk, nseq=nseq, has_init=s0 is not None),
        out_shape=(jax.ShapeDtypeStruct((n_seq * seq_len, RET_V), BF16),
                   jax.ShapeDtypeStruct((n_seq, H_RET, DK_RET, DV_RET), F32)),
        grid=(n_seq // nseq, nt),
        in_specs=in_specs,
        out_specs=(pl.BlockSpec((rows, RET_V), lambda b, t: (b * nt + t, 0)), state_spec),
        scratch_shapes=[pltpu.VMEM((nseq, H_RET, DK_RET, DV_RET), F32)],
        compiler_params=pltpu.CompilerParams(dimension_semantics=("arbitrary", "arbitrary"),
                                             vmem_limit_bytes=V7X_SCOPED_VMEM_BYTES),
        name="ret_init" if s0 is not None else "ret",
    )(*args)


def _rope_tables(pos, reps):
    half = DK_RET // 2
    inv = ROPE_BASE ** (-jnp.arange(half, dtype=F32) / half)
    ang = pos[:, None] * inv[None, :]
    return jnp.tile(jnp.cos(ang), (reps, 1)), jnp.tile(jnp.sin(ang), (reps, 1))


def _hn_row(token, n_tokens):
    return (token * HN_ROW_STRIDE) % n_tokens


def _outproj_body(oap_ref, oas_ref, orp_ref, ors_ref, xp_ref, xs_ref, w_ref, ln_ref, wrt_ref, brt_ref,
                  x1_ref, hn_hbm, route_ref, counts_ref, hbuf, hsem, *, n_prompt_blocks):
    step = pl.program_id(0)
    n_steps = pl.num_programs(0)
    tm = PROJ_TM
    n_tokens = hn_hbm.shape[0]
    is_prompt = step < n_prompt_blocks
    oa = jnp.where(is_prompt, oap_ref[...], oas_ref[...])
    orr = jnp.where(is_prompt, orp_ref[...], ors_ref[...])
    x = jnp.where(is_prompt, xp_ref[...], xs_ref[...])
    x1 = x + _dot(oa, w_ref[:GLA_V, :]) + _dot(orr, w_ref[GLA_V:, :])
    x1_ref[...] = x1
    hn = _rms(x1) * ln_ref[...]

    b = step % 2

    @pl.when(step >= 2)
    def _():
        _row_dma_wait(hbuf, hsem, b)
    hbuf[b] = hn.reshape(tm // SUBLANES, SUBLANES, D_MODEL)
    first = lax.rem(step * tm * HN_ROW_STRIDE, n_tokens)
    for r in range(tm):
        row = first + _hn_row(r, n_tokens)
        row = jnp.where(row >= n_tokens, row - n_tokens, row)
        pltpu.make_async_copy(hbuf.at[b, r // SUBLANES, pl.ds(r % SUBLANES, 1), :],
                              hn_hbm.at[pl.ds(row, 1), :], hsem.at[b]).start()

    @pl.when(step == n_steps - 1)
    def _():
        _row_dma_wait(hbuf, hsem, b)

        @pl.when(step >= 1)
        def _():
            _row_dma_wait(hbuf, hsem, 1 - b)

    logits = _dot_split(hn, wrt_ref[0], wrt_ref[1]) + brt_ref[...]
    lane_i = lax.broadcasted_iota(I32, logits.shape, 1)
    lane = lane_i.astype(F32)
    grp = (lane_i >> (EXPERTS_PER_GROUP.bit_length() - 1)).astype(F32)
    neg = -jnp.inf
    far = float(LANES)

    gl = jnp.where((lane_i >= N_EXPERTS) & (lane_i < N_EXPERTS + N_GROUPS), logits, neg)
    gmax = jnp.max(gl, axis=-1, keepdims=True)
    g_w = 1.0 / jnp.sum(jnp.exp(gl - gmax), axis=-1, keepdims=True)
    g_idx = jnp.min(jnp.where(gl == gmax, lane, far), axis=-1, keepdims=True) - float(N_EXPERTS)

    el = jnp.where((lane_i < N_EXPERTS) & (grp == g_idx), logits, neg)
    m1 = jnp.max(el, axis=-1, keepdims=True)
    esum = jnp.sum(jnp.exp(el - m1), axis=-1, keepdims=True)
    i1 = jnp.min(jnp.where(el == m1, lane, far), axis=-1, keepdims=True)
    el2 = jnp.where(lane == i1, neg, el)
    m2 = jnp.max(el2, axis=-1, keepdims=True)
    i2 = jnp.min(jnp.where(el2 == m2, lane, far), axis=-1, keepdims=True)
    p1 = 1.0 / esum
    p2 = jnp.exp(m2 - m1) / esum
    gate1 = g_w * (p1 / (p1 + p2))
    gate2 = g_w * (p2 / (p1 + p2))
    @pl.when(step == 0)
    def _():
        counts_ref[...] = jnp.zeros_like(counts_ref)

    tm = logits.shape[0]
    earlier = (lax.broadcasted_iota(I32, (tm, tm), 0) > lax.broadcasted_iota(I32, (tm, tm), 1)).astype(BF16)
    running = counts_ref[0:1, :]
    ranks = []
    for idx in (i1, i2):
        onehot = (lane == idx).astype(F32)
        before = _dot(earlier, onehot.astype(BF16)) + running
        ranks.append(jnp.sum(onehot * before, axis=-1, keepdims=True))
        running = running + jnp.sum(onehot, axis=0, keepdims=True)
    counts_ref[...] = jnp.broadcast_to(running, counts_ref.shape)

    route_ref[...] = jnp.where(lane_i == 0, i1,
                               jnp.where(lane_i == 1, i2,
                                         jnp.where(lane_i == 2, gate1,
                                                   jnp.where(lane_i == 3, gate2,
                                                             jnp.where(lane_i == 4, ranks[0],
                                                                       jnp.where(lane_i == 5, ranks[1], 0.0))))))


def _outproj_router(oa_p, oa_s, or_p, or_s, xp, xs, w_out, ln, w_rt, b_rt):
    n = xp.shape[0] + xs.shape[0]
    nbp = xp.shape[0] // PROJ_TM
    pmap = lambda i: (jnp.minimum(i, nbp - 1), 0)
    smap = lambda i: (jnp.maximum(i - nbp, 0), 0)
    row = lambda i: (i, 0)
    fixed = lambda i: (0, 0)
    return pl.pallas_call(
        functools.partial(_outproj_body, n_prompt_blocks=nbp),
        out_shape=(jax.ShapeDtypeStruct((n, D_MODEL), F32),
                   jax.ShapeDtypeStruct((n, D_MODEL), F32),
                   jax.ShapeDtypeStruct((n, LANES), F32),
                   jax.ShapeDtypeStruct((SUBLANES, LANES), F32)),
        grid=(n // PROJ_TM,),
        in_specs=[pl.BlockSpec((PROJ_TM, GLA_V), pmap), pl.BlockSpec((PROJ_TM, GLA_V), smap),
                  pl.BlockSpec((PROJ_TM, RET_V), pmap), pl.BlockSpec((PROJ_TM, RET_V), smap),
                  pl.BlockSpec((PROJ_TM, D_MODEL), pmap), pl.BlockSpec((PROJ_TM, D_MODEL), smap),
                  pl.BlockSpec((GLA_V + RET_V, D_MODEL), fixed, pipeline_mode=pl.Buffered(1)),
                  pl.BlockSpec((1, D_MODEL), fixed),
                  pl.BlockSpec((2, D_MODEL, LANES), lambda i: (0, 0, 0)),
                  pl.BlockSpec((1, LANES), fixed)],
        out_specs=(pl.BlockSpec((PROJ_TM, D_MODEL), row), pl.BlockSpec(memory_space=pl.ANY),
                   pl.BlockSpec((PROJ_TM, LANES), row), pl.BlockSpec((SUBLANES, LANES), fixed)),
        scratch_shapes=[pltpu.VMEM((2, PROJ_TM // SUBLANES, SUBLANES, D_MODEL), F32),
                        pltpu.SemaphoreType.DMA((2,))],
        compiler_params=pltpu.CompilerParams(dimension_semantics=("arbitrary",),
                                             vmem_limit_bytes=V7X_SCOPED_VMEM_BYTES),
        name="outproj_router",
    )(oa_p, oa_s, or_p, or_s, xp, xs, w_out, ln, w_rt, b_rt)


def _row_dma_wait(buf, sem, s):
    pltpu.make_async_copy(buf.at[s], buf.at[s], sem.at[s]).wait()


MOE_XBUFS = 3


def _moe_body(texp_ref, nval_ref, wslot_ref, nexp_ref, pos_ref, src_ref, hn_hbm, wg_hbm, wu_hbm, wd_hbm, o_ref,
              inv, zeros_v, xbuf, wg_f, wu_f, wd_f, wg_b, wu_b, wd_b, gsem, wsem):
    tm = MOE_TM
    i = pl.program_id(0)
    nt = pl.num_programs(0)
    nv = nval_ref[i]
    nv_prev = jnp.where(i >= 1, nval_ref[jnp.maximum(i - 1, 0)], 0)

    def gather_tile(tile, s):
        for lo in range(0, tm, SUBLANES):
            rows = [inv[tile * tm + lo + j] for j in range(SUBLANES)]
            for j in range(SUBLANES):
                pltpu.make_async_copy(hn_hbm.at[pl.ds(rows[j], 1), :],
                                      xbuf.at[s, lo // SUBLANES, pl.ds(j, 1), :], gsem.at[s]).start()

    def weight_copies(e, p):
        return [pltpu.make_async_copy(src.at[e], dst.at[p], wsem.at[p])
                for src, dst in ((wg_hbm, wg_f), (wu_hbm, wu_f), (wd_hbm, wd_f))]

    @pl.when(i == 0)
    def _():
        for c in weight_copies(texp_ref[0], wslot_ref[0]):
            c.start()
        zeros_v[...] = jnp.zeros_like(zeros_v)
        clear = pltpu.make_async_copy(zeros_v, inv, gsem.at[0])
        clear.start()
        clear.wait()

        def body(a, c):
            inv[pos_ref[a]] = src_ref[a]
            return c
        lax.fori_loop(0, pos_ref.shape[0], body, 0, unroll=8)
        gather_tile(0, 0)
        gather_tile(1, 1)

    def step(s):
        @pl.when(nv > 0)
        def _():
            @pl.when(i + 2 < nt)
            def _():
                gather_tile(i + 2, (s + 2) % MOE_XBUFS)

            @pl.when((i == 0) | (texp_ref[i] != texp_ref[jnp.maximum(i - 1, 0)]))
            def _():
                p = wslot_ref[i]
                for c in weight_copies(0, p):
                    c.wait()
                wg_b[...] = wg_f[p].astype(BF16)
                wu_b[...] = wu_f[p].astype(BF16)
                wd_b[...] = wd_f[p].astype(BF16)

                @pl.when(nexp_ref[i] >= 0)
                def _():
                    for c in weight_copies(nexp_ref[i], 1 - p):
                        c.start()

            _row_dma_wait(xbuf, gsem, s)
            h = xbuf[s].reshape(tm, D_MODEL).astype(BF16)
            hid = _silu(_dot(h, wg_b[...])) * _dot(h, wu_b[...])
            o_ref[...] = _dot(hid.astype(BF16), wd_b[...])

        @pl.when(nv == 0)
        def _():
            o_ref[...] = jnp.zeros_like(o_ref)

            @pl.when(nv_prev > 0)
            def _():
                _row_dma_wait(xbuf, gsem, s)

                @pl.when(i + 1 < nt)
                def _():
                    _row_dma_wait(xbuf, gsem, (s + 1) % MOE_XBUFS)

    for s in range(MOE_XBUFS):
        pl.when(i % MOE_XBUFS == s)(functools.partial(step, s))


def _moe(tile_expert, n_valid, w_slot, next_expert, pos, src_row, hn, w_gate, w_up, w_down):
    n_tiles = tile_expert.shape[0]
    assert n_tiles >= 2
    any_spec = pl.BlockSpec(memory_space=pl.ANY)
    return pl.pallas_call(
        _moe_body,
        out_shape=jax.ShapeDtypeStruct((n_tiles * MOE_TM, D_MODEL), F32),
        grid_spec=pltpu.PrefetchScalarGridSpec(
            num_scalar_prefetch=6,
            grid=(n_tiles,),
            in_specs=[any_spec, any_spec, any_spec, any_spec],
            out_specs=pl.BlockSpec((MOE_TM, D_MODEL), lambda i, *_: (i, 0)),
            scratch_shapes=[pltpu.SMEM((n_tiles * MOE_TM,), I32),
                            pltpu.VMEM((n_tiles * MOE_TM,), I32),
                            pltpu.VMEM((MOE_XBUFS, MOE_TM // SUBLANES, SUBLANES, D_MODEL), F32),
                            pltpu.VMEM((2, D_MODEL, D_EXPERT), F32),
                            pltpu.VMEM((2, D_MODEL, D_EXPERT), F32),
                            pltpu.VMEM((2, D_EXPERT, D_MODEL), F32),
                            pltpu.VMEM((D_MODEL, D_EXPERT), BF16),
                            pltpu.VMEM((D_MODEL, D_EXPERT), BF16),
                            pltpu.VMEM((D_EXPERT, D_MODEL), BF16),
                            pltpu.SemaphoreType.DMA((MOE_XBUFS,)),
                            pltpu.SemaphoreType.DMA((2,))]),
        compiler_params=pltpu.CompilerParams(dimension_semantics=("arbitrary",),
                                             vmem_limit_bytes=V7X_SCOPED_VMEM_BYTES),
        name="moe",
    )(tile_expert, n_valid, w_slot, next_expert, pos, src_row, hn, w_gate, w_up, w_down)


def _routing_tables(route, counts, n):
    tm = MOE_TM
    n_tiles = 2 * n // tm + N_EXPERTS
    experts = route[:, 0:2].astype(I32)
    rank = route[:, 4:6].astype(I32)
    counts = counts[0, :N_EXPERTS].astype(I32)
    tiles = (counts + tm - 1) // tm
    tile_end = jnp.cumsum(tiles)
    tile_start = tile_end - tiles
    onehot = experts[:, :, None] == jnp.arange(N_EXPERTS, dtype=I32)[None, None, :]
    pos = jnp.sum(jnp.where(onehot, (tile_start * tm)[None, None, :], 0), axis=-1) + rank
    tidx = jnp.arange(n_tiles, dtype=I32)
    t_exp = jnp.sum((tidx[:, None] >= tile_end[None, :]).astype(I32), axis=1)
    used = tidx < tile_end[-1]
    last_exp = jnp.sum((tile_end[-1] - 1 >= tile_end).astype(I32))
    t_exp = jnp.where(used, t_exp, last_exp)
    n_valid = jnp.where(used, jnp.clip(counts[t_exp] - (tidx - tile_start[t_exp]) * tm, 0, tm), 0)
    eidx = jnp.arange(N_EXPERTS, dtype=I32)
    owns = tiles > 0
    slot_e = (jnp.cumsum(owns.astype(I32)) - 1) & 1
    later = jnp.where(owns[None, :] & (eidx[None, :] > eidx[:, None]), eidx[None, :], N_EXPERTS)
    next_e = jnp.min(later, axis=1)
    next_e = jnp.where(next_e < N_EXPERTS, next_e, -1)
    return (t_exp.astype(I32), n_valid.astype(I32), slot_e[t_exp].astype(I32), next_e[t_exp].astype(I32),
            pos.reshape(2 * n))


def _final_body(pos_ref, x1_ref, route_ref, ln_ref, y_hbm, o_ref, ybuf, sem, *, row0):
    tm = PROJ_TM
    i = pl.program_id(0)
    n_steps = pl.num_programs(0)

    def fetch(step, b):
        a0 = 2 * (row0 + step * tm)
        for g in range(tm // SUBLANES):
            ps = [pos_ref[a0 + 2 * (g * SUBLANES + j) + k] for j in range(SUBLANES) for k in range(2)]
            for j in range(SUBLANES):
                for k in range(2):
                    pltpu.make_async_copy(y_hbm.at[pl.ds(ps[2 * j + k], 1), :],
                                          ybuf.at[b, k * (tm // SUBLANES) + g, pl.ds(j, 1), :], sem.at[b]).start()

    @pl.when(i == 0)
    def _():
        fetch(0, 0)

    def step(b):
        @pl.when(i + 1 < n_steps)
        def _():
            fetch(i + 1, 1 - b)
        _row_dma_wait(ybuf, sem, b)
        y = ybuf[b].reshape(2, tm, D_MODEL)
        z = x1_ref[...] + route_ref[:, 2:3] * y[0] + route_ref[:, 3:4] * y[1]
        o_ref[...] = _rms(z) * ln_ref[...]

    for b in range(2):
        pl.when(i % 2 == b)(functools.partial(step, b))


def _final(pos, x1, route, ln, y_sorted, *, row0, rows):
    base = row0 // PROJ_TM
    return pl.pallas_call(
        functools.partial(_final_body, row0=row0),
        out_shape=jax.ShapeDtypeStruct((rows, D_MODEL), F32),
        grid_spec=pltpu.PrefetchScalarGridSpec(
            num_scalar_prefetch=1,
            grid=(rows // PROJ_TM,),
            in_specs=[pl.BlockSpec((PROJ_TM, D_MODEL), lambda i, pos_: (base + i, 0)),
                      pl.BlockSpec((PROJ_TM, LANES), lambda i, pos_: (base + i, 0)),
                      pl.BlockSpec((1, D_MODEL), lambda i, pos_: (0, 0)),
                      pl.BlockSpec(memory_space=pl.ANY)],
            out_specs=pl.BlockSpec((PROJ_TM, D_MODEL), lambda i, pos_: (i, 0)),
            scratch_shapes=[pltpu.VMEM((2, 2 * PROJ_TM // SUBLANES, SUBLANES, D_MODEL), F32),
                            pltpu.SemaphoreType.DMA((2,))]),
        compiler_params=pltpu.CompilerParams(dimension_semantics=("arbitrary",),
                                             vmem_limit_bytes=V7X_SCOPED_VMEM_BYTES),
        name="final_norm",
    )(pos, x1, route, ln, y_sorted)


def kernel(x_prompt, x_sample, state_gla, state_ret, ln_attn, w_in, w_gk2, b_gk, gla_norm_w, w_out, ln_ffn, w_router_group, b_router_group, w_router_expert, b_router_expert, w_exp_gate, w_exp_up, w_exp_down, ln_final):
    bp, tp, d = x_prompt.shape
    bs, ts, _ = x_sample.shape
    assert d == D_MODEL and w_in.shape == (1, D_MODEL, IN_COLS_SRC)
    n_p, n_s = bp * tp, bs * ts
    n = n_p + n_s
    assert n_p % PROJ_TM == 0 and n_s % PROJ_TM == 0 and tp % GLA_CHUNK == 0 and tp % RET_CHUNK == 0

    xp = x_prompt.reshape(n_p, d)
    xs = x_sample.reshape(n_s, d)

    hi_lo = lambda a: jnp.stack(_split_bf16(a, 2))
    w_in_p, w_lr = _wprep(w_in[0].T)
    w_gk2p = hi_lo(jnp.concatenate([w_gk2[0], jnp.zeros((LANES - GLA_LOWRANK, GLA_QK), F32)], axis=0))
    w_rt = hi_lo(jnp.concatenate([w_router_expert[0], w_router_group[0],
                                  jnp.zeros((d, LANES - N_EXPERTS - N_GROUPS), F32)], axis=1))
    b_rt = jnp.concatenate([b_router_expert[0], b_router_group[0],
                            jnp.zeros((LANES - N_EXPERTS - N_GROUPS,), F32)])[None, :]

    proj = _inproj(xp, xs, ln_attn, w_in_p, w_lr)

    nw = gla_norm_w
    bgk = b_gk
    oa_p, sg_p = _gla(proj, w_gk2p, bgk, nw, None, row0=0, n_seq=bp, seq_len=tp, chunk=GLA_CHUNK, nseq=1)
    oa_s, sg_s = _gla(proj, w_gk2p, bgk, nw, state_gla[0], row0=n_p, n_seq=bs, seq_len=ts, chunk=ts, nseq=8)

    cos_p, sin_p = _rope_tables(jnp.arange(tp, dtype=F32), 1)
    ret_nseq_s = 4
    cos_s, sin_s = _rope_tables(jnp.arange(ts, dtype=F32) + float(PAST_LEN), ret_nseq_s)
    or_p, sr_p = _ret(proj, cos_p, sin_p, None, row0=0, n_seq=bp, seq_len=tp, chunk=RET_CHUNK, nseq=1)
    or_s, sr_s = _ret(proj, cos_s, sin_s, state_ret[0], row0=n_p, n_seq=bs, seq_len=ts, chunk=ts, nseq=ret_nseq_s)

    x1, hn, route, counts = _outproj_router(oa_p, oa_s, or_p, or_s, xp, xs, w_out[0].astype(BF16),
                                            ln_ffn, w_rt, b_rt)

    t_exp, n_valid, w_slot, next_exp, pos = _routing_tables(route, counts, n)
    assert math.gcd(HN_ROW_STRIDE, n) == 1 and 2 * n * HN_ROW_STRIDE < 2 ** 31
    src_row = _hn_row(jnp.arange(2 * n, dtype=I32) // 2, n)
    y_sorted = _moe(t_exp, n_valid, w_slot, next_exp, pos, src_row, hn,
                    w_exp_gate[0], w_exp_up[0], w_exp_down[0])

    ln_f = ln_final[None, :]
    y_p = _final(pos, x1, route, ln_f, y_sorted, row0=0, rows=n_p).reshape(bp, tp, d)
    y_s = _final(pos, x1, route, ln_f, y_sorted, row0=n_p, rows=n_s).reshape(bs, ts, d)
    return (y_p, y_s, sg_p[None], sr_p[None], sg_s[None], sr_s[None])
```

```python
import functools
import math

import jax
import jax.numpy as jnp
from jax import lax
from jax.experimental import pallas as pl
from jax.experimental.pallas import tpu as pltpu

F32 = jnp.float32
BF16 = jnp.bfloat16
I32 = jnp.int32

D_MODEL = 2048
PAST_LEN = 16384
H_GLA, DK_GLA, DV_GLA = 4, 128, 256
GLA_LOWRANK = 16
GLA_GATE_NORM = 16.0
GLA_SUBCHUNK = 16
H_RET, DK_RET, DV_RET = 4, 256, 256
ROPE_BASE = 10000.0
N_GROUPS, EXPERTS_PER_GROUP, D_EXPERT = 4, 8, 512
N_EXPERTS = N_GROUPS * EXPERTS_PER_GROUP
EPS = 1e-6

LANES = 128
SUBLANES = 8
V7X_SCOPED_VMEM_BYTES = 56 * 1024 * 1024

GLA_QK = H_GLA * DK_GLA
GLA_V = H_GLA * DV_GLA
RET_QK = H_RET * DK_RET
RET_V = H_RET * DV_RET
C_QA, C_KA, C_VA, C_GA = 0, GLA_QK, 2 * GLA_QK, 2 * GLA_QK + GLA_V
C_QR = C_GA + GLA_V
C_KR, C_VR, C_GR = C_QR + RET_QK, C_QR + 2 * RET_QK, C_QR + 2 * RET_QK + RET_V
C_LR = C_GR + RET_V
P_COLS = C_LR + LANES
IN_COLS_SRC = 2 * GLA_QK + GLA_V + GLA_LOWRANK + GLA_V + 2 * RET_QK + 2 * RET_V

PROJ_TM = 256
W_CHUNK = 1024
MOE_TM = 256
ATTN_CHUNK = 128

def _dot(a, b):
    return jnp.dot(a, b, preferred_element_type=F32)


def _dot_nt(a, b):
    return lax.dot_general(a, b, (((1,), (1,)), ((), ())), preferred_element_type=F32)


def _dot_tn(a, b):
    return lax.dot_general(a, b, (((0,), (0,)), ((), ())), preferred_element_type=F32)


def _split_bf16(x, parts):
    out = []
    for _ in range(parts):
        p = x.astype(BF16)
        out.append(p)
        x = x - p.astype(F32)
    return out


def _dot_split(a, b_hi, b_lo):
    a_hi, a_lo = _split_bf16(a, 2)
    return _dot(a_hi, b_hi) + (_dot(a_lo, b_hi) + _dot(a_hi, b_lo))


def _rms(x):
    return x * lax.rsqrt(jnp.mean(x * x, axis=-1, keepdims=True) + EPS)


def _silu(x):
    return x * jax.nn.sigmoid(x)


def _wprep_body(a_ref, b_ref, o_ref, olr_ref):
    j = pl.program_id(0)
    a = a_ref[...]
    b = b_ref[...]
    shifted = jnp.concatenate([a[GLA_LOWRANK:, :], b], axis=0)
    o_ref[...] = jnp.where(j >= C_GA // W_CHUNK, shifted, a).T.astype(BF16)

    @pl.when(j == C_GA // W_CHUNK - 1)
    def _():
        padded = jnp.concatenate([b, jnp.zeros((LANES - GLA_LOWRANK, D_MODEL), F32)], axis=0)
        olr_ref[...] = padded.T.astype(BF16)


def _wprep(wt):
    per = W_CHUNK // GLA_LOWRANK
    return pl.pallas_call(
        _wprep_body,
        out_shape=(jax.ShapeDtypeStruct((D_MODEL, C_LR), BF16), jax.ShapeDtypeStruct((D_MODEL, LANES), BF16)),
        grid=(C_LR // W_CHUNK,),
        in_specs=[pl.BlockSpec((W_CHUNK, D_MODEL), lambda j: (j, 0)),
                  pl.BlockSpec((GLA_LOWRANK, D_MODEL), lambda j: (per * (j + 1), 0))],
        out_specs=(pl.BlockSpec((D_MODEL, W_CHUNK), lambda j: (0, j)),
                   pl.BlockSpec((D_MODEL, LANES), lambda j: (0, 0))),
        compiler_params=pltpu.CompilerParams(dimension_semantics=("arbitrary",),
                                             vmem_limit_bytes=V7X_SCOPED_VMEM_BYTES),
        name="wprep",
    )(wt, wt)


def _inproj_body(xp_ref, xs_ref, ln_ref, w_ref, wlr_ref, o_ref, *, n_prompt_blocks):
    x = jnp.where(pl.program_id(0) < n_prompt_blocks, xp_ref[...], xs_ref[...])
    h = (_rms(x) * ln_ref[...]).astype(BF16)
    for c in range(C_LR // W_CHUNK):
        o_ref[:, c * W_CHUNK:(c + 1) * W_CHUNK] = _dot(h, w_ref[:, c * W_CHUNK:(c + 1) * W_CHUNK])
    o_ref[:, C_LR:] = _dot(h, wlr_ref[...])


def _inproj(xp, xs, ln, w, w_lr):
    nbp = xp.shape[0] // PROJ_TM
    n = xp.shape[0] + xs.shape[0]
    fixed = lambda i: (0, 0)
    return pl.pallas_call(
        functools.partial(_inproj_body, n_prompt_blocks=nbp),
        out_shape=jax.ShapeDtypeStruct((n, P_COLS), F32),
        grid=(n // PROJ_TM,),
        in_specs=[pl.BlockSpec((PROJ_TM, D_MODEL), lambda i: (jnp.minimum(i, nbp - 1), 0)),
                  pl.BlockSpec((PROJ_TM, D_MODEL), lambda i: (jnp.maximum(i - nbp, 0), 0)),
                  pl.BlockSpec((1, D_MODEL), fixed),
                  pl.BlockSpec((D_MODEL, C_LR), fixed, pipeline_mode=pl.Buffered(1)),
                  pl.BlockSpec((D_MODEL, LANES), fixed)],
        out_specs=pl.BlockSpec((PROJ_TM, P_COLS), lambda i: (i, 0)),
        compiler_params=pltpu.CompilerParams(dimension_semantics=("arbitrary",),
                                             vmem_limit_bytes=V7X_SCOPED_VMEM_BYTES),
        name="inproj",
    )(xp, xs, ln, w, w_lr)


def _seg_cumsum(x, seg):
    pos = lax.broadcasted_iota(I32, x.shape, 0) & (seg - 1)
    s = 1
    while s < seg:
        x = x + jnp.where(pos >= s, pltpu.roll(x, s, axis=0), 0.0)
        s *= 2
    return x


def _block_ends(p, m):
    rows, w = p.shape
    return jnp.concatenate(
        [jnp.broadcast_to(p[j * m + m - 1:j * m + m, :], (m, w)) for j in range(rows // m)], axis=0)


def _block_starts(p, m, chunk):
    rows, w = p.shape
    pieces = []
    for j in range(rows // m):
        if (j * m) % chunk == 0:
            pieces.append(jnp.zeros((m, w), F32))
        else:
            pieces.append(jnp.broadcast_to(p[j * m - 1:j * m, :], (m, w)))
    return jnp.concatenate(pieces, axis=0)


def _gla_body(*refs, chunk, nseq, has_init, defer_state_out=False):
    if has_init:
        (q_ref, k_ref, v_ref, g_ref, lr_ref, wgk_ref, bgk_ref, nw_ref, s0_ref,
         o_ref, sout_ref, s_scr) = refs
    else:
        (q_ref, k_ref, v_ref, g_ref, lr_ref, wgk_ref, bgk_ref, nw_ref,
         o_ref, sout_ref, s_scr) = refs
    t = pl.program_id(1)

    @pl.when(t == 0)
    def _():
        if has_init:
            s_scr[...] = s0_ref[...]
        else:
            s_scr[...] = jnp.zeros_like(s_scr)

    x = _dot_split(lr_ref[...], wgk_ref[0], wgk_ref[1]) + bgk_ref[...]
    log_g = -(jnp.maximum(-x, 0.0) + jnp.log1p(jnp.exp(-jnp.abs(x)))) / GLA_GATE_NORM
    p = _seg_cumsum(log_g, chunk)
    p_end = _block_ends(p, chunk)
    eq_c = jnp.exp(p)
    ek_c = jnp.exp(p_end - p)
    sub = min(GLA_SUBCHUNK, chunk)
    p_sub = p - _block_starts(p, sub, chunk)
    eq_d = jnp.exp(p_sub)
    ek_d = jnp.exp(-p_sub)
    levels = []
    m = sub
    while m < chunk:
        levels.append((m, jnp.exp(p - _block_starts(p, m, chunk)), jnp.exp(_block_ends(p, m) - p)))
        m *= 2

    ti = lax.broadcasted_iota(I32, (chunk, chunk), 0)
    si = lax.broadcasted_iota(I32, (chunk, chunk), 1)
    blk = lambda idx, size: idx >> (size.bit_length() - 1)
    mask_d = (blk(ti, sub) == blk(si, sub)) & (si <= ti)
    masks = [((blk(ti, m) & 1) == 1) & (blk(si, m) == blk(ti, m) - 1) for (m, _, _) in levels]
    sel8 = (lax.broadcasted_iota(I32, (8, DV_GLA), 0) == 0).astype(BF16)
    nw = nw_ref[...]

    outs = [[None] * nseq for _ in range(H_GLA)]
    for b in range(nseq):
        r0 = b * chunk
        rows = slice(r0, r0 + chunk)
        for h in range(H_GLA):
            kc = slice(h * DK_GLA, (h + 1) * DK_GLA)
            vc = slice(h * DV_GLA, (h + 1) * DV_GLA)
            q = q_ref[rows, kc] * (DK_GLA ** -0.5)
            k = k_ref[rows, kc]
            v = v_ref[rows, vc].astype(BF16)
            att = jnp.where(mask_d, _dot_nt((q * eq_d[rows, kc]).astype(BF16),
                                            (k * ek_d[rows, kc]).astype(BF16)), 0.0)
            for (m, eq_m, ek_m), mask in zip(levels, masks):
                att = jnp.where(mask, _dot_nt((q * eq_m[rows, kc]).astype(BF16),
                                              (k * ek_m[rows, kc]).astype(BF16)), att)
            s_old = s_scr[b, h]
            o = _dot(att.astype(BF16), v) + _dot((q * eq_c[rows, kc]).astype(BF16), s_old.astype(BF16))
            kv = _dot_tn((k * ek_c[rows, kc]).astype(BF16), v)
            tot = jnp.broadcast_to(p[r0 + chunk - 1:r0 + chunk, kc], (8, DK_GLA))
            t_hi, t_mid, t_lo = _split_bf16(tot, 3)
            decay = jnp.exp(_dot_tn(t_hi, sel8) + (_dot_tn(t_mid, sel8) + _dot_tn(t_lo, sel8)))
            s_scr[b, h] = decay * s_old + kv
            outs[h][b] = _rms(o) * nw * _silu(g_ref[rows, vc])
    for h in range(H_GLA):
        o_ref[:, h * DV_GLA:(h + 1) * DV_GLA] = jnp.concatenate(outs[h], axis=0).astype(BF16)

    def emit_state():
        @pl.when(t == pl.num_programs(1) - 1)
        def _():
            sout_ref[...] = s_scr[...]
    if defer_state_out:
        return emit_state
    emit_state()


def _gla(proj, w_gk2p, b_gk, nw, s0, *, row0, n_seq, seq_len, chunk, nseq):
    rows = nseq * chunk
    nt = seq_len // chunk
    base = row0 // rows

    def rmap(col):
        return lambda b, t: (base + b * nt + t, col)

    in_specs = [pl.BlockSpec((rows, GLA_QK), rmap(C_QA // GLA_QK)),
                pl.BlockSpec((rows, GLA_QK), rmap(C_KA // GLA_QK)),
                pl.BlockSpec((rows, GLA_V), rmap(C_VA // GLA_V)),
                pl.BlockSpec((rows, GLA_V), rmap(C_GA // GLA_V)),
                pl.BlockSpec((rows, LANES), rmap(C_LR // LANES)),
                pl.BlockSpec((2, LANES, GLA_QK), lambda b, t: (0, 0, 0)),
                pl.BlockSpec((1, GLA_QK), lambda b, t: (0, 0)),
                pl.BlockSpec((1, DV_GLA), lambda b, t: (0, 0))]
    args = [proj, proj, proj, proj, proj, w_gk2p, b_gk, nw]
    state_spec = pl.BlockSpec((nseq, H_GLA, DK_GLA, DV_GLA), lambda b, t: (b, 0, 0, 0))
    if s0 is not None:
        in_specs.append(state_spec)
        args.append(s0)
    return pl.pallas_call(
        functools.partial(_gla_body, chunk=chunk, nseq=nseq, has_init=s0 is not None),
        out_shape=(jax.ShapeDtypeStruct((n_seq * seq_len, GLA_V), BF16),
                   jax.ShapeDtypeStruct((n_seq, H_GLA, DK_GLA, DV_GLA), F32)),
        grid=(n_seq // nseq, nt),
        in_specs=in_specs,
        out_specs=(pl.BlockSpec((rows, GLA_V), lambda b, t: (b * nt + t, 0)), state_spec),
        scratch_shapes=[pltpu.VMEM((nseq, H_GLA, DK_GLA, DV_GLA), F32)],
        compiler_params=pltpu.CompilerParams(dimension_semantics=("arbitrary", "arbitrary"),
                                             vmem_limit_bytes=V7X_SCOPED_VMEM_BYTES),
        name="gla_init" if s0 is not None else "gla",
    )(*args)


def _ret_body(*refs, chunk, nseq, has_init, state_ready=False):
    if has_init:
        (q_ref, k_ref, v_ref, g_ref, cos_ref, sin_ref, s0_ref, o_ref, sout_ref, s_scr) = refs
    else:
        (q_ref, k_ref, v_ref, g_ref, cos_ref, sin_ref, o_ref, sout_ref, s_scr) = refs
    t = pl.program_id(1)

    if not state_ready:
        @pl.when(t == 0)
        def _():
            if has_init:
                s_scr[...] = s0_ref[...]
            else:
                s_scr[...] = jnp.zeros_like(s_scr)

    half = DK_RET // 2
    ti = lax.broadcasted_iota(I32, (chunk, chunk), 0)
    si = lax.broadcasted_iota(I32, (chunk, chunk), 1)
    diff = (ti - si).astype(F32)
    idx = lax.broadcasted_iota(I32, (chunk, 1), 0).astype(F32)

    outs = [[None] * nseq for _ in range(H_RET)]
    for h in range(H_RET):
        lg = math.log(1.0 - 2.0 ** (-5.0 - h))
        dmat = jnp.where(diff >= 0, jnp.exp(lg * jnp.maximum(diff, 0.0)), 0.0)
        q_dec = jnp.exp(lg * (idx + 1.0))
        k_dec = jnp.exp(lg * (chunk - 1.0 - idx))
        c_dec = math.exp(lg * chunk)
        for b in range(nseq):
            rows = slice(b * chunk, (b + 1) * chunk)
            cos = cos_ref[rows, :]
            sin = sin_ref[rows, :]
            c1 = slice(h * DK_RET, h * DK_RET + half)
            c2 = slice(h * DK_RET + half, (h + 1) * DK_RET)
            vc = slice(h * DV_RET, (h + 1) * DV_RET)
            q1, q2 = q_ref[rows, c1], q_ref[rows, c2]
            k1, k2 = k_ref[rows, c1], k_ref[rows, c2]
            q = jnp.concatenate([q1 * cos - q2 * sin, q1 * sin + q2 * cos], axis=-1)
            k = jnp.concatenate([k1 * cos - k2 * sin, k1 * sin + k2 * cos], axis=-1) * (DK_RET ** -0.5)
            v = v_ref[rows, vc].astype(BF16)
            qb = q.astype(BF16)
            s_old = s_scr[b, h]
            att = _dot_nt(qb, k.astype(BF16)) * dmat
            o = _dot(att.astype(BF16), v) + _dot(qb, s_old.astype(BF16)) * q_dec
            s_scr[b, h] = c_dec * s_old + _dot_tn((k * k_dec).astype(BF16), v)
            outs[h][b] = _rms(o) * _silu(g_ref[rows, vc])
    for h in range(H_RET):
        o_ref[:, h * DV_RET:(h + 1) * DV_RET] = jnp.concatenate(outs[h], axis=0).astype(BF16)

    @pl.when(t == pl.num_programs(1) - 1)
    def _():
        sout_ref[...] = s_scr[...]


def _ret(proj, cos, sin, s0, *, row0, n_seq, seq_len, chunk, nseq):
    rows = nseq * chunk
    nt = seq_len // chunk
    base = row0 // rows

    def rmap(col):
        return lambda b, t: (base + b * nt + t, col)

    tab_spec = pl.BlockSpec((rows, DK_RET // 2), lambda b, t: (t, 0))
    in_specs = [pl.BlockSpec((rows, RET_QK), rmap(C_QR // RET_QK)),
                pl.BlockSpec((rows, RET_QK), rmap(C_KR // RET_QK)),
                pl.BlockSpec((rows, RET_V), rmap(C_VR // RET_V)),
                pl.BlockSpec((rows, RET_V), rmap(C_GR // RET_V)),
                tab_spec, tab_spec]
    args = [proj, proj, proj, proj, cos, sin]
    state_spec = pl.BlockSpec((nseq, H_RET, DK_RET, DV_RET), lambda b, t: (b, 0, 0, 0))
    if s0 is not None:
        in_specs.append(state_spec)
        args.append(s0)
    return pl.pallas_call(
        functools.partial(_ret_body, chunk=chunk, nseq=nseq, has_init=s0 is not None),
        out_shape=(jax.ShapeDtypeStruct((n_seq * seq_len, RET_V), BF16),
                   jax.ShapeDtypeStruct((n_seq, H_RET, DK_RET, DV_RET), F32)),
        grid=(n_seq // nseq, nt),
        in_specs=in_specs,
        out_specs=(pl.BlockSpec((rows, RET_V), lambda b, t: (b * nt + t, 0)), state_spec),
        scratch_shapes=[pltpu.VMEM((nseq, H_RET, DK_RET, DV_RET), F32)],
        compiler_params=pltpu.CompilerParams(dimension_semantics=("arbitrary", "arbitrary"),
                                             vmem_limit_bytes=V7X_SCOPED_VMEM_BYTES),
        name="ret_init" if s0 is not None else "ret",
    )(*args)


def _gla_ret_body(*refs, chunk):
    gla_in, ret_in = refs[:8], refs[8:14]
    oa_ref, sg_ref, or_ref, sr_ref, sg_scr, sr_scr = refs[14:]

    @pl.when(pl.program_id(1) == 0)
    def _():
        sr_scr[...] = jnp.zeros_like(sr_scr)
    emit_gla_state = _gla_body(*gla_in, oa_ref, sg_ref, sg_scr, chunk=chunk, nseq=1, has_init=False,
                               defer_state_out=True)
    _ret_body(*ret_in, or_ref, sr_ref, sr_scr, chunk=chunk, nseq=1, has_init=False, state_ready=True)
    emit_gla_state()


def _gla_ret_prompt(proj, w_gk2p, b_gk, nw, cos, sin, *, n_seq, seq_len, chunk):
    nt = seq_len // chunk
    rmap = lambda col: (lambda b, t: (b * nt + t, col))
    fixed2 = lambda b, t: (0, 0)
    tab_spec = pl.BlockSpec((chunk, DK_RET // 2), lambda b, t: (t, 0))
    in_specs = [pl.BlockSpec((chunk, GLA_QK), rmap(C_QA // GLA_QK)),
                pl.BlockSpec((chunk, GLA_QK), rmap(C_KA // GLA_QK)),
                pl.BlockSpec((chunk, GLA_V), rmap(C_VA // GLA_V)),
                pl.BlockSpec((chunk, GLA_V), rmap(C_GA // GLA_V)),
                pl.BlockSpec((chunk, LANES), rmap(C_LR // LANES)),
                pl.BlockSpec((2, LANES, GLA_QK), lambda b, t: (0, 0, 0)),
                pl.BlockSpec((1, GLA_QK), fixed2),
                pl.BlockSpec((1, DV_GLA), fixed2),
                pl.BlockSpec((chunk, RET_QK), rmap(C_QR // RET_QK)),
                pl.BlockSpec((chunk, RET_QK), rmap(C_KR // RET_QK)),
                pl.BlockSpec((chunk, RET_V), rmap(C_VR // RET_V)),
                pl.BlockSpec((chunk, RET_V), rmap(C_GR // RET_V)),
                tab_spec, tab_spec]
    gla_state = pl.BlockSpec((1, H_GLA, DK_GLA, DV_GLA), lambda b, t: (b, 0, 0, 0))
    ret_state = pl.BlockSpec((1, H_RET, DK_RET, DV_RET), lambda b, t: (b, 0, 0, 0))
    return pl.pallas_call(
        functools.partial(_gla_ret_body, chunk=chunk),
        out_shape=(jax.ShapeDtypeStruct((n_seq * seq_len, GLA_V), BF16),
                   jax.ShapeDtypeStruct((n_seq, H_GLA, DK_GLA, DV_GLA), F32),
                   jax.ShapeDtypeStruct((n_seq * seq_len, RET_V), BF16),
                   jax.ShapeDtypeStruct((n_seq, H_RET, DK_RET, DV_RET), F32)),
        grid=(n_seq, nt),
        in_specs=in_specs,
        out_specs=(pl.BlockSpec((chunk, GLA_V), rmap(0)), gla_state,
                   pl.BlockSpec((chunk, RET_V), rmap(0)), ret_state),
        scratch_shapes=[pltpu.VMEM((1, H_GLA, DK_GLA, DV_GLA), F32),
                        pltpu.VMEM((1, H_RET, DK_RET, DV_RET), F32)],
        compiler_params=pltpu.CompilerParams(dimension_semantics=("arbitrary", "arbitrary"),
                                             vmem_limit_bytes=V7X_SCOPED_VMEM_BYTES),
        name="gla_ret",
    )(proj, proj, proj, proj, proj, w_gk2p, b_gk, nw, proj, proj, proj, proj, cos, sin)


def _rope_tables(pos, reps):
    half = DK_RET // 2
    inv = ROPE_BASE ** (-jnp.arange(half, dtype=F32) / half)
    ang = pos[:, None] * inv[None, :]
    return jnp.tile(jnp.cos(ang), (reps, 1)), jnp.tile(jnp.sin(ang), (reps, 1))


def _outproj_body(oap_ref, oas_ref, orp_ref, ors_ref, xp_ref, xs_ref, w_ref, ln_ref, wrt_ref, brt_ref,
                  x1_ref, hn_ref, route_ref, counts_ref, *, n_prompt_blocks):
    step = pl.program_id(0)
    is_prompt = step < n_prompt_blocks
    oa = jnp.where(is_prompt, oap_ref[...], oas_ref[...])
    orr = jnp.where(is_prompt, orp_ref[...], ors_ref[...])
    x = jnp.where(is_prompt, xp_ref[...], xs_ref[...])
    x1 = x + _dot(oa, w_ref[:GLA_V, :]) + _dot(orr, w_ref[GLA_V:, :])
    x1_ref[...] = x1
    hn = _rms(x1) * ln_ref[...]
    hn_ref[...] = hn

    logits = _dot_split(hn, wrt_ref[0], wrt_ref[1]) + brt_ref[...]
    lane_i = lax.broadcasted_iota(I32, logits.shape, 1)
    lane = lane_i.astype(F32)
    grp = (lane_i >> (EXPERTS_PER_GROUP.bit_length() - 1)).astype(F32)
    neg = -jnp.inf
    far = float(LANES)

    gl = jnp.where((lane_i >= N_EXPERTS) & (lane_i < N_EXPERTS + N_GROUPS), logits, neg)
    gmax = jnp.max(gl, axis=-1, keepdims=True)
    g_w = 1.0 / jnp.sum(jnp.exp(gl - gmax), axis=-1, keepdims=True)
    g_idx = jnp.min(jnp.where(gl == gmax, lane, far), axis=-1, keepdims=True) - float(N_EXPERTS)

    el = jnp.where((lane_i < N_EXPERTS) & (grp == g_idx), logits, neg)
    m1 = jnp.max(el, axis=-1, keepdims=True)
    esum = jnp.sum(jnp.exp(el - m1), axis=-1, keepdims=True)
    i1 = jnp.min(jnp.where(el == m1, lane, far), axis=-1, keepdims=True)
    el2 = jnp.where(lane == i1, neg, el)
    m2 = jnp.max(el2, axis=-1, keepdims=True)
    i2 = jnp.min(jnp.where(el2 == m2, lane, far), axis=-1, keepdims=True)
    p1 = 1.0 / esum
    p2 = jnp.exp(m2 - m1) / esum
    gate1 = g_w * (p1 / (p1 + p2))
    gate2 = g_w * (p2 / (p1 + p2))
    @pl.when(step == 0)
    def _():
        counts_ref[...] = jnp.zeros_like(counts_ref)

    tm = logits.shape[0]
    earlier = (lax.broadcasted_iota(I32, (tm, tm), 0) > lax.broadcasted_iota(I32, (tm, tm), 1)).astype(BF16)
    running = counts_ref[0:1, :]
    ranks = []
    for idx in (i1, i2):
        onehot = (lane == idx).astype(F32)
        before = _dot(earlier, onehot.astype(BF16)) + running
        ranks.append(jnp.sum(onehot * before, axis=-1, keepdims=True))
        running = running + jnp.sum(onehot, axis=0, keepdims=True)
    counts_ref[...] = jnp.broadcast_to(running, counts_ref.shape)

    route_ref[...] = jnp.where(lane_i == 0, i1,
                               jnp.where(lane_i == 1, i2,
                                         jnp.where(lane_i == 2, gate1,
                                                   jnp.where(lane_i == 3, gate2,
                                                             jnp.where(lane_i == 4, ranks[0],
                                                                       jnp.where(lane_i == 5, ranks[1], 0.0))))))


def _outproj_router(oa_p, oa_s, or_p, or_s, xp, xs, w_out, ln, w_rt, b_rt):
    n = xp.shape[0] + xs.shape[0]
    nbp = xp.shape[0] // PROJ_TM
    pmap = lambda i: (jnp.minimum(i, nbp - 1), 0)
    smap = lambda i: (jnp.maximum(i - nbp, 0), 0)
    row = lambda i: (i, 0)
    fixed = lambda i: (0, 0)
    return pl.pallas_call(
        functools.partial(_outproj_body, n_prompt_blocks=nbp),
        out_shape=(jax.ShapeDtypeStruct((n, D_MODEL), F32),
                   jax.ShapeDtypeStruct((n, D_MODEL), F32),
                   jax.ShapeDtypeStruct((n, LANES), F32),
                   jax.ShapeDtypeStruct((SUBLANES, LANES), F32)),
        grid=(n // PROJ_TM,),
        in_specs=[pl.BlockSpec((PROJ_TM, GLA_V), pmap), pl.BlockSpec((PROJ_TM, GLA_V), smap),
                  pl.BlockSpec((PROJ_TM, RET_V), pmap), pl.BlockSpec((PROJ_TM, RET_V), smap),
                  pl.BlockSpec((PROJ_TM, D_MODEL), pmap), pl.BlockSpec((PROJ_TM, D_MODEL), smap),
                  pl.BlockSpec((GLA_V + RET_V, D_MODEL), fixed, pipeline_mode=pl.Buffered(1)),
                  pl.BlockSpec((1, D_MODEL), fixed),
                  pl.BlockSpec((2, D_MODEL, LANES), lambda i: (0, 0, 0)),
                  pl.BlockSpec((1, LANES), fixed)],
        out_specs=(pl.BlockSpec((PROJ_TM, D_MODEL), row), pl.BlockSpec((PROJ_TM, D_MODEL), row),
                   pl.BlockSpec((PROJ_TM, LANES), row), pl.BlockSpec((SUBLANES, LANES), fixed)),
        compiler_params=pltpu.CompilerParams(dimension_semantics=("arbitrary",),
                                             vmem_limit_bytes=V7X_SCOPED_VMEM_BYTES),
        name="outproj_router",
    )(oa_p, oa_s, or_p, or_s, xp, xs, w_out, ln, w_rt, b_rt)


def _moe_body(texp_ref, nval_ref, pos_ref, hn_hbm, wg_ref, wu_ref, wd_ref, y_hbm,
              inv, xbuf, ybuf, wg_b, wu_b, wd_b, gsem, ssem, *, n_tokens):
    tm = MOE_TM
    i = pl.program_id(0)
    nt = pl.num_programs(0)
    slot = i % 2

    def for_rows(n_rows, fn):
        n_grp = n_rows // SUBLANES

        def group(g, c):
            for j in range(SUBLANES):
                fn(g * SUBLANES + j)
            return c
        lax.fori_loop(0, n_grp, group, 0)

        def single(r, c):
            fn(r)
            return c
        lax.fori_loop(n_grp * SUBLANES, n_rows, single, 0)

    def gather_copy(tok, r, slot_):
        return pltpu.make_async_copy(hn_hbm.at[pl.ds(tok, 1), :], xbuf.at[slot_, pl.ds(r, 1), :], gsem.at[slot_])

    def scatter_copy(r, dst, slot_):
        return pltpu.make_async_copy(ybuf.at[slot_, pl.ds(r, 1), :], y_hbm.at[pl.ds(dst, 1), :], ssem.at[slot_])

    def start_gather(tile, slot_):
        for_rows(nval_ref[tile], lambda r: gather_copy(inv[tile * tm + r] >> 1, r, slot_).start())

    def wait_rows(n_rows, group_copy, row_copy):
        n_grp = pl.multiple_of((n_rows // SUBLANES) * SUBLANES, SUBLANES)

        @pl.when(n_grp > 0)
        def _():
            group_copy(n_grp).wait()

        def single(r, c):
            row_copy().wait()
            return c
        lax.fori_loop(n_grp, n_rows, single, 0)

    def wait_gather(slot_, n_rows):
        wait_rows(n_rows,
                  lambda m: pltpu.make_async_copy(hn_hbm.at[pl.ds(0, m), :], xbuf.at[slot_, pl.ds(0, m), :],
                                                  gsem.at[slot_]),
                  lambda: gather_copy(0, 0, slot_))

    def wait_scatter(slot_, n_rows):
        wait_rows(n_rows,
                  lambda m: pltpu.make_async_copy(ybuf.at[slot_, pl.ds(0, m), :], y_hbm.at[pl.ds(0, m), :],
                                                  ssem.at[slot_]),
                  lambda: scatter_copy(0, 0, slot_))

    @pl.when(i == 0)
    def _():
        def body(a, c):
            inv[pos_ref[a]] = a
            return c
        lax.fori_loop(0, 2 * n_tokens, body, 0, unroll=8)
        xbuf[...] = jnp.zeros_like(xbuf)
        start_gather(0, 0)

    @pl.when(i + 1 < nt)
    def _():
        start_gather(jnp.minimum(i + 1, nt - 1), 1 - slot)

    @pl.when(i >= 2)
    def _():
        wait_scatter(slot, nval_ref[jnp.maximum(i - 2, 0)])

    nv = nval_ref[i]
    wait_gather(slot, nv)

    @pl.when(nv > 0)
    def _():
        @pl.when((i == 0) | (texp_ref[i] != texp_ref[jnp.maximum(i - 1, 0)]))
        def _():
            wg_b[...] = wg_ref[0].astype(BF16)
            wu_b[...] = wu_ref[0].astype(BF16)
            wd_b[...] = wd_ref[0].astype(BF16)

        h = xbuf[slot].astype(BF16)
        hid = _silu(_dot(h, wg_b[...])) * _dot(h, wu_b[...])
        ybuf[slot] = _dot(hid.astype(BF16), wd_b[...])

        def scatter_one(r):
            a = inv[i * tm + r]
            scatter_copy(r, (a & 1) * n_tokens + (a >> 1), slot).start()
        for_rows(nv, scatter_one)

    @pl.when(i == nt - 1)
    def _():
        @pl.when(i >= 1)
        def _():
            wait_scatter(1 - slot, nval_ref[jnp.maximum(i - 1, 0)])
        wait_scatter(slot, nv)


def _moe(tile_expert, n_valid, pos, hn, w_gate, w_up, w_down):
    n = hn.shape[0]
    n_tiles = tile_expert.shape[0]
    wmap = lambda i, te, nv, pos_: (te[i], 0, 0)
    return pl.pallas_call(
        functools.partial(_moe_body, n_tokens=n),
        out_shape=jax.ShapeDtypeStruct((2 * n, D_MODEL), F32),
        grid_spec=pltpu.PrefetchScalarGridSpec(
            num_scalar_prefetch=3,
            grid=(n_tiles,),
            in_specs=[pl.BlockSpec(memory_space=pl.ANY),
                      pl.BlockSpec((1, D_MODEL, D_EXPERT), wmap),
                      pl.BlockSpec((1, D_MODEL, D_EXPERT), wmap),
                      pl.BlockSpec((1, D_EXPERT, D_MODEL), wmap)],
            out_specs=pl.BlockSpec(memory_space=pl.ANY),
            scratch_shapes=[pltpu.SMEM((n_tiles * MOE_TM,), I32),
                            pltpu.VMEM((2, MOE_TM, D_MODEL), F32),
                            pltpu.VMEM((2, MOE_TM, D_MODEL), F32),
                            pltpu.VMEM((D_MODEL, D_EXPERT), BF16),
                            pltpu.VMEM((D_MODEL, D_EXPERT), BF16),
                            pltpu.VMEM((D_EXPERT, D_MODEL), BF16),
                            pltpu.SemaphoreType.DMA((2,)),
                            pltpu.SemaphoreType.DMA((2,))]),
        compiler_params=pltpu.CompilerParams(dimension_semantics=("arbitrary",),
                                             vmem_limit_bytes=V7X_SCOPED_VMEM_BYTES),
        name="moe",
    )(tile_expert, n_valid, pos, hn, w_gate, w_up, w_down)


def _routing_tables(route, counts, n):
    tm = MOE_TM
    n_tiles = 2 * n // tm + N_EXPERTS
    experts = route[:, 0:2].astype(I32)
    rank = route[:, 4:6].astype(I32)
    counts = counts[0, :N_EXPERTS].astype(I32)
    tiles = (counts + tm - 1) // tm
    tile_end = jnp.cumsum(tiles)
    tile_start = tile_end - tiles
    onehot = experts[:, :, None] == jnp.arange(N_EXPERTS, dtype=I32)[None, None, :]
    pos = jnp.sum(jnp.where(onehot, (tile_start * tm)[None, None, :], 0), axis=-1) + rank
    tidx = jnp.arange(n_tiles, dtype=I32)
    t_exp = jnp.sum((tidx[:, None] >= tile_end[None, :]).astype(I32), axis=1)
    used = tidx < tile_end[-1]
    last_exp = jnp.sum((tile_end[-1] - 1 >= tile_end).astype(I32))
    t_exp = jnp.where(used, t_exp, last_exp)
    n_valid = jnp.where(used, jnp.clip(counts[t_exp] - (tidx - tile_start[t_exp]) * tm, 0, tm), 0)
    return t_exp.astype(I32), n_valid.astype(I32), pos.reshape(2 * n)


def _final_body(x1_ref, y0_ref, y1_ref, route_ref, ln_ref, o_ref):
    z = x1_ref[...] + route_ref[:, 2:3] * y0_ref[...] + route_ref[:, 3:4] * y1_ref[...]
    o_ref[...] = _rms(z) * ln_ref[...]


def _final(x1, y2, route, ln, *, row0, rows):
    base = row0 // PROJ_TM
    slot1 = x1.shape[0] // PROJ_TM
    return pl.pallas_call(
        _final_body,
        out_shape=jax.ShapeDtypeStruct((rows, D_MODEL), F32),
        grid=(rows // PROJ_TM,),
        in_specs=[pl.BlockSpec((PROJ_TM, D_MODEL), lambda i: (base + i, 0)),
                  pl.BlockSpec((PROJ_TM, D_MODEL), lambda i: (base + i, 0)),
                  pl.BlockSpec((PROJ_TM, D_MODEL), lambda i: (slot1 + base + i, 0)),
                  pl.BlockSpec((PROJ_TM, LANES), lambda i: (base + i, 0)),
                  pl.BlockSpec((1, D_MODEL), lambda i: (0, 0))],
        out_specs=pl.BlockSpec((PROJ_TM, D_MODEL), lambda i: (i, 0)),
        compiler_params=pltpu.CompilerParams(dimension_semantics=("arbitrary",)),
        name="final_norm",
    )(x1, y2, y2, route, ln)


def kernel(x_prompt, x_sample, state_gla, state_ret, ln_attn, w_in, w_gk2, b_gk, gla_norm_w, w_out, ln_ffn, w_router_group, b_router_group, w_router_expert, b_router_expert, w_exp_gate, w_exp_up, w_exp_down, ln_final):
    bp, tp, d = x_prompt.shape
    bs, ts, _ = x_sample.shape
    assert d == D_MODEL and w_in.shape == (1, D_MODEL, IN_COLS_SRC)
    n_p, n_s = bp * tp, bs * ts
    n = n_p + n_s
    assert n_p % PROJ_TM == 0 and n_s % PROJ_TM == 0 and tp % ATTN_CHUNK == 0

    xp = x_prompt.reshape(n_p, d)
    xs = x_sample.reshape(n_s, d)

    hi_lo = lambda a: jnp.stack(_split_bf16(a, 2))
    w_in_p, w_lr = _wprep(w_in[0].T)
    w_gk2p = hi_lo(jnp.concatenate([w_gk2[0], jnp.zeros((LANES - GLA_LOWRANK, GLA_QK), F32)], axis=0))
    w_rt = hi_lo(jnp.concatenate([w_router_expert[0], w_router_group[0],
                                  jnp.zeros((d, LANES - N_EXPERTS - N_GROUPS), F32)], axis=1))
    b_rt = jnp.concatenate([b_router_expert[0], b_router_group[0],
                            jnp.zeros((LANES - N_EXPERTS - N_GROUPS,), F32)])[None, :]

    proj = _inproj(xp, xs, ln_attn, w_in_p, w_lr)

    nw = gla_norm_w
    bgk = b_gk
    cos_p, sin_p = _rope_tables(jnp.arange(tp, dtype=F32), 1)
    ret_nseq_s = 4
    cos_s, sin_s = _rope_tables(jnp.arange(ts, dtype=F32) + float(PAST_LEN), ret_nseq_s)
    oa_p, sg_p, or_p, sr_p = _gla_ret_prompt(proj, w_gk2p, bgk, nw, cos_p, sin_p,
                                             n_seq=bp, seq_len=tp, chunk=ATTN_CHUNK)
    oa_s, sg_s = _gla(proj, w_gk2p, bgk, nw, state_gla[0], row0=n_p, n_seq=bs, seq_len=ts, chunk=ts, nseq=8)
    or_s, sr_s = _ret(proj, cos_s, sin_s, state_ret[0], row0=n_p, n_seq=bs, seq_len=ts, chunk=ts, nseq=ret_nseq_s)

    x1, hn, route, counts = _outproj_router(oa_p, oa_s, or_p, or_s, xp, xs, w_out[0].astype(BF16),
                                            ln_ffn, w_rt, b_rt)

    t_exp, n_valid, pos = _routing_tables(route, counts, n)
    y2 = _moe(t_exp, n_valid, pos, hn, w_exp_gate[0], w_exp_up[0], w_exp_down[0])

    ln_f = ln_final[None, :]
    y_p = _final(x1, y2, route, ln_f, row0=0, rows=n_p).reshape(bp, tp, d)
    y_s = _final(x1, y2, route, ln_f, row0=n_p, rows=n_s).reshape(bs, ts, d)
    return (y_p, y_s, sg_p[None], sr_p[None], sg_s[None], sr_s[None])
```

```python
import functools
import math

import jax
import jax.numpy as jnp
from jax import lax
from jax.experimental import pallas as pl
from jax.experimental.pallas import tpu as pltpu

F32 = jnp.float32
BF16 = jnp.bfloat16
I32 = jnp.int32

D_MODEL = 2048
PAST_LEN = 16384
H_GLA, DK_GLA, DV_GLA = 4, 128, 256
GLA_LOWRANK = 16
GLA_GATE_NORM = 16.0
GLA_SUBCHUNK = 16
H_RET, DK_RET, DV_RET = 4, 256, 256
ROPE_BASE = 10000.0
N_GROUPS, EXPERTS_PER_GROUP, D_EXPERT = 4, 8, 512
N_EXPERTS = N_GROUPS * EXPERTS_PER_GROUP
EPS = 1e-6

LANES = 128
SUBLANES = 8
V7X_SCOPED_VMEM_BYTES = 56 * 1024 * 1024

GLA_QK = H_GLA * DK_GLA
GLA_V = H_GLA * DV_GLA
RET_QK = H_RET * DK_RET
RET_V = H_RET * DV_RET
C_QA, C_KA, C_VA, C_GA = 0, GLA_QK, 2 * GLA_QK, 2 * GLA_QK + GLA_V
C_QR = C_GA + GLA_V
C_KR, C_VR, C_GR = C_QR + RET_QK, C_QR + 2 * RET_QK, C_QR + 2 * RET_QK + RET_V
C_LR = C_GR + RET_V
P_COLS = C_LR + LANES
IN_COLS_SRC = 2 * GLA_QK + GLA_V + GLA_LOWRANK + GLA_V + 2 * RET_QK + 2 * RET_V

PROJ_TM = 256
W_CHUNK = 1024
MOE_TM = 256
ATTN_CHUNK = 128
SAMPLE_SEQS_PER_STEP = 4

def _dot(a, b):
    return jnp.dot(a, b, preferred_element_type=F32)


def _dot_nt(a, b):
    return lax.dot_general(a, b, (((1,), (1,)), ((), ())), preferred_element_type=F32)


def _dot_tn(a, b):
    return lax.dot_general(a, b, (((0,), (0,)), ((), ())), preferred_element_type=F32)


def _split_bf16(x, parts):
    out = []
    for _ in range(parts):
        p = x.astype(BF16)
        out.append(p)
        x = x - p.astype(F32)
    return out


def _dot_split(a, b_hi, b_lo):
    a_hi, a_lo = _split_bf16(a, 2)
    return _dot(a_hi, b_hi) + (_dot(a_lo, b_hi) + _dot(a_hi, b_lo))


def _rms(x):
    return x * lax.rsqrt(jnp.mean(x * x, axis=-1, keepdims=True) + EPS)


def _silu(x):
    return x * jax.nn.sigmoid(x)


def _wprep_body(a_ref, b_ref, o_ref, olr_ref):
    j = pl.program_id(0)
    a = a_ref[...]
    b = b_ref[...]
    shifted = jnp.concatenate([a[GLA_LOWRANK:, :], b], axis=0)
    o_ref[...] = jnp.where(j >= C_GA // W_CHUNK, shifted, a).T.astype(BF16)

    @pl.when(j == C_GA // W_CHUNK - 1)
    def _():
        padded = jnp.concatenate([b, jnp.zeros((LANES - GLA_LOWRANK, D_MODEL), F32)], axis=0)
        olr_ref[...] = padded.T.astype(BF16)


def _wprep(wt):
    per = W_CHUNK // GLA_LOWRANK
    return pl.pallas_call(
        _wprep_body,
        out_shape=(jax.ShapeDtypeStruct((D_MODEL, C_LR), BF16), jax.ShapeDtypeStruct((D_MODEL, LANES), BF16)),
        grid=(C_LR // W_CHUNK,),
        in_specs=[pl.BlockSpec((W_CHUNK, D_MODEL), lambda j: (j, 0)),
                  pl.BlockSpec((GLA_LOWRANK, D_MODEL), lambda j: (per * (j + 1), 0))],
        out_specs=(pl.BlockSpec((D_MODEL, W_CHUNK), lambda j: (0, j)),
                   pl.BlockSpec((D_MODEL, LANES), lambda j: (0, 0))),
        compiler_params=pltpu.CompilerParams(dimension_semantics=("arbitrary",),
                                             vmem_limit_bytes=V7X_SCOPED_VMEM_BYTES),
        name="wprep",
    )(wt, wt)


def _inproj_body(xp_ref, xs_ref, ln_ref, w_ref, wlr_ref, o_ref, *, n_prompt_blocks):
    x = jnp.where(pl.program_id(0) < n_prompt_blocks, xp_ref[...], xs_ref[...])
    h = (_rms(x) * ln_ref[...]).astype(BF16)
    for c in range(C_LR // W_CHUNK):
        o_ref[:, c * W_CHUNK:(c + 1) * W_CHUNK] = _dot(h, w_ref[:, c * W_CHUNK:(c + 1) * W_CHUNK])
    o_ref[:, C_LR:] = _dot(h, wlr_ref[...])


def _inproj(xp, xs, ln, w, w_lr):
    nbp = xp.shape[0] // PROJ_TM
    n = xp.shape[0] + xs.shape[0]
    fixed = lambda i: (0, 0)
    return pl.pallas_call(
        functools.partial(_inproj_body, n_prompt_blocks=nbp),
        out_shape=jax.ShapeDtypeStruct((n, P_COLS), F32),
        grid=(n // PROJ_TM,),
        in_specs=[pl.BlockSpec((PROJ_TM, D_MODEL), lambda i: (jnp.minimum(i, nbp - 1), 0)),
                  pl.BlockSpec((PROJ_TM, D_MODEL), lambda i: (jnp.maximum(i - nbp, 0), 0)),
                  pl.BlockSpec((1, D_MODEL), fixed),
                  pl.BlockSpec((D_MODEL, C_LR), fixed, pipeline_mode=pl.Buffered(1)),
                  pl.BlockSpec((D_MODEL, LANES), fixed)],
        out_specs=pl.BlockSpec((PROJ_TM, P_COLS), lambda i: (i, 0)),
        compiler_params=pltpu.CompilerParams(dimension_semantics=("arbitrary",),
                                             vmem_limit_bytes=V7X_SCOPED_VMEM_BYTES),
        name="inproj",
    )(xp, xs, ln, w, w_lr)


def _seg_cumsum(x, seg):
    pos = lax.broadcasted_iota(I32, x.shape, 0) & (seg - 1)
    s = 1
    while s < seg:
        x = x + jnp.where(pos >= s, pltpu.roll(x, s, axis=0), 0.0)
        s *= 2
    return x


def _block_ends(p, m):
    rows, w = p.shape
    return jnp.concatenate(
        [jnp.broadcast_to(p[j * m + m - 1:j * m + m, :], (m, w)) for j in range(rows // m)], axis=0)


def _block_starts(p, m, chunk):
    rows, w = p.shape
    pieces = []
    for j in range(rows // m):
        if (j * m) % chunk == 0:
            pieces.append(jnp.zeros((m, w), F32))
        else:
            pieces.append(jnp.broadcast_to(p[j * m - 1:j * m, :], (m, w)))
    return jnp.concatenate(pieces, axis=0)


def _gla_body(*refs, chunk, nseq, has_init):
    if has_init:
        (q_ref, k_ref, v_ref, g_ref, lr_ref, wgk_ref, bgk_ref, nw_ref, s0_ref,
         o_ref, sout_ref, s_scr) = refs
    else:
        (q_ref, k_ref, v_ref, g_ref, lr_ref, wgk_ref, bgk_ref, nw_ref,
         o_ref, sout_ref, s_scr) = refs
    t = pl.program_id(1)

    @pl.when(t == 0)
    def _():
        if has_init:
            s_scr[...] = s0_ref[...]
        else:
            s_scr[...] = jnp.zeros_like(s_scr)

    x = _dot_split(lr_ref[...], wgk_ref[0], wgk_ref[1]) + bgk_ref[...]
    log_g = -(jnp.maximum(-x, 0.0) + jnp.log1p(jnp.exp(-jnp.abs(x)))) / GLA_GATE_NORM
    p = _seg_cumsum(log_g, chunk)
    p_end = _block_ends(p, chunk)
    eq_c = jnp.exp(p)
    ek_c = jnp.exp(p_end - p)
    sub = min(GLA_SUBCHUNK, chunk)
    p_sub = p - _block_starts(p, sub, chunk)
    eq_d = jnp.exp(p_sub)
    ek_d = jnp.exp(-p_sub)
    levels = []
    m = sub
    while m < chunk:
        levels.append((m, jnp.exp(p - _block_starts(p, m, chunk)), jnp.exp(_block_ends(p, m) - p)))
        m *= 2

    ti = lax.broadcasted_iota(I32, (chunk, chunk), 0)
    si = lax.broadcasted_iota(I32, (chunk, chunk), 1)
    blk = lambda idx, size: idx >> (size.bit_length() - 1)
    mask_d = (blk(ti, sub) == blk(si, sub)) & (si <= ti)
    masks = [((blk(ti, m) & 1) == 1) & (blk(si, m) == blk(ti, m) - 1) for (m, _, _) in levels]
    sel8 = (lax.broadcasted_iota(I32, (8, DV_GLA), 0) == 0).astype(BF16)
    nw = nw_ref[...]

    outs = [[None] * nseq for _ in range(H_GLA)]
    for b in range(nseq):
        r0 = b * chunk
        rows = slice(r0, r0 + chunk)
        for h in range(H_GLA):
            kc = slice(h * DK_GLA, (h + 1) * DK_GLA)
            vc = slice(h * DV_GLA, (h + 1) * DV_GLA)
            q = q_ref[rows, kc] * (DK_GLA ** -0.5)
            k = k_ref[rows, kc]
            v = v_ref[rows, vc].astype(BF16)
            att = jnp.where(mask_d, _dot_nt((q * eq_d[rows, kc]).astype(BF16),
                                            (k * ek_d[rows, kc]).astype(BF16)), 0.0)
            for (m, eq_m, ek_m), mask in zip(levels, masks):
                att = jnp.where(mask, _dot_nt((q * eq_m[rows, kc]).astype(BF16),
                                              (k * ek_m[rows, kc]).astype(BF16)), att)
            s_old = s_scr[b, h]
            o = _dot(att.astype(BF16), v) + _dot((q * eq_c[rows, kc]).astype(BF16), s_old.astype(BF16))
            kv = _dot_tn((k * ek_c[rows, kc]).astype(BF16), v)
            tot = jnp.broadcast_to(p[r0 + chunk - 1:r0 + chunk, kc], (8, DK_GLA))
            t_hi, t_mid, t_lo = _split_bf16(tot, 3)
            decay = jnp.exp(_dot_tn(t_hi, sel8) + (_dot_tn(t_mid, sel8) + _dot_tn(t_lo, sel8)))
            s_scr[b, h] = decay * s_old + kv
            outs[h][b] = _rms(o) * nw * _silu(g_ref[rows, vc])
    for h in range(H_GLA):
        o_ref[:, h * DV_GLA:(h + 1) * DV_GLA] = jnp.concatenate(outs[h], axis=0).astype(BF16)

    def emit_state():
        @pl.when(t == pl.num_programs(1) - 1)
        def _():
            sout_ref[...] = s_scr[...]
    return emit_state


def _ret_body(q_ref, k_ref, v_ref, g_ref, cos_ref, sin_ref, o_ref, sout_ref, s_scr, *, chunk, nseq):
    t = pl.program_id(1)

    half = DK_RET // 2
    ti = lax.broadcasted_iota(I32, (chunk, chunk), 0)
    si = lax.broadcasted_iota(I32, (chunk, chunk), 1)
    diff = (ti - si).astype(F32)
    idx = lax.broadcasted_iota(I32, (chunk, 1), 0).astype(F32)

    outs = [[None] * nseq for _ in range(H_RET)]
    for h in range(H_RET):
        lg = math.log(1.0 - 2.0 ** (-5.0 - h))
        dmat = jnp.where(diff >= 0, jnp.exp(lg * jnp.maximum(diff, 0.0)), 0.0)
        q_dec = jnp.exp(lg * (idx + 1.0))
        k_dec = jnp.exp(lg * (chunk - 1.0 - idx))
        c_dec = math.exp(lg * chunk)
        for b in range(nseq):
            rows = slice(b * chunk, (b + 1) * chunk)
            cos = cos_ref[rows, :]
            sin = sin_ref[rows, :]
            c1 = slice(h * DK_RET, h * DK_RET + half)
            c2 = slice(h * DK_RET + half, (h + 1) * DK_RET)
            vc = slice(h * DV_RET, (h + 1) * DV_RET)
            q1, q2 = q_ref[rows, c1], q_ref[rows, c2]
            k1, k2 = k_ref[rows, c1], k_ref[rows, c2]
            q = jnp.concatenate([q1 * cos - q2 * sin, q1 * sin + q2 * cos], axis=-1)
            k = jnp.concatenate([k1 * cos - k2 * sin, k1 * sin + k2 * cos], axis=-1) * (DK_RET ** -0.5)
            v = v_ref[rows, vc].astype(BF16)
            qb = q.astype(BF16)
            s_old = s_scr[b, h]
            att = _dot_nt(qb, k.astype(BF16)) * dmat
            o = _dot(att.astype(BF16), v) + _dot(qb, s_old.astype(BF16)) * q_dec
            s_scr[b, h] = c_dec * s_old + _dot_tn((k * k_dec).astype(BF16), v)
            outs[h][b] = _rms(o) * _silu(g_ref[rows, vc])
    for h in range(H_RET):
        o_ref[:, h * DV_RET:(h + 1) * DV_RET] = jnp.concatenate(outs[h], axis=0).astype(BF16)

    @pl.when(t == pl.num_programs(1) - 1)
    def _():
        sout_ref[...] = s_scr[...]


def _gla_ret_body(*refs, chunk, nseq, has_init):
    n_gla, n_ret = 8 + has_init, 6 + has_init
    gla_in, ret_in = refs[:n_gla], refs[n_gla:n_gla + n_ret]
    oa_ref, sg_ref, or_ref, sr_ref, sg_scr, sr_scr = refs[n_gla + n_ret:]

    @pl.when(pl.program_id(1) == 0)
    def _():
        sr_scr[...] = ret_in[-1][...] if has_init else jnp.zeros_like(sr_scr)
    emit_gla_state = _gla_body(*gla_in, oa_ref, sg_ref, sg_scr, chunk=chunk, nseq=nseq, has_init=has_init)
    _ret_body(*ret_in[:6], or_ref, sr_ref, sr_scr, chunk=chunk, nseq=nseq)
    emit_gla_state()


def _gla_ret(proj, w_gk2p, b_gk, nw, cos, sin, s0_gla, s0_ret, *, row0, n_seq, seq_len, chunk, nseq):
    has_init = s0_gla is not None
    rows = nseq * chunk
    nt = seq_len // chunk
    base = row0 // rows
    rmap = lambda col: (lambda b, t: (base + b * nt + t, col))
    omap = lambda b, t: (b * nt + t, 0)
    fixed2 = lambda b, t: (0, 0)
    smap = lambda b, t: (b, 0, 0, 0)
    tab_spec = pl.BlockSpec((rows, DK_RET // 2), lambda b, t: (t, 0))
    gla_state = pl.BlockSpec((nseq, H_GLA, DK_GLA, DV_GLA), smap)
    ret_state = pl.BlockSpec((nseq, H_RET, DK_RET, DV_RET), smap)
    gla_specs = [pl.BlockSpec((rows, GLA_QK), rmap(C_QA // GLA_QK)),
                 pl.BlockSpec((rows, GLA_QK), rmap(C_KA // GLA_QK)),
                 pl.BlockSpec((rows, GLA_V), rmap(C_VA // GLA_V)),
                 pl.BlockSpec((rows, GLA_V), rmap(C_GA // GLA_V)),
                 pl.BlockSpec((rows, LANES), rmap(C_LR // LANES)),
                 pl.BlockSpec((2, LANES, GLA_QK), lambda b, t: (0, 0, 0)),
                 pl.BlockSpec((1, GLA_QK), fixed2),
                 pl.BlockSpec((1, DV_GLA), fixed2)]
    ret_specs = [pl.BlockSpec((rows, RET_QK), rmap(C_QR // RET_QK)),
                 pl.BlockSpec((rows, RET_QK), rmap(C_KR // RET_QK)),
                 pl.BlockSpec((rows, RET_V), rmap(C_VR // RET_V)),
                 pl.BlockSpec((rows, RET_V), rmap(C_GR // RET_V)),
                 tab_spec, tab_spec]
    gla_args = [proj, proj, proj, proj, proj, w_gk2p, b_gk, nw]
    ret_args = [proj, proj, proj, proj, cos, sin]
    if has_init:
        gla_specs.append(gla_state)
        gla_args.append(s0_gla)
        ret_specs.append(ret_state)
        ret_args.append(s0_ret)
    return pl.pallas_call(
        functools.partial(_gla_ret_body, chunk=chunk, nseq=nseq, has_init=has_init),
        out_shape=(jax.ShapeDtypeStruct((n_seq * seq_len, GLA_V), BF16),
                   jax.ShapeDtypeStruct((n_seq, H_GLA, DK_GLA, DV_GLA), F32),
                   jax.ShapeDtypeStruct((n_seq * seq_len, RET_V), BF16),
                   jax.ShapeDtypeStruct((n_seq, H_RET, DK_RET, DV_RET), F32)),
        grid=(n_seq // nseq, nt),
        in_specs=gla_specs + ret_specs,
        out_specs=(pl.BlockSpec((rows, GLA_V), omap), gla_state, pl.BlockSpec((rows, RET_V), omap), ret_state),
        scratch_shapes=[pltpu.VMEM((nseq, H_GLA, DK_GLA, DV_GLA), F32),
                        pltpu.VMEM((nseq, H_RET, DK_RET, DV_RET), F32)],
        compiler_params=pltpu.CompilerParams(dimension_semantics=("arbitrary", "arbitrary"),
                                             vmem_limit_bytes=V7X_SCOPED_VMEM_BYTES),
        name="gla_ret_init" if has_init else "gla_ret",
    )(*gla_args, *ret_args)


def _rope_tables(pos, reps):
    half = DK_RET // 2
    inv = ROPE_BASE ** (-jnp.arange(half, dtype=F32) / half)
    ang = pos[:, None] * inv[None, :]
    return jnp.tile(jnp.cos(ang), (reps, 1)), jnp.tile(jnp.sin(ang), (reps, 1))


def _outproj_body(oap_ref, oas_ref, orp_ref, ors_ref, xp_ref, xs_ref, w_ref, ln_ref, wrt_ref, brt_ref,
                  x1_ref, hn_ref, route_ref, counts_ref, *, n_prompt_blocks):
    step = pl.program_id(0)
    is_prompt = step < n_prompt_blocks
    oa = jnp.where(is_prompt, oap_ref[...], oas_ref[...])
    orr = jnp.where(is_prompt, orp_ref[...], ors_ref[...])
    x = jnp.where(is_prompt, xp_ref[...], xs_ref[...])
    x1 = x + _dot(oa, w_ref[:GLA_V, :]) + _dot(orr, w_ref[GLA_V:, :])
    x1_ref[...] = x1
    hn = _rms(x1) * ln_ref[...]
    hn_ref[...] = hn

    logits = _dot_split(hn, wrt_ref[0], wrt_ref[1]) + brt_ref[...]
    lane_i = lax.broadcasted_iota(I32, logits.shape, 1)
    lane = lane_i.astype(F32)
    grp = (lane_i >> (EXPERTS_PER_GROUP.bit_length() - 1)).astype(F32)
    neg = -jnp.inf
    far = float(LANES)

    gl = jnp.where((lane_i >= N_EXPERTS) & (lane_i < N_EXPERTS + N_GROUPS), logits, neg)
    gmax = jnp.max(gl, axis=-1, keepdims=True)
    g_w = 1.0 / jnp.sum(jnp.exp(gl - gmax), axis=-1, keepdims=True)
    g_idx = jnp.min(jnp.where(gl == gmax, lane, far), axis=-1, keepdims=True) - float(N_EXPERTS)

    el = jnp.where((lane_i < N_EXPERTS) & (grp == g_idx), logits, neg)
    m1 = jnp.max(el, axis=-1, keepdims=True)
    esum = jnp.sum(jnp.exp(el - m1), axis=-1, keepdims=True)
    i1 = jnp.min(jnp.where(el == m1, lane, far), axis=-1, keepdims=True)
    el2 = jnp.where(lane == i1, neg, el)
    m2 = jnp.max(el2, axis=-1, keepdims=True)
    i2 = jnp.min(jnp.where(el2 == m2, lane, far), axis=-1, keepdims=True)
    p1 = 1.0 / esum
    p2 = jnp.exp(m2 - m1) / esum
    gate1 = g_w * (p1 / (p1 + p2))
    gate2 = g_w * (p2 / (p1 + p2))
    @pl.when(step == 0)
    def _():
        counts_ref[...] = jnp.zeros_like(counts_ref)

    tm = logits.shape[0]
    earlier = (lax.broadcasted_iota(I32, (tm, tm), 0) > lax.broadcasted_iota(I32, (tm, tm), 1)).astype(BF16)
    running = counts_ref[0:1, :]
    ranks = []
    for idx in (i1, i2):
        onehot = (lane == idx).astype(F32)
        before = _dot(earlier, onehot.astype(BF16)) + running
        ranks.append(jnp.sum(onehot * before, axis=-1, keepdims=True))
        running = running + jnp.sum(onehot, axis=0, keepdims=True)
    counts_ref[...] = jnp.broadcast_to(running, counts_ref.shape)

    route_ref[...] = jnp.where(lane_i == 0, i1,
                               jnp.where(lane_i == 1, i2,
                                         jnp.where(lane_i == 2, gate1,
                                                   jnp.where(lane_i == 3, gate2,
                                                             jnp.where(lane_i == 4, ranks[0],
                                                                       jnp.where(lane_i == 5, ranks[1], 0.0))))))


def _outproj_router(oa_p, oa_s, or_p, or_s, xp, xs, w_out, ln, w_rt, b_rt):
    n = xp.shape[0] + xs.shape[0]
    nbp = xp.shape[0] // PROJ_TM
    pmap = lambda i: (jnp.minimum(i, nbp - 1), 0)
    smap = lambda i: (jnp.maximum(i - nbp, 0), 0)
    row = lambda i: (i, 0)
    fixed = lambda i: (0, 0)
    return pl.pallas_call(
        functools.partial(_outproj_body, n_prompt_blocks=nbp),
        out_shape=(jax.ShapeDtypeStruct((n, D_MODEL), F32),
                   jax.ShapeDtypeStruct((n, D_MODEL), F32),
                   jax.ShapeDtypeStruct((n, LANES), F32),
                   jax.ShapeDtypeStruct((SUBLANES, LANES), F32)),
        grid=(n // PROJ_TM,),
        in_specs=[pl.BlockSpec((PROJ_TM, GLA_V), pmap), pl.BlockSpec((PROJ_TM, GLA_V), smap),
                  pl.BlockSpec((PROJ_TM, RET_V), pmap), pl.BlockSpec((PROJ_TM, RET_V), smap),
                  pl.BlockSpec((PROJ_TM, D_MODEL), pmap), pl.BlockSpec((PROJ_TM, D_MODEL), smap),
                  pl.BlockSpec((GLA_V + RET_V, D_MODEL), fixed, pipeline_mode=pl.Buffered(1)),
                  pl.BlockSpec((1, D_MODEL), fixed),
                  pl.BlockSpec((2, D_MODEL, LANES), lambda i: (0, 0, 0)),
                  pl.BlockSpec((1, LANES), fixed)],
        out_specs=(pl.BlockSpec((PROJ_TM, D_MODEL), row), pl.BlockSpec((PROJ_TM, D_MODEL), row),
                   pl.BlockSpec((PROJ_TM, LANES), row), pl.BlockSpec((SUBLANES, LANES), fixed)),
        compiler_params=pltpu.CompilerParams(dimension_semantics=("arbitrary",),
                                             vmem_limit_bytes=V7X_SCOPED_VMEM_BYTES),
        name="outproj_router",
    )(oa_p, oa_s, or_p, or_s, xp, xs, w_out, ln, w_rt, b_rt)


def _moe_body(texp_ref, nval_ref, pos_ref, hn_hbm, wg_ref, wu_ref, wd_ref, y_hbm,
              inv, xbuf, ybuf, wg_b, wu_b, wd_b, gsem, ssem, *, n_tokens):
    tm = MOE_TM
    i = pl.program_id(0)
    nt = pl.num_programs(0)
    slot = i % 2

    def for_rows(n_rows, fn):
        n_grp = n_rows // SUBLANES

        def group(g, c):
            for j in range(SUBLANES):
                fn(g * SUBLANES + j)
            return c
        lax.fori_loop(0, n_grp, group, 0)

        def single(r, c):
            fn(r)
            return c
        lax.fori_loop(n_grp * SUBLANES, n_rows, single, 0)

    def gather_copy(tok, r, slot_):
        return pltpu.make_async_copy(hn_hbm.at[pl.ds(tok, 1), :], xbuf.at[slot_, pl.ds(r, 1), :], gsem.at[slot_])

    def scatter_copy(r, dst, slot_):
        return pltpu.make_async_copy(ybuf.at[slot_, pl.ds(r, 1), :], y_hbm.at[pl.ds(dst, 1), :], ssem.at[slot_])

    def start_gather(tile, slot_):
        for_rows(nval_ref[tile], lambda r: gather_copy(inv[tile * tm + r] >> 1, r, slot_).start())

    def wait_rows(n_rows, group_copy, row_copy):
        n_grp = pl.multiple_of((n_rows // SUBLANES) * SUBLANES, SUBLANES)

        @pl.when(n_grp > 0)
        def _():
            group_copy(n_grp).wait()

        def single(r, c):
            row_copy().wait()
            return c
        lax.fori_loop(n_grp, n_rows, single, 0)

    def wait_gather(slot_, n_rows):
        wait_rows(n_rows,
                  lambda m: pltpu.make_async_copy(hn_hbm.at[pl.ds(0, m), :], xbuf.at[slot_, pl.ds(0, m), :],
                                                  gsem.at[slot_]),
                  lambda: gather_copy(0, 0, slot_))

    def wait_scatter(slot_, n_rows):
        wait_rows(n_rows,
                  lambda m: pltpu.make_async_copy(ybuf.at[slot_, pl.ds(0, m), :], y_hbm.at[pl.ds(0, m), :],
                                                  ssem.at[slot_]),
                  lambda: scatter_copy(0, 0, slot_))

    @pl.when(i == 0)
    def _():
        def body(a, c):
            inv[pos_ref[a]] = a
            return c
        lax.fori_loop(0, 2 * n_tokens, body, 0, unroll=8)
        xbuf[...] = jnp.zeros_like(xbuf)
        start_gather(0, 0)

    @pl.when(i + 1 < nt)
    def _():
        start_gather(jnp.minimum(i + 1, nt - 1), 1 - slot)

    @pl.when(i >= 2)
    def _():
        wait_scatter(slot, nval_ref[jnp.maximum(i - 2, 0)])

    nv = nval_ref[i]
    wait_gather(slot, nv)

    @pl.when(nv > 0)
    def _():
        @pl.when((i == 0) | (texp_ref[i] != texp_ref[jnp.maximum(i - 1, 0)]))
        def _():
            wg_b[...] = wg_ref[0].astype(BF16)
            wu_b[...] = wu_ref[0].astype(BF16)
            wd_b[...] = wd_ref[0].astype(BF16)

        h = xbuf[slot].astype(BF16)
        hid = _silu(_dot(h, wg_b[...])) * _dot(h, wu_b[...])
        ybuf[slot] = _dot(hid.astype(BF16), wd_b[...])

        def scatter_one(r):
            a = inv[i * tm + r]
            scatter_copy(r, (a & 1) * n_tokens + (a >> 1), slot).start()
        for_rows(nv, scatter_one)

    @pl.when(i == nt - 1)
    def _():
        @pl.when(i >= 1)
        def _():
            wait_scatter(1 - slot, nval_ref[jnp.maximum(i - 1, 0)])
        wait_scatter(slot, nv)


def _moe(tile_expert, n_valid, pos, hn, w_gate, w_up, w_down):
    n = hn.shape[0]
    n_tiles = tile_expert.shape[0]
    wmap = lambda i, te, nv, pos_: (te[i], 0, 0)
    return pl.pallas_call(
        functools.partial(_moe_body, n_tokens=n),
        out_shape=jax.ShapeDtypeStruct((2 * n, D_MODEL), F32),
        grid_spec=pltpu.PrefetchScalarGridSpec(
            num_scalar_prefetch=3,
            grid=(n_tiles,),
            in_specs=[pl.BlockSpec(memory_space=pl.ANY),
                      pl.BlockSpec((1, D_MODEL, D_EXPERT), wmap),
                      pl.BlockSpec((1, D_MODEL, D_EXPERT), wmap),
                      pl.BlockSpec((1, D_EXPERT, D_MODEL), wmap)],
            out_specs=pl.BlockSpec(memory_space=pl.ANY),
            scratch_shapes=[pltpu.SMEM((n_tiles * MOE_TM,), I32),
                            pltpu.VMEM((2, MOE_TM, D_MODEL), F32),
                            pltpu.VMEM((2, MOE_TM, D_MODEL), F32),
                            pltpu.VMEM((D_MODEL, D_EXPERT), BF16),
                            pltpu.VMEM((D_MODEL, D_EXPERT), BF16),
                            pltpu.VMEM((D_EXPERT, D_MODEL), BF16),
                            pltpu.SemaphoreType.DMA((2,)),
                            pltpu.SemaphoreType.DMA((2,))]),
        compiler_params=pltpu.CompilerParams(dimension_semantics=("arbitrary",),
                                             vmem_limit_bytes=V7X_SCOPED_VMEM_BYTES),
        name="moe",
    )(tile_expert, n_valid, pos, hn, w_gate, w_up, w_down)


def _routing_tables(route, counts, n):
    tm = MOE_TM
    n_tiles = 2 * n // tm + N_EXPERTS
    experts = route[:, 0:2].astype(I32)
    rank = route[:, 4:6].astype(I32)
    counts = counts[0, :N_EXPERTS].astype(I32)
    tiles = (counts + tm - 1) // tm
    tile_end = jnp.cumsum(tiles)
    tile_start = tile_end - tiles
    onehot = experts[:, :, None] == jnp.arange(N_EXPERTS, dtype=I32)[None, None, :]
    pos = jnp.sum(jnp.where(onehot, (tile_start * tm)[None, None, :], 0), axis=-1) + rank
    tidx = jnp.arange(n_tiles, dtype=I32)
    t_exp = jnp.sum((tidx[:, None] >= tile_end[None, :]).astype(I32), axis=1)
    used = tidx < tile_end[-1]
    last_exp = jnp.sum((tile_end[-1] - 1 >= tile_end).astype(I32))
    t_exp = jnp.where(used, t_exp, last_exp)
    n_valid = jnp.where(used, jnp.clip(counts[t_exp] - (tidx - tile_start[t_exp]) * tm, 0, tm), 0)
    return t_exp.astype(I32), n_valid.astype(I32), pos.reshape(2 * n)


def _final_body(x1_ref, y0_ref, y1_ref, route_ref, ln_ref, o_ref):
    z = x1_ref[...] + route_ref[:, 2:3] * y0_ref[...] + route_ref[:, 3:4] * y1_ref[...]
    o_ref[...] = _rms(z) * ln_ref[...]


def _final(x1, y2, route, ln, *, row0, rows):
    base = row0 // PROJ_TM
    slot1 = x1.shape[0] // PROJ_TM
    return pl.pallas_call(
        _final_body,
        out_shape=jax.ShapeDtypeStruct((rows, D_MODEL), F32),
        grid=(rows // PROJ_TM,),
        in_specs=[pl.BlockSpec((PROJ_TM, D_MODEL), lambda i: (base + i, 0)),
                  pl.BlockSpec((PROJ_TM, D_MODEL), lambda i: (base + i, 0)),
                  pl.BlockSpec((PROJ_TM, D_MODEL), lambda i: (slot1 + base + i, 0)),
                  pl.BlockSpec((PROJ_TM, LANES), lambda i: (base + i, 0)),
                  pl.BlockSpec((1, D_MODEL), lambda i: (0, 0))],
        out_specs=pl.BlockSpec((PROJ_TM, D_MODEL), lambda i: (i, 0)),
        compiler_params=pltpu.CompilerParams(dimension_semantics=("arbitrary",)),
        name="final_norm",
    )(x1, y2, y2, route, ln)


def kernel(x_prompt, x_sample, state_gla, state_ret, ln_attn, w_in, w_gk2, b_gk, gla_norm_w, w_out, ln_ffn, w_router_group, b_router_group, w_router_expert, b_router_expert, w_exp_gate, w_exp_up, w_exp_down, ln_final):
    bp, tp, d = x_prompt.shape
    bs, ts, _ = x_sample.shape
    assert d == D_MODEL and w_in.shape == (1, D_MODEL, IN_COLS_SRC)
    n_p, n_s = bp * tp, bs * ts
    n = n_p + n_s
    assert n_p % PROJ_TM == 0 and n_s % PROJ_TM == 0 and tp % ATTN_CHUNK == 0

    xp = x_prompt.reshape(n_p, d)
    xs = x_sample.reshape(n_s, d)

    hi_lo = lambda a: jnp.stack(_split_bf16(a, 2))
    w_in_p, w_lr = _wprep(w_in[0].T)
    w_gk2p = hi_lo(jnp.concatenate([w_gk2[0], jnp.zeros((LANES - GLA_LOWRANK, GLA_QK), F32)], axis=0))
    w_rt = hi_lo(jnp.concatenate([w_router_expert[0], w_router_group[0],
                                  jnp.zeros((d, LANES - N_EXPERTS - N_GROUPS), F32)], axis=1))
    b_rt = jnp.concatenate([b_router_expert[0], b_router_group[0],
                            jnp.zeros((LANES - N_EXPERTS - N_GROUPS,), F32)])[None, :]

    proj = _inproj(xp, xs, ln_attn, w_in_p, w_lr)

    cos_p, sin_p = _rope_tables(jnp.arange(tp, dtype=F32), 1)
    cos_s, sin_s = _rope_tables(jnp.arange(ts, dtype=F32) + float(PAST_LEN), SAMPLE_SEQS_PER_STEP)
    oa_p, sg_p, or_p, sr_p = _gla_ret(proj, w_gk2p, b_gk, gla_norm_w, cos_p, sin_p, None, None,
                                      row0=0, n_seq=bp, seq_len=tp, chunk=ATTN_CHUNK, nseq=1)
    oa_s, sg_s, or_s, sr_s = _gla_ret(proj, w_gk2p, b_gk, gla_norm_w, cos_s, sin_s, state_gla[0], state_ret[0],
                                      row0=n_p, n_seq=bs, seq_len=ts, chunk=ts, nseq=SAMPLE_SEQS_PER_STEP)

    x1, hn, route, counts = _outproj_router(oa_p, oa_s, or_p, or_s, xp, xs, w_out[0].astype(BF16),
                                            ln_ffn, w_rt, b_rt)

    t_exp, n_valid, pos = _routing_tables(route, counts, n)
    y2 = _moe(t_exp, n_valid, pos, hn, w_exp_gate[0], w_exp_up[0], w_exp_down[0])

    ln_f = ln_final[None, :]
    y_p = _final(x1, y2, route, ln_f, row0=0, rows=n_p).reshape(bp, tp, d)
    y_s = _final(x1, y2, route, ln_f, row0=n_p, rows=n_s).reshape(bs, ts, d)
    return (y_p, y_s, sg_p[None], sr_p[None], sg_s[None], sr_s[None])
```

```python
import functools
import math

import jax
import jax.numpy as jnp
from jax import lax
from jax.experimental import pallas as pl
from jax.experimental.pallas import tpu as pltpu

F32 = jnp.float32
BF16 = jnp.bfloat16
I32 = jnp.int32

D_MODEL = 2048
PAST_LEN = 16384
H_GLA, DK_GLA, DV_GLA = 4, 128, 256
GLA_LOWRANK = 16
GLA_GATE_NORM = 16.0
GLA_SUBCHUNK = 16
H_RET, DK_RET, DV_RET = 4, 256, 256
ROPE_BASE = 10000.0
N_GROUPS, EXPERTS_PER_GROUP, D_EXPERT = 4, 8, 512
N_EXPERTS = N_GROUPS * EXPERTS_PER_GROUP
EPS = 1e-6

LANES = 128
SUBLANES = 8
V7X_SCOPED_VMEM_BYTES = 56 * 1024 * 1024

GLA_QK = H_GLA * DK_GLA
GLA_V = H_GLA * DV_GLA
RET_QK = H_RET * DK_RET
RET_V = H_RET * DV_RET
C_QA, C_KA, C_VA, C_GA = 0, GLA_QK, 2 * GLA_QK, 2 * GLA_QK + GLA_V
C_QR = C_GA + GLA_V
C_KR, C_VR, C_GR = C_QR + RET_QK, C_QR + 2 * RET_QK, C_QR + 2 * RET_QK + RET_V
C_LR = C_GR + RET_V
P_COLS = C_LR + LANES
IN_COLS_SRC = 2 * GLA_QK + GLA_V + GLA_LOWRANK + GLA_V + 2 * RET_QK + 2 * RET_V

PROJ_TM = 256
W_CHUNK = 1024
MOE_TM = 256
ATTN_CHUNK = 128
SAMPLE_SEQS_PER_STEP = 4

def _dot(a, b):
    return jnp.dot(a, b, preferred_element_type=F32)


def _dot_nt(a, b):
    return lax.dot_general(a, b, (((1,), (1,)), ((), ())), preferred_element_type=F32)


def _dot_tn(a, b):
    return lax.dot_general(a, b, (((0,), (0,)), ((), ())), preferred_element_type=F32)


def _split_bf16(x, parts):
    out = []
    for _ in range(parts):
        p = x.astype(BF16)
        out.append(p)
        x = x - p.astype(F32)
    return out


def _dot_split(a, b_hi, b_lo):
    a_hi, a_lo = _split_bf16(a, 2)
    return _dot(a_hi, b_hi) + (_dot(a_lo, b_hi) + _dot(a_hi, b_lo))


def _rms(x):
    return x * lax.rsqrt(jnp.mean(x * x, axis=-1, keepdims=True) + EPS)


def _silu(x):
    return x * jax.nn.sigmoid(x)


def _wprep_body(a_ref, b_ref, o_ref, olr_ref):
    j = pl.program_id(0)
    a = a_ref[...]
    b = b_ref[...]
    shifted = jnp.concatenate([a[GLA_LOWRANK:, :], b], axis=0)
    o_ref[...] = jnp.where(j >= C_GA // W_CHUNK, shifted, a).T.astype(BF16)

    @pl.when(j == C_GA // W_CHUNK - 1)
    def _():
        padded = jnp.concatenate([b, jnp.zeros((LANES - GLA_LOWRANK, D_MODEL), F32)], axis=0)
        olr_ref[...] = padded.T.astype(BF16)


def _wprep(wt):
    per = W_CHUNK // GLA_LOWRANK
    return pl.pallas_call(
        _wprep_body,
        out_shape=(jax.ShapeDtypeStruct((D_MODEL, C_LR), BF16), jax.ShapeDtypeStruct((D_MODEL, LANES), BF16)),
        grid=(C_LR // W_CHUNK,),
        in_specs=[pl.BlockSpec((W_CHUNK, D_MODEL), lambda j: (j, 0)),
                  pl.BlockSpec((GLA_LOWRANK, D_MODEL), lambda j: (per * (j + 1), 0))],
        out_specs=(pl.BlockSpec((D_MODEL, W_CHUNK), lambda j: (0, j)),
                   pl.BlockSpec((D_MODEL, LANES), lambda j: (0, 0))),
        compiler_params=pltpu.CompilerParams(dimension_semantics=("arbitrary",),
                                             vmem_limit_bytes=V7X_SCOPED_VMEM_BYTES),
        name="wprep",
    )(wt, wt)


def _inproj_body(xp_ref, xs_ref, ln_ref, w_ref, wlr_ref, o_ref, *, n_prompt_blocks):
    x = jnp.where(pl.program_id(0) < n_prompt_blocks, xp_ref[...], xs_ref[...])
    h = (_rms(x) * ln_ref[...]).astype(BF16)
    for c in range(C_LR // W_CHUNK):
        o_ref[:, c * W_CHUNK:(c + 1) * W_CHUNK] = _dot(h, w_ref[:, c * W_CHUNK:(c + 1) * W_CHUNK])
    o_ref[:, C_LR:] = _dot(h, wlr_ref[...])


def _inproj(xp, xs, ln, w, w_lr):
    nbp = xp.shape[0] // PROJ_TM
    n = xp.shape[0] + xs.shape[0]
    fixed = lambda i: (0, 0)
    return pl.pallas_call(
        functools.partial(_inproj_body, n_prompt_blocks=nbp),
        out_shape=jax.ShapeDtypeStruct((n, P_COLS), F32),
        grid=(n // PROJ_TM,),
        in_specs=[pl.BlockSpec((PROJ_TM, D_MODEL), lambda i: (jnp.minimum(i, nbp - 1), 0)),
                  pl.BlockSpec((PROJ_TM, D_MODEL), lambda i: (jnp.maximum(i - nbp, 0), 0)),
                  pl.BlockSpec((1, D_MODEL), fixed),
                  pl.BlockSpec((D_MODEL, C_LR), fixed, pipeline_mode=pl.Buffered(1)),
                  pl.BlockSpec((D_MODEL, LANES), fixed)],
        out_specs=pl.BlockSpec((PROJ_TM, P_COLS), lambda i: (i, 0)),
        compiler_params=pltpu.CompilerParams(dimension_semantics=("arbitrary",),
                                             vmem_limit_bytes=V7X_SCOPED_VMEM_BYTES),
        name="inproj",
    )(xp, xs, ln, w, w_lr)


def _seg_cumsum(x, seg):
    pos = lax.broadcasted_iota(I32, x.shape, 0) & (seg - 1)
    s = 1
    while s < seg:
        x = x + jnp.where(pos >= s, pltpu.roll(x, s, axis=0), 0.0)
        s *= 2
    return x


def _block_ends(p, m):
    rows, w = p.shape
    return jnp.concatenate(
        [jnp.broadcast_to(p[j * m + m - 1:j * m + m, :], (m, w)) for j in range(rows // m)], axis=0)


def _block_starts(p, m, chunk):
    rows, w = p.shape
    pieces = []
    for j in range(rows // m):
        if (j * m) % chunk == 0:
            pieces.append(jnp.zeros((m, w), F32))
        else:
            pieces.append(jnp.broadcast_to(p[j * m - 1:j * m, :], (m, w)))
    return jnp.concatenate(pieces, axis=0)


def _gla_body(*refs, chunk, nseq, has_init):
    if has_init:
        (q_ref, k_ref, v_ref, g_ref, lr_ref, wgk_ref, bgk_ref, nw_ref, s0_ref,
         o_ref, sout_ref, s_scr) = refs
    else:
        (q_ref, k_ref, v_ref, g_ref, lr_ref, wgk_ref, bgk_ref, nw_ref,
         o_ref, sout_ref, s_scr) = refs
    t = pl.program_id(1)

    @pl.when(t == 0)
    def _():
        if has_init:
            s_scr[...] = s0_ref[...]
        else:
            s_scr[...] = jnp.zeros_like(s_scr)

    x = _dot_split(lr_ref[...], wgk_ref[0], wgk_ref[1]) + bgk_ref[...]
    log_g = -(jnp.maximum(-x, 0.0) + jnp.log1p(jnp.exp(-jnp.abs(x)))) / GLA_GATE_NORM
    p = _seg_cumsum(log_g, chunk)
    p_end = _block_ends(p, chunk)
    eq_c = jnp.exp(p)
    ek_c = jnp.exp(p_end - p)
    sub = min(GLA_SUBCHUNK, chunk)
    p_sub = p - _block_starts(p, sub, chunk)
    eq_d = jnp.exp(p_sub)
    ek_d = jnp.exp(-p_sub)
    levels = []
    m = sub
    while m < chunk:
        levels.append((m, jnp.exp(p - _block_starts(p, m, chunk)), jnp.exp(_block_ends(p, m) - p)))
        m *= 2

    ti = lax.broadcasted_iota(I32, (chunk, chunk), 0)
    si = lax.broadcasted_iota(I32, (chunk, chunk), 1)
    blk = lambda idx, size: idx >> (size.bit_length() - 1)
    mask_d = (blk(ti, sub) == blk(si, sub)) & (si <= ti)
    masks = [((blk(ti, m) & 1) == 1) & (blk(si, m) == blk(ti, m) - 1) for (m, _, _) in levels]
    sel8 = (lax.broadcasted_iota(I32, (8, DV_GLA), 0) == 0).astype(BF16)
    nw = nw_ref[...]

    outs = [[None] * nseq for _ in range(H_GLA)]
    for b in range(nseq):
        r0 = b * chunk
        rows = slice(r0, r0 + chunk)
        for h in range(H_GLA):
            kc = slice(h * DK_GLA, (h + 1) * DK_GLA)
            vc = slice(h * DV_GLA, (h + 1) * DV_GLA)
            q = q_ref[rows, kc] * (DK_GLA ** -0.5)
            k = k_ref[rows, kc]
            v = v_ref[rows, vc].astype(BF16)
            att = jnp.where(mask_d, _dot_nt((q * eq_d[rows, kc]).astype(BF16),
                                            (k * ek_d[rows, kc]).astype(BF16)), 0.0)
            for (m, eq_m, ek_m), mask in zip(levels, masks):
                att = jnp.where(mask, _dot_nt((q * eq_m[rows, kc]).astype(BF16),
                                              (k * ek_m[rows, kc]).astype(BF16)), att)
            s_old = s_scr[b, h]
            o = _dot(att.astype(BF16), v) + _dot((q * eq_c[rows, kc]).astype(BF16), s_old.astype(BF16))
            kv = _dot_tn((k * ek_c[rows, kc]).astype(BF16), v)
            tot = jnp.broadcast_to(p[r0 + chunk - 1:r0 + chunk, kc], (8, DK_GLA))
            t_hi, t_mid, t_lo = _split_bf16(tot, 3)
            decay = jnp.exp(_dot_tn(t_hi, sel8) + (_dot_tn(t_mid, sel8) + _dot_tn(t_lo, sel8)))
            s_scr[b, h] = decay * s_old + kv
            outs[h][b] = _rms(o) * nw * _silu(g_ref[rows, vc])
    for h in range(H_GLA):
        o_ref[:, h * DV_GLA:(h + 1) * DV_GLA] = jnp.concatenate(outs[h], axis=0).astype(BF16)

    def emit_state():
        @pl.when(t == pl.num_programs(1) - 1)
        def _():
            sout_ref[...] = s_scr[...]
    return emit_state


def _ret_body(q_ref, k_ref, v_ref, g_ref, cos_ref, sin_ref, o_ref, sout_ref, s_scr, *, chunk, nseq):
    t = pl.program_id(1)

    half = DK_RET // 2
    ti = lax.broadcasted_iota(I32, (chunk, chunk), 0)
    si = lax.broadcasted_iota(I32, (chunk, chunk), 1)
    diff = (ti - si).astype(F32)
    idx = lax.broadcasted_iota(I32, (chunk, 1), 0).astype(F32)

    outs = [[None] * nseq for _ in range(H_RET)]
    for h in range(H_RET):
        lg = math.log(1.0 - 2.0 ** (-5.0 - h))
        dmat = jnp.where(diff >= 0, jnp.exp(lg * jnp.maximum(diff, 0.0)), 0.0)
        q_dec = jnp.exp(lg * (idx + 1.0))
        k_dec = jnp.exp(lg * (chunk - 1.0 - idx))
        c_dec = math.exp(lg * chunk)
        for b in range(nseq):
            rows = slice(b * chunk, (b + 1) * chunk)
            cos = cos_ref[rows, :]
            sin = sin_ref[rows, :]
            c1 = slice(h * DK_RET, h * DK_RET + half)
            c2 = slice(h * DK_RET + half, (h + 1) * DK_RET)
            vc = slice(h * DV_RET, (h + 1) * DV_RET)
            q1, q2 = q_ref[rows, c1], q_ref[rows, c2]
            k1, k2 = k_ref[rows, c1], k_ref[rows, c2]
            q = jnp.concatenate([q1 * cos - q2 * sin, q1 * sin + q2 * cos], axis=-1)
            k = jnp.concatenate([k1 * cos - k2 * sin, k1 * sin + k2 * cos], axis=-1) * (DK_RET ** -0.5)
            v = v_ref[rows, vc].astype(BF16)
            qb = q.astype(BF16)
            s_old = s_scr[b, h]
            att = _dot_nt(qb, k.astype(BF16)) * dmat
            o = _dot(att.astype(BF16), v) + _dot(qb, s_old.astype(BF16)) * q_dec
            s_scr[b, h] = c_dec * s_old + _dot_tn((k * k_dec).astype(BF16), v)
            outs[h][b] = _rms(o) * _silu(g_ref[rows, vc])
    for h in range(H_RET):
        o_ref[:, h * DV_RET:(h + 1) * DV_RET] = jnp.concatenate(outs[h], axis=0).astype(BF16)

    @pl.when(t == pl.num_programs(1) - 1)
    def _():
        sout_ref[...] = s_scr[...]


def _gla_ret_body(*refs, chunk, nseq, has_init):
    n_gla, n_ret = 8 + has_init, 6 + has_init
    gla_in, ret_in = refs[:n_gla], refs[n_gla:n_gla + n_ret]
    oa_ref, sg_ref, or_ref, sr_ref, sg_scr, sr_scr = refs[n_gla + n_ret:]

    @pl.when(pl.program_id(1) == 0)
    def _():
        sr_scr[...] = ret_in[-1][...] if has_init else jnp.zeros_like(sr_scr)
    emit_gla_state = _gla_body(*gla_in, oa_ref, sg_ref, sg_scr, chunk=chunk, nseq=nseq, has_init=has_init)
    _ret_body(*ret_in[:6], or_ref, sr_ref, sr_scr, chunk=chunk, nseq=nseq)
    emit_gla_state()


def _gla_ret(proj, w_gk2p, b_gk, nw, cos, sin, s0_gla, s0_ret, *, row0, n_seq, seq_len, chunk, nseq):
    has_init = s0_gla is not None
    rows = nseq * chunk
    nt = seq_len // chunk
    base = row0 // rows
    rmap = lambda col: (lambda b, t: (base + b * nt + t, col))
    omap = lambda b, t: (b * nt + t, 0)
    fixed2 = lambda b, t: (0, 0)
    smap = lambda b, t: (b, 0, 0, 0)
    tab_spec = pl.BlockSpec((rows, DK_RET // 2), lambda b, t: (t, 0))
    gla_state = pl.BlockSpec((nseq, H_GLA, DK_GLA, DV_GLA), smap)
    ret_state = pl.BlockSpec((nseq, H_RET, DK_RET, DV_RET), smap)
    gla_specs = [pl.BlockSpec((rows, GLA_QK), rmap(C_QA // GLA_QK)),
                 pl.BlockSpec((rows, GLA_QK), rmap(C_KA // GLA_QK)),
                 pl.BlockSpec((rows, GLA_V), rmap(C_VA // GLA_V)),
                 pl.BlockSpec((rows, GLA_V), rmap(C_GA // GLA_V)),
                 pl.BlockSpec((rows, LANES), rmap(C_LR // LANES)),
                 pl.BlockSpec((2, LANES, GLA_QK), lambda b, t: (0, 0, 0)),
                 pl.BlockSpec((1, GLA_QK), fixed2),
                 pl.BlockSpec((1, DV_GLA), fixed2)]
    ret_specs = [pl.BlockSpec((rows, RET_QK), rmap(C_QR // RET_QK)),
                 pl.BlockSpec((rows, RET_QK), rmap(C_KR // RET_QK)),
                 pl.BlockSpec((rows, RET_V), rmap(C_VR // RET_V)),
                 pl.BlockSpec((rows, RET_V), rmap(C_GR // RET_V)),
                 tab_spec, tab_spec]
    gla_args = [proj, proj, proj, proj, proj, w_gk2p, b_gk, nw]
    ret_args = [proj, proj, proj, proj, cos, sin]
    if has_init:
        gla_specs.append(gla_state)
        gla_args.append(s0_gla)
        ret_specs.append(ret_state)
        ret_args.append(s0_ret)
    return pl.pallas_call(
        functools.partial(_gla_ret_body, chunk=chunk, nseq=nseq, has_init=has_init),
        out_shape=(jax.ShapeDtypeStruct((n_seq * seq_len, GLA_V), BF16),
                   jax.ShapeDtypeStruct((n_seq, H_GLA, DK_GLA, DV_GLA), F32),
                   jax.ShapeDtypeStruct((n_seq * seq_len, RET_V), BF16),
                   jax.ShapeDtypeStruct((n_seq, H_RET, DK_RET, DV_RET), F32)),
        grid=(n_seq // nseq, nt),
        in_specs=gla_specs + ret_specs,
        out_specs=(pl.BlockSpec((rows, GLA_V), omap), gla_state, pl.BlockSpec((rows, RET_V), omap), ret_state),
        scratch_shapes=[pltpu.VMEM((nseq, H_GLA, DK_GLA, DV_GLA), F32),
                        pltpu.VMEM((nseq, H_RET, DK_RET, DV_RET), F32)],
        compiler_params=pltpu.CompilerParams(dimension_semantics=("arbitrary", "arbitrary"),
                                             vmem_limit_bytes=V7X_SCOPED_VMEM_BYTES),
        name="gla_ret_init" if has_init else "gla_ret",
    )(*gla_args, *ret_args)


def _rope_tables(pos, reps):
    half = DK_RET // 2
    inv = ROPE_BASE ** (-jnp.arange(half, dtype=F32) / half)
    ang = pos[:, None] * inv[None, :]
    return jnp.tile(jnp.cos(ang), (reps, 1)), jnp.tile(jnp.sin(ang), (reps, 1))


def _outproj_body(oap_ref, oas_ref, orp_ref, ors_ref, xp_ref, xs_ref, w_ref, ln_ref, wrt_ref, brt_ref,
                  x1_ref, hn_ref, route_ref, counts_ref, *, n_prompt_blocks):
    step = pl.program_id(0)
    is_prompt = step < n_prompt_blocks
    oa = jnp.where(is_prompt, oap_ref[...], oas_ref[...])
    orr = jnp.where(is_prompt, orp_ref[...], ors_ref[...])
    x = jnp.where(is_prompt, xp_ref[...], xs_ref[...])
    x1 = x + _dot(oa, w_ref[:GLA_V, :]) + _dot(orr, w_ref[GLA_V:, :])
    x1_ref[...] = x1
    hn = _rms(x1) * ln_ref[...]
    half = D_MODEL // 2
    as_bits = lambda v: lax.bitcast_convert_type(v.astype(BF16).astype(F32), I32)
    hn_ref[...] = as_bits(hn[:, :half]) | lax.shift_right_logical(as_bits(hn[:, half:]), 16)

    logits = _dot_split(hn, wrt_ref[0], wrt_ref[1]) + brt_ref[...]
    lane_i = lax.broadcasted_iota(I32, logits.shape, 1)
    lane = lane_i.astype(F32)
    grp = (lane_i >> (EXPERTS_PER_GROUP.bit_length() - 1)).astype(F32)
    neg = -jnp.inf
    far = float(LANES)

    gl = jnp.where((lane_i >= N_EXPERTS) & (lane_i < N_EXPERTS + N_GROUPS), logits, neg)
    gmax = jnp.max(gl, axis=-1, keepdims=True)
    g_w = 1.0 / jnp.sum(jnp.exp(gl - gmax), axis=-1, keepdims=True)
    g_idx = jnp.min(jnp.where(gl == gmax, lane, far), axis=-1, keepdims=True) - float(N_EXPERTS)

    el = jnp.where((lane_i < N_EXPERTS) & (grp == g_idx), logits, neg)
    m1 = jnp.max(el, axis=-1, keepdims=True)
    esum = jnp.sum(jnp.exp(el - m1), axis=-1, keepdims=True)
    i1 = jnp.min(jnp.where(el == m1, lane, far), axis=-1, keepdims=True)
    el2 = jnp.where(lane == i1, neg, el)
    m2 = jnp.max(el2, axis=-1, keepdims=True)
    i2 = jnp.min(jnp.where(el2 == m2, lane, far), axis=-1, keepdims=True)
    p1 = 1.0 / esum
    p2 = jnp.exp(m2 - m1) / esum
    gate1 = g_w * (p1 / (p1 + p2))
    gate2 = g_w * (p2 / (p1 + p2))
    @pl.when(step == 0)
    def _():
        counts_ref[...] = jnp.zeros_like(counts_ref)

    tm = logits.shape[0]
    earlier = (lax.broadcasted_iota(I32, (tm, tm), 0) > lax.broadcasted_iota(I32, (tm, tm), 1)).astype(BF16)
    running = counts_ref[0:1, :]
    ranks = []
    for idx in (i1, i2):
        onehot = (lane == idx).astype(F32)
        before = _dot(earlier, onehot.astype(BF16)) + running
        ranks.append(jnp.sum(onehot * before, axis=-1, keepdims=True))
        running = running + jnp.sum(onehot, axis=0, keepdims=True)
    counts_ref[...] = jnp.broadcast_to(running, counts_ref.shape)

    route_ref[...] = jnp.where(lane_i == 0, i1,
                               jnp.where(lane_i == 1, i2,
                                         jnp.where(lane_i == 2, gate1,
                                                   jnp.where(lane_i == 3, gate2,
                                                             jnp.where(lane_i == 4, ranks[0],
                                                                       jnp.where(lane_i == 5, ranks[1], 0.0))))))


def _outproj_router(oa_p, oa_s, or_p, or_s, xp, xs, w_out, ln, w_rt, b_rt):
    n = xp.shape[0] + xs.shape[0]
    nbp = xp.shape[0] // PROJ_TM
    pmap = lambda i: (jnp.minimum(i, nbp - 1), 0)
    smap = lambda i: (jnp.maximum(i - nbp, 0), 0)
    row = lambda i: (i, 0)
    fixed = lambda i: (0, 0)
    return pl.pallas_call(
        functools.partial(_outproj_body, n_prompt_blocks=nbp),
        out_shape=(jax.ShapeDtypeStruct((n, D_MODEL), F32),
                   jax.ShapeDtypeStruct((n, D_MODEL // 2), I32),
                   jax.ShapeDtypeStruct((n, LANES), F32),
                   jax.ShapeDtypeStruct((SUBLANES, LANES), F32)),
        grid=(n // PROJ_TM,),
        in_specs=[pl.BlockSpec((PROJ_TM, GLA_V), pmap), pl.BlockSpec((PROJ_TM, GLA_V), smap),
                  pl.BlockSpec((PROJ_TM, RET_V), pmap), pl.BlockSpec((PROJ_TM, RET_V), smap),
                  pl.BlockSpec((PROJ_TM, D_MODEL), pmap), pl.BlockSpec((PROJ_TM, D_MODEL), smap),
                  pl.BlockSpec((GLA_V + RET_V, D_MODEL), fixed, pipeline_mode=pl.Buffered(1)),
                  pl.BlockSpec((1, D_MODEL), fixed),
                  pl.BlockSpec((2, D_MODEL, LANES), lambda i: (0, 0, 0)),
                  pl.BlockSpec((1, LANES), fixed)],
        out_specs=(pl.BlockSpec((PROJ_TM, D_MODEL), row), pl.BlockSpec((PROJ_TM, D_MODEL // 2), row),
                   pl.BlockSpec((PROJ_TM, LANES), row), pl.BlockSpec((SUBLANES, LANES), fixed)),
        compiler_params=pltpu.CompilerParams(dimension_semantics=("arbitrary",),
                                             vmem_limit_bytes=V7X_SCOPED_VMEM_BYTES),
        name="outproj_router",
    )(oa_p, oa_s, or_p, or_s, xp, xs, w_out, ln, w_rt, b_rt)


def _moe_body(texp_ref, nval_ref, pos_ref, hn_hbm, wg_ref, wu_ref, wd_ref, y_hbm,
              inv, xbuf, ybuf, wg_b, wu_b, wd_b, gsem, ssem, *, n_tokens):
    tm = MOE_TM
    i = pl.program_id(0)
    nt = pl.num_programs(0)
    slot = i % 2

    def for_rows(n_rows, fn):
        n_grp = n_rows // SUBLANES

        def group(g, c):
            for j in range(SUBLANES):
                fn(g * SUBLANES + j)
            return c
        lax.fori_loop(0, n_grp, group, 0)

        def single(r, c):
            fn(r)
            return c
        lax.fori_loop(n_grp * SUBLANES, n_rows, single, 0)

    def gather_copy(tok, r, slot_):
        return pltpu.make_async_copy(hn_hbm.at[pl.ds(tok, 1), :], xbuf.at[slot_, pl.ds(r, 1), :], gsem.at[slot_])

    def scatter_copy(r, dst, slot_):
        return pltpu.make_async_copy(ybuf.at[slot_, pl.ds(r, 1), :], y_hbm.at[pl.ds(dst, 1), :], ssem.at[slot_])

    def start_gather(tile, slot_):
        for_rows(nval_ref[tile], lambda r: gather_copy(inv[tile * tm + r] >> 1, r, slot_).start())

    def wait_rows(n_rows, group_copy, row_copy):
        n_grp = pl.multiple_of((n_rows // SUBLANES) * SUBLANES, SUBLANES)

        @pl.when(n_grp > 0)
        def _():
            group_copy(n_grp).wait()

        def single(r, c):
            row_copy().wait()
            return c
        lax.fori_loop(n_grp, n_rows, single, 0)

    def wait_gather(slot_, n_rows):
        wait_rows(n_rows,
                  lambda m: pltpu.make_async_copy(hn_hbm.at[pl.ds(0, m), :], xbuf.at[slot_, pl.ds(0, m), :],
                                                  gsem.at[slot_]),
                  lambda: gather_copy(0, 0, slot_))

    def wait_scatter(slot_, n_rows):
        wait_rows(n_rows,
                  lambda m: pltpu.make_async_copy(ybuf.at[slot_, pl.ds(0, m), :], y_hbm.at[pl.ds(0, m), :],
                                                  ssem.at[slot_]),
                  lambda: scatter_copy(0, 0, slot_))

    @pl.when(i == 0)
    def _():
        def body(a, c):
            inv[pos_ref[a]] = a
            return c
        lax.fori_loop(0, 2 * n_tokens, body, 0, unroll=8)
        xbuf[...] = jnp.zeros_like(xbuf)
        start_gather(0, 0)

    @pl.when(i + 1 < nt)
    def _():
        start_gather(jnp.minimum(i + 1, nt - 1), 1 - slot)

    @pl.when(i >= 2)
    def _():
        wait_scatter(slot, nval_ref[jnp.maximum(i - 2, 0)])

    nv = nval_ref[i]
    wait_gather(slot, nv)

    @pl.when(nv > 0)
    def _():
        @pl.when((i == 0) | (texp_ref[i] != texp_ref[jnp.maximum(i - 1, 0)]))
        def _():
            wg_b[...] = wg_ref[0].astype(BF16)
            wu_b[...] = wu_ref[0].astype(BF16)
            wd_b[...] = wd_ref[0].astype(BF16)

        bits = xbuf[slot]
        as_bf16 = lambda b: lax.bitcast_convert_type(b, F32).astype(BF16)
        h = jnp.concatenate([as_bf16(bits & -(1 << 16)), as_bf16(bits << 16)], axis=1)
        hid = _silu(_dot(h, wg_b[...])) * _dot(h, wu_b[...])
        ybuf[slot] = _dot(hid.astype(BF16), wd_b[...])

        def scatter_one(r):
            a = inv[i * tm + r]
            scatter_copy(r, (a & 1) * n_tokens + (a >> 1), slot).start()
        for_rows(nv, scatter_one)

    @pl.when(i == nt - 1)
    def _():
        @pl.when(i >= 1)
        def _():
            wait_scatter(1 - slot, nval_ref[jnp.maximum(i - 1, 0)])
        wait_scatter(slot, nv)


def _moe(tile_expert, n_valid, pos, hn, w_gate, w_up, w_down):
    n = hn.shape[0]
    n_tiles = tile_expert.shape[0]
    wmap = lambda i, te, nv, pos_: (te[i], 0, 0)
    return pl.pallas_call(
        functools.partial(_moe_body, n_tokens=n),
        out_shape=jax.ShapeDtypeStruct((2 * n, D_MODEL), F32),
        grid_spec=pltpu.PrefetchScalarGridSpec(
            num_scalar_prefetch=3,
            grid=(n_tiles,),
            in_specs=[pl.BlockSpec(memory_space=pl.ANY),
                      pl.BlockSpec((1, D_MODEL, D_EXPERT), wmap),
                      pl.BlockSpec((1, D_MODEL, D_EXPERT), wmap),
                      pl.BlockSpec((1, D_EXPERT, D_MODEL), wmap)],
            out_specs=pl.BlockSpec(memory_space=pl.ANY),
            scratch_shapes=[pltpu.SMEM((n_tiles * MOE_TM,), I32),
                            pltpu.VMEM((2, MOE_TM, D_MODEL // 2), I32),
                            pltpu.VMEM((2, MOE_TM, D_MODEL), F32),
                            pltpu.VMEM((D_MODEL, D_EXPERT), BF16),
                            pltpu.VMEM((D_MODEL, D_EXPERT), BF16),
                            pltpu.VMEM((D_EXPERT, D_MODEL), BF16),
                            pltpu.SemaphoreType.DMA((2,)),
                            pltpu.SemaphoreType.DMA((2,))]),
        compiler_params=pltpu.CompilerParams(dimension_semantics=("arbitrary",),
                                             vmem_limit_bytes=V7X_SCOPED_VMEM_BYTES),
        name="moe",
    )(tile_expert, n_valid, pos, hn, w_gate, w_up, w_down)


def _routing_tables(route, counts, n):
    tm = MOE_TM
    n_tiles = 2 * n // tm + N_EXPERTS
    experts = route[:, 0:2].astype(I32)
    rank = route[:, 4:6].astype(I32)
    counts = counts[0, :N_EXPERTS].astype(I32)
    tiles = (counts + tm - 1) // tm
    tile_end = jnp.cumsum(tiles)
    tile_start = tile_end - tiles
    onehot = experts[:, :, None] == jnp.arange(N_EXPERTS, dtype=I32)[None, None, :]
    pos = jnp.sum(jnp.where(onehot, (tile_start * tm)[None, None, :], 0), axis=-1) + rank
    tidx = jnp.arange(n_tiles, dtype=I32)
    t_exp = jnp.sum((tidx[:, None] >= tile_end[None, :]).astype(I32), axis=1)
    used = tidx < tile_end[-1]
    last_exp = jnp.sum((tile_end[-1] - 1 >= tile_end).astype(I32))
    t_exp = jnp.where(used, t_exp, last_exp)
    n_valid = jnp.where(used, jnp.clip(counts[t_exp] - (tidx - tile_start[t_exp]) * tm, 0, tm), 0)
    return t_exp.astype(I32), n_valid.astype(I32), pos.reshape(2 * n)


def _final_body(x1_ref, y0_ref, y1_ref, route_ref, ln_ref, o_ref):
    z = x1_ref[...] + route_ref[:, 2:3] * y0_ref[...] + route_ref[:, 3:4] * y1_ref[...]
    o_ref[...] = _rms(z) * ln_ref[...]


def _final(x1, y2, route, ln, *, row0, rows):
    base = row0 // PROJ_TM
    slot1 = x1.shape[0] // PROJ_TM
    return pl.pallas_call(
        _final_body,
        out_shape=jax.ShapeDtypeStruct((rows, D_MODEL), F32),
        grid=(rows // PROJ_TM,),
        in_specs=[pl.BlockSpec((PROJ_TM, D_MODEL), lambda i: (base + i, 0)),
                  pl.BlockSpec((PROJ_TM, D_MODEL), lambda i: (base + i, 0)),
                  pl.BlockSpec((PROJ_TM, D_MODEL), lambda i: (slot1 + base + i, 0)),
                  pl.BlockSpec((PROJ_TM, LANES), lambda i: (base + i, 0)),
                  pl.BlockSpec((1, D_MODEL), lambda i: (0, 0))],
        out_specs=pl.BlockSpec((PROJ_TM, D_MODEL), lambda i: (i, 0)),
        compiler_params=pltpu.CompilerParams(dimension_semantics=("arbitrary",)),
        name="final_norm",
    )(x1, y2, y2, route, ln)


def kernel(x_prompt, x_sample, state_gla, state_ret, ln_attn, w_in, w_gk2, b_gk, gla_norm_w, w_out, ln_ffn, w_router_group, b_router_group, w_router_expert, b_router_expert, w_exp_gate, w_exp_up, w_exp_down, ln_final):
    bp, tp, d = x_prompt.shape
    bs, ts, _ = x_sample.shape
    assert d == D_MODEL and w_in.shape == (1, D_MODEL, IN_COLS_SRC)
    n_p, n_s = bp * tp, bs * ts
    n = n_p + n_s
    assert n_p % PROJ_TM == 0 and n_s % PROJ_TM == 0 and tp % ATTN_CHUNK == 0

    xp = x_prompt.reshape(n_p, d)
    xs = x_sample.reshape(n_s, d)

    hi_lo = lambda a: jnp.stack(_split_bf16(a, 2))
    w_in_p, w_lr = _wprep(w_in[0].T)
    w_gk2p = hi_lo(jnp.concatenate([w_gk2[0], jnp.zeros((LANES - GLA_LOWRANK, GLA_QK), F32)], axis=0))
    w_rt = hi_lo(jnp.concatenate([w_router_expert[0], w_router_group[0],
                                  jnp.zeros((d, LANES - N_EXPERTS - N_GROUPS), F32)], axis=1))
    b_rt = jnp.concatenate([b_router_expert[0], b_router_group[0],
                            jnp.zeros((LANES - N_EXPERTS - N_GROUPS,), F32)])[None, :]

    proj = _inproj(xp, xs, ln_attn, w_in_p, w_lr)

    cos_p, sin_p = _rope_tables(jnp.arange(tp, dtype=F32), 1)
    cos_s, sin_s = _rope_tables(jnp.arange(ts, dtype=F32) + float(PAST_LEN), SAMPLE_SEQS_PER_STEP)
    oa_p, sg_p, or_p, sr_p = _gla_ret(proj, w_gk2p, b_gk, gla_norm_w, cos_p, sin_p, None, None,
                                      row0=0, n_seq=bp, seq_len=tp, chunk=ATTN_CHUNK, nseq=1)
    oa_s, sg_s, or_s, sr_s = _gla_ret(proj, w_gk2p, b_gk, gla_norm_w, cos_s, sin_s, state_gla[0], state_ret[0],
                                      row0=n_p, n_seq=bs, seq_len=ts, chunk=ts, nseq=SAMPLE_SEQS_PER_STEP)

    x1, hn, route, counts = _outproj_router(oa_p, oa_s, or_p, or_s, xp, xs, w_out[0].astype(BF16),
                                            ln_ffn, w_rt, b_rt)

    t_exp, n_valid, pos = _routing_tables(route, counts, n)
    y2 = _moe(t_exp, n_valid, pos, hn, w_exp_gate[0], w_exp_up[0], w_exp_down[0])

    ln_f = ln_final[None, :]
    y_p = _final(x1, y2, route, ln_f, row0=0, rows=n_p).reshape(bp, tp, d)
    y_s = _final(x1, y2, route, ln_f, row0=n_p, rows=n_s).reshape(bs, ts, d)
    return (y_p, y_s, sg_p[None], sr_p[None], sg_s[None], sr_s[None])
```

```python
import functools
import math

import jax
import jax.numpy as jnp
from jax import lax
from jax.experimental import pallas as pl
from jax.experimental.pallas import tpu as pltpu

F32 = jnp.float32
BF16 = jnp.bfloat16
I32 = jnp.int32

D_MODEL = 2048
PAST_LEN = 16384
H_GLA, DK_GLA, DV_GLA = 4, 128, 256
GLA_LOWRANK = 16
GLA_GATE_NORM = 16.0
GLA_SUBCHUNK = 16
H_RET, DK_RET, DV_RET = 4, 256, 256
ROPE_BASE = 10000.0
N_GROUPS, EXPERTS_PER_GROUP, D_EXPERT = 4, 8, 512
N_EXPERTS = N_GROUPS * EXPERTS_PER_GROUP
EPS = 1e-6

LANES = 128
SUBLANES = 8
V7X_SCOPED_VMEM_BYTES = 56 * 1024 * 1024

GLA_QK = H_GLA * DK_GLA
GLA_V = H_GLA * DV_GLA
RET_QK = H_RET * DK_RET
RET_V = H_RET * DV_RET
C_QA, C_KA, C_VA, C_GA = 0, GLA_QK, 2 * GLA_QK, 2 * GLA_QK + GLA_V
C_QR = C_GA + GLA_V
C_KR, C_VR, C_GR = C_QR + RET_QK, C_QR + 2 * RET_QK, C_QR + 2 * RET_QK + RET_V
C_LR = C_GR + RET_V
P_COLS = C_LR + LANES
IN_COLS_SRC = 2 * GLA_QK + GLA_V + GLA_LOWRANK + GLA_V + 2 * RET_QK + 2 * RET_V

PROJ_TM = 256
W_CHUNK = 1024
MOE_TM = 256
ATTN_CHUNK = 128
SAMPLE_SEQS_PER_STEP = 4

def _dot(a, b):
    return jnp.dot(a, b, preferred_element_type=F32)


def _dot_nt(a, b):
    return lax.dot_general(a, b, (((1,), (1,)), ((), ())), preferred_element_type=F32)


def _dot_tn(a, b):
    return lax.dot_general(a, b, (((0,), (0,)), ((), ())), preferred_element_type=F32)


def _split_bf16(x, parts):
    out = []
    for _ in range(parts):
        p = x.astype(BF16)
        out.append(p)
        x = x - p.astype(F32)
    return out


def _dot_split(a, b_hi, b_lo):
    a_hi, a_lo = _split_bf16(a, 2)
    return _dot(a_hi, b_hi) + (_dot(a_lo, b_hi) + _dot(a_hi, b_lo))


def _rms(x):
    return x * lax.rsqrt(jnp.mean(x * x, axis=-1, keepdims=True) + EPS)


def _silu(x):
    return x * jax.nn.sigmoid(x)


def _wprep_body(a_ref, b_ref, o_ref, olr_ref):
    j = pl.program_id(0)
    a = a_ref[...]
    b = b_ref[...]
    shifted = jnp.concatenate([a[GLA_LOWRANK:, :], b], axis=0)
    o_ref[...] = jnp.where(j >= C_GA // W_CHUNK, shifted, a).T.astype(BF16)

    @pl.when(j == C_GA // W_CHUNK - 1)
    def _():
        padded = jnp.concatenate([b, jnp.zeros((LANES - GLA_LOWRANK, D_MODEL), F32)], axis=0)
        olr_ref[...] = padded.T.astype(BF16)


def _wprep(wt):
    per = W_CHUNK // GLA_LOWRANK
    return pl.pallas_call(
        _wprep_body,
        out_shape=(jax.ShapeDtypeStruct((D_MODEL, C_LR), BF16), jax.ShapeDtypeStruct((D_MODEL, LANES), BF16)),
        grid=(C_LR // W_CHUNK,),
        in_specs=[pl.BlockSpec((W_CHUNK, D_MODEL), lambda j: (j, 0)),
                  pl.BlockSpec((GLA_LOWRANK, D_MODEL), lambda j: (per * (j + 1), 0))],
        out_specs=(pl.BlockSpec((D_MODEL, W_CHUNK), lambda j: (0, j)),
                   pl.BlockSpec((D_MODEL, LANES), lambda j: (0, 0))),
        compiler_params=pltpu.CompilerParams(dimension_semantics=("arbitrary",),
                                             vmem_limit_bytes=V7X_SCOPED_VMEM_BYTES),
        name="wprep",
    )(wt, wt)


def _inproj_body(xp_ref, xs_ref, ln_ref, w_ref, wlr_ref, o_ref, *, n_prompt_blocks):
    x = jnp.where(pl.program_id(0) < n_prompt_blocks, xp_ref[...], xs_ref[...])
    h = (_rms(x) * ln_ref[...]).astype(BF16)
    for c in range(C_LR // W_CHUNK):
        o_ref[:, c * W_CHUNK:(c + 1) * W_CHUNK] = _dot(h, w_ref[:, c * W_CHUNK:(c + 1) * W_CHUNK])
    o_ref[:, C_LR:] = _dot(h, wlr_ref[...])


def _inproj(xp, xs, ln, w, w_lr):
    nbp = xp.shape[0] // PROJ_TM
    n = xp.shape[0] + xs.shape[0]
    fixed = lambda i: (0, 0)
    return pl.pallas_call(
        functools.partial(_inproj_body, n_prompt_blocks=nbp),
        out_shape=jax.ShapeDtypeStruct((n, P_COLS), F32),
        grid=(n // PROJ_TM,),
        in_specs=[pl.BlockSpec((PROJ_TM, D_MODEL), lambda i: (jnp.minimum(i, nbp - 1), 0)),
                  pl.BlockSpec((PROJ_TM, D_MODEL), lambda i: (jnp.maximum(i - nbp, 0), 0)),
                  pl.BlockSpec((1, D_MODEL), fixed),
                  pl.BlockSpec((D_MODEL, C_LR), fixed, pipeline_mode=pl.Buffered(1)),
                  pl.BlockSpec((D_MODEL, LANES), fixed)],
        out_specs=pl.BlockSpec((PROJ_TM, P_COLS), lambda i: (i, 0)),
        compiler_params=pltpu.CompilerParams(dimension_semantics=("arbitrary",),
                                             vmem_limit_bytes=V7X_SCOPED_VMEM_BYTES),
        name="inproj",
    )(xp, xs, ln, w, w_lr)


def _seg_cumsum(x, seg):
    pos = lax.broadcasted_iota(I32, x.shape, 0) & (seg - 1)
    s = 1
    while s < seg:
        x = x + jnp.where(pos >= s, pltpu.roll(x, s, axis=0), 0.0)
        s *= 2
    return x


def _block_ends(p, m):
    rows, w = p.shape
    return jnp.concatenate(
        [jnp.broadcast_to(p[j * m + m - 1:j * m + m, :], (m, w)) for j in range(rows // m)], axis=0)


def _block_starts(p, m, chunk):
    rows, w = p.shape
    pieces = []
    for j in range(rows // m):
        if (j * m) % chunk == 0:
            pieces.append(jnp.zeros((m, w), F32))
        else:
            pieces.append(jnp.broadcast_to(p[j * m - 1:j * m, :], (m, w)))
    return jnp.concatenate(pieces, axis=0)


def _gla_body(*refs, chunk, nseq, has_init):
    if has_init:
        (q_ref, k_ref, v_ref, g_ref, lr_ref, wgk_ref, bgk_ref, nw_ref, s0_ref,
         o_ref, sout_ref, s_scr) = refs
    else:
        (q_ref, k_ref, v_ref, g_ref, lr_ref, wgk_ref, bgk_ref, nw_ref,
         o_ref, sout_ref, s_scr) = refs
    t = pl.program_id(1)

    @pl.when(t == 0)
    def _():
        if has_init:
            s_scr[...] = s0_ref[...]
        else:
            s_scr[...] = jnp.zeros_like(s_scr)

    x = _dot_split(lr_ref[...], wgk_ref[0], wgk_ref[1]) + bgk_ref[...]
    log_g = -(jnp.maximum(-x, 0.0) + jnp.log1p(jnp.exp(-jnp.abs(x)))) / GLA_GATE_NORM
    p = _seg_cumsum(log_g, chunk)
    p_end = _block_ends(p, chunk)
    eq_c = jnp.exp(p)
    ek_c = jnp.exp(p_end - p)
    sub = min(GLA_SUBCHUNK, chunk)
    p_sub = p - _block_starts(p, sub, chunk)
    eq_d = jnp.exp(p_sub)
    ek_d = jnp.exp(-p_sub)
    levels = []
    m = sub
    while m < chunk:
        levels.append((m, jnp.exp(p - _block_starts(p, m, chunk)), jnp.exp(_block_ends(p, m) - p)))
        m *= 2

    ti = lax.broadcasted_iota(I32, (chunk, chunk), 0)
    si = lax.broadcasted_iota(I32, (chunk, chunk), 1)
    blk = lambda idx, size: idx >> (size.bit_length() - 1)
    mask_d = (blk(ti, sub) == blk(si, sub)) & (si <= ti)
    masks = [((blk(ti, m) & 1) == 1) & (blk(si, m) == blk(ti, m) - 1) for (m, _, _) in levels]
    sel8 = (lax.broadcasted_iota(I32, (8, DV_GLA), 0) == 0).astype(BF16)
    nw = nw_ref[...]

    outs = [[None] * nseq for _ in range(H_GLA)]
    for b in range(nseq):
        r0 = b * chunk
        rows = slice(r0, r0 + chunk)
        for h in range(H_GLA):
            kc = slice(h * DK_GLA, (h + 1) * DK_GLA)
            vc = slice(h * DV_GLA, (h + 1) * DV_GLA)
            q = q_ref[rows, kc] * (DK_GLA ** -0.5)
            k = k_ref[rows, kc]
            v = v_ref[rows, vc].astype(BF16)
            att = jnp.where(mask_d, _dot_nt((q * eq_d[rows, kc]).astype(BF16),
                                            (k * ek_d[rows, kc]).astype(BF16)), 0.0)
            for (m, eq_m, ek_m), mask in zip(levels, masks):
                att = jnp.where(mask, _dot_nt((q * eq_m[rows, kc]).astype(BF16),
                                              (k * ek_m[rows, kc]).astype(BF16)), att)
            s_old = s_scr[b, h]
            o = _dot(att.astype(BF16), v) + _dot((q * eq_c[rows, kc]).astype(BF16), s_old.astype(BF16))
            kv = _dot_tn((k * ek_c[rows, kc]).astype(BF16), v)
            tot = jnp.broadcast_to(p[r0 + chunk - 1:r0 + chunk, kc], (8, DK_GLA))
            t_hi, t_mid, t_lo = _split_bf16(tot, 3)
            decay = jnp.exp(_dot_tn(t_hi, sel8) + (_dot_tn(t_mid, sel8) + _dot_tn(t_lo, sel8)))
            s_scr[b, h] = decay * s_old + kv
            outs[h][b] = _rms(o) * nw * _silu(g_ref[rows, vc])
    for h in range(H_GLA):
        o_ref[:, h * DV_GLA:(h + 1) * DV_GLA] = jnp.concatenate(outs[h], axis=0).astype(BF16)

    def emit_state():
        @pl.when(t == pl.num_programs(1) - 1)
        def _():
            sout_ref[...] = s_scr[...]
    return emit_state


def _ret_body(q_ref, k_ref, v_ref, g_ref, cos_ref, sin_ref, o_ref, sout_ref, s_scr, *, chunk, nseq):
    t = pl.program_id(1)

    half = DK_RET // 2
    ti = lax.broadcasted_iota(I32, (chunk, chunk), 0)
    si = lax.broadcasted_iota(I32, (chunk, chunk), 1)
    diff = (ti - si).astype(F32)
    idx = lax.broadcasted_iota(I32, (chunk, 1), 0).astype(F32)

    outs = [[None] * nseq for _ in range(H_RET)]
    for h in range(H_RET):
        lg = math.log(1.0 - 2.0 ** (-5.0 - h))
        dmat = jnp.where(diff >= 0, jnp.exp(lg * jnp.maximum(diff, 0.0)), 0.0)
        q_dec = jnp.exp(lg * (idx + 1.0))
        k_dec = jnp.exp(lg * (chunk - 1.0 - idx))
        c_dec = math.exp(lg * chunk)
        for b in range(nseq):
            rows = slice(b * chunk, (b + 1) * chunk)
            cos = cos_ref[rows, :]
            sin = sin_ref[rows, :]
            c1 = slice(h * DK_RET, h * DK_RET + half)
            c2 = slice(h * DK_RET + half, (h + 1) * DK_RET)
            vc = slice(h * DV_RET, (h + 1) * DV_RET)
            q1, q2 = q_ref[rows, c1], q_ref[rows, c2]
            k1, k2 = k_ref[rows, c1], k_ref[rows, c2]
            q = jnp.concatenate([q1 * cos - q2 * sin, q1 * sin + q2 * cos], axis=-1)
            k = jnp.concatenate([k1 * cos - k2 * sin, k1 * sin + k2 * cos], axis=-1) * (DK_RET ** -0.5)
            v = v_ref[rows, vc].astype(BF16)
            qb = q.astype(BF16)
            s_old = s_scr[b, h]
            att = _dot_nt(qb, k.astype(BF16)) * dmat
            o = _dot(att.astype(BF16), v) + _dot(qb, s_old.astype(BF16)) * q_dec
            s_scr[b, h] = c_dec * s_old + _dot_tn((k * k_dec).astype(BF16), v)
            outs[h][b] = _rms(o) * _silu(g_ref[rows, vc])
    for h in range(H_RET):
        o_ref[:, h * DV_RET:(h + 1) * DV_RET] = jnp.concatenate(outs[h], axis=0).astype(BF16)

    @pl.when(t == pl.num_programs(1) - 1)
    def _():
        sout_ref[...] = s_scr[...]


def _gla_ret_body(*refs, chunk, nseq, has_init):
    n_gla, n_ret = 8 + has_init, 6 + has_init
    gla_in, ret_in = refs[:n_gla], refs[n_gla:n_gla + n_ret]
    oa_ref, sg_ref, or_ref, sr_ref, sg_scr, sr_scr = refs[n_gla + n_ret:]

    @pl.when(pl.program_id(1) == 0)
    def _():
        sr_scr[...] = ret_in[-1][...] if has_init else jnp.zeros_like(sr_scr)
    emit_gla_state = _gla_body(*gla_in, oa_ref, sg_ref, sg_scr, chunk=chunk, nseq=nseq, has_init=has_init)
    _ret_body(*ret_in[:6], or_ref, sr_ref, sr_scr, chunk=chunk, nseq=nseq)
    emit_gla_state()


def _gla_ret(proj, w_gk2p, b_gk, nw, cos, sin, s0_gla, s0_ret, *, row0, n_seq, seq_len, chunk, nseq):
    has_init = s0_gla is not None
    rows = nseq * chunk
    nt = seq_len // chunk
    base = row0 // rows
    rmap = lambda col: (lambda b, t: (base + b * nt + t, col))
    omap = lambda b, t: (b * nt + t, 0)
    fixed2 = lambda b, t: (0, 0)
    smap = lambda b, t: (b, 0, 0, 0)
    tab_spec = pl.BlockSpec((rows, DK_RET // 2), lambda b, t: (t, 0))
    gla_state = pl.BlockSpec((nseq, H_GLA, DK_GLA, DV_GLA), smap)
    ret_state = pl.BlockSpec((nseq, H_RET, DK_RET, DV_RET), smap)
    gla_specs = [pl.BlockSpec((rows, GLA_QK), rmap(C_QA // GLA_QK)),
                 pl.BlockSpec((rows, GLA_QK), rmap(C_KA // GLA_QK)),
                 pl.BlockSpec((rows, GLA_V), rmap(C_VA // GLA_V)),
                 pl.BlockSpec((rows, GLA_V), rmap(C_GA // GLA_V)),
                 pl.BlockSpec((rows, LANES), rmap(C_LR // LANES)),
                 pl.BlockSpec((2, LANES, GLA_QK), lambda b, t: (0, 0, 0)),
                 pl.BlockSpec((1, GLA_QK), fixed2),
                 pl.BlockSpec((1, DV_GLA), fixed2)]
    ret_specs = [pl.BlockSpec((rows, RET_QK), rmap(C_QR // RET_QK)),
                 pl.BlockSpec((rows, RET_QK), rmap(C_KR // RET_QK)),
                 pl.BlockSpec((rows, RET_V), rmap(C_VR // RET_V)),
                 pl.BlockSpec((rows, RET_V), rmap(C_GR // RET_V)),
                 tab_spec, tab_spec]
    gla_args = [proj, proj, proj, proj, proj, w_gk2p, b_gk, nw]
    ret_args = [proj, proj, proj, proj, cos, sin]
    if has_init:
        gla_specs.append(gla_state)
        gla_args.append(s0_gla)
        ret_specs.append(ret_state)
        ret_args.append(s0_ret)
    return pl.pallas_call(
        functools.partial(_gla_ret_body, chunk=chunk, nseq=nseq, has_init=has_init),
        out_shape=(jax.ShapeDtypeStruct((n_seq * seq_len, GLA_V), BF16),
                   jax.ShapeDtypeStruct((n_seq, H_GLA, DK_GLA, DV_GLA), F32),
                   jax.ShapeDtypeStruct((n_seq * seq_len, RET_V), BF16),
                   jax.ShapeDtypeStruct((n_seq, H_RET, DK_RET, DV_RET), F32)),
        grid=(n_seq // nseq, nt),
        in_specs=gla_specs + ret_specs,
        out_specs=(pl.BlockSpec((rows, GLA_V), omap), gla_state, pl.BlockSpec((rows, RET_V), omap), ret_state),
        scratch_shapes=[pltpu.VMEM((nseq, H_GLA, DK_GLA, DV_GLA), F32),
                        pltpu.VMEM((nseq, H_RET, DK_RET, DV_RET), F32)],
        compiler_params=pltpu.CompilerParams(dimension_semantics=("arbitrary", "arbitrary"),
                                             vmem_limit_bytes=V7X_SCOPED_VMEM_BYTES),
        name="gla_ret_init" if has_init else "gla_ret",
    )(*gla_args, *ret_args)


def _rope_tables(pos, reps):
    half = DK_RET // 2
    inv = ROPE_BASE ** (-jnp.arange(half, dtype=F32) / half)
    ang = pos[:, None] * inv[None, :]
    return jnp.tile(jnp.cos(ang), (reps, 1)), jnp.tile(jnp.sin(ang), (reps, 1))


def _outproj_body(oap_ref, oas_ref, orp_ref, ors_ref, xp_ref, xs_ref, w_ref, ln_ref, wrt_ref, brt_ref,
                  x1_ref, hn_ref, route_ref, counts_ref, *, n_prompt_blocks):
    step = pl.program_id(0)
    is_prompt = step < n_prompt_blocks
    oa = jnp.where(is_prompt, oap_ref[...], oas_ref[...])
    orr = jnp.where(is_prompt, orp_ref[...], ors_ref[...])
    x = jnp.where(is_prompt, xp_ref[...], xs_ref[...])
    x1 = x + _dot(oa, w_ref[:GLA_V, :]) + _dot(orr, w_ref[GLA_V:, :])
    x1_ref[...] = x1
    hn = _rms(x1) * ln_ref[...]
    hn_ref[...] = hn

    logits = _dot_split(hn, wrt_ref[0], wrt_ref[1]) + brt_ref[...]
    lane_i = lax.broadcasted_iota(I32, logits.shape, 1)
    lane = lane_i.astype(F32)
    grp = (lane_i >> (EXPERTS_PER_GROUP.bit_length() - 1)).astype(F32)
    neg = -jnp.inf
    far = float(LANES)

    gl = jnp.where((lane_i >= N_EXPERTS) & (lane_i < N_EXPERTS + N_GROUPS), logits, neg)
    gmax = jnp.max(gl, axis=-1, keepdims=True)
    g_w = 1.0 / jnp.sum(jnp.exp(gl - gmax), axis=-1, keepdims=True)
    g_idx = jnp.min(jnp.where(gl == gmax, lane, far), axis=-1, keepdims=True) - float(N_EXPERTS)

    el = jnp.where((lane_i < N_EXPERTS) & (grp == g_idx), logits, neg)
    m1 = jnp.max(el, axis=-1, keepdims=True)
    esum = jnp.sum(jnp.exp(el - m1), axis=-1, keepdims=True)
    i1 = jnp.min(jnp.where(el == m1, lane, far), axis=-1, keepdims=True)
    el2 = jnp.where(lane == i1, neg, el)
    m2 = jnp.max(el2, axis=-1, keepdims=True)
    i2 = jnp.min(jnp.where(el2 == m2, lane, far), axis=-1, keepdims=True)
    p1 = 1.0 / esum
    p2 = jnp.exp(m2 - m1) / esum
    gate1 = g_w * (p1 / (p1 + p2))
    gate2 = g_w * (p2 / (p1 + p2))
    @pl.when(step == 0)
    def _():
        counts_ref[...] = jnp.zeros_like(counts_ref)

    tm = logits.shape[0]
    earlier = (lax.broadcasted_iota(I32, (tm, tm), 0) > lax.broadcasted_iota(I32, (tm, tm), 1)).astype(BF16)
    running = counts_ref[0:1, :]
    ranks = []
    for idx in (i1, i2):
        onehot = (lane == idx).astype(F32)
        before = _dot(earlier, onehot.astype(BF16)) + running
        ranks.append(jnp.sum(onehot * before, axis=-1, keepdims=True))
        running = running + jnp.sum(onehot, axis=0, keepdims=True)
    counts_ref[...] = jnp.broadcast_to(running, counts_ref.shape)

    route_ref[...] = jnp.where(lane_i == 0, i1,
                               jnp.where(lane_i == 1, i2,
                                         jnp.where(lane_i == 2, gate1,
                                                   jnp.where(lane_i == 3, gate2,
                                                             jnp.where(lane_i == 4, ranks[0],
                                                                       jnp.where(lane_i == 5, ranks[1], 0.0))))))


def _outproj_router(oa_p, oa_s, or_p, or_s, xp, xs, w_out, ln, w_rt, b_rt):
    n = xp.shape[0] + xs.shape[0]
    nbp = xp.shape[0] // PROJ_TM
    pmap = lambda i: (jnp.minimum(i, nbp - 1), 0)
    smap = lambda i: (jnp.maximum(i - nbp, 0), 0)
    row = lambda i: (i, 0)
    fixed = lambda i: (0, 0)
    return pl.pallas_call(
        functools.partial(_outproj_body, n_prompt_blocks=nbp),
        out_shape=(jax.ShapeDtypeStruct((n, D_MODEL), F32),
                   jax.ShapeDtypeStruct((n, D_MODEL), F32),
                   jax.ShapeDtypeStruct((n, LANES), F32),
                   jax.ShapeDtypeStruct((SUBLANES, LANES), F32)),
        grid=(n // PROJ_TM,),
        in_specs=[pl.BlockSpec((PROJ_TM, GLA_V), pmap), pl.BlockSpec((PROJ_TM, GLA_V), smap),
                  pl.BlockSpec((PROJ_TM, RET_V), pmap), pl.BlockSpec((PROJ_TM, RET_V), smap),
                  pl.BlockSpec((PROJ_TM, D_MODEL), pmap), pl.BlockSpec((PROJ_TM, D_MODEL), smap),
                  pl.BlockSpec((GLA_V + RET_V, D_MODEL), fixed, pipeline_mode=pl.Buffered(1)),
                  pl.BlockSpec((1, D_MODEL), fixed),
                  pl.BlockSpec((2, D_MODEL, LANES), lambda i: (0, 0, 0)),
                  pl.BlockSpec((1, LANES), fixed)],
        out_specs=(pl.BlockSpec((PROJ_TM, D_MODEL), row), pl.BlockSpec((PROJ_TM, D_MODEL), row),
                   pl.BlockSpec((PROJ_TM, LANES), row), pl.BlockSpec((SUBLANES, LANES), fixed)),
        compiler_params=pltpu.CompilerParams(dimension_semantics=("arbitrary",),
                                             vmem_limit_bytes=V7X_SCOPED_VMEM_BYTES),
        name="outproj_router",
    )(oa_p, oa_s, or_p, or_s, xp, xs, w_out, ln, w_rt, b_rt)


def _moe_body(texp_ref, nval_ref, pos_ref, hn_hbm, wg_ref, wu_ref, wd_ref, y_hbm,
              inv, xbuf, ybuf, gsem, ssem, *, n_tokens):
    tm = MOE_TM
    i = pl.program_id(0)
    nt = pl.num_programs(0)
    slot = i % 2

    def for_rows(n_rows, fn):
        n_grp = n_rows // SUBLANES

        def group(g, c):
            for j in range(SUBLANES):
                fn(g * SUBLANES + j)
            return c
        lax.fori_loop(0, n_grp, group, 0)

        def single(r, c):
            fn(r)
            return c
        lax.fori_loop(n_grp * SUBLANES, n_rows, single, 0)

    def gather_copy(tok, r, slot_):
        return pltpu.make_async_copy(hn_hbm.at[pl.ds(tok, 1), :], xbuf.at[slot_, pl.ds(r, 1), :], gsem.at[slot_])

    def scatter_copy(r, dst, slot_):
        return pltpu.make_async_copy(ybuf.at[slot_, pl.ds(r, 1), :], y_hbm.at[pl.ds(dst, 1), :], ssem.at[slot_])

    def start_gather(tile, slot_):
        for_rows(nval_ref[tile], lambda r: gather_copy(inv[tile * tm + r] >> 1, r, slot_).start())

    def wait_rows(n_rows, group_copy, row_copy):
        n_grp = pl.multiple_of((n_rows // SUBLANES) * SUBLANES, SUBLANES)

        @pl.when(n_grp > 0)
        def _():
            group_copy(n_grp).wait()

        def single(r, c):
            row_copy().wait()
            return c
        lax.fori_loop(n_grp, n_rows, single, 0)

    def wait_gather(slot_, n_rows):
        wait_rows(n_rows,
                  lambda m: pltpu.make_async_copy(hn_hbm.at[pl.ds(0, m), :], xbuf.at[slot_, pl.ds(0, m), :],
                                                  gsem.at[slot_]),
                  lambda: gather_copy(0, 0, slot_))

    def wait_scatter(slot_, n_rows):
        wait_rows(n_rows,
                  lambda m: pltpu.make_async_copy(ybuf.at[slot_, pl.ds(0, m), :], y_hbm.at[pl.ds(0, m), :],
                                                  ssem.at[slot_]),
                  lambda: scatter_copy(0, 0, slot_))

    @pl.when(i == 0)
    def _():
        def body(a, c):
            inv[pos_ref[a]] = a
            return c
        lax.fori_loop(0, 2 * n_tokens, body, 0, unroll=8)
        xbuf[...] = jnp.zeros_like(xbuf)
        start_gather(0, 0)

    @pl.when(i + 1 < nt)
    def _():
        start_gather(jnp.minimum(i + 1, nt - 1), 1 - slot)

    @pl.when(i >= 2)
    def _():
        wait_scatter(slot, nval_ref[jnp.maximum(i - 2, 0)])

    nv = nval_ref[i]
    wait_gather(slot, nv)

    @pl.when(nv > 0)
    def _():
        h = xbuf[slot].astype(BF16)
        hid = _silu(_dot(h, wg_ref[0].astype(BF16))) * _dot(h, wu_ref[0].astype(BF16))
        ybuf[slot] = _dot(hid.astype(BF16), wd_ref[0].astype(BF16))

        def scatter_one(r):
            a = inv[i * tm + r]
            scatter_copy(r, (a & 1) * n_tokens + (a >> 1), slot).start()
        for_rows(nv, scatter_one)

    @pl.when(i == nt - 1)
    def _():
        @pl.when(i >= 1)
        def _():
            wait_scatter(1 - slot, nval_ref[jnp.maximum(i - 1, 0)])
        wait_scatter(slot, nv)


def _moe(tile_expert, n_valid, pos, hn, w_gate, w_up, w_down):
    n = hn.shape[0]
    n_tiles = tile_expert.shape[0]
    wmap = lambda i, te, nv, pos_: (te[i], 0, 0)
    return pl.pallas_call(
        functools.partial(_moe_body, n_tokens=n),
        out_shape=jax.ShapeDtypeStruct((2 * n, D_MODEL), F32),
        grid_spec=pltpu.PrefetchScalarGridSpec(
            num_scalar_prefetch=3,
            grid=(n_tiles,),
            in_specs=[pl.BlockSpec(memory_space=pl.ANY),
                      pl.BlockSpec((1, D_MODEL, D_EXPERT), wmap),
                      pl.BlockSpec((1, D_MODEL, D_EXPERT), wmap),
                      pl.BlockSpec((1, D_EXPERT, D_MODEL), wmap)],
            out_specs=pl.BlockSpec(memory_space=pl.ANY),
            scratch_shapes=[pltpu.SMEM((n_tiles * MOE_TM,), I32),
                            pltpu.VMEM((2, MOE_TM, D_MODEL), F32),
                            pltpu.VMEM((2, MOE_TM, D_MODEL), F32),
                            pltpu.SemaphoreType.DMA((2,)),
                            pltpu.SemaphoreType.DMA((2,))]),
        compiler_params=pltpu.CompilerParams(dimension_semantics=("arbitrary",),
                                             vmem_limit_bytes=V7X_SCOPED_VMEM_BYTES),
        name="moe",
    )(tile_expert, n_valid, pos, hn, w_gate, w_up, w_down)


def _routing_tables(route, counts, n):
    tm = MOE_TM
    n_tiles = 2 * n // tm + N_EXPERTS
    experts = route[:, 0:2].astype(I32)
    rank = route[:, 4:6].astype(I32)
    counts = counts[0, :N_EXPERTS].astype(I32)
    tiles = (counts + tm - 1) // tm
    tile_end = jnp.cumsum(tiles)
    tile_start = tile_end - tiles
    onehot = experts[:, :, None] == jnp.arange(N_EXPERTS, dtype=I32)[None, None, :]
    pos = jnp.sum(jnp.where(onehot, (tile_start * tm)[None, None, :], 0), axis=-1) + rank
    tidx = jnp.arange(n_tiles, dtype=I32)
    t_exp = jnp.sum((tidx[:, None] >= tile_end[None, :]).astype(I32), axis=1)
    used = tidx < tile_end[-1]
    last_exp = jnp.sum((tile_end[-1] - 1 >= tile_end).astype(I32))
    t_exp = jnp.where(used, t_exp, last_exp)
    n_valid = jnp.where(used, jnp.clip(counts[t_exp] - (tidx - tile_start[t_exp]) * tm, 0, tm), 0)
    return t_exp.astype(I32), n_valid.astype(I32), pos.reshape(2 * n)


def _final_body(x1_ref, y0_ref, y1_ref, route_ref, ln_ref, o_ref):
    z = x1_ref[...] + route_ref[:, 2:3] * y0_ref[...] + route_ref[:, 3:4] * y1_ref[...]
    o_ref[...] = _rms(z) * ln_ref[...]


def _final(x1, y2, route, ln, *, row0, rows):
    base = row0 // PROJ_TM
    slot1 = x1.shape[0] // PROJ_TM
    return pl.pallas_call(
        _final_body,
        out_shape=jax.ShapeDtypeStruct((rows, D_MODEL), F32),
        grid=(rows // PROJ_TM,),
        in_specs=[pl.BlockSpec((PROJ_TM, D_MODEL), lambda i: (base + i, 0)),
                  pl.BlockSpec((PROJ_TM, D_MODEL), lambda i: (base + i, 0)),
                  pl.BlockSpec((PROJ_TM, D_MODEL), lambda i: (slot1 + base + i, 0)),
                  pl.BlockSpec((PROJ_TM, LANES), lambda i: (base + i, 0)),
                  pl.BlockSpec((1, D_MODEL), lambda i: (0, 0))],
        out_specs=pl.BlockSpec((PROJ_TM, D_MODEL), lambda i: (i, 0)),
        compiler_params=pltpu.CompilerParams(dimension_semantics=("arbitrary",)),
        name="final_norm",
    )(x1, y2, y2, route, ln)


def kernel(x_prompt, x_sample, state_gla, state_ret, ln_attn, w_in, w_gk2, b_gk, gla_norm_w, w_out, ln_ffn, w_router_group, b_router_group, w_router_expert, b_router_expert, w_exp_gate, w_exp_up, w_exp_down, ln_final):
    bp, tp, d = x_prompt.shape
    bs, ts, _ = x_sample.shape
    assert d == D_MODEL and w_in.shape == (1, D_MODEL, IN_COLS_SRC)
    n_p, n_s = bp * tp, bs * ts
    n = n_p + n_s
    assert n_p % PROJ_TM == 0 and n_s % PROJ_TM == 0 and tp % ATTN_CHUNK == 0

    xp = x_prompt.reshape(n_p, d)
    xs = x_sample.reshape(n_s, d)

    hi_lo = lambda a: jnp.stack(_split_bf16(a, 2))
    w_in_p, w_lr = _wprep(w_in[0].T)
    w_gk2p = hi_lo(jnp.concatenate([w_gk2[0], jnp.zeros((LANES - GLA_LOWRANK, GLA_QK), F32)], axis=0))
    w_rt = hi_lo(jnp.concatenate([w_router_expert[0], w_router_group[0],
                                  jnp.zeros((d, LANES - N_EXPERTS - N_GROUPS), F32)], axis=1))
    b_rt = jnp.concatenate([b_router_expert[0], b_router_group[0],
                            jnp.zeros((LANES - N_EXPERTS - N_GROUPS,), F32)])[None, :]

    proj = _inproj(xp, xs, ln_attn, w_in_p, w_lr)

    cos_p, sin_p = _rope_tables(jnp.arange(tp, dtype=F32), 1)
    cos_s, sin_s = _rope_tables(jnp.arange(ts, dtype=F32) + float(PAST_LEN), SAMPLE_SEQS_PER_STEP)
    oa_p, sg_p, or_p, sr_p = _gla_ret(proj, w_gk2p, b_gk, gla_norm_w, cos_p, sin_p, None, None,
                                      row0=0, n_seq=bp, seq_len=tp, chunk=ATTN_CHUNK, nseq=1)
    oa_s, sg_s, or_s, sr_s = _gla_ret(proj, w_gk2p, b_gk, gla_norm_w, cos_s, sin_s, state_gla[0], state_ret[0],
                                      row0=n_p, n_seq=bs, seq_len=ts, chunk=ts, nseq=SAMPLE_SEQS_PER_STEP)

    x1, hn, route, counts = _outproj_router(oa_p, oa_s, or_p, or_s, xp, xs, w_out[0].astype(BF16),
                                            ln_ffn, w_rt, b_rt)

    t_exp, n_valid, pos = _routing_tables(route, counts, n)
    y2 = _moe(t_exp, n_valid, pos, hn, w_exp_gate[0], w_exp_up[0], w_exp_down[0])

    ln_f = ln_final[None, :]
    y_p = _final(x1, y2, route, ln_f, row0=0, rows=n_p).reshape(bp, tp, d)
    y_s = _final(x1, y2, route, ln_f, row0=n_p, rows=n_s).reshape(bs, ts, d)
    return (y_p, y_s, sg_p[None], sr_p[None], sg_s[None], sr_s[None])
```

```python
import functools
import math

import jax
import jax.numpy as jnp
from jax import lax
from jax.experimental import pallas as pl
from jax.experimental.pallas import tpu as pltpu

F32 = jnp.float32
BF16 = jnp.bfloat16
I32 = jnp.int32

D_MODEL = 2048
PAST_LEN = 16384
H_GLA, DK_GLA, DV_GLA = 4, 128, 256
GLA_LOWRANK = 16
GLA_GATE_NORM = 16.0
GLA_SUBCHUNK = 16
H_RET, DK_RET, DV_RET = 4, 256, 256
ROPE_BASE = 10000.0
N_GROUPS, EXPERTS_PER_GROUP, D_EXPERT = 4, 8, 512
N_EXPERTS = N_GROUPS * EXPERTS_PER_GROUP
EPS = 1e-6

LANES = 128
SUBLANES = 8
V7X_SCOPED_VMEM_BYTES = 56 * 1024 * 1024

GLA_QK = H_GLA * DK_GLA
GLA_V = H_GLA * DV_GLA
RET_QK = H_RET * DK_RET
RET_V = H_RET * DV_RET
C_QA, C_KA, C_VA, C_GA = 0, GLA_QK, 2 * GLA_QK, 2 * GLA_QK + GLA_V
C_QR = C_GA + GLA_V
C_KR, C_VR, C_GR = C_QR + RET_QK, C_QR + 2 * RET_QK, C_QR + 2 * RET_QK + RET_V
C_LR = C_GR + RET_V
P_COLS = C_LR + LANES
IN_COLS_SRC = 2 * GLA_QK + GLA_V + GLA_LOWRANK + GLA_V + 2 * RET_QK + 2 * RET_V

PROJ_TM = 256
W_CHUNK = 1024
MOE_TM = 256
ATTN_CHUNK = 128
SAMPLE_SEQS_PER_STEP = 4

def _dot(a, b):
    return jnp.dot(a, b, preferred_element_type=F32)


def _dot_nt(a, b):
    return lax.dot_general(a, b, (((1,), (1,)), ((), ())), preferred_element_type=F32)


def _dot_tn(a, b):
    return lax.dot_general(a, b, (((0,), (0,)), ((), ())), preferred_element_type=F32)


def _split_bf16(x, parts):
    out = []
    for _ in range(parts):
        p = x.astype(BF16)
        out.append(p)
        x = x - p.astype(F32)
    return out


def _dot_split(a, b_hi, b_lo):
    a_hi, a_lo = _split_bf16(a, 2)
    return _dot(a_hi, b_hi) + (_dot(a_lo, b_hi) + _dot(a_hi, b_lo))


def _rms(x):
    return x * lax.rsqrt(jnp.mean(x * x, axis=-1, keepdims=True) + EPS)


def _silu(x):
    return x * jax.nn.sigmoid(x)


def _wprep_body(a_ref, b_ref, o_ref, olr_ref):
    j = pl.program_id(0)
    a = a_ref[...]
    b = b_ref[...]
    shifted = jnp.concatenate([a[GLA_LOWRANK:, :], b], axis=0)
    o_ref[...] = jnp.where(j >= C_GA // W_CHUNK, shifted, a).T.astype(BF16)

    @pl.when(j == C_GA // W_CHUNK - 1)
    def _():
        padded = jnp.concatenate([b, jnp.zeros((LANES - GLA_LOWRANK, D_MODEL), F32)], axis=0)
        olr_ref[...] = padded.T.astype(BF16)


def _wprep(wt):
    per = W_CHUNK // GLA_LOWRANK
    return pl.pallas_call(
        _wprep_body,
        out_shape=(jax.ShapeDtypeStruct((D_MODEL, C_LR), BF16), jax.ShapeDtypeStruct((D_MODEL, LANES), BF16)),
        grid=(C_LR // W_CHUNK,),
        in_specs=[pl.BlockSpec((W_CHUNK, D_MODEL), lambda j: (j, 0)),
                  pl.BlockSpec((GLA_LOWRANK, D_MODEL), lambda j: (per * (j + 1), 0))],
        out_specs=(pl.BlockSpec((D_MODEL, W_CHUNK), lambda j: (0, j)),
                   pl.BlockSpec((D_MODEL, LANES), lambda j: (0, 0))),
        compiler_params=pltpu.CompilerParams(dimension_semantics=("arbitrary",),
                                             vmem_limit_bytes=V7X_SCOPED_VMEM_BYTES),
        name="wprep",
    )(wt, wt)


def _inproj_body(xp_ref, xs_ref, ln_ref, w_ref, wlr_ref, o_ref, *, n_prompt_blocks):
    x = jnp.where(pl.program_id(0) < n_prompt_blocks, xp_ref[...], xs_ref[...])
    h = (_rms(x) * ln_ref[...]).astype(BF16)
    for c in range(C_LR // W_CHUNK):
        o_ref[:, c * W_CHUNK:(c + 1) * W_CHUNK] = _dot(h, w_ref[:, c * W_CHUNK:(c + 1) * W_CHUNK])
    o_ref[:, C_LR:] = _dot(h, wlr_ref[...])


def _inproj(xp, xs, ln, w, w_lr):
    nbp = xp.shape[0] // PROJ_TM
    n = xp.shape[0] + xs.shape[0]
    fixed = lambda i: (0, 0)
    return pl.pallas_call(
        functools.partial(_inproj_body, n_prompt_blocks=nbp),
        out_shape=jax.ShapeDtypeStruct((n, P_COLS), F32),
        grid=(n // PROJ_TM,),
        in_specs=[pl.BlockSpec((PROJ_TM, D_MODEL), lambda i: (jnp.minimum(i, nbp - 1), 0)),
                  pl.BlockSpec((PROJ_TM, D_MODEL), lambda i: (jnp.maximum(i - nbp, 0), 0)),
                  pl.BlockSpec((1, D_MODEL), fixed),
                  pl.BlockSpec((D_MODEL, C_LR), fixed, pipeline_mode=pl.Buffered(1)),
                  pl.BlockSpec((D_MODEL, LANES), fixed)],
        out_specs=pl.BlockSpec((PROJ_TM, P_COLS), lambda i: (i, 0)),
        compiler_params=pltpu.CompilerParams(dimension_semantics=("arbitrary",),
                                             vmem_limit_bytes=V7X_SCOPED_VMEM_BYTES),
        name="inproj",
    )(xp, xs, ln, w, w_lr)


def _seg_cumsum(x, seg):
    pos = lax.broadcasted_iota(I32, x.shape, 0) & (seg - 1)
    s = 1
    while s < seg:
        x = x + jnp.where(pos >= s, pltpu.roll(x, s, axis=0), 0.0)
        s *= 2
    return x


def _block_ends(p, m):
    rows, w = p.shape
    return jnp.concatenate(
        [jnp.broadcast_to(p[j * m + m - 1:j * m + m, :], (m, w)) for j in range(rows // m)], axis=0)


def _block_starts(p, m, chunk):
    rows, w = p.shape
    pieces = []
    for j in range(rows // m):
        if (j * m) % chunk == 0:
            pieces.append(jnp.zeros((m, w), F32))
        else:
            pieces.append(jnp.broadcast_to(p[j * m - 1:j * m, :], (m, w)))
    return jnp.concatenate(pieces, axis=0)


def _gla_body(*refs, chunk, nseq, has_init):
    if has_init:
        (q_ref, k_ref, v_ref, g_ref, lr_ref, wgk_ref, bgk_ref, nw_ref, s0_ref,
         o_ref, sout_ref, s_scr) = refs
    else:
        (q_ref, k_ref, v_ref, g_ref, lr_ref, wgk_ref, bgk_ref, nw_ref,
         o_ref, sout_ref, s_scr) = refs
    t = pl.program_id(1)

    @pl.when(t == 0)
    def _():
        if has_init:
            s_scr[...] = s0_ref[...]
        else:
            s_scr[...] = jnp.zeros_like(s_scr)

    x = _dot_split(lr_ref[...], wgk_ref[0], wgk_ref[1]) + bgk_ref[...]
    log_g = -(jnp.maximum(-x, 0.0) + jnp.log1p(jnp.exp(-jnp.abs(x)))) / GLA_GATE_NORM
    p = _seg_cumsum(log_g, chunk)
    p_end = _block_ends(p, chunk)
    eq_c = jnp.exp(p)
    ek_c = jnp.exp(p_end - p)
    sub = min(GLA_SUBCHUNK, chunk)
    p_sub = p - _block_starts(p, sub, chunk)
    eq_d = jnp.exp(p_sub)
    ek_d = jnp.exp(-p_sub)
    levels = []
    m = sub
    while m < chunk:
        levels.append((m, jnp.exp(p - _block_starts(p, m, chunk)), jnp.exp(_block_ends(p, m) - p)))
        m *= 2

    ti = lax.broadcasted_iota(I32, (chunk, chunk), 0)
    si = lax.broadcasted_iota(I32, (chunk, chunk), 1)
    blk = lambda idx, size: idx >> (size.bit_length() - 1)
    mask_d = (blk(ti, sub) == blk(si, sub)) & (si <= ti)
    masks = [((blk(ti, m) & 1) == 1) & (blk(si, m) == blk(ti, m) - 1) for (m, _, _) in levels]
    sel8 = (lax.broadcasted_iota(I32, (8, DV_GLA), 0) == 0).astype(BF16)
    nw = nw_ref[...]

    outs = [[None] * nseq for _ in range(H_GLA)]
    for b in range(nseq):
        r0 = b * chunk
        rows = slice(r0, r0 + chunk)
        for h in range(H_GLA):
            kc = slice(h * DK_GLA, (h + 1) * DK_GLA)
            vc = slice(h * DV_GLA, (h + 1) * DV_GLA)
            q = q_ref[rows, kc] * (DK_GLA ** -0.5)
            k = k_ref[rows, kc]
            v = v_ref[rows, vc].astype(BF16)
            att = jnp.where(mask_d, _dot_nt((q * eq_d[rows, kc]).astype(BF16),
                                            (k * ek_d[rows, kc]).astype(BF16)), 0.0)
            for (m, eq_m, ek_m), mask in zip(levels, masks):
                att = jnp.where(mask, _dot_nt((q * eq_m[rows, kc]).astype(BF16),
                                              (k * ek_m[rows, kc]).astype(BF16)), att)
            s_old = s_scr[b, h]
            o = _dot(att.astype(BF16), v) + _dot((q * eq_c[rows, kc]).astype(BF16), s_old.astype(BF16))
            kv = _dot_tn((k * ek_c[rows, kc]).astype(BF16), v)
            tot = jnp.broadcast_to(p[r0 + chunk - 1:r0 + chunk, kc], (8, DK_GLA))
            t_hi, t_mid, t_lo = _split_bf16(tot, 3)
            decay = jnp.exp(_dot_tn(t_hi, sel8) + (_dot_tn(t_mid, sel8) + _dot_tn(t_lo, sel8)))
            s_scr[b, h] = decay * s_old + kv
            outs[h][b] = _rms(o) * nw * _silu(g_ref[rows, vc])
    for h in range(H_GLA):
        o_ref[:, h * DV_GLA:(h + 1) * DV_GLA] = jnp.concatenate(outs[h], axis=0).astype(BF16)

    def emit_state():
        @pl.when(t == pl.num_programs(1) - 1)
        def _():
            sout_ref[...] = s_scr[...]
    return emit_state


def _ret_body(q_ref, k_ref, v_ref, g_ref, cos_ref, sin_ref, o_ref, sout_ref, s_scr, *, chunk, nseq):
    t = pl.program_id(1)

    half = DK_RET // 2
    ti = lax.broadcasted_iota(I32, (chunk, chunk), 0)
    si = lax.broadcasted_iota(I32, (chunk, chunk), 1)
    diff = (ti - si).astype(F32)
    idx = lax.broadcasted_iota(I32, (chunk, 1), 0).astype(F32)

    outs = [[None] * nseq for _ in range(H_RET)]
    for h in range(H_RET):
        lg = math.log(1.0 - 2.0 ** (-5.0 - h))
        dmat = jnp.where(diff >= 0, jnp.exp(lg * jnp.maximum(diff, 0.0)), 0.0)
        q_dec = jnp.exp(lg * (idx + 1.0))
        k_dec = jnp.exp(lg * (chunk - 1.0 - idx))
        c_dec = math.exp(lg * chunk)
        for b in range(nseq):
            rows = slice(b * chunk, (b + 1) * chunk)
            cos = cos_ref[rows, :]
            sin = sin_ref[rows, :]
            c1 = slice(h * DK_RET, h * DK_RET + half)
            c2 = slice(h * DK_RET + half, (h + 1) * DK_RET)
            vc = slice(h * DV_RET, (h + 1) * DV_RET)
            q1, q2 = q_ref[rows, c1], q_ref[rows, c2]
            k1, k2 = k_ref[rows, c1], k_ref[rows, c2]
            q = jnp.concatenate([q1 * cos - q2 * sin, q1 * sin + q2 * cos], axis=-1)
            k = jnp.concatenate([k1 * cos - k2 * sin, k1 * sin + k2 * cos], axis=-1) * (DK_RET ** -0.5)
            v = v_ref[rows, vc].astype(BF16)
            qb = q.astype(BF16)
            s_old = s_scr[b, h]
            att = _dot_nt(qb, k.astype(BF16)) * dmat
            o = _dot(att.astype(BF16), v) + _dot(qb, s_old.astype(BF16)) * q_dec
            s_scr[b, h] = c_dec * s_old + _dot_tn((k * k_dec).astype(BF16), v)
            outs[h][b] = _rms(o) * _silu(g_ref[rows, vc])
    for h in range(H_RET):
        o_ref[:, h * DV_RET:(h + 1) * DV_RET] = jnp.concatenate(outs[h], axis=0).astype(BF16)

    @pl.when(t == pl.num_programs(1) - 1)
    def _():
        sout_ref[...] = s_scr[...]


def _gla_ret_body(*refs, chunk, nseq, has_init):
    n_gla, n_ret = 8 + has_init, 6 + has_init
    gla_in, ret_in = refs[:n_gla], refs[n_gla:n_gla + n_ret]
    oa_ref, sg_ref, or_ref, sr_ref, sg_scr, sr_scr = refs[n_gla + n_ret:]

    @pl.when(pl.program_id(1) == 0)
    def _():
        sr_scr[...] = ret_in[-1][...] if has_init else jnp.zeros_like(sr_scr)
    emit_gla_state = _gla_body(*gla_in, oa_ref, sg_ref, sg_scr, chunk=chunk, nseq=nseq, has_init=has_init)
    _ret_body(*ret_in[:6], or_ref, sr_ref, sr_scr, chunk=chunk, nseq=nseq)
    emit_gla_state()


def _gla_ret(proj, w_gk2p, b_gk, nw, cos, sin, s0_gla, s0_ret, *, row0, n_seq, seq_len, chunk, nseq):
    has_init = s0_gla is not None
    rows = nseq * chunk
    nt = seq_len // chunk
    base = row0 // rows
    rmap = lambda col: (lambda b, t: (base + b * nt + t, col))
    omap = lambda b, t: (b * nt + t, 0)
    fixed2 = lambda b, t: (0, 0)
    smap = lambda b, t: (b, 0, 0, 0)
    tab_spec = pl.BlockSpec((rows, DK_RET // 2), lambda b, t: (t, 0))
    gla_state = pl.BlockSpec((nseq, H_GLA, DK_GLA, DV_GLA), smap)
    ret_state = pl.BlockSpec((nseq, H_RET, DK_RET, DV_RET), smap)
    gla_specs = [pl.BlockSpec((rows, GLA_QK), rmap(C_QA // GLA_QK)),
                 pl.BlockSpec((rows, GLA_QK), rmap(C_KA // GLA_QK)),
                 pl.BlockSpec((rows, GLA_V), rmap(C_VA // GLA_V)),
                 pl.BlockSpec((rows, GLA_V), rmap(C_GA // GLA_V)),
                 pl.BlockSpec((rows, LANES), rmap(C_LR // LANES)),
                 pl.BlockSpec((2, LANES, GLA_QK), lambda b, t: (0, 0, 0)),
                 pl.BlockSpec((1, GLA_QK), fixed2),
                 pl.BlockSpec((1, DV_GLA), fixed2)]
    ret_specs = [pl.BlockSpec((rows, RET_QK), rmap(C_QR // RET_QK)),
                 pl.BlockSpec((rows, RET_QK), rmap(C_KR // RET_QK)),
                 pl.BlockSpec((rows, RET_V), rmap(C_VR // RET_V)),
                 pl.BlockSpec((rows, RET_V), rmap(C_GR // RET_V)),
                 tab_spec, tab_spec]
    gla_args = [proj, proj, proj, proj, proj, w_gk2p, b_gk, nw]
    ret_args = [proj, proj, proj, proj, cos, sin]
    if has_init:
        gla_specs.append(gla_state)
        gla_args.append(s0_gla)
        ret_specs.append(ret_state)
        ret_args.append(s0_ret)
    return pl.pallas_call(
        functools.partial(_gla_ret_body, chunk=chunk, nseq=nseq, has_init=has_init),
        out_shape=(jax.ShapeDtypeStruct((n_seq * seq_len, GLA_V), BF16),
                   jax.ShapeDtypeStruct((n_seq, H_GLA, DK_GLA, DV_GLA), F32),
                   jax.ShapeDtypeStruct((n_seq * seq_len, RET_V), BF16),
                   jax.ShapeDtypeStruct((n_seq, H_RET, DK_RET, DV_RET), F32)),
        grid=(n_seq // nseq, nt),
        in_specs=gla_specs + ret_specs,
        out_specs=(pl.BlockSpec((rows, GLA_V), omap), gla_state, pl.BlockSpec((rows, RET_V), omap), ret_state),
        scratch_shapes=[pltpu.VMEM((nseq, H_GLA, DK_GLA, DV_GLA), F32),
                        pltpu.VMEM((nseq, H_RET, DK_RET, DV_RET), F32)],
        compiler_params=pltpu.CompilerParams(dimension_semantics=("arbitrary", "arbitrary"),
                                             vmem_limit_bytes=V7X_SCOPED_VMEM_BYTES),
        name="gla_ret_init" if has_init else "gla_ret",
    )(*gla_args, *ret_args)


def _rope_tables(pos, reps):
    half = DK_RET // 2
    inv = ROPE_BASE ** (-jnp.arange(half, dtype=F32) / half)
    ang = pos[:, None] * inv[None, :]
    return jnp.tile(jnp.cos(ang), (reps, 1)), jnp.tile(jnp.sin(ang), (reps, 1))


def _outproj_body(oap_ref, oas_ref, orp_ref, ors_ref, xp_ref, xs_ref, w_ref, ln_ref, wrt_ref, brt_ref,
                  x1_ref, hn_ref, route_ref, counts_ref, *, n_prompt_blocks):
    step = pl.program_id(0)
    is_prompt = step < n_prompt_blocks
    oa = jnp.where(is_prompt, oap_ref[...], oas_ref[...])
    orr = jnp.where(is_prompt, orp_ref[...], ors_ref[...])
    x = jnp.where(is_prompt, xp_ref[...], xs_ref[...])
    x1 = x + _dot(oa, w_ref[:GLA_V, :]) + _dot(orr, w_ref[GLA_V:, :])
    x1_ref[...] = x1
    hn = _rms(x1) * ln_ref[...]
    hn_ref[...] = hn

    logits = _dot_split(hn, wrt_ref[0], wrt_ref[1]) + brt_ref[...]
    lane_i = lax.broadcasted_iota(I32, logits.shape, 1)
    lane = lane_i.astype(F32)
    grp = (lane_i >> (EXPERTS_PER_GROUP.bit_length() - 1)).astype(F32)
    neg = -jnp.inf
    far = float(LANES)

    gl = jnp.where((lane_i >= N_EXPERTS) & (lane_i < N_EXPERTS + N_GROUPS), logits, neg)
    gmax = jnp.max(gl, axis=-1, keepdims=True)
    g_w = 1.0 / jnp.sum(jnp.exp(gl - gmax), axis=-1, keepdims=True)
    g_idx = jnp.min(jnp.where(gl == gmax, lane, far), axis=-1, keepdims=True) - float(N_EXPERTS)

    el = jnp.where((lane_i < N_EXPERTS) & (grp == g_idx), logits, neg)
    m1 = jnp.max(el, axis=-1, keepdims=True)
    esum = jnp.sum(jnp.exp(el - m1), axis=-1, keepdims=True)
    i1 = jnp.min(jnp.where(el == m1, lane, far), axis=-1, keepdims=True)
    el2 = jnp.where(lane == i1, neg, el)
    m2 = jnp.max(el2, axis=-1, keepdims=True)
    i2 = jnp.min(jnp.where(el2 == m2, lane, far), axis=-1, keepdims=True)
    p1 = 1.0 / esum
    p2 = jnp.exp(m2 - m1) / esum
    gate1 = g_w * (p1 / (p1 + p2))
    gate2 = g_w * (p2 / (p1 + p2))
    @pl.when(step == 0)
    def _():
        counts_ref[...] = jnp.zeros_like(counts_ref)

    tm = logits.shape[0]
    earlier = (lax.broadcasted_iota(I32, (tm, tm), 0) > lax.broadcasted_iota(I32, (tm, tm), 1)).astype(BF16)
    running = counts_ref[0:1, :]
    ranks = []
    for idx in (i1, i2):
        onehot = (lane == idx).astype(F32)
        before = _dot(earlier, onehot.astype(BF16)) + running
        ranks.append(jnp.sum(onehot * before, axis=-1, keepdims=True))
        running = running + jnp.sum(onehot, axis=0, keepdims=True)
    counts_ref[...] = jnp.broadcast_to(running, counts_ref.shape)

    route_ref[...] = jnp.where(lane_i == 0, i1,
                               jnp.where(lane_i == 1, i2,
                                         jnp.where(lane_i == 2, gate1,
                                                   jnp.where(lane_i == 3, gate2,
                                                             jnp.where(lane_i == 4, ranks[0],
                                                                       jnp.where(lane_i == 5, ranks[1], 0.0))))))


def _outproj_router(oa_p, oa_s, or_p, or_s, xp, xs, w_out, ln, w_rt, b_rt):
    n = xp.shape[0] + xs.shape[0]
    nbp = xp.shape[0] // PROJ_TM
    pmap = lambda i: (jnp.minimum(i, nbp - 1), 0)
    smap = lambda i: (jnp.maximum(i - nbp, 0), 0)
    row = lambda i: (i, 0)
    fixed = lambda i: (0, 0)
    return pl.pallas_call(
        functools.partial(_outproj_body, n_prompt_blocks=nbp),
        out_shape=(jax.ShapeDtypeStruct((n, D_MODEL), F32),
                   jax.ShapeDtypeStruct((n, D_MODEL), F32),
                   jax.ShapeDtypeStruct((n, LANES), F32),
                   jax.ShapeDtypeStruct((SUBLANES, LANES), F32)),
        grid=(n // PROJ_TM,),
        in_specs=[pl.BlockSpec((PROJ_TM, GLA_V), pmap), pl.BlockSpec((PROJ_TM, GLA_V), smap),
                  pl.BlockSpec((PROJ_TM, RET_V), pmap), pl.BlockSpec((PROJ_TM, RET_V), smap),
                  pl.BlockSpec((PROJ_TM, D_MODEL), pmap), pl.BlockSpec((PROJ_TM, D_MODEL), smap),
                  pl.BlockSpec((GLA_V + RET_V, D_MODEL), fixed, pipeline_mode=pl.Buffered(1)),
                  pl.BlockSpec((1, D_MODEL), fixed),
                  pl.BlockSpec((2, D_MODEL, LANES), lambda i: (0, 0, 0)),
                  pl.BlockSpec((1, LANES), fixed)],
        out_specs=(pl.BlockSpec((PROJ_TM, D_MODEL), row), pl.BlockSpec((PROJ_TM, D_MODEL), row),
                   pl.BlockSpec((PROJ_TM, LANES), row), pl.BlockSpec((SUBLANES, LANES), fixed)),
        compiler_params=pltpu.CompilerParams(dimension_semantics=("arbitrary",),
                                             vmem_limit_bytes=V7X_SCOPED_VMEM_BYTES),
        name="outproj_router",
    )(oa_p, oa_s, or_p, or_s, xp, xs, w_out, ln, w_rt, b_rt)


def _moe_body(texp_ref, nval_ref, pos_ref, hn_hbm, wg_ref, wu_ref, wd_ref, y_hbm,
              inv, xbuf, ybuf, gsem, ssem, *, n_tokens):
    tm = MOE_TM
    i = pl.program_id(0)
    nt = pl.num_programs(0)
    slot = i % 2

    def for_rows(n_rows, fn):
        n_grp = lax.shift_right_logical(n_rows, SUBLANES.bit_length() - 1)

        def group(g, c):
            for j in range(SUBLANES):
                fn(g, j)
            return c
        lax.fori_loop(0, n_grp, group, 0)

        def single(r, c):
            fn(lax.shift_right_logical(r, SUBLANES.bit_length() - 1), r & (SUBLANES - 1))
            return c
        lax.fori_loop(n_grp * SUBLANES, n_rows, single, 0)

    def gather_copy(tok, g, j, slot_):
        return pltpu.make_async_copy(hn_hbm.at[pl.ds(tok, 1), :], xbuf.at[slot_, g, pl.ds(j, 1), :], gsem.at[slot_])

    def scatter_copy(g, j, dst, slot_):
        return pltpu.make_async_copy(ybuf.at[slot_, g, pl.ds(j, 1), :], y_hbm.at[pl.ds(dst, 1), :], ssem.at[slot_])

    def start_gather(tile, slot_):
        for_rows(nval_ref[tile],
                 lambda g, j: gather_copy(inv[tile * tm + g * SUBLANES + j] >> 1, g, j, slot_).start())

    def wait_rows(n_rows, group_copy, row_copy):
        n_grp = lax.shift_right_logical(n_rows, SUBLANES.bit_length() - 1)

        @pl.when(n_grp > 0)
        def _():
            group_copy(n_grp).wait()

        def single(r, c):
            row_copy().wait()
            return c
        lax.fori_loop(n_grp * SUBLANES, n_rows, single, 0)

    def wait_gather(slot_, n_rows):
        wait_rows(n_rows,
                  lambda m: pltpu.make_async_copy(xbuf.at[slot_, pl.ds(0, m)], xbuf.at[slot_, pl.ds(0, m)],
                                                  gsem.at[slot_]),
                  lambda: gather_copy(0, 0, 0, slot_))

    def wait_scatter(slot_, n_rows):
        wait_rows(n_rows,
                  lambda m: pltpu.make_async_copy(ybuf.at[slot_, pl.ds(0, m)], ybuf.at[slot_, pl.ds(0, m)],
                                                  ssem.at[slot_]),
                  lambda: scatter_copy(0, 0, 0, slot_))

    @pl.when(i == 0)
    def _():
        def body(a, c):
            inv[pos_ref[a]] = a
            return c
        lax.fori_loop(0, 2 * n_tokens, body, 0, unroll=8)
        xbuf[...] = jnp.zeros_like(xbuf)
        start_gather(0, 0)

    @pl.when(i + 1 < nt)
    def _():
        start_gather(jnp.minimum(i + 1, nt - 1), 1 - slot)

    @pl.when(i >= 2)
    def _():
        wait_scatter(slot, nval_ref[jnp.maximum(i - 2, 0)])

    nv = nval_ref[i]
    wait_gather(slot, nv)

    @pl.when(nv > 0)
    def _():
        h = xbuf[slot].reshape(tm, D_MODEL).astype(BF16)
        hid = _silu(_dot(h, wg_ref[0].astype(BF16))) * _dot(h, wu_ref[0].astype(BF16))
        ybuf[slot] = _dot(hid.astype(BF16), wd_ref[0].astype(BF16)).reshape(tm // SUBLANES, SUBLANES, D_MODEL)

        def scatter_one(g, j):
            a = inv[i * tm + g * SUBLANES + j]
            scatter_copy(g, j, (a & 1) * n_tokens + (a >> 1), slot).start()
        for_rows(nv, scatter_one)

    @pl.when(i == nt - 1)
    def _():
        @pl.when(i >= 1)
        def _():
            wait_scatter(1 - slot, nval_ref[jnp.maximum(i - 1, 0)])
        wait_scatter(slot, nv)


def _moe(tile_expert, n_valid, pos, hn, w_gate, w_up, w_down):
    n = hn.shape[0]
    n_tiles = tile_expert.shape[0]
    wmap = lambda i, te, nv, pos_: (te[i], 0, 0)
    return pl.pallas_call(
        functools.partial(_moe_body, n_tokens=n),
        out_shape=jax.ShapeDtypeStruct((2 * n, D_MODEL), F32),
        grid_spec=pltpu.PrefetchScalarGridSpec(
            num_scalar_prefetch=3,
            grid=(n_tiles,),
            in_specs=[pl.BlockSpec(memory_space=pl.ANY),
                      pl.BlockSpec((1, D_MODEL, D_EXPERT), wmap),
                      pl.BlockSpec((1, D_MODEL, D_EXPERT), wmap),
                      pl.BlockSpec((1, D_EXPERT, D_MODEL), wmap)],
            out_specs=pl.BlockSpec(memory_space=pl.ANY),
            scratch_shapes=[pltpu.SMEM((n_tiles * MOE_TM,), I32),
                            pltpu.VMEM((2, MOE_TM // SUBLANES, SUBLANES, D_MODEL), F32),
                            pltpu.VMEM((2, MOE_TM // SUBLANES, SUBLANES, D_MODEL), F32),
                            pltpu.SemaphoreType.DMA((2,)),
                            pltpu.SemaphoreType.DMA((2,))]),
        compiler_params=pltpu.CompilerParams(dimension_semantics=("arbitrary",),
                                             vmem_limit_bytes=V7X_SCOPED_VMEM_BYTES),
        name="moe",
    )(tile_expert, n_valid, pos, hn, w_gate, w_up, w_down)


def _routing_tables(route, counts, n):
    tm = MOE_TM
    n_tiles = 2 * n // tm + N_EXPERTS
    experts = route[:, 0:2].astype(I32)
    rank = route[:, 4:6].astype(I32)
    counts = counts[0, :N_EXPERTS].astype(I32)
    tiles = (counts + tm - 1) // tm
    tile_end = jnp.cumsum(tiles)
    tile_start = tile_end - tiles
    onehot = experts[:, :, None] == jnp.arange(N_EXPERTS, dtype=I32)[None, None, :]
    pos = jnp.sum(jnp.where(onehot, (tile_start * tm)[None, None, :], 0), axis=-1) + rank
    tidx = jnp.arange(n_tiles, dtype=I32)
    t_exp = jnp.sum((tidx[:, None] >= tile_end[None, :]).astype(I32), axis=1)
    used = tidx < tile_end[-1]
    last_exp = jnp.sum((tile_end[-1] - 1 >= tile_end).astype(I32))
    t_exp = jnp.where(used, t_exp, last_exp)
    n_valid = jnp.where(used, jnp.clip(counts[t_exp] - (tidx - tile_start[t_exp]) * tm, 0, tm), 0)
    return t_exp.astype(I32), n_valid.astype(I32), pos.reshape(2 * n)


def _final_body(x1_ref, y0_ref, y1_ref, route_ref, ln_ref, o_ref):
    z = x1_ref[...] + route_ref[:, 2:3] * y0_ref[...] + route_ref[:, 3:4] * y1_ref[...]
    o_ref[...] = _rms(z) * ln_ref[...]


def _final(x1, y2, route, ln, *, row0, rows):
    base = row0 // PROJ_TM
    slot1 = x1.shape[0] // PROJ_TM
    return pl.pallas_call(
        _final_body,
        out_shape=jax.ShapeDtypeStruct((rows, D_MODEL), F32),
        grid=(rows // PROJ_TM,),
        in_specs=[pl.BlockSpec((PROJ_TM, D_MODEL), lambda i: (base + i, 0)),
                  pl.BlockSpec((PROJ_TM, D_MODEL), lambda i: (base + i, 0)),
                  pl.BlockSpec((PROJ_TM, D_MODEL), lambda i: (slot1 + base + i, 0)),
                  pl.BlockSpec((PROJ_TM, LANES), lambda i: (base + i, 0)),
                  pl.BlockSpec((1, D_MODEL), lambda i: (0, 0))],
        out_specs=pl.BlockSpec((PROJ_TM, D_MODEL), lambda i: (i, 0)),
        compiler_params=pltpu.CompilerParams(dimension_semantics=("arbitrary",)),
        name="final_norm",
    )(x1, y2, y2, route, ln)


def kernel(x_prompt, x_sample, state_gla, state_ret, ln_attn, w_in, w_gk2, b_gk, gla_norm_w, w_out, ln_ffn, w_router_group, b_router_group, w_router_expert, b_router_expert, w_exp_gate, w_exp_up, w_exp_down, ln_final):
    bp, tp, d = x_prompt.shape
    bs, ts, _ = x_sample.shape
    assert d == D_MODEL and w_in.shape == (1, D_MODEL, IN_COLS_SRC)
    n_p, n_s = bp * tp, bs * ts
    n = n_p + n_s
    assert n_p % PROJ_TM == 0 and n_s % PROJ_TM == 0 and tp % ATTN_CHUNK == 0

    xp = x_prompt.reshape(n_p, d)
    xs = x_sample.reshape(n_s, d)

    hi_lo = lambda a: jnp.stack(_split_bf16(a, 2))
    w_in_p, w_lr = _wprep(w_in[0].T)
    w_gk2p = hi_lo(jnp.concatenate([w_gk2[0], jnp.zeros((LANES - GLA_LOWRANK, GLA_QK), F32)], axis=0))
    w_rt = hi_lo(jnp.concatenate([w_router_expert[0], w_router_group[0],
                                  jnp.zeros((d, LANES - N_EXPERTS - N_GROUPS), F32)], axis=1))
    b_rt = jnp.concatenate([b_router_expert[0], b_router_group[0],
                            jnp.zeros((LANES - N_EXPERTS - N_GROUPS,), F32)])[None, :]

    proj = _inproj(xp, xs, ln_attn, w_in_p, w_lr)

    cos_p, sin_p = _rope_tables(jnp.arange(tp, dtype=F32), 1)
    cos_s, sin_s = _rope_tables(jnp.arange(ts, dtype=F32) + float(PAST_LEN), SAMPLE_SEQS_PER_STEP)
    oa_p, sg_p, or_p, sr_p = _gla_ret(proj, w_gk2p, b_gk, gla_norm_w, cos_p, sin_p, None, None,
                                      row0=0, n_seq=bp, seq_len=tp, chunk=ATTN_CHUNK, nseq=1)
    oa_s, sg_s, or_s, sr_s = _gla_ret(proj, w_gk2p, b_gk, gla_norm_w, cos_s, sin_s, state_gla[0], state_ret[0],
                                      row0=n_p, n_seq=bs, seq_len=ts, chunk=ts, nseq=SAMPLE_SEQS_PER_STEP)

    x1, hn, route, counts = _outproj_router(oa_p, oa_s, or_p, or_s, xp, xs, w_out[0].astype(BF16),
                                            ln_ffn, w_rt, b_rt)

    t_exp, n_valid, pos = _routing_tables(route, counts, n)
    y2 = _moe(t_exp, n_valid, pos, hn, w_exp_gate[0], w_exp_up[0], w_exp_down[0])

    ln_f = ln_final[None, :]
    y_p = _final(x1, y2, route, ln_f, row0=0, rows=n_p).reshape(bp, tp, d)
    y_s = _final(x1, y2, route, ln_f, row0=n_p, rows=n_s).reshape(bs, ts, d)
    return (y_p, y_s, sg_p[None], sr_p[None], sg_s[None], sr_s[None])
```
